```python
import math
import jax, jax.numpy as jnp
from jax import lax
import numpy as np

D_MODEL = 1024
BATCH = 2
SEQ = 8192
DEPTH = 2
DEC_BATCH = 128
DEC_SEQ = 1
PAST_LEN = 8192
PAGE_SIZE = 128

SSM_WIDTH = D_MODEL // 2
SSM_GROUP = 16
N_SSM_GROUPS = SSM_WIDTH // SSM_GROUP
SSM_STATE = 64
ATTN_WIDTH = D_MODEL - SSM_WIDTH
HEAD_DIM = 64
N_HEADS = ATTN_WIDTH // HEAD_DIM
N_KV_HEADS = 2
Q_PER_KV = N_HEADS // N_KV_HEADS
KV_WIDTH = N_KV_HEADS * HEAD_DIM
WINDOW = 128
ATTN_SCALE = HEAD_DIM ** -0.5
D_FF = 2816
IN_COLS = SSM_WIDTH + ATTN_WIDTH + 2 * KV_WIDTH
ALPHA = (2.0 * DEPTH) ** 0.25
BETA = (8.0 * DEPTH) ** -0.25
LN_EPS = 1e-5
RMS_EPS = 1e-6
DT_MIN, DT_MAX = 1e-3, 1e-1

kernel_name = "hymba_s5_swa_macaron_deepnorm_step"


def layer_norm(x, g, b):
    xf = x.astype(jnp.float32)
    mu = xf.mean(-1, keepdims=True)
    var = jnp.square(xf - mu).mean(-1, keepdims=True)
    y = (xf - mu) * lax.rsqrt(var + LN_EPS) * g.astype(jnp.float32) + b.astype(jnp.float32)
    return y.astype(x.dtype)


def rms_norm(x, g):
    xf = x.astype(jnp.float32)
    y = xf * lax.rsqrt(jnp.square(xf).mean(-1, keepdims=True) + RMS_EPS) * g.astype(jnp.float32)
    return y.astype(x.dtype)


def swiglu(x, w_in, w_down):
    gate, up = jnp.split(x @ w_in, 2, axis=-1)
    return (jax.nn.silu(gate) * up) @ w_down


def ssm_discretise(lam_re, lam_im, log_dt, b_re, b_im):
    lam_re = lam_re.astype(jnp.float32); lam_im = lam_im.astype(jnp.float32)
    dt = jnp.exp(log_dt.astype(jnp.float32))[:, None]
    mag = jnp.exp(lam_re * dt)
    a_re = mag * jnp.cos(lam_im * dt)
    a_im = mag * jnp.sin(lam_im * dt)
    num_re, num_im = a_re - 1.0, a_im
    den = jnp.square(lam_re) + jnp.square(lam_im)
    f_re = (num_re * lam_re + num_im * lam_im) / den
    f_im = (num_im * lam_re - num_re * lam_im) / den
    b_re = b_re.astype(jnp.float32); b_im = b_im.astype(jnp.float32)
    bb_re = f_re[..., None] * b_re - f_im[..., None] * b_im
    bb_im = f_re[..., None] * b_im + f_im[..., None] * b_re
    return a_re, a_im, bb_re, bb_im


def _complex_affine_combine(e1, e2):
    ar1, ai1, br1, bi1 = e1
    ar2, ai2, br2, bi2 = e2
    ar = ar2 * ar1 - ai2 * ai1
    ai = ar2 * ai1 + ai2 * ar1
    br = ar2 * br1 - ai2 * bi1 + br2
    bi = ar2 * bi1 + ai2 * br1 + bi2
    return ar, ai, br, bi


def ssm_mixer(u, h0_re, h0_im, lam_re, lam_im, log_dt, b_re, b_im, c_re, c_im, d_skip, glu_w, glu_b):
    Bn, L, _ = u.shape
    a_re, a_im, bb_re, bb_im = ssm_discretise(lam_re, lam_im, log_dt, b_re, b_im)
    uf = u.astype(jnp.float32)
    ug = uf.reshape(Bn, L, N_SSM_GROUPS, SSM_GROUP)
    bu_re = jnp.einsum('blgh,gph->blgp', ug, bb_re)
    bu_im = jnp.einsum('blgh,gph->blgp', ug, bb_im)
    h0r = h0_re.astype(jnp.float32); h0i = h0_im.astype(jnp.float32)
    bu_re = bu_re.at[:, 0].add(a_re * h0r - a_im * h0i)
    bu_im = bu_im.at[:, 0].add(a_re * h0i + a_im * h0r)
    ar_full = jnp.broadcast_to(a_re, bu_re.shape)
    ai_full = jnp.broadcast_to(a_im, bu_re.shape)
    _, _, h_re, h_im = lax.associative_scan(_complex_affine_combine,
                                            (ar_full, ai_full, bu_re, bu_im), axis=1)
    y = (jnp.einsum('blgp,ghp->blgh', h_re, c_re.astype(jnp.float32))
         - jnp.einsum('blgp,ghp->blgh', h_im, c_im.astype(jnp.float32)))
    y = y.reshape(Bn, L, SSM_WIDTH) + d_skip.astype(jnp.float32) * uf
    y = jax.nn.gelu(y).astype(u.dtype)
    val, gate = jnp.split(y @ glu_w + glu_b, 2, axis=-1)
    out = val * jax.nn.sigmoid(gate)
    return out, h_re[:, -1].astype(h0_re.dtype), h_im[:, -1].astype(h0_im.dtype)


def sink_softmax(s, valid, sink):
    s = jnp.where(valid, s, -jnp.inf)
    m = jnp.maximum(s.max(-1, keepdims=True), sink)
    p = jnp.exp(s - m)
    return p / (p.sum(-1, keepdims=True) + jnp.exp(sink - m))


def swa_prompt(q, k, v, sinks):
    Bn, L = q.shape[:2]
    nb = L // WINDOW
    qb = q.reshape(Bn, nb, WINDOW, N_KV_HEADS, Q_PER_KV, HEAD_DIM)

    def band(t):
        tb = t.reshape(Bn, nb, WINDOW, N_KV_HEADS, HEAD_DIM)
        prev = jnp.concatenate([jnp.zeros_like(tb[:, :1]), tb[:, :-1]], axis=1)
        return jnp.concatenate([prev, tb], axis=2)

    kk, vv = band(k), band(v)
    s = jnp.einsum('bnqkgd,bnjkd->bnkgqj', qb, kk).astype(jnp.float32) * ATTN_SCALE
    qpos = WINDOW + jnp.arange(WINDOW)
    kpos = jnp.arange(2 * WINDOW)
    diff = qpos[:, None] - kpos[None, :]
    band_ok = (diff >= 0) & (diff <= WINDOW)
    valid = band_ok[None] & ((jnp.arange(nb)[:, None, None] > 0) | (kpos[None, None, :] >= WINDOW))
    valid = valid[None, :, None, None]
    sink = sinks.astype(jnp.float32).reshape(N_KV_HEADS, Q_PER_KV)[None, None, :, :, None, None]
    probs = sink_softmax(s, valid, sink)
    out = jnp.einsum('bnkgqj,bnjkd->bnqkgd', probs.astype(v.dtype), vv).reshape(Bn, L, ATTN_WIDTH)
    return out, k[:, -WINDOW:], v[:, -WINDOW:]


def swa_sample(q, k, v, k_buf, v_buf, sinks):
    N, T = q.shape[:2]
    qg = q.reshape(N, T, N_KV_HEADS, Q_PER_KV, HEAD_DIM)
    kk = jnp.concatenate([k_buf.astype(k.dtype), k], axis=1)
    vv = jnp.concatenate([v_buf.astype(v.dtype), v], axis=1)
    s = jnp.einsum('ntkgd,njkd->nkgtj', qg, kk).astype(jnp.float32) * ATTN_SCALE
    qpos = WINDOW + jnp.arange(T)
    kpos = jnp.arange(WINDOW + T)
    diff = qpos[:, None] - kpos[None, :]
    valid = ((diff >= 0) & (diff <= WINDOW))[None, None, None]
    sink = sinks.astype(jnp.float32).reshape(N_KV_HEADS, Q_PER_KV)[None, :, :, None, None]
    probs = sink_softmax(s, valid, sink)
    out = jnp.einsum('nkgtj,njkd->ntkgd', probs.astype(vv.dtype), vv).reshape(N, T, ATTN_WIDTH)
    return out, kk[:, -WINDOW:], vv[:, -WINDOW:]


def token_mixer(h, l, p, h0_re, h0_im, k_buf, v_buf):
    Bn, L, _ = h.shape
    z = h @ p['w_in'][l]
    o1 = SSM_WIDTH
    o2 = o1 + ATTN_WIDTH
    o3 = o2 + KV_WIDTH
    u = z[..., :o1]
    q = z[..., o1:o2].reshape(Bn, L, N_HEADS, HEAD_DIM)
    k = z[..., o2:o3].reshape(Bn, L, N_KV_HEADS, HEAD_DIM)
    v = z[..., o3:].reshape(Bn, L, N_KV_HEADS, HEAD_DIM)
    y_ssm, hre, him = ssm_mixer(u, h0_re, h0_im, p['ssm_lam_re'][l], p['ssm_lam_im'][l],
                                p['ssm_log_dt'][l], p['ssm_b_re'][l], p['ssm_b_im'][l],
                                p['ssm_c_re'][l], p['ssm_c_im'][l], p['ssm_d'][l],
                                p['glu_w'][l], p['glu_b'][l])
    if k_buf is None:
        y_att, k_new, v_new = swa_prompt(q, k, v, p['attn_sinks'][l])
    else:
        y_att, k_new, v_new = swa_sample(q, k, v, k_buf, v_buf, p['attn_sinks'][l])
    merged = jnp.concatenate([rms_norm(y_ssm, p['g_ssm_out'][l]),
                              rms_norm(y_att.astype(h.dtype), p['g_attn_out'][l])], axis=-1)
    return merged @ p['w_out'][l], hre, him, k_new, v_new


def trunk(x, h0_re, h0_im, k_buf, v_buf, p):
    ks, vs, hrs, his = [], [], [], []
    for l in range(DEPTH):
        x = layer_norm(ALPHA * x + 0.5 * swiglu(x, p['ffn1_w_in'][l], p['ffn1_w_out'][l]),
                       p['ln_g'][l, 0], p['ln_b'][l, 0])
        kb = None if k_buf is None else k_buf[l]
        vb = None if v_buf is None else v_buf[l]
        m, hre, him, kn, vn = token_mixer(x, l, p, h0_re[l], h0_im[l], kb, vb)
        x = layer_norm(ALPHA * x + m, p['ln_g'][l, 1], p['ln_b'][l, 1])
        x = layer_norm(ALPHA * x + 0.5 * swiglu(x, p['ffn2_w_in'][l], p['ffn2_w_out'][l]),
                       p['ln_g'][l, 2], p['ln_b'][l, 2])
        ks.append(kn); vs.append(vn); hrs.append(hre); his.append(him)
    return x, jnp.stack(ks), jnp.stack(vs), jnp.stack(hrs), jnp.stack(his)


def setup_inputs(seed: int = 0) -> dict:
    key = jax.random.key(seed)
    ks = iter(jax.random.split(key, 40))
    nrm = lambda shape, scale: jax.random.normal(next(ks), shape, jnp.float32) * scale
    G, P, H = N_SSM_GROUPS, SSM_STATE, SSM_GROUP
    inp = {}
    inp['x_prompt'] = nrm((BATCH, SEQ, D_MODEL), 1.0)
    inp['x_sample'] = nrm((DEC_BATCH, DEC_SEQ, D_MODEL), 1.0)
    inp['cache_k_win'] = nrm((DEPTH, DEC_BATCH, WINDOW, N_KV_HEADS, HEAD_DIM), 1.0)
    inp['cache_v_win'] = nrm((DEPTH, DEC_BATCH, WINDOW, N_KV_HEADS, HEAD_DIM), 1.0)
    inp['state_ssm_re'] = nrm((DEPTH, DEC_BATCH, G, P), 0.1)
    inp['state_ssm_im'] = nrm((DEPTH, DEC_BATCH, G, P), 0.1)
    inp['ln_g'] = 1.0 + nrm((DEPTH, 3, D_MODEL), 0.02)
    inp['ln_b'] = nrm((DEPTH, 3, D_MODEL), 0.02)
    inp['ffn1_w_in'] = nrm((DEPTH, D_MODEL, 2 * D_FF), D_MODEL ** -0.5)
    inp['ffn1_w_out'] = nrm((DEPTH, D_FF, D_MODEL), BETA * D_FF ** -0.5)
    inp['ffn2_w_in'] = nrm((DEPTH, D_MODEL, 2 * D_FF), D_MODEL ** -0.5)
    inp['ffn2_w_out'] = nrm((DEPTH, D_FF, D_MODEL), BETA * D_FF ** -0.5)
    inp['w_in'] = nrm((DEPTH, D_MODEL, IN_COLS), D_MODEL ** -0.5)
    inp['ssm_lam_re'] = -0.5 + nrm((DEPTH, G, P), 0.01)
    inp['ssm_lam_im'] = (jnp.pi * jnp.arange(P, dtype=jnp.float32))[None, None, :] + nrm((DEPTH, G, P), 0.01)
    inp['ssm_log_dt'] = jax.random.uniform(next(ks), (DEPTH, G), jnp.float32,
                                           math.log(DT_MIN), math.log(DT_MAX))
    inp['ssm_b_re'] = nrm((DEPTH, G, P, H), H ** -0.5)
    inp['ssm_b_im'] = nrm((DEPTH, G, P, H), H ** -0.5)
    inp['ssm_c_re'] = nrm((DEPTH, G, H, P), P ** -0.5)
    inp['ssm_c_im'] = nrm((DEPTH, G, H, P), P ** -0.5)
    inp['ssm_d'] = nrm((DEPTH, SSM_WIDTH), 1.0)
    inp['glu_w'] = nrm((DEPTH, SSM_WIDTH, 2 * SSM_WIDTH), SSM_WIDTH ** -0.5)
    inp['glu_b'] = nrm((DEPTH, 2 * SSM_WIDTH), 0.01)
    inp['attn_sinks'] = nrm((DEPTH, N_HEADS), 1.0)
    inp['g_ssm_out'] = 1.0 + nrm((DEPTH, SSM_WIDTH), 0.02)
    inp['g_attn_out'] = 1.0 + nrm((DEPTH, ATTN_WIDTH), 0.02)
    inp['w_out'] = nrm((DEPTH, D_MODEL, D_MODEL), BETA * D_MODEL ** -0.5)
    return inp


def reference(x_prompt, x_sample, cache_k_win, cache_v_win, state_ssm_re, state_ssm_im,
              ln_g, ln_b, ffn1_w_in, ffn1_w_out, ffn2_w_in, ffn2_w_out, w_in,
              ssm_lam_re, ssm_lam_im, ssm_log_dt, ssm_b_re, ssm_b_im, ssm_c_re, ssm_c_im,
              ssm_d, glu_w, glu_b, attn_sinks, g_ssm_out, g_attn_out, w_out):
    p = dict(ln_g=ln_g, ln_b=ln_b, ffn1_w_in=ffn1_w_in, ffn1_w_out=ffn1_w_out,
             ffn2_w_in=ffn2_w_in, ffn2_w_out=ffn2_w_out, w_in=w_in,
             ssm_lam_re=ssm_lam_re, ssm_lam_im=ssm_lam_im, ssm_log_dt=ssm_log_dt,
             ssm_b_re=ssm_b_re, ssm_b_im=ssm_b_im, ssm_c_re=ssm_c_re, ssm_c_im=ssm_c_im,
             ssm_d=ssm_d, glu_w=glu_w, glu_b=glu_b, attn_sinks=attn_sinks,
             g_ssm_out=g_ssm_out, g_attn_out=g_attn_out, w_out=w_out)
    Bp = x_prompt.shape[0]
    zeros_h = jnp.zeros((DEPTH, Bp, N_SSM_GROUPS, SSM_STATE), state_ssm_re.dtype)
    y_prompt, kp, vp, hrp, hip = trunk(x_prompt, zeros_h, zeros_h, None, None, p)
    y_sample, ks_, vs_, hrs, his = trunk(x_sample, state_ssm_re, state_ssm_im,
                                         cache_k_win, cache_v_win, p)
    return (y_prompt, y_sample, kp, vp, hrp, hip, ks_, vs_, hrs, his)
```

```python
import functools
import math

import jax
import jax.numpy as jnp
from jax import lax
from jax.experimental import pallas as pl
from jax.experimental.pallas import tpu as pltpu

D_MODEL = 1024
DEPTH = 2
SSM_WIDTH = 512
SSM_GROUP = 16
N_SSM_GROUPS = 32
SSM_STATE = 64
ATTN_WIDTH = 512
HEAD_DIM = 64
N_HEADS = 8
N_KV_HEADS = 2
Q_PER_KV = 4
KV_WIDTH = 128
WINDOW = 128
ATTN_SCALE = HEAD_DIM ** -0.5
D_FF = 2816
ALPHA = (2.0 * DEPTH) ** 0.25
LN_EPS = 1e-5
RMS_EPS = 1e-6

LANES = 128
VMEM_LIMIT_BYTES = 56 * 1024 * 1024

F32 = jnp.float32
BF16 = jnp.bfloat16


def _layer_norm(r, g, b):
    mu = jnp.mean(r, axis=-1, keepdims=True)
    c = r - mu
    var = jnp.mean(c * c, axis=-1, keepdims=True)
    return c * lax.rsqrt(var + LN_EPS) * g + b


def _rms_norm(y, g):
    return y * lax.rsqrt(jnp.mean(y * y, axis=-1, keepdims=True) + RMS_EPS) * g


def _resident(shape):
    return pl.BlockSpec(shape, lambda i: (0,) * len(shape), pipeline_mode=pl.Buffered(1))


def _rows_kernel(*refs, has_mix, has_proj):
    it = iter(refs)
    x_ref = next(it)
    if has_mix:
        ssm_ref, att_ref, wo_s_ref, wo_a_ref, gm_ref, bm_ref = (next(it) for _ in range(6))
    wgu_ref, wdn_ref, g_ref, b_ref = (next(it) for _ in range(4))
    if has_proj:
        wp_ref = next(it)
    o_ref = next(it)
    if has_proj:
        u_ref, q_ref, kv_ref = (next(it) for _ in range(3))

    x = x_ref[...]
    if has_mix:
        m = jnp.dot(ssm_ref[...], wo_s_ref[...], preferred_element_type=F32)
        m = m + jnp.dot(att_ref[...], wo_a_ref[...], preferred_element_type=F32)
        x = _layer_norm(ALPHA * x + m, gm_ref[...], bm_ref[...])
    xb = x.astype(BF16)
    gu = jnp.dot(xb, wgu_ref[...], preferred_element_type=F32)
    gate = gu[:, :D_FF]
    up = gu[:, D_FF:]
    h = (gate * jax.nn.sigmoid(gate) * up).astype(BF16)
    y = jnp.dot(h, wdn_ref[...], preferred_element_type=F32)
    x = _layer_norm(ALPHA * x + 0.5 * y, g_ref[...], b_ref[...])
    o_ref[...] = x
    if has_proj:
        z = jnp.dot(x.astype(BF16), wp_ref[...], preferred_element_type=F32)
        for j in range(SSM_WIDTH // LANES):
            u_ref[j] = z[:, j * LANES:(j + 1) * LANES]
        q_ref[...] = z[:, SSM_WIDTH:SSM_WIDTH + ATTN_WIDTH].astype(BF16)
        kv_ref[...] = z[:, SSM_WIDTH + ATTN_WIDTH:]


def _rows_call(x, ffn, ln, mix=None, proj=None, *, tm, name):
    M = x.shape[0]
    assert M % tm == 0
    row = lambda w: pl.BlockSpec((tm, w), lambda i: (i, 0))
    args = [x]
    specs = [row(D_MODEL)]
    if mix is not None:
        ssm_n, att_n, wo_s, wo_a, gm, bm = mix
        args += [ssm_n, att_n, wo_s, wo_a, gm, bm]
        specs += [row(SSM_WIDTH), row(ATTN_WIDTH), _resident(wo_s.shape), _resident(wo_a.shape),
                  _resident(gm.shape), _resident(bm.shape)]
    args += [ffn[0], ffn[1], ln[0], ln[1]]
    specs += [_resident(ffn[0].shape), _resident(ffn[1].shape), _resident(ln[0].shape), _resident(ln[1].shape)]
    out_shape = [jax.ShapeDtypeStruct((M, D_MODEL), F32)]
    out_specs = [row(D_MODEL)]
    if proj is not None:
        args.append(proj)
        specs.append(_resident(proj.shape))
        n_slab = SSM_WIDTH // LANES
        out_shape += [jax.ShapeDtypeStruct((n_slab, M, LANES), F32),
                      jax.ShapeDtypeStruct((M, ATTN_WIDTH), BF16),
                      jax.ShapeDtypeStruct((M, 2 * KV_WIDTH), F32)]
        out_specs += [pl.BlockSpec((n_slab, tm, LANES), lambda i: (0, i, 0)), row(ATTN_WIDTH), row(2 * KV_WIDTH)]
    return pl.pallas_call(
        functools.partial(_rows_kernel, has_mix=mix is not None, has_proj=proj is not None),
        grid=(M // tm,),
        in_specs=specs,
        out_specs=out_specs,
        out_shape=out_shape,
        compiler_params=pltpu.CompilerParams(dimension_semantics=("arbitrary",),
                                             vmem_limit_bytes=VMEM_LIMIT_BYTES),
        name=name,
    )(*args)


def _discretise(lam_re, lam_im, log_dt, b_re, b_im):
    dt = jnp.exp(log_dt)[:, None]
    mag = jnp.exp(lam_re * dt)
    a_re = mag * jnp.cos(lam_im * dt)
    a_im = mag * jnp.sin(lam_im * dt)
    num_re, num_im = a_re - 1.0, a_im
    den = jnp.square(lam_re) + jnp.square(lam_im)
    f_re = (num_re * lam_re + num_im * lam_im) / den
    f_im = (num_im * lam_re - num_re * lam_im) / den
    bb_re = f_re[..., None] * b_re - f_im[..., None] * b_im
    bb_im = f_re[..., None] * b_im + f_im[..., None] * b_re
    return a_re, a_im, bb_re, bb_im


def _combine(e1, e2):
    ar1, ai1, br1, bi1 = e1
    ar2, ai2, br2, bi2 = e2
    return (ar2 * ar1 - ai2 * ai1, ar2 * ai1 + ai2 * ar1,
            ar2 * br1 - ai2 * bi1 + br2, ar2 * bi1 + ai2 * br1 + bi2)


def _ssm_jax(u, h0_re, h0_im, p, l):
    Bn, L, _ = u.shape
    a_re, a_im, bb_re, bb_im = _discretise(p['ssm_lam_re'][l], p['ssm_lam_im'][l], p['ssm_log_dt'][l],
                                           p['ssm_b_re'][l], p['ssm_b_im'][l])
    ug = u.reshape(Bn, L, N_SSM_GROUPS, SSM_GROUP)
    bu_re = jnp.einsum('blgh,gph->blgp', ug, bb_re)
    bu_im = jnp.einsum('blgh,gph->blgp', ug, bb_im)
    bu_re = bu_re.at[:, 0].add(a_re * h0_re - a_im * h0_im)
    bu_im = bu_im.at[:, 0].add(a_re * h0_im + a_im * h0_re)
    ar = jnp.broadcast_to(a_re, bu_re.shape)
    ai = jnp.broadcast_to(a_im, bu_re.shape)
    _, _, h_re, h_im = lax.associative_scan(_combine, (ar, ai, bu_re, bu_im), axis=1)
    y = (jnp.einsum('blgp,ghp->blgh', h_re, p['ssm_c_re'][l]) - jnp.einsum('blgp,ghp->blgh', h_im, p['ssm_c_im'][l]))
    y = y.reshape(Bn, L, SSM_WIDTH) + p['ssm_d'][l] * u
    y = jax.nn.gelu(y)
    val, gate = jnp.split(y @ p['glu_w'][l] + p['glu_b'][l], 2, axis=-1)
    return val * jax.nn.sigmoid(gate), h_re[:, -1], h_im[:, -1]


def _sink_softmax(s, valid, sink):
    s = jnp.where(valid, s, -jnp.inf)
    m = jnp.maximum(s.max(-1, keepdims=True), sink)
    pr = jnp.exp(s - m)
    return pr / (pr.sum(-1, keepdims=True) + jnp.exp(sink - m))


def _swa_prompt_jax(q, k, v, sinks):
    Bn, L = q.shape[:2]
    nb = L // WINDOW
    qb = q.reshape(Bn, nb, WINDOW, N_KV_HEADS, Q_PER_KV, HEAD_DIM)

    def band(t):
        tb = t.reshape(Bn, nb, WINDOW, N_KV_HEADS, HEAD_DIM)
        prev = jnp.concatenate([jnp.zeros_like(tb[:, :1]), tb[:, :-1]], axis=1)
        return jnp.concatenate([prev, tb], axis=2)

    kk, vv = band(k), band(v)
    s = jnp.einsum('bnqkgd,bnjkd->bnkgqj', qb, kk) * ATTN_SCALE
    qpos = WINDOW + jnp.arange(WINDOW)
    kpos = jnp.arange(2 * WINDOW)
    diff = qpos[:, None] - kpos[None, :]
    band_ok = (diff >= 0) & (diff <= WINDOW)
    valid = band_ok[None] & ((jnp.arange(nb)[:, None, None] > 0) | (kpos[None, None, :] >= WINDOW))
    valid = valid[None, :, None, None]
    sink = sinks.reshape(N_KV_HEADS, Q_PER_KV)[None, None, :, :, None, None]
    probs = _sink_softmax(s, valid, sink)
    out = jnp.einsum('bnkgqj,bnjkd->bnqkgd', probs, vv).reshape(Bn, L, ATTN_WIDTH)
    return out, k[:, -WINDOW:], v[:, -WINDOW:]


def _swa_sample_jax(q, k, v, k_buf, v_buf, sinks):
    N, T = q.shape[:2]
    qg = q.reshape(N, T, N_KV_HEADS, Q_PER_KV, HEAD_DIM)
    kk = jnp.concatenate([k_buf, k], axis=1)
    vv = jnp.concatenate([v_buf, v], axis=1)
    s = jnp.einsum('ntkgd,njkd->nkgtj', qg, kk) * ATTN_SCALE
    sink = sinks.reshape(N_KV_HEADS, Q_PER_KV)[None, :, :, None, None]
    probs = _sink_softmax(s, True, sink)
    out = jnp.einsum('nkgtj,njkd->ntkgd', probs, vv).reshape(N, T, ATTN_WIDTH)
    return out, kk[:, -WINDOW:], vv[:, -WINDOW:]


def _mixer_jax(u_slab, q, kv, l, p, h0_re, h0_im, k_buf, v_buf, Bn, L):
    u = jnp.transpose(u_slab, (1, 0, 2)).reshape(Bn, L, SSM_WIDTH)
    q = q.astype(F32).reshape(Bn, L, N_HEADS, HEAD_DIM)
    k = kv[:, :KV_WIDTH].reshape(Bn, L, N_KV_HEADS, HEAD_DIM)
    v = kv[:, KV_WIDTH:].reshape(Bn, L, N_KV_HEADS, HEAD_DIM)
    y_ssm, hre, him = _ssm_jax(u, h0_re, h0_im, p, l)
    if k_buf is None:
        y_att, kn, vn = _swa_prompt_jax(q, k, v, p['attn_sinks'][l])
    else:
        y_att, kn, vn = _swa_sample_jax(q, k, v, k_buf, v_buf, p['attn_sinks'][l])
    ssm_n = _rms_norm(y_ssm, p['g_ssm_out'][l]).reshape(Bn * L, SSM_WIDTH).astype(BF16)
    att_n = _rms_norm(y_att, p['g_attn_out'][l]).reshape(Bn * L, ATTN_WIDTH).astype(BF16)
    return ssm_n, att_n, hre, him, kn, vn


def _trunk(x, h0_re, h0_im, k_buf, v_buf, p, w, *, tm, tag):
    Bn, L, _ = x.shape
    x = x.reshape(Bn * L, D_MODEL)
    ks, vs, hrs, his = [], [], [], []
    mix = None
    for l in range(DEPTH):
        if l == 0:
            x, u, q, kv = _rows_call(x, w['ffn1'][l], w['ln'][l][0], proj=w['w_in'][l], tm=tm, name=f"{tag}_ffn1_{l}")
        else:
            x = _rows_call(x, w['ffn2'][l - 1], w['ln'][l - 1][2], mix=mix, tm=tm, name=f"{tag}_mix_ffn2_{l - 1}")[0]
            x, u, q, kv = _rows_call(x, w['ffn1'][l], w['ln'][l][0], proj=w['w_in'][l], tm=tm, name=f"{tag}_ffn1_{l}")
        kb = None if k_buf is None else k_buf[l]
        vb = None if v_buf is None else v_buf[l]
        ssm_n, att_n, hre, him, kn, vn = _mixer_jax(u, q, kv, l, p, h0_re[l], h0_im[l], kb, vb, Bn, L)
        mix = (ssm_n, att_n, w['wo_s'][l], w['wo_a'][l], w['ln'][l][1][0], w['ln'][l][1][1])
        ks.append(kn); vs.append(vn); hrs.append(hre); his.append(him)
    x = _rows_call(x, w['ffn2'][DEPTH - 1], w['ln'][DEPTH - 1][2], mix=mix, tm=tm, name=f"{tag}_mix_ffn2_{DEPTH - 1}")[0]
    return x.reshape(Bn, L, D_MODEL), jnp.stack(ks), jnp.stack(vs), jnp.stack(hrs), jnp.stack(his)


def kernel(x_prompt, x_sample, cache_k_win, cache_v_win, state_ssm_re, state_ssm_im, ln_g, ln_b, ffn1_w_in, ffn1_w_out, ffn2_w_in, ffn2_w_out, w_in, ssm_lam_re, ssm_lam_im, ssm_log_dt, ssm_b_re, ssm_b_im, ssm_c_re, ssm_c_im, ssm_d, glu_w, glu_b, attn_sinks, g_ssm_out, g_attn_out, w_out):
    p = dict(ssm_lam_re=ssm_lam_re, ssm_lam_im=ssm_lam_im, ssm_log_dt=ssm_log_dt,
             ssm_b_re=ssm_b_re, ssm_b_im=ssm_b_im, ssm_c_re=ssm_c_re, ssm_c_im=ssm_c_im,
             ssm_d=ssm_d, glu_w=glu_w, glu_b=glu_b, attn_sinks=attn_sinks,
             g_ssm_out=g_ssm_out, g_attn_out=g_attn_out)
    w = dict(
        ffn1=[(ffn1_w_in[l].astype(BF16), ffn1_w_out[l].astype(BF16)) for l in range(DEPTH)],
        ffn2=[(ffn2_w_in[l].astype(BF16), ffn2_w_out[l].astype(BF16)) for l in range(DEPTH)],
        ln=[[(ln_g[l, i][None, :], ln_b[l, i][None, :]) for i in range(3)] for l in range(DEPTH)],
        w_in=[w_in[l].astype(BF16) for l in range(DEPTH)],
        wo_s=[w_out[l, :SSM_WIDTH].astype(BF16) for l in range(DEPTH)],
        wo_a=[w_out[l, SSM_WIDTH:].astype(BF16) for l in range(DEPTH)],
    )
    Bp = x_prompt.shape[0]
    zeros_h = jnp.zeros((DEPTH, Bp, N_SSM_GROUPS, SSM_STATE), state_ssm_re.dtype)
    y_prompt, kp, vp, hrp, hip = _trunk(x_prompt, zeros_h, zeros_h, None, None, p, w, tm=512, tag="p")
    y_sample, ks_, vs_, hrs, his = _trunk(x_sample, state_ssm_re, state_ssm_im, cache_k_win, cache_v_win, p, w,
                                          tm=128, tag="s")
    return (y_prompt, y_sample, kp, vp, hrp, hip, ks_, vs_, hrs, his)
```

```python
import functools

import jax
import jax.numpy as jnp
from jax import lax
from jax.experimental import pallas as pl
from jax.experimental.pallas import tpu as pltpu

D_MODEL = 1024
DEPTH = 2
SSM_WIDTH = 512
SSM_GROUP = 16
N_SSM_GROUPS = 32
SSM_STATE = 64
N_STATES = N_SSM_GROUPS * SSM_STATE
ATTN_WIDTH = 512
HEAD_DIM = 64
N_HEADS = 8
N_KV_HEADS = 2
Q_PER_KV = 4
KV_WIDTH = 128
WINDOW = 128
ATTN_SCALE = HEAD_DIM ** -0.5
D_FF = 2816
ALPHA = (2.0 * DEPTH) ** 0.25
LN_EPS = 1e-5
RMS_EPS = 1e-6

LANES = 128
SUBLANES = 8
VMEM_LIMIT_BYTES = 56 * 1024 * 1024
N_SLAB = SSM_WIDTH // LANES
GROUPS_PER_SLAB = LANES // SSM_GROUP
SLAB_STATES = GROUPS_PER_SLAB * SSM_STATE
Q_EXP = N_HEADS * LANES
PROJ_COLS = SSM_WIDTH + Q_EXP + 2 * KV_WIDTH
SSM_CHUNK = 8
SSM_ROWS = 1024
MASKED = -1e30

F32 = jnp.float32
BF16 = jnp.bfloat16


def _layer_norm(r, g, b):
    mu = jnp.mean(r, axis=-1, keepdims=True)
    c = r - mu
    var = jnp.mean(c * c, axis=-1, keepdims=True)
    return c * lax.rsqrt(var + LN_EPS) * g + b


def _rms_norm(y, g):
    return y * lax.rsqrt(jnp.mean(y * y, axis=-1, keepdims=True) + RMS_EPS) * g


def _resident(shape, n_grid=1):
    zeros = (0,) * len(shape)
    if n_grid == 1:
        return pl.BlockSpec(shape, lambda i: zeros, pipeline_mode=pl.Buffered(1))
    return pl.BlockSpec(shape, lambda i, j: zeros, pipeline_mode=pl.Buffered(1))


def _rows_kernel(*refs, has_mix, has_proj):
    it = iter(refs)
    x_ref = next(it)
    if has_mix:
        ssm_ref, att_ref, wo_s_ref, wo_a_ref, gm_ref, bm_ref = (next(it) for _ in range(6))
    wgu_ref, wdn_ref, g_ref, b_ref = (next(it) for _ in range(4))
    if has_proj:
        wp_ref = next(it)
    o_ref = next(it)
    if has_proj:
        u_ref, q_ref, kv_ref = (next(it) for _ in range(3))

    x = x_ref[...]
    if has_mix:
        ssm = jnp.concatenate([ssm_ref[j] for j in range(N_SLAB)], axis=1).astype(BF16)
        att = jnp.concatenate([att_ref[h] for h in range(N_HEADS)], axis=1).astype(BF16)
        m = jnp.dot(ssm, wo_s_ref[...], preferred_element_type=F32)
        m = m + jnp.dot(att, wo_a_ref[...], preferred_element_type=F32)
        x = _layer_norm(ALPHA * x + m, gm_ref[...], bm_ref[...])
    xb = x.astype(BF16)
    gu = jnp.dot(xb, wgu_ref[...], preferred_element_type=F32)
    gate = gu[:, :D_FF]
    up = gu[:, D_FF:]
    h = (gate * jax.nn.sigmoid(gate) * up).astype(BF16)
    y = jnp.dot(h, wdn_ref[...], preferred_element_type=F32)
    x = _layer_norm(ALPHA * x + 0.5 * y, g_ref[...], b_ref[...])
    o_ref[...] = x
    if has_proj:
        z = jnp.dot(x.astype(BF16), wp_ref[...], preferred_element_type=F32)
        for j in range(N_SLAB):
            u_ref[j] = z[:, j * LANES:(j + 1) * LANES]
        for hd in range(N_HEADS):
            q_ref[hd] = z[:, SSM_WIDTH + hd * LANES:SSM_WIDTH + (hd + 1) * LANES].astype(q_ref.dtype)
        kv_ref[...] = z[:, SSM_WIDTH + Q_EXP:]


def _rows_call(x, ffn, ln, mix=None, proj=None, *, tm, q_dtype=BF16, name):
    M = x.shape[0]
    assert M % tm == 0
    row = lambda w: pl.BlockSpec((tm, w), lambda i: (i, 0))
    slab = lambda n: pl.BlockSpec((n, tm, LANES), lambda i: (0, i, 0))
    args = [x]
    specs = [row(D_MODEL)]
    if mix is not None:
        ssm_n, att_n, wo_s, wo_a, gm, bm = mix
        args += [ssm_n, att_n, wo_s, wo_a, gm, bm]
        specs += [slab(N_SLAB), slab(N_HEADS), _resident(wo_s.shape), _resident(wo_a.shape),
                  _resident(gm.shape), _resident(bm.shape)]
    args += [ffn[0], ffn[1], ln[0], ln[1]]
    specs += [_resident(ffn[0].shape), _resident(ffn[1].shape), _resident(ln[0].shape), _resident(ln[1].shape)]
    out_shape = [jax.ShapeDtypeStruct((M, D_MODEL), F32)]
    out_specs = [row(D_MODEL)]
    if proj is not None:
        args.append(proj)
        specs.append(_resident(proj.shape))
        out_shape += [jax.ShapeDtypeStruct((N_SLAB, M, LANES), F32),
                      jax.ShapeDtypeStruct((N_HEADS, M, LANES), q_dtype),
                      jax.ShapeDtypeStruct((M, 2 * KV_WIDTH), F32)]
        out_specs += [slab(N_SLAB), slab(N_HEADS), row(2 * KV_WIDTH)]
    return pl.pallas_call(
        functools.partial(_rows_kernel, has_mix=mix is not None, has_proj=proj is not None),
        grid=(M // tm,),
        in_specs=specs,
        out_specs=out_specs,
        out_shape=out_shape,
        compiler_params=pltpu.CompilerParams(dimension_semantics=("arbitrary",),
                                             vmem_limit_bytes=VMEM_LIMIT_BYTES),
        name=name,
    )(*args)


def _cmul(xr, xi, yr, yi):
    return xr * yr - xi * yi, xr * yi + xi * yr


def _ssm_tables(lam_re, lam_im, log_dt, b_re, b_im, c_re, c_im, t1):
    G, P, H = N_SSM_GROUPS, SSM_STATE, SSM_GROUP
    hi = lax.Precision.HIGHEST
    dt = jnp.exp(log_dt)[:, None]
    mag = jnp.exp(lam_re * dt)
    a_re = mag * jnp.cos(lam_im * dt)
    a_im = mag * jnp.sin(lam_im * dt)
    num_re, num_im = a_re - 1.0, a_im
    den = jnp.square(lam_re) + jnp.square(lam_im)
    f_re = (num_re * lam_re + num_im * lam_im) / den
    f_im = (num_im * lam_re - num_re * lam_im) / den
    bb_re = f_re[..., None] * b_re - f_im[..., None] * b_im
    bb_im = f_re[..., None] * b_im + f_im[..., None] * b_re
    pr, pi = [jnp.ones_like(a_re)], [jnp.zeros_like(a_re)]
    for _ in range(t1):
        r, i = _cmul(pr[-1], pi[-1], a_re, a_im)
        pr.append(r); pi.append(i)
    PR, PI = jnp.stack(pr), jnp.stack(pi)
    ABr = PR[:t1, :, :, None] * bb_re - PI[:t1, :, :, None] * bb_im
    ABi = PR[:t1, :, :, None] * bb_im + PI[:t1, :, :, None] * bb_re
    eye = jnp.eye(GROUPS_PER_SLAB, dtype=F32)

    X = jnp.stack([ABr[::-1], ABi[::-1]], axis=0).reshape(2, t1, N_SLAB, GROUPS_PER_SLAB, P, H)
    wst = jnp.einsum('asjgph,gk->jsghakp', X, eye, precision=hi).reshape(N_SLAB, t1 * LANES, 2 * SLAB_STATES)

    cr = c_re[None] * PR[1:, :, None, :] - c_im[None] * PI[1:, :, None, :]
    ci = -(c_re[None] * PI[1:, :, None, :] + c_im[None] * PR[1:, :, None, :])
    Cf = jnp.stack([cr, ci], axis=0).reshape(2, t1, N_SLAB, GROUPS_PER_SLAB, H, P)
    ca = jnp.einsum('atjghp,gk->jagptkh', Cf, eye, precision=hi).reshape(N_SLAB, 2 * SLAB_STATES, t1 * LANES)

    K = (jnp.einsum('ghp,lgpk->lgkh', c_re, ABr, precision=hi)
         - jnp.einsum('ghp,lgpk->lgkh', c_im, ABi, precision=hi))
    K = K.reshape(t1, N_SLAB, GROUPS_PER_SLAB, H, H)
    kd = jnp.einsum('ljgkh,gm->ljgkmh', K, eye, precision=hi).reshape(t1, N_SLAB, LANES, LANES)
    kpad = jnp.concatenate([jnp.zeros_like(kd[:1]), kd], axis=0)
    idx = (t1 - 1 - jnp.arange(t1))[:, None] + jnp.arange(2)[None, :]
    strip = kpad[idx]
    strip = jnp.transpose(strip, (2, 0, 3, 1, 4)).reshape(N_SLAB, t1 * LANES, 2 * LANES)
    return dict(wst=wst.astype(BF16), ca=ca.astype(BF16), strip=strip.astype(BF16), kd=kd.astype(BF16),
                pw_re=PR.reshape(t1 + 1, N_STATES), pw_im=PI.reshape(t1 + 1, N_STATES))


def _scan_tables(base_re, base_im):
    pr, pi = [base_re], [base_im]
    for _ in range(SUBLANES - 1):
        r, i = _cmul(pr[-1], pi[-1], base_re, base_im)
        pr.append(r); pi.append(i)
    steps = []
    for k in (1, 2, 4):
        steps += [pr[k - 1], pi[k - 1]]
    ak = jnp.broadcast_to(jnp.stack(steps)[:, None, :], (6, SUBLANES, N_STATES))
    tr = jnp.stack([jnp.stack(pr), jnp.stack(pi)])
    return ak, tr


def _glu_rms(y, gluw_ref, glub_ref, g_ref):
    g = jax.nn.gelu(y).astype(BF16)
    zz = jnp.dot(g, gluw_ref[...], preferred_element_type=F32) + glub_ref[...]
    o = zz[:, :SSM_WIDTH] * jax.nn.sigmoid(zz[:, SSM_WIDTH:])
    return _rms_norm(o, g_ref[...])


def _ssm_kernel(u_ref, wst_ref, ca_ref, strip_ref, ak_ref, tr_ref, d_ref, gluw_ref, glub_ref, g_ref,
                o_ref, hre_ref, him_ref,
                ucb_ref, ucf_ref, sre_ref, sim_ref, hcr_ref, hci_ref):
    t1 = SSM_CHUNK
    nc = SSM_ROWS // t1
    i = pl.program_id(1)

    @pl.when(i == 0)
    def _():
        hcr_ref[...] = jnp.zeros_like(hcr_ref)
        hci_ref[...] = jnp.zeros_like(hci_ref)

    for j in range(N_SLAB):
        for s in range(t1):
            blk = u_ref[j, pl.ds(s, nc, stride=t1), :]
            ucf_ref[j, :, s * LANES:(s + 1) * LANES] = blk
            ucb_ref[j, :, s * LANES:(s + 1) * LANES] = blk.astype(BF16)

    for j in range(N_SLAB):
        st = jnp.dot(ucb_ref[j], wst_ref[j], preferred_element_type=F32)
        sre_ref[:, j * SLAB_STATES:(j + 1) * SLAB_STATES] = st[:, :SLAB_STATES]
        sim_ref[:, j * SLAB_STATES:(j + 1) * SLAB_STATES] = st[:, SLAB_STATES:]

    row = lax.broadcasted_iota(jnp.int32, (SUBLANES, SLAB_STATES), 0)
    for j in range(N_SLAB):
        cols = pl.ds(j * SLAB_STATES, SLAB_STATES)

        def body(gi, carry, cols=cols):
            hr, hi = carry
            r0 = pl.multiple_of(gi * SUBLANES, SUBLANES)
            xr = sre_ref[pl.ds(r0, SUBLANES), cols]
            xi = sim_ref[pl.ds(r0, SUBLANES), cols]
            for n, k in enumerate((1, 2, 4)):
                akr = ak_ref[2 * n, :, cols]
                aki = ak_ref[2 * n + 1, :, cols]
                sr = jnp.where(row >= k, pltpu.roll(xr, k, 0), 0.0)
                si = jnp.where(row >= k, pltpu.roll(xi, k, 0), 0.0)
                xr, xi = xr + akr * sr - aki * si, xi + akr * si + aki * sr
            tr = tr_ref[0, :, cols]
            ti = tr_ref[1, :, cols]
            er = xr + tr * hr - ti * hi
            ei = xi + tr * hi + ti * hr
            sre_ref[pl.ds(r0, SUBLANES), cols] = jnp.where(row >= 1, pltpu.roll(er, 1, 0), hr)
            sim_ref[pl.ds(r0, SUBLANES), cols] = jnp.where(row >= 1, pltpu.roll(ei, 1, 0), hi)
            last = SUBLANES - 1
            return (jnp.broadcast_to(er[last:last + 1], er.shape), jnp.broadcast_to(ei[last:last + 1], ei.shape))

        hr, hi = lax.fori_loop(0, nc // SUBLANES, body, (hcr_ref[:, cols], hci_ref[:, cols]))
        hcr_ref[:, cols] = hr
        hci_ref[:, cols] = hi

    hre_ref[0] = hcr_ref[...]
    him_ref[0] = hci_ref[...]

    ys = []
    for j in range(N_SLAB):
        cols = slice(j * SLAB_STATES, (j + 1) * SLAB_STATES)
        hprev = jnp.concatenate([sre_ref[:, cols], sim_ref[:, cols]], axis=1).astype(BF16)
        y = jnp.dot(hprev, ca_ref[j], preferred_element_type=F32)
        parts = []
        for p2 in range(t1 // 2):
            kk = (2 * p2 + 2) * LANES
            yi = jnp.dot(ucb_ref[j, :, :kk], strip_ref[j, (t1 - 2 - 2 * p2) * LANES:, :],
                         preferred_element_type=F32)
            parts.append(y[:, 2 * p2 * LANES:(2 * p2 + 2) * LANES] + yi)
        ys.append(jnp.concatenate(parts, axis=1) + d_ref[j] * ucf_ref[j])

    for t in range(t1):
        y = jnp.concatenate([ys[j][:, t * LANES:(t + 1) * LANES] for j in range(N_SLAB)], axis=1)
        n = _glu_rms(y, gluw_ref, glub_ref, g_ref)
        for j in range(N_SLAB):
            o_ref[j, pl.ds(t, nc, stride=t1), :] = n[:, j * LANES:(j + 1) * LANES]


def _ssm_prompt_call(u, tab, ak, tr, d_t, gluw, glub, g, *, batch, seq, name):
    t1 = SSM_CHUNK
    nc = SSM_ROWS // t1
    steps = seq // SSM_ROWS
    blk = pl.BlockSpec((N_SLAB, SSM_ROWS, LANES), lambda b, i: (0, b * steps + i, 0))
    st = pl.BlockSpec((1, SUBLANES, N_STATES), lambda b, i: (b, 0, 0))
    consts = [tab['wst'], tab['ca'], tab['strip'], ak, tr, d_t, gluw, glub, g]
    out, hre, him = pl.pallas_call(
        _ssm_kernel,
        grid=(batch, steps),
        in_specs=[blk] + [_resident(c.shape, 2) for c in consts],
        out_specs=[blk, st, st],
        out_shape=[jax.ShapeDtypeStruct(u.shape, F32),
                   jax.ShapeDtypeStruct((batch, SUBLANES, N_STATES), F32),
                   jax.ShapeDtypeStruct((batch, SUBLANES, N_STATES), F32)],
        scratch_shapes=[pltpu.VMEM((N_SLAB, nc, t1 * LANES), BF16),
                        pltpu.VMEM((N_SLAB, nc, t1 * LANES), F32),
                        pltpu.VMEM((nc, N_STATES), F32),
                        pltpu.VMEM((nc, N_STATES), F32),
                        pltpu.VMEM((SUBLANES, N_STATES), F32),
                        pltpu.VMEM((SUBLANES, N_STATES), F32)],
        compiler_params=pltpu.CompilerParams(dimension_semantics=("arbitrary", "arbitrary"),
                                             vmem_limit_bytes=VMEM_LIMIT_BYTES),
        name=name,
    )(u, *consts)
    return out, hre[:, 0], him[:, 0]


def _ssm_sample_kernel(u_ref, h0r_ref, h0i_ref, wst_ref, ca_ref, kd_ref, a_ref, d_ref, gluw_ref, glub_ref, g_ref,
                       o_ref, hr_ref, hi_ref):
    ys = []
    for j in range(N_SLAB):
        cols = slice(j * SLAB_STATES, (j + 1) * SLAB_STATES)
        uf = u_ref[j]
        ub = uf.astype(BF16)
        st = jnp.dot(ub, wst_ref[j], preferred_element_type=F32)
        h0r = h0r_ref[:, cols]
        h0i = h0i_ref[:, cols]
        ar = a_ref[0, :, cols]
        ai = a_ref[1, :, cols]
        hr_ref[:, cols] = ar * h0r - ai * h0i + st[:, :SLAB_STATES]
        hi_ref[:, cols] = ar * h0i + ai * h0r + st[:, SLAB_STATES:]
        hcat = jnp.concatenate([h0r, h0i], axis=1).astype(BF16)
        y = jnp.dot(hcat, ca_ref[j], preferred_element_type=F32)
        y = y + jnp.dot(ub, kd_ref[j], preferred_element_type=F32)
        ys.append(y + d_ref[j] * uf)
    n = _glu_rms(jnp.concatenate(ys, axis=1), gluw_ref, glub_ref, g_ref)
    for j in range(N_SLAB):
        o_ref[j] = n[:, j * LANES:(j + 1) * LANES]


def _ssm_sample_call(u, h0r, h0i, tab, d_t, gluw, glub, g, *, name):
    n_seq = u.shape[1]
    a = jnp.stack([tab['pw_re'][1], tab['pw_im'][1]])[:, None, :]
    return pl.pallas_call(
        _ssm_sample_kernel,
        out_shape=[jax.ShapeDtypeStruct(u.shape, F32),
                   jax.ShapeDtypeStruct((n_seq, N_STATES), F32),
                   jax.ShapeDtypeStruct((n_seq, N_STATES), F32)],
        compiler_params=pltpu.CompilerParams(vmem_limit_bytes=VMEM_LIMIT_BYTES),
        name=name,
    )(u, h0r, h0i, tab['wst'], tab['ca'], tab['kd'][0], a, d_t, gluw, glub, g)


def _attn_kernel(q_ref, kvc_ref, kvp_ref, sink_ref, g_ref, own_ref, o_ref):
    i = pl.program_id(1)
    q = jnp.concatenate([q_ref[h] for h in range(N_HEADS)], axis=0)
    kc = kvc_ref[...].astype(BF16)
    kp = kvp_ref[...].astype(BF16)
    kcat = jnp.concatenate([kp[:, :KV_WIDTH], kc[:, :KV_WIDTH]], axis=0)
    vcat = jnp.concatenate([kp[:, KV_WIDTH:], kc[:, KV_WIDTH:]], axis=0)
    s = lax.dot_general(q, kcat, (((1,), (1,)), ((), ())), preferred_element_type=F32) * ATTN_SCALE
    rows = N_HEADS * WINDOW
    qi = lax.broadcasted_iota(jnp.int32, (rows, 2 * WINDOW), 0) % WINDOW
    kj = lax.broadcasted_iota(jnp.int32, (rows, 2 * WINDOW), 1)
    valid = (kj >= qi) & (kj <= qi + WINDOW) & ((i > 0) | (kj >= WINDOW))
    s = jnp.where(valid, s, MASKED)
    sink = sink_ref[...]
    m = jnp.maximum(jnp.max(s, axis=-1, keepdims=True), sink)
    p = jnp.exp(s - m)
    den = jnp.sum(p, axis=-1, keepdims=True) + jnp.exp(sink - m)
    o = jnp.dot(p.astype(BF16), vcat, preferred_element_type=F32) / den
    ss = None
    for h in range(N_HEADS):
        oh = o[h * WINDOW:(h + 1) * WINDOW]
        t = jnp.sum(oh * oh * own_ref[h], axis=-1, keepdims=True)
        ss = t if ss is None else ss + t
    inv = lax.rsqrt(ss / ATTN_WIDTH + RMS_EPS)
    for h in range(N_HEADS):
        o_ref[h] = (o[h * WINDOW:(h + 1) * WINDOW] * inv * g_ref[h]).astype(o_ref.dtype)


def _attn_prompt_call(q, kv, sink_col, g_exp, own, *, batch, seq, name):
    nb = seq // WINDOW
    qblk = pl.BlockSpec((N_HEADS, WINDOW, LANES), lambda b, i: (0, b * nb + i, 0))
    return pl.pallas_call(
        _attn_kernel,
        grid=(batch, nb),
        in_specs=[qblk,
                  pl.BlockSpec((WINDOW, 2 * KV_WIDTH), lambda b, i: (b * nb + i, 0)),
                  pl.BlockSpec((WINDOW, 2 * KV_WIDTH), lambda b, i: (b * nb + jnp.maximum(i - 1, 0), 0)),
                  _resident(sink_col.shape, 2), _resident(g_exp.shape, 2), _resident(own.shape, 2)],
        out_specs=qblk,
        out_shape=jax.ShapeDtypeStruct(q.shape, BF16),
        compiler_params=pltpu.CompilerParams(dimension_semantics=("arbitrary", "arbitrary"),
                                             vmem_limit_bytes=VMEM_LIMIT_BYTES),
        name=name,
    )(q, kv, kv, sink_col, g_exp, own)


SEQ_PER_STEP = 8


def _attn_sample_kernel(q_ref, kv_ref, ck_ref, cv_ref, sink_ref, g_ref, own_ref, o_ref, ko_ref, vo_ref, *, n_seq):
    i = pl.program_id(0)
    sink = sink_ref[...]
    for t in range(SEQ_PER_STEP):
        n = i * SEQ_PER_STEP + t
        qn = q_ref[pl.ds(n, N_HEADS, stride=n_seq), :].astype(BF16)
        kvn = kv_ref[pl.ds(n, 1), :]
        kn = kvn[:, :KV_WIDTH]
        vn = kvn[:, KV_WIDTH:]
        kc = ck_ref[t]
        vc = cv_ref[t]
        s = lax.dot_general(qn, kc.astype(BF16), (((1,), (1,)), ((), ())), preferred_element_type=F32) * ATTN_SCALE
        s_new = jnp.sum(qn.astype(F32) * kn.astype(BF16).astype(F32), axis=-1, keepdims=True) * ATTN_SCALE
        m = jnp.maximum(jnp.maximum(jnp.max(s, axis=-1, keepdims=True), s_new), sink)
        p = jnp.exp(s - m)
        p_new = jnp.exp(s_new - m)
        den = jnp.sum(p, axis=-1, keepdims=True) + p_new + jnp.exp(sink - m)
        o = jnp.dot(p.astype(BF16), vc.astype(BF16), preferred_element_type=F32)
        o = (o + p_new.astype(BF16).astype(F32) * vn.astype(BF16).astype(F32)) / den
        ss = jnp.sum(jnp.sum(o * o * own_ref[...], axis=-1, keepdims=True), axis=0, keepdims=True)
        o_ref[pl.ds(n, N_HEADS, stride=n_seq), :] = o * lax.rsqrt(ss / ATTN_WIDTH + RMS_EPS) * g_ref[...]
        ko_ref[t, pl.ds(0, WINDOW - 1), :] = ck_ref[t, pl.ds(1, WINDOW - 1), :]
        ko_ref[t, pl.ds(WINDOW - 1, 1), :] = kn
        vo_ref[t, pl.ds(0, WINDOW - 1), :] = cv_ref[t, pl.ds(1, WINDOW - 1), :]
        vo_ref[t, pl.ds(WINDOW - 1, 1), :] = vn


def _attn_sample_call(q, kv, ck, cv, sink8, g8, own8, *, name):
    n_seq = kv.shape[0]
    assert n_seq % SEQ_PER_STEP == 0
    cblk = pl.BlockSpec((SEQ_PER_STEP, WINDOW, 2 * HEAD_DIM), lambda i: (i, 0, 0))
    whole = lambda a: pl.BlockSpec(a.shape, lambda i: (0,) * a.ndim)
    return pl.pallas_call(
        functools.partial(_attn_sample_kernel, n_seq=n_seq),
        grid=(n_seq // SEQ_PER_STEP,),
        in_specs=[whole(q), whole(kv), cblk, cblk, whole(sink8), whole(g8), whole(own8)],
        out_specs=[whole(q), cblk, cblk],
        out_shape=[jax.ShapeDtypeStruct(q.shape, F32),
                   jax.ShapeDtypeStruct(ck.shape, F32),
                   jax.ShapeDtypeStruct(cv.shape, F32)],
        compiler_params=pltpu.CompilerParams(dimension_semantics=("arbitrary",),
                                             vmem_limit_bytes=VMEM_LIMIT_BYTES),
        name=name,
    )(q, kv, ck, cv, sink8, g8, own8)


def _expand_heads(a):
    lead = a.shape[:-1]
    a = a.reshape(lead + (N_KV_HEADS, Q_PER_KV, 1, HEAD_DIM))
    sel = jnp.eye(N_KV_HEADS, dtype=a.dtype).reshape(N_KV_HEADS, 1, N_KV_HEADS, 1)
    return (a * sel).reshape(lead + (Q_EXP,))


def _prep_weights(ln_g, ln_b, ffn1_w_in, ffn1_w_out, ffn2_w_in, ffn2_w_out, w_in, ssm_lam_re, ssm_lam_im, ssm_log_dt,
                  ssm_b_re, ssm_b_im, ssm_c_re, ssm_c_im, ssm_d, glu_w, glu_b, attn_sinks, g_ssm_out, g_attn_out,
                  w_out):
    w = dict(ffn1=[], ffn2=[], ln=[], w_in=[], wo_s=[], wo_a=[], tab=[], tab1=[], ak=[], tr=[], d_t=[], d_1=[],
             gluw=[], glub=[], g_ssm=[], sink_col=[], sink8=[], g_att=[], own=None)
    own = _expand_heads(jnp.ones((ATTN_WIDTH,), F32)).reshape(N_HEADS, 1, LANES)
    w['own'] = own
    for l in range(DEPTH):
        w['ffn1'].append((ffn1_w_in[l].astype(BF16), ffn1_w_out[l].astype(BF16)))
        w['ffn2'].append((ffn2_w_in[l].astype(BF16), ffn2_w_out[l].astype(BF16)))
        w['ln'].append([(ln_g[l, i][None, :], ln_b[l, i][None, :]) for i in range(3)])
        o1, o2 = SSM_WIDTH, SSM_WIDTH + ATTN_WIDTH
        w['w_in'].append(jnp.concatenate([w_in[l][:, :o1], _expand_heads(w_in[l][:, o1:o2]), w_in[l][:, o2:]],
                                         axis=1).astype(BF16))
        w['wo_s'].append(w_out[l, :SSM_WIDTH].astype(BF16))
        w['wo_a'].append(_expand_heads(w_out[l, SSM_WIDTH:].T).T.astype(BF16))
        args = (ssm_lam_re[l], ssm_lam_im[l], ssm_log_dt[l], ssm_b_re[l], ssm_b_im[l], ssm_c_re[l], ssm_c_im[l])
        tab = _ssm_tables(*args, SSM_CHUNK)
        w['tab'].append(tab)
        w['tab1'].append(_ssm_tables(*args, 1))
        ak, tr = _scan_tables(tab['pw_re'][SSM_CHUNK], tab['pw_im'][SSM_CHUNK])
        w['ak'].append(ak); w['tr'].append(tr)
        d_slab = ssm_d[l].reshape(N_SLAB, 1, LANES)
        w['d_1'].append(d_slab)
        w['d_t'].append(jnp.tile(d_slab, (1, 1, SSM_CHUNK)))
        w['gluw'].append(glu_w[l].astype(BF16))
        w['glub'].append(glu_b[l][None, :])
        w['g_ssm'].append(g_ssm_out[l][None, :])
        w['sink8'].append(attn_sinks[l][:, None])
        w['sink_col'].append(jnp.repeat(attn_sinks[l], WINDOW)[:, None])
        w['g_att'].append(_expand_heads(g_attn_out[l]).reshape(N_HEADS, 1, LANES))
    return w


def _prompt_mixer(u, q, kv, l, w, batch, seq):
    ssm_n, hre, him = _ssm_prompt_call(u, w['tab'][l], w['ak'][l], w['tr'][l], w['d_t'][l], w['gluw'][l],
                                       w['glub'][l], w['g_ssm'][l], batch=batch, seq=seq, name=f"p_ssm_{l}")
    att_n = _attn_prompt_call(q, kv, w['sink_col'][l], w['g_att'][l], w['own'], batch=batch, seq=seq,
                              name=f"p_attn_{l}")
    kvw = kv.reshape(batch, seq, 2, N_KV_HEADS, HEAD_DIM)[:, -WINDOW:]
    return (ssm_n, att_n, hre.reshape(batch, N_SSM_GROUPS, SSM_STATE), him.reshape(batch, N_SSM_GROUPS, SSM_STATE),
            kvw[:, :, 0], kvw[:, :, 1])


def _sample_mixer(u, q, kv, l, w, h0_re, h0_im, k_buf, v_buf):
    n_seq = kv.shape[0]
    ssm_n, hre, him = _ssm_sample_call(u, h0_re.reshape(n_seq, N_STATES), h0_im.reshape(n_seq, N_STATES),
                                       w['tab1'][l], w['d_1'][l], w['gluw'][l], w['glub'][l], w['g_ssm'][l],
                                       name=f"s_ssm_{l}")
    att, kn, vn = _attn_sample_call(q.reshape(N_HEADS * n_seq, LANES), kv,
                                    k_buf.reshape(n_seq, WINDOW, 2 * HEAD_DIM), v_buf.reshape(n_seq, WINDOW, 2 * HEAD_DIM),
                                    w['sink8'][l], w['g_att'][l].reshape(N_HEADS, LANES),
                                    w['own'].reshape(N_HEADS, LANES), name=f"s_attn_{l}")
    return (ssm_n, att.reshape(N_HEADS, n_seq, LANES), hre.reshape(n_seq, N_SSM_GROUPS, SSM_STATE),
            him.reshape(n_seq, N_SSM_GROUPS, SSM_STATE), kn.reshape(k_buf.shape), vn.reshape(v_buf.shape))


def _trunk(x, h0_re, h0_im, k_buf, v_buf, w, *, tm, tag):
    Bn, L, _ = x.shape
    sample = k_buf is not None
    q_dtype = F32 if sample else BF16
    x = x.reshape(Bn * L, D_MODEL)
    ks, vs, hrs, his = [], [], [], []
    mix = None
    for l in range(DEPTH):
        if l > 0:
            x = _rows_call(x, w['ffn2'][l - 1], w['ln'][l - 1][2], mix=mix, tm=tm, name=f"{tag}_mix_ffn2_{l - 1}")[0]
        x, u, q, kv = _rows_call(x, w['ffn1'][l], w['ln'][l][0], proj=w['w_in'][l], tm=tm, q_dtype=q_dtype,
                                 name=f"{tag}_ffn1_{l}")
        if sample:
            ssm_n, att_n, hre, him, kn, vn = _sample_mixer(u, q, kv, l, w, h0_re[l], h0_im[l], k_buf[l], v_buf[l])
        else:
            ssm_n, att_n, hre, him, kn, vn = _prompt_mixer(u, q, kv, l, w, Bn, L)
        mix = (ssm_n, att_n, w['wo_s'][l], w['wo_a'][l], w['ln'][l][1][0], w['ln'][l][1][1])
        ks.append(kn); vs.append(vn); hrs.append(hre); his.append(him)
    x = _rows_call(x, w['ffn2'][DEPTH - 1], w['ln'][DEPTH - 1][2], mix=mix, tm=tm, name=f"{tag}_mix_ffn2_{DEPTH - 1}")[0]
    return x.reshape(Bn, L, D_MODEL), jnp.stack(ks), jnp.stack(vs), jnp.stack(hrs), jnp.stack(his)


def kernel(x_prompt, x_sample, cache_k_win, cache_v_win, state_ssm_re, state_ssm_im, ln_g, ln_b, ffn1_w_in, ffn1_w_out, ffn2_w_in, ffn2_w_out, w_in, ssm_lam_re, ssm_lam_im, ssm_log_dt, ssm_b_re, ssm_b_im, ssm_c_re, ssm_c_im, ssm_d, glu_w, glu_b, attn_sinks, g_ssm_out, g_attn_out, w_out):
    w = _prep_weights(ln_g, ln_b, ffn1_w_in, ffn1_w_out, ffn2_w_in, ffn2_w_out, w_in, ssm_lam_re, ssm_lam_im,
                      ssm_log_dt, ssm_b_re, ssm_b_im, ssm_c_re, ssm_c_im, ssm_d, glu_w, glu_b, attn_sinks,
                      g_ssm_out, g_attn_out, w_out)
    y_prompt, kp, vp, hrp, hip = _trunk(x_prompt, None, None, None, None, w, tm=512, tag="p")
    y_sample, ks_, vs_, hrs, his = _trunk(x_sample, state_ssm_re, state_ssm_im, cache_k_win, cache_v_win, w,
                                          tm=128, tag="s")
    return (y_prompt, y_sample, kp, vp, hrp, hip, ks_, vs_, hrs, his)
```

```python
import functools

import jax
import jax.numpy as jnp
from jax import lax
from jax.experimental import pallas as pl
from jax.experimental.pallas import tpu as pltpu

D_MODEL = 1024
DEPTH = 2
SSM_WIDTH = 512
SSM_GROUP = 16
N_SSM_GROUPS = 32
SSM_STATE = 64
N_STATES = N_SSM_GROUPS * SSM_STATE
ATTN_WIDTH = 512
HEAD_DIM = 64
N_HEADS = 8
N_KV_HEADS = 2
Q_PER_KV = 4
KV_WIDTH = 128
WINDOW = 128
ATTN_SCALE = HEAD_DIM ** -0.5
D_FF = 2816
ALPHA = (2.0 * DEPTH) ** 0.25
LN_EPS = 1e-5
RMS_EPS = 1e-6

LANES = 128
SUBLANES = 8
VMEM_LIMIT_BYTES = 56 * 1024 * 1024
N_SLAB = SSM_WIDTH // LANES
GROUPS_PER_SLAB = LANES // SSM_GROUP
SLAB_STATES = GROUPS_PER_SLAB * SSM_STATE
Q_EXP = N_HEADS * LANES
PROJ_COLS = SSM_WIDTH + Q_EXP + 2 * KV_WIDTH
SSM_CHUNK = 8
SSM_ROWS = 2048
MASKED = -1e30

F32 = jnp.float32
BF16 = jnp.bfloat16


def _layer_norm(r, g, b):
    mu = jnp.mean(r, axis=-1, keepdims=True)
    c = r - mu
    var = jnp.mean(c * c, axis=-1, keepdims=True)
    return c * lax.rsqrt(var + LN_EPS) * g + b


def _rms_norm(y, g):
    return y * lax.rsqrt(jnp.mean(y * y, axis=-1, keepdims=True) + RMS_EPS) * g


def _resident(shape, n_grid=1):
    zeros = (0,) * len(shape)
    if n_grid == 1:
        return pl.BlockSpec(shape, lambda i: zeros, pipeline_mode=pl.Buffered(1))
    return pl.BlockSpec(shape, lambda i, j: zeros, pipeline_mode=pl.Buffered(1))


def _rows_kernel(*refs, has_mix, has_proj):
    it = iter(refs)
    x_ref = next(it)
    if has_mix:
        ssm_ref, att_ref, wo_s_ref, wo_a_ref, gm_ref, bm_ref = (next(it) for _ in range(6))
    wgu_ref, wdn_ref, g_ref, b_ref = (next(it) for _ in range(4))
    if has_proj:
        wp_ref = next(it)
    o_ref = next(it)
    if has_proj:
        u_ref, q_ref, kv_ref = (next(it) for _ in range(3))

    x = x_ref[...]
    if has_mix:
        ssm = jnp.concatenate([ssm_ref[j] for j in range(N_SLAB)], axis=1).astype(BF16)
        att = jnp.concatenate([att_ref[j] for j in range(Q_PER_KV)], axis=1).astype(BF16)
        m = jnp.dot(ssm, wo_s_ref[...], preferred_element_type=F32)
        m = m + jnp.dot(att, wo_a_ref[...], preferred_element_type=F32)
        x = _layer_norm(ALPHA * x + m, gm_ref[...], bm_ref[...])
    xb = x.astype(BF16)
    gu = jnp.dot(xb, wgu_ref[...], preferred_element_type=F32)
    gate = gu[:, :D_FF]
    up = gu[:, D_FF:]
    h = (gate * jax.nn.sigmoid(gate) * up).astype(BF16)
    y = jnp.dot(h, wdn_ref[...], preferred_element_type=F32)
    x = _layer_norm(ALPHA * x + 0.5 * y, g_ref[...], b_ref[...])
    o_ref[...] = x
    if has_proj:
        z = jnp.dot(x.astype(BF16), wp_ref[...], preferred_element_type=F32)
        for j in range(N_SLAB):
            u_ref[j] = z[:, j * LANES:(j + 1) * LANES]
        for hd in range(N_HEADS):
            q_ref[hd] = z[:, SSM_WIDTH + hd * LANES:SSM_WIDTH + (hd + 1) * LANES].astype(q_ref.dtype)
        kv_ref[...] = z[:, SSM_WIDTH + Q_EXP:]


def _rows_call(x, ffn, ln, mix=None, proj=None, *, tm, q_dtype=BF16, name):
    M = x.shape[0]
    assert M % tm == 0
    row = lambda w: pl.BlockSpec((tm, w), lambda i: (i, 0))
    slab = lambda n: pl.BlockSpec((n, tm, LANES), lambda i: (0, i, 0))
    args = [x]
    specs = [row(D_MODEL)]
    if mix is not None:
        ssm_n, att_n, wo_s, wo_a, gm, bm = mix
        args += [ssm_n, att_n, wo_s, wo_a, gm, bm]
        specs += [slab(N_SLAB), slab(Q_PER_KV), _resident(wo_s.shape), _resident(wo_a.shape),
                  _resident(gm.shape), _resident(bm.shape)]
    args += [ffn[0], ffn[1], ln[0], ln[1]]
    specs += [_resident(ffn[0].shape), _resident(ffn[1].shape), _resident(ln[0].shape), _resident(ln[1].shape)]
    out_shape = [jax.ShapeDtypeStruct((M, D_MODEL), F32)]
    out_specs = [row(D_MODEL)]
    if proj is not None:
        args.append(proj)
        specs.append(_resident(proj.shape))
        out_shape += [jax.ShapeDtypeStruct((N_SLAB, M, LANES), F32),
                      jax.ShapeDtypeStruct((N_HEADS, M, LANES), q_dtype),
                      jax.ShapeDtypeStruct((M, 2 * KV_WIDTH), F32)]
        out_specs += [slab(N_SLAB), slab(N_HEADS), row(2 * KV_WIDTH)]
    return pl.pallas_call(
        functools.partial(_rows_kernel, has_mix=mix is not None, has_proj=proj is not None),
        grid=(M // tm,),
        in_specs=specs,
        out_specs=out_specs,
        out_shape=out_shape,
        compiler_params=pltpu.CompilerParams(dimension_semantics=("arbitrary",),
                                             vmem_limit_bytes=VMEM_LIMIT_BYTES),
        name=name,
    )(*args)


def _cmul(xr, xi, yr, yi):
    return xr * yr - xi * yi, xr * yi + xi * yr


def _ssm_tables(lam_re, lam_im, log_dt, b_re, b_im, c_re, c_im, t1):
    G, P, H = N_SSM_GROUPS, SSM_STATE, SSM_GROUP
    hi = lax.Precision.HIGHEST
    dt = jnp.exp(log_dt)[:, None]
    mag = jnp.exp(lam_re * dt)
    a_re = mag * jnp.cos(lam_im * dt)
    a_im = mag * jnp.sin(lam_im * dt)
    num_re, num_im = a_re - 1.0, a_im
    den = jnp.square(lam_re) + jnp.square(lam_im)
    f_re = (num_re * lam_re + num_im * lam_im) / den
    f_im = (num_im * lam_re - num_re * lam_im) / den
    bb_re = f_re[..., None] * b_re - f_im[..., None] * b_im
    bb_im = f_re[..., None] * b_im + f_im[..., None] * b_re
    pr, pi = [jnp.ones_like(a_re)], [jnp.zeros_like(a_re)]
    for _ in range(t1):
        r, i = _cmul(pr[-1], pi[-1], a_re, a_im)
        pr.append(r); pi.append(i)
    PR, PI = jnp.stack(pr), jnp.stack(pi)
    ABr = PR[:t1, :, :, None] * bb_re - PI[:t1, :, :, None] * bb_im
    ABi = PR[:t1, :, :, None] * bb_im + PI[:t1, :, :, None] * bb_re
    gps = GROUPS_PER_SLAB
    eye = jnp.eye(gps, dtype=F32)

    X = jnp.stack([ABr[::-1], ABi[::-1]], axis=0).reshape(2, t1, N_SLAB, gps, P, H)
    X = jnp.transpose(X, (2, 1, 3, 5, 0, 4))
    wst = (X[:, :, :, :, :, None, :] * eye[None, None, :, None, None, :, None]).astype(BF16)
    wst = wst.reshape(N_SLAB, t1 * LANES, 2 * SLAB_STATES)

    cr = c_re[None] * PR[1:, :, None, :] - c_im[None] * PI[1:, :, None, :]
    ci = -(c_re[None] * PI[1:, :, None, :] + c_im[None] * PR[1:, :, None, :])
    Cf = jnp.stack([cr, ci], axis=0).reshape(2, t1, N_SLAB, gps, H, P)
    Cf = jnp.transpose(Cf, (2, 0, 3, 5, 1, 4))
    ca = (Cf[:, :, :, :, :, None, :] * eye[None, None, :, None, None, :, None]).astype(BF16)
    ca = ca.reshape(N_SLAB, 2 * SLAB_STATES, t1 * LANES)

    K = (jnp.einsum('ghp,lgpk->lgkh', c_re, ABr, precision=hi)
         - jnp.einsum('ghp,lgpk->lgkh', c_im, ABi, precision=hi))
    K = K.reshape(t1, N_SLAB, gps, H, H)
    kd = (K[:, :, :, :, None, :] * eye[None, None, :, None, :, None]).astype(BF16).reshape(t1, N_SLAB, LANES, LANES)
    kpad = jnp.concatenate([jnp.zeros_like(kd[:1]), kd], axis=0)
    lo = jnp.transpose(kpad[:t1][::-1], (1, 0, 2, 3))
    hi_ = jnp.transpose(kpad[1:][::-1], (1, 0, 2, 3))
    strip = jnp.stack([lo, hi_], axis=3).reshape(N_SLAB, t1 * LANES, 2 * LANES)
    return dict(wst=wst, ca=ca, strip=strip, kd=kd,
                pw_re=PR.reshape(t1 + 1, N_STATES), pw_im=PI.reshape(t1 + 1, N_STATES))


def _scan_tables(base_re, base_im):
    pr, pi = [base_re], [base_im]
    for _ in range(SUBLANES - 1):
        r, i = _cmul(pr[-1], pi[-1], base_re, base_im)
        pr.append(r); pi.append(i)
    steps = []
    for k in (1, 2, 4):
        steps += [pr[k - 1], pi[k - 1]]
    ak = jnp.broadcast_to(jnp.stack(steps)[:, None, :], (6, SUBLANES, N_STATES))
    tr = jnp.stack([jnp.stack(pr), jnp.stack(pi)])
    return ak, tr


def _glu_rms(y, gluw_ref, glub_ref, g_ref):
    g = jax.nn.gelu(y).astype(BF16)
    zz = jnp.dot(g, gluw_ref[...], preferred_element_type=F32) + glub_ref[...]
    o = zz[:, :SSM_WIDTH] * jax.nn.sigmoid(zz[:, SSM_WIDTH:])
    return _rms_norm(o, g_ref[...])


def _ssm_kernel(u_ref, wst_ref, ca_ref, strip_ref, ak_ref, tr_ref, d_ref, gluw_ref, glub_ref, g_ref,
                o_ref, hre_ref, him_ref,
                ucb_ref, sre_ref, sim_ref, hcr_ref, hci_ref):
    t1 = SSM_CHUNK
    nc = SSM_ROWS // t1
    i = pl.program_id(1)

    @pl.when(i == 0)
    def _():
        hcr_ref[...] = jnp.zeros_like(hcr_ref)
        hci_ref[...] = jnp.zeros_like(hci_ref)

    for j in range(N_SLAB):
        for s in range(t1):
            ucb_ref[j, :, s * LANES:(s + 1) * LANES] = u_ref[j, pl.ds(s, nc, stride=t1), :].astype(BF16)

    for j in range(N_SLAB):
        st = jnp.dot(ucb_ref[j], wst_ref[j], preferred_element_type=F32)
        sre_ref[:, j * SLAB_STATES:(j + 1) * SLAB_STATES] = st[:, :SLAB_STATES]
        sim_ref[:, j * SLAB_STATES:(j + 1) * SLAB_STATES] = st[:, SLAB_STATES:]

    row = lax.broadcasted_iota(jnp.int32, (SUBLANES, SLAB_STATES), 0)
    for j in range(N_SLAB):
        cols = pl.ds(j * SLAB_STATES, SLAB_STATES)

        def body(gi, carry, cols=cols):
            hr, hi = carry
            r0 = pl.multiple_of(gi * SUBLANES, SUBLANES)
            xr = sre_ref[pl.ds(r0, SUBLANES), cols]
            xi = sim_ref[pl.ds(r0, SUBLANES), cols]
            for n, k in enumerate((1, 2, 4)):
                akr = ak_ref[2 * n, :, cols]
                aki = ak_ref[2 * n + 1, :, cols]
                sr = jnp.where(row >= k, pltpu.roll(xr, k, 0), 0.0)
                si = jnp.where(row >= k, pltpu.roll(xi, k, 0), 0.0)
                xr, xi = xr + akr * sr - aki * si, xi + akr * si + aki * sr
            tr = tr_ref[0, :, cols]
            ti = tr_ref[1, :, cols]
            er = xr + tr * hr - ti * hi
            ei = xi + tr * hi + ti * hr
            sre_ref[pl.ds(r0, SUBLANES), cols] = jnp.where(row >= 1, pltpu.roll(er, 1, 0), hr)
            sim_ref[pl.ds(r0, SUBLANES), cols] = jnp.where(row >= 1, pltpu.roll(ei, 1, 0), hi)
            last = SUBLANES - 1
            return (jnp.broadcast_to(er[last:last + 1], er.shape), jnp.broadcast_to(ei[last:last + 1], ei.shape))

        hr, hi = lax.fori_loop(0, nc // SUBLANES, body, (hcr_ref[:, cols], hci_ref[:, cols]))
        hcr_ref[:, cols] = hr
        hci_ref[:, cols] = hi

    hre_ref[0] = hcr_ref[...]
    him_ref[0] = hci_ref[...]

    ys = []
    for j in range(N_SLAB):
        cols = slice(j * SLAB_STATES, (j + 1) * SLAB_STATES)
        hprev = jnp.concatenate([sre_ref[:, cols], sim_ref[:, cols]], axis=1).astype(BF16)
        y = jnp.dot(hprev, ca_ref[j], preferred_element_type=F32)
        parts = []
        for p2 in range(t1 // 2):
            kk = (2 * p2 + 2) * LANES
            yi = jnp.dot(ucb_ref[j, :, :kk], strip_ref[j, (t1 - 2 - 2 * p2) * LANES:, :],
                         preferred_element_type=F32)
            parts.append(y[:, 2 * p2 * LANES:(2 * p2 + 2) * LANES] + yi)
        ys.append(jnp.concatenate(parts, axis=1))

    for t in range(t1):
        y = jnp.concatenate([ys[j][:, t * LANES:(t + 1) * LANES] + d_ref[j] * u_ref[j, pl.ds(t, nc, stride=t1), :]
                             for j in range(N_SLAB)], axis=1)
        n = _glu_rms(y, gluw_ref, glub_ref, g_ref)
        for j in range(N_SLAB):
            o_ref[j, pl.ds(t, nc, stride=t1), :] = n[:, j * LANES:(j + 1) * LANES]


def _ssm_prompt_call(u, tab, ak, tr, d_t, gluw, glub, g, *, batch, seq, name):
    t1 = SSM_CHUNK
    nc = SSM_ROWS // t1
    steps = seq // SSM_ROWS
    blk = pl.BlockSpec((N_SLAB, SSM_ROWS, LANES), lambda b, i: (0, b * steps + i, 0))
    st = pl.BlockSpec((1, SUBLANES, N_STATES), lambda b, i: (b, 0, 0))
    consts = [tab['wst'], tab['ca'], tab['strip'], ak, tr, d_t, gluw, glub, g]
    out, hre, him = pl.pallas_call(
        _ssm_kernel,
        grid=(batch, steps),
        in_specs=[blk] + [_resident(c.shape, 2) for c in consts],
        out_specs=[blk, st, st],
        out_shape=[jax.ShapeDtypeStruct(u.shape, F32),
                   jax.ShapeDtypeStruct((batch, SUBLANES, N_STATES), F32),
                   jax.ShapeDtypeStruct((batch, SUBLANES, N_STATES), F32)],
        scratch_shapes=[pltpu.VMEM((N_SLAB, nc, t1 * LANES), BF16),
                        pltpu.VMEM((nc, N_STATES), F32),
                        pltpu.VMEM((nc, N_STATES), F32),
                        pltpu.VMEM((SUBLANES, N_STATES), F32),
                        pltpu.VMEM((SUBLANES, N_STATES), F32)],
        compiler_params=pltpu.CompilerParams(dimension_semantics=("arbitrary", "arbitrary"),
                                             vmem_limit_bytes=VMEM_LIMIT_BYTES),
        name=name,
    )(u, *consts)
    return out, hre[:, 0], him[:, 0]


def _ssm_sample_kernel(u_ref, h0r_ref, h0i_ref, wst_ref, ca_ref, kd_ref, a_ref, d_ref, gluw_ref, glub_ref, g_ref,
                       o_ref, hr_ref, hi_ref):
    ys = []
    for j in range(N_SLAB):
        cols = slice(j * SLAB_STATES, (j + 1) * SLAB_STATES)
        uf = u_ref[j]
        ub = uf.astype(BF16)
        st = jnp.dot(ub, wst_ref[j], preferred_element_type=F32)
        h0r = h0r_ref[:, cols]
        h0i = h0i_ref[:, cols]
        ar = a_ref[0, :, cols]
        ai = a_ref[1, :, cols]
        hr_ref[:, cols] = ar * h0r - ai * h0i + st[:, :SLAB_STATES]
        hi_ref[:, cols] = ar * h0i + ai * h0r + st[:, SLAB_STATES:]
        hcat = jnp.concatenate([h0r, h0i], axis=1).astype(BF16)
        y = jnp.dot(hcat, ca_ref[j], preferred_element_type=F32)
        y = y + jnp.dot(ub, kd_ref[j], preferred_element_type=F32)
        ys.append(y + d_ref[j] * uf)
    n = _glu_rms(jnp.concatenate(ys, axis=1), gluw_ref, glub_ref, g_ref)
    for j in range(N_SLAB):
        o_ref[j] = n[:, j * LANES:(j + 1) * LANES]


def _ssm_sample_call(u, h0r, h0i, tab, d_t, gluw, glub, g, *, name):
    n_seq = u.shape[1]
    a = jnp.stack([tab['pw_re'][1], tab['pw_im'][1]])[:, None, :]
    return pl.pallas_call(
        _ssm_sample_kernel,
        out_shape=[jax.ShapeDtypeStruct(u.shape, F32),
                   jax.ShapeDtypeStruct((n_seq, N_STATES), F32),
                   jax.ShapeDtypeStruct((n_seq, N_STATES), F32)],
        compiler_params=pltpu.CompilerParams(vmem_limit_bytes=VMEM_LIMIT_BYTES),
        name=name,
    )(u, h0r, h0i, tab['wst'], tab['ca'], tab['kd'][0], a, d_t, gluw, glub, g)


def _attn_kernel(sink_ref, q_ref, kvc_ref, kvp_ref, g_ref, o_ref):
    i = pl.program_id(1)
    kc = kvc_ref[...]
    kp = kvp_ref[...]
    kcat = (jnp.concatenate([kp[:, :KV_WIDTH], kc[:, :KV_WIDTH]], axis=0) * ATTN_SCALE).astype(BF16)
    v_t = jnp.concatenate([kp[:, KV_WIDTH:], kc[:, KV_WIDTH:]], axis=0).T.astype(BF16)
    kj = lax.broadcasted_iota(jnp.int32, (2 * WINDOW, WINDOW), 0)
    qi = lax.broadcasted_iota(jnp.int32, (2 * WINDOW, WINDOW), 1)
    valid = (kj >= qi) & (kj <= qi + WINDOW) & ((i > 0) | (kj >= WINDOW))
    outs = []
    for h in range(N_HEADS):
        kvh = h // Q_PER_KV
        s_t = lax.dot_general(kcat, q_ref[h], (((1,), (1,)), ((), ())), preferred_element_type=F32)
        s_t = jnp.where(valid, s_t, MASKED)
        sink = sink_ref[h]
        m = jnp.maximum(jnp.max(s_t, axis=0, keepdims=True), sink)
        p_t = jnp.exp(s_t - m)
        den = jnp.sum(p_t, axis=0, keepdims=True) + jnp.exp(sink - m)
        o_t = jnp.dot(v_t[kvh * HEAD_DIM:(kvh + 1) * HEAD_DIM], p_t.astype(BF16), preferred_element_type=F32)
        outs.append(o_t / den)
    tiles = [jnp.concatenate([outs[pr], outs[pr + Q_PER_KV]], axis=0) for pr in range(Q_PER_KV)]
    ss = None
    for tile in tiles:
        t = jnp.sum(tile * tile, axis=0, keepdims=True)
        ss = t if ss is None else ss + t
    inv = lax.rsqrt(ss / ATTN_WIDTH + RMS_EPS)
    for pr, tile in enumerate(tiles):
        o_ref[pr] = ((tile * inv).T * g_ref[pr]).astype(o_ref.dtype)


def _attn_prompt_call(q, kv, sinks, g_perm, *, batch, seq, name):
    nb = seq // WINDOW
    M = kv.shape[0]
    return pl.pallas_call(
        _attn_kernel,
        grid=(batch, nb),
        in_specs=[pl.BlockSpec(memory_space=pltpu.SMEM),
                  pl.BlockSpec((N_HEADS, WINDOW, LANES), lambda b, i: (0, b * nb + i, 0)),
                  pl.BlockSpec((WINDOW, 2 * KV_WIDTH), lambda b, i: (b * nb + i, 0)),
                  pl.BlockSpec((WINDOW, 2 * KV_WIDTH), lambda b, i: (b * nb + jnp.maximum(i - 1, 0), 0)),
                  _resident(g_perm.shape, 2)],
        out_specs=pl.BlockSpec((Q_PER_KV, WINDOW, LANES), lambda b, i: (0, b * nb + i, 0)),
        out_shape=jax.ShapeDtypeStruct((Q_PER_KV, M, LANES), BF16),
        compiler_params=pltpu.CompilerParams(dimension_semantics=("arbitrary", "arbitrary"),
                                             vmem_limit_bytes=VMEM_LIMIT_BYTES),
        name=name,
    )(sinks, q, kv, kv, g_perm)


SEQ_PER_STEP = 8


def _attn_sample_kernel(q_ref, kv_ref, ck_ref, cv_ref, sink_ref, g_ref, own_ref, o_ref, ko_ref, vo_ref, *, n_seq):
    i = pl.program_id(0)
    sink = sink_ref[...]
    for t in range(SEQ_PER_STEP):
        n = i * SEQ_PER_STEP + t
        qn = q_ref[pl.ds(n, N_HEADS, stride=n_seq), :].astype(BF16)
        kvn = kv_ref[pl.ds(n, 1), :]
        kn = kvn[:, :KV_WIDTH]
        vn = kvn[:, KV_WIDTH:]
        kc = ck_ref[t]
        vc = cv_ref[t]
        s = lax.dot_general(qn, kc.astype(BF16), (((1,), (1,)), ((), ())), preferred_element_type=F32) * ATTN_SCALE
        s_new = jnp.sum(qn.astype(F32) * kn.astype(BF16).astype(F32), axis=-1, keepdims=True) * ATTN_SCALE
        m = jnp.maximum(jnp.maximum(jnp.max(s, axis=-1, keepdims=True), s_new), sink)
        p = jnp.exp(s - m)
        p_new = jnp.exp(s_new - m)
        den = jnp.sum(p, axis=-1, keepdims=True) + p_new + jnp.exp(sink - m)
        o = jnp.dot(p.astype(BF16), vc.astype(BF16), preferred_element_type=F32)
        o = (o + p_new.astype(BF16).astype(F32) * vn.astype(BF16).astype(F32)) / den
        ss = jnp.sum(jnp.sum(o * o * own_ref[...], axis=-1, keepdims=True), axis=0, keepdims=True)
        nrm = o * lax.rsqrt(ss / ATTN_WIDTH + RMS_EPS) * g_ref[...]
        o_ref[pl.ds(n, Q_PER_KV, stride=n_seq), :] = nrm[:Q_PER_KV] + nrm[Q_PER_KV:]
        ko_ref[t, pl.ds(0, WINDOW - 1), :] = ck_ref[t, pl.ds(1, WINDOW - 1), :]
        ko_ref[t, pl.ds(WINDOW - 1, 1), :] = kn
        vo_ref[t, pl.ds(0, WINDOW - 1), :] = cv_ref[t, pl.ds(1, WINDOW - 1), :]
        vo_ref[t, pl.ds(WINDOW - 1, 1), :] = vn


def _attn_sample_call(q, kv, ck, cv, sink8, g8, own8, *, name):
    n_seq = kv.shape[0]
    assert n_seq % SEQ_PER_STEP == 0
    cblk = pl.BlockSpec((SEQ_PER_STEP, WINDOW, 2 * HEAD_DIM), lambda i: (i, 0, 0))
    whole = lambda a: pl.BlockSpec(a.shape, lambda i: (0,) * len(a.shape))
    o_shape = jax.ShapeDtypeStruct((Q_PER_KV * n_seq, LANES), F32)
    return pl.pallas_call(
        functools.partial(_attn_sample_kernel, n_seq=n_seq),
        grid=(n_seq // SEQ_PER_STEP,),
        in_specs=[whole(q), whole(kv), cblk, cblk, whole(sink8), whole(g8), whole(own8)],
        out_specs=[whole(o_shape), cblk, cblk],
        out_shape=[o_shape,
                   jax.ShapeDtypeStruct(ck.shape, F32),
                   jax.ShapeDtypeStruct(cv.shape, F32)],
        compiler_params=pltpu.CompilerParams(dimension_semantics=("arbitrary",),
                                             vmem_limit_bytes=VMEM_LIMIT_BYTES),
        name=name,
    )(q, kv, ck, cv, sink8, g8, own8)


def _expand_heads(a):
    lead = a.shape[:-1]
    a = a.reshape(lead + (N_KV_HEADS, Q_PER_KV, 1, HEAD_DIM))
    sel = jnp.eye(N_KV_HEADS, dtype=a.dtype).reshape(N_KV_HEADS, 1, N_KV_HEADS, 1)
    return (a * sel).reshape(lead + (Q_EXP,))


def _pair_heads(a):
    lead = a.shape[:-1]
    a = a.reshape(lead + (N_KV_HEADS, Q_PER_KV, HEAD_DIM))
    return jnp.swapaxes(a, -3, -2).reshape(lead + (ATTN_WIDTH,))


def _prep_weights(ln_g, ln_b, ffn1_w_in, ffn1_w_out, ffn2_w_in, ffn2_w_out, w_in, ssm_lam_re, ssm_lam_im, ssm_log_dt,
                  ssm_b_re, ssm_b_im, ssm_c_re, ssm_c_im, ssm_d, glu_w, glu_b, attn_sinks, g_ssm_out, g_attn_out,
                  w_out):
    w = dict(ffn1=[], ffn2=[], ln=[], w_in=[], wo_s=[], wo_a=[], tab=[], tab1=[], ak=[], tr=[], d_1=[],
             gluw=[], glub=[], g_ssm=[], sinks=[], g_pair=[], sink8=[], g_att=[], own=None)
    own = _expand_heads(jnp.ones((ATTN_WIDTH,), F32)).reshape(N_HEADS, 1, LANES)
    w['own'] = own
    for l in range(DEPTH):
        w['ffn1'].append((ffn1_w_in[l].astype(BF16), ffn1_w_out[l].astype(BF16)))
        w['ffn2'].append((ffn2_w_in[l].astype(BF16), ffn2_w_out[l].astype(BF16)))
        w['ln'].append([(ln_g[l, i][None, :], ln_b[l, i][None, :]) for i in range(3)])
        o1, o2 = SSM_WIDTH, SSM_WIDTH + ATTN_WIDTH
        w['w_in'].append(jnp.concatenate([w_in[l][:, :o1], _expand_heads(w_in[l][:, o1:o2]), w_in[l][:, o2:]],
                                         axis=1).astype(BF16))
        w['wo_s'].append(w_out[l, :SSM_WIDTH].astype(BF16))
        w['wo_a'].append(_pair_heads(w_out[l, SSM_WIDTH:].T).T.astype(BF16))
        args = (ssm_lam_re[l], ssm_lam_im[l], ssm_log_dt[l], ssm_b_re[l], ssm_b_im[l], ssm_c_re[l], ssm_c_im[l])
        tab = _ssm_tables(*args, SSM_CHUNK)
        w['tab'].append(tab)
        w['tab1'].append(_ssm_tables(*args, 1))
        ak, tr = _scan_tables(tab['pw_re'][SSM_CHUNK], tab['pw_im'][SSM_CHUNK])
        w['ak'].append(ak); w['tr'].append(tr)
        d_slab = ssm_d[l].reshape(N_SLAB, 1, LANES)
        w['d_1'].append(d_slab)
        w['gluw'].append(glu_w[l].astype(BF16))
        w['glub'].append(glu_b[l][None, :])
        w['g_ssm'].append(g_ssm_out[l][None, :])
        w['sink8'].append(attn_sinks[l][:, None])
        w['sinks'].append(attn_sinks[l])
        w['g_pair'].append(_pair_heads(g_attn_out[l]).reshape(Q_PER_KV, 1, LANES))
        w['g_att'].append(_expand_heads(g_attn_out[l]).reshape(N_HEADS, 1, LANES))
    return w


def _prompt_mixer(u, q, kv, l, w, batch, seq):
    ssm_n, hre, him = _ssm_prompt_call(u, w['tab'][l], w['ak'][l], w['tr'][l], w['d_1'][l], w['gluw'][l],
                                       w['glub'][l], w['g_ssm'][l], batch=batch, seq=seq, name=f"p_ssm_{l}")
    att_n = _attn_prompt_call(q, kv, w['sinks'][l], w['g_pair'][l], batch=batch, seq=seq, name=f"p_attn_{l}")
    kvw = kv.reshape(batch, seq, 2, N_KV_HEADS, HEAD_DIM)[:, -WINDOW:]
    return (ssm_n, att_n, hre.reshape(batch, N_SSM_GROUPS, SSM_STATE), him.reshape(batch, N_SSM_GROUPS, SSM_STATE),
            kvw[:, :, 0], kvw[:, :, 1])


def _sample_mixer(u, q, kv, l, w, h0_re, h0_im, k_buf, v_buf):
    n_seq = kv.shape[0]
    ssm_n, hre, him = _ssm_sample_call(u, h0_re.reshape(n_seq, N_STATES), h0_im.reshape(n_seq, N_STATES),
                                       w['tab1'][l], w['d_1'][l], w['gluw'][l], w['glub'][l], w['g_ssm'][l],
                                       name=f"s_ssm_{l}")
    att, kn, vn = _attn_sample_call(q.reshape(N_HEADS * n_seq, LANES), kv,
                                    k_buf.reshape(n_seq, WINDOW, 2 * HEAD_DIM), v_buf.reshape(n_seq, WINDOW, 2 * HEAD_DIM),
                                    w['sink8'][l], w['g_att'][l].reshape(N_HEADS, LANES),
                                    w['own'].reshape(N_HEADS, LANES), name=f"s_attn_{l}")
    return (ssm_n, att.reshape(Q_PER_KV, n_seq, LANES), hre.reshape(n_seq, N_SSM_GROUPS, SSM_STATE),
            him.reshape(n_seq, N_SSM_GROUPS, SSM_STATE), kn.reshape(k_buf.shape), vn.reshape(v_buf.shape))


def _trunk(x, h0_re, h0_im, k_buf, v_buf, w, *, tm, tag):
    Bn, L, _ = x.shape
    sample = k_buf is not None
    q_dtype = F32 if sample else BF16
    x = x.reshape(Bn * L, D_MODEL)
    ks, vs, hrs, his = [], [], [], []
    mix = None
    for l in range(DEPTH):
        if l > 0:
            x = _rows_call(x, w['ffn2'][l - 1], w['ln'][l - 1][2], mix=mix, tm=tm, name=f"{tag}_mix_ffn2_{l - 1}")[0]
        x, u, q, kv = _rows_call(x, w['ffn1'][l], w['ln'][l][0], proj=w['w_in'][l], tm=tm, q_dtype=q_dtype,
                                 name=f"{tag}_ffn1_{l}")
        if sample:
            ssm_n, att_n, hre, him, kn, vn = _sample_mixer(u, q, kv, l, w, h0_re[l], h0_im[l], k_buf[l], v_buf[l])
        else:
            ssm_n, att_n, hre, him, kn, vn = _prompt_mixer(u, q, kv, l, w, Bn, L)
        mix = (ssm_n, att_n, w['wo_s'][l], w['wo_a'][l], w['ln'][l][1][0], w['ln'][l][1][1])
        ks.append(kn); vs.append(vn); hrs.append(hre); his.append(him)
    x = _rows_call(x, w['ffn2'][DEPTH - 1], w['ln'][DEPTH - 1][2], mix=mix, tm=tm, name=f"{tag}_mix_ffn2_{DEPTH - 1}")[0]
    return x.reshape(Bn, L, D_MODEL), jnp.stack(ks), jnp.stack(vs), jnp.stack(hrs), jnp.stack(his)


def kernel(x_prompt, x_sample, cache_k_win, cache_v_win, state_ssm_re, state_ssm_im, ln_g, ln_b, ffn1_w_in, ffn1_w_out, ffn2_w_in, ffn2_w_out, w_in, ssm_lam_re, ssm_lam_im, ssm_log_dt, ssm_b_re, ssm_b_im, ssm_c_re, ssm_c_im, ssm_d, glu_w, glu_b, attn_sinks, g_ssm_out, g_attn_out, w_out):
    w = _prep_weights(ln_g, ln_b, ffn1_w_in, ffn1_w_out, ffn2_w_in, ffn2_w_out, w_in, ssm_lam_re, ssm_lam_im,
                      ssm_log_dt, ssm_b_re, ssm_b_im, ssm_c_re, ssm_c_im, ssm_d, glu_w, glu_b, attn_sinks,
                      g_ssm_out, g_attn_out, w_out)
    y_prompt, kp, vp, hrp, hip = _trunk(x_prompt, None, None, None, None, w, tm=512, tag="p")
    y_sample, ks_, vs_, hrs, his = _trunk(x_sample, state_ssm_re, state_ssm_im, cache_k_win, cache_v_win, w,
                                          tm=128, tag="s")
    return (y_prompt, y_sample, kp, vp, hrp, hip, ks_, vs_, hrs, his)
```

```python
import functools

import jax
import jax.numpy as jnp
from jax import lax
from jax.experimental import pallas as pl
from jax.experimental.pallas import tpu as pltpu

D_MODEL = 1024
DEPTH = 2
SSM_WIDTH = 512
SSM_GROUP = 16
N_SSM_GROUPS = 32
SSM_STATE = 64
N_STATES = N_SSM_GROUPS * SSM_STATE
ATTN_WIDTH = 512
HEAD_DIM = 64
N_HEADS = 8
N_KV_HEADS = 2
Q_PER_KV = 4
KV_WIDTH = 128
WINDOW = 128
ATTN_SCALE = HEAD_DIM ** -0.5
D_FF = 2816
ALPHA = (2.0 * DEPTH) ** 0.25
LN_EPS = 1e-5
RMS_EPS = 1e-6

LANES = 128
SUBLANES = 8
VMEM_LIMIT_BYTES = 56 * 1024 * 1024
N_SLAB = SSM_WIDTH // LANES
GROUPS_PER_SLAB = LANES // SSM_GROUP
SLAB_STATES = GROUPS_PER_SLAB * SSM_STATE
Q_EXP = N_HEADS * LANES
PROJ_COLS = SSM_WIDTH + Q_EXP + 2 * KV_WIDTH
SSM_CHUNK = 8
SSM_ROWS = 2048
ATTN_BLOCKS = 4
MASKED = -1e30

F32 = jnp.float32
BF16 = jnp.bfloat16


def _layer_norm(r, g, b):
    mu = jnp.mean(r, axis=-1, keepdims=True)
    c = r - mu
    var = jnp.mean(c * c, axis=-1, keepdims=True)
    return c * lax.rsqrt(var + LN_EPS) * g + b


def _rms_norm(y, g):
    return y * lax.rsqrt(jnp.mean(y * y, axis=-1, keepdims=True) + RMS_EPS) * g


def _resident(shape, n_grid=1):
    zeros = (0,) * len(shape)
    if n_grid == 1:
        return pl.BlockSpec(shape, lambda i: zeros, pipeline_mode=pl.Buffered(1))
    return pl.BlockSpec(shape, lambda i, j: zeros, pipeline_mode=pl.Buffered(1))


def _rows_kernel(*refs, has_mix, has_proj):
    it = iter(refs)
    x_ref = next(it)
    if has_mix:
        ssm_ref, att_ref, wo_s_ref, wo_a_ref, gm_ref, bm_ref = (next(it) for _ in range(6))
    wgu_ref, wdn_ref, g_ref, b_ref = (next(it) for _ in range(4))
    if has_proj:
        wp_ref = next(it)
    o_ref = next(it)
    if has_proj:
        u_ref, q_ref, kv_ref = (next(it) for _ in range(3))

    x = x_ref[...]
    if has_mix:
        ssm = jnp.concatenate([ssm_ref[j] for j in range(N_SLAB)], axis=1).astype(BF16)
        att = jnp.concatenate([att_ref[j] for j in range(Q_PER_KV)], axis=1).astype(BF16)
        m = jnp.dot(ssm, wo_s_ref[...], preferred_element_type=F32)
        m = m + jnp.dot(att, wo_a_ref[...], preferred_element_type=F32)
        x = _layer_norm(ALPHA * x + m, gm_ref[...], bm_ref[...])
    xb = x.astype(BF16)
    gu = jnp.dot(xb, wgu_ref[...], preferred_element_type=F32)
    gate = gu[:, :D_FF]
    up = gu[:, D_FF:]
    h = (gate * jax.nn.sigmoid(gate) * up).astype(BF16)
    y = jnp.dot(h, wdn_ref[...], preferred_element_type=F32)
    x = _layer_norm(ALPHA * x + 0.5 * y, g_ref[...], b_ref[...])
    o_ref[...] = x
    if has_proj:
        z = jnp.dot(x.astype(BF16), wp_ref[...], preferred_element_type=F32)
        for j in range(N_SLAB):
            u_ref[j] = z[:, j * LANES:(j + 1) * LANES]
        for hd in range(N_HEADS):
            q_ref[hd] = z[:, SSM_WIDTH + hd * LANES:SSM_WIDTH + (hd + 1) * LANES].astype(q_ref.dtype)
        kv_ref[...] = z[:, SSM_WIDTH + Q_EXP:]


def _rows_call(x, ffn, ln, mix=None, proj=None, *, tm, q_dtype=BF16, name):
    M = x.shape[0]
    assert M % tm == 0
    row = lambda w: pl.BlockSpec((tm, w), lambda i: (i, 0))
    slab = lambda n: pl.BlockSpec((n, tm, LANES), lambda i: (0, i, 0))
    args = [x]
    specs = [row(D_MODEL)]
    if mix is not None:
        ssm_n, att_n, wo_s, wo_a, gm, bm = mix
        args += [ssm_n, att_n, wo_s, wo_a, gm, bm]
        specs += [slab(N_SLAB), slab(Q_PER_KV), _resident(wo_s.shape), _resident(wo_a.shape),
                  _resident(gm.shape), _resident(bm.shape)]
    args += [ffn[0], ffn[1], ln[0], ln[1]]
    specs += [_resident(ffn[0].shape), _resident(ffn[1].shape), _resident(ln[0].shape), _resident(ln[1].shape)]
    out_shape = [jax.ShapeDtypeStruct((M, D_MODEL), F32)]
    out_specs = [row(D_MODEL)]
    if proj is not None:
        args.append(proj)
        specs.append(_resident(proj.shape))
        out_shape += [jax.ShapeDtypeStruct((N_SLAB, M, LANES), F32),
                      jax.ShapeDtypeStruct((N_HEADS, M, LANES), q_dtype),
                      jax.ShapeDtypeStruct((M, 2 * KV_WIDTH), F32)]
        out_specs += [slab(N_SLAB), slab(N_HEADS), row(2 * KV_WIDTH)]
    return pl.pallas_call(
        functools.partial(_rows_kernel, has_mix=mix is not None, has_proj=proj is not None),
        grid=(M // tm,),
        in_specs=specs,
        out_specs=out_specs,
        out_shape=out_shape,
        compiler_params=pltpu.CompilerParams(dimension_semantics=("arbitrary",),
                                             vmem_limit_bytes=VMEM_LIMIT_BYTES),
        name=name,
    )(*args)


def _cmul(xr, xi, yr, yi):
    return xr * yr - xi * yi, xr * yi + xi * yr


def _ssm_tables_kernel(p_ref, af_ref, wst_ref, cat_ref, strip_ref, ak_ref, tr_ref):
    t1 = SSM_CHUNK
    a_re, a_im, lam_re, lam_im, b_re, b_im, c_re, c_im = (p_ref[k] for k in range(8))
    num_re, num_im = a_re - 1.0, a_im
    den = lam_re * lam_re + lam_im * lam_im
    f_re = (num_re * lam_re + num_im * lam_im) / den
    f_im = (num_im * lam_re - num_re * lam_im) / den
    bb_re, bb_im = _cmul(f_re, f_im, b_re, b_im)

    def same_group(shape, row_div, col_div):
        r = lax.broadcasted_iota(jnp.int32, shape, 0) // row_div
        c = lax.broadcasted_iota(jnp.int32, shape, 1) // col_div
        return (r == c).astype(F32)

    m_state = same_group((LANES, SLAB_STATES), SSM_GROUP, SSM_STATE)
    m_chan = 0.5 * same_group((LANES, LANES), SSM_GROUP, SSM_GROUP)

    def expand(x):
        return (jnp.concatenate([x] * (SLAB_STATES // LANES), axis=1) * m_state).astype(BF16)

    def dot_nt(a, b):
        return lax.dot_general(a, b, (((1,), (1,)), ((), ())), precision=lax.Precision.HIGHEST,
                               preferred_element_type=F32)

    pw_re, pw_im = jnp.ones_like(a_re), jnp.zeros_like(a_re)
    kd = []
    for l in range(t1):
        ab_re, ab_im = _cmul(pw_re, pw_im, bb_re, bb_im)
        s = t1 - 1 - l
        kd_l = []
        for j in range(N_SLAB):
            rows = slice(j * LANES, (j + 1) * LANES)
            wst_ref[j, s * LANES:(s + 1) * LANES, :SLAB_STATES] = expand(ab_re[rows])
            wst_ref[j, s * LANES:(s + 1) * LANES, SLAB_STATES:] = expand(ab_im[rows])
            k = dot_nt(ab_re[rows], c_re[rows]) - dot_nt(ab_im[rows], c_im[rows])
            kd_l.append((k * m_chan).astype(BF16))
        kd.append(kd_l)
        pw_re, pw_im = _cmul(pw_re, pw_im, a_re, a_im)
        cf_re = c_re * pw_re - c_im * pw_im
        cf_im = -(c_re * pw_im + c_im * pw_re)
        for j in range(N_SLAB):
            rows = slice(j * LANES, (j + 1) * LANES)
            cat_ref[j, l * LANES:(l + 1) * LANES, :SLAB_STATES] = expand(cf_re[rows])
            cat_ref[j, l * LANES:(l + 1) * LANES, SLAB_STATES:] = expand(cf_im[rows])
    for j in range(N_SLAB):
        for rho in range(t1):
            for c in range(2):
                lag = t1 - 2 - rho + c
                blk = kd[lag][j] if lag >= 0 else jnp.zeros((LANES, LANES), BF16)
                strip_ref[j, rho * LANES:(rho + 1) * LANES, c * LANES:(c + 1) * LANES] = blk

    f_re, f_im = af_ref[0], af_ref[1]
    base_re, base_im = f_re, f_im
    for _ in range(t1 - 1):
        base_re, base_im = _cmul(base_re, base_im, f_re, f_im)
    row = lax.broadcasted_iota(jnp.int32, (SUBLANES, N_STATES), 0)
    pws = [(base_re, base_im)]
    for _ in range(SUBLANES - 1):
        pws.append(_cmul(pws[-1][0], pws[-1][1], base_re, base_im))
    for n, k in enumerate((1, 2, 4)):
        for part in range(2):
            ak_ref[2 * n + part] = jnp.where(row >= k, jnp.broadcast_to(pws[k - 1][part], row.shape), 0.0)
    for part in range(2):
        acc = jnp.zeros(row.shape, F32)
        for r in range(SUBLANES):
            acc = jnp.where(row == r, jnp.broadcast_to(pws[r][part], row.shape), acc)
        tr_ref[part] = acc


def _ssm_tables_call(params, a_flat, *, name):
    t1 = SSM_CHUNK
    return pl.pallas_call(
        _ssm_tables_kernel,
        out_shape=[jax.ShapeDtypeStruct((N_SLAB, t1 * LANES, 2 * SLAB_STATES), BF16),
                   jax.ShapeDtypeStruct((N_SLAB, t1 * LANES, 2 * SLAB_STATES), BF16),
                   jax.ShapeDtypeStruct((N_SLAB, t1 * LANES, 2 * LANES), BF16),
                   jax.ShapeDtypeStruct((6, SUBLANES, N_STATES), F32),
                   jax.ShapeDtypeStruct((2, SUBLANES, N_STATES), F32)],
        compiler_params=pltpu.CompilerParams(vmem_limit_bytes=VMEM_LIMIT_BYTES),
        name=name,
    )(params, a_flat)


def _ssm_table_inputs(lam_re, lam_im, log_dt, b_re, b_im, c_re, c_im):
    dt = jnp.exp(log_dt)[:, None]
    mag = jnp.exp(lam_re * dt)
    a_re = mag * jnp.cos(lam_im * dt)
    a_im = mag * jnp.sin(lam_im * dt)
    rep = lambda x: jnp.repeat(x, SSM_GROUP, axis=0)
    flat = lambda x: x.reshape(SSM_WIDTH, SSM_STATE)
    params = jnp.stack([rep(a_re), rep(a_im), rep(lam_re), rep(lam_im),
                        flat(jnp.swapaxes(b_re, 1, 2)), flat(jnp.swapaxes(b_im, 1, 2)), flat(c_re), flat(c_im)])
    params = jnp.concatenate([params, params], axis=-1)
    a_flat = jnp.stack([a_re.reshape(1, N_STATES), a_im.reshape(1, N_STATES)])
    return params, a_flat


def _glu_rms(y, gluw_ref, glub_ref, g_ref):
    g = jax.nn.gelu(y).astype(BF16)
    zz = jnp.dot(g, gluw_ref[...], preferred_element_type=F32) + glub_ref[...]
    o = zz[:, :SSM_WIDTH] * jax.nn.sigmoid(zz[:, SSM_WIDTH:])
    return _rms_norm(o, g_ref[...])


def _ssm_kernel(u_ref, wst_ref, cat_ref, strip_ref, ak_ref, tr_ref, d_ref, gluw_ref, glub_ref, g_ref,
                o_ref, hre_ref, him_ref,
                ucb_ref, sre_ref, sim_ref, hcr_ref, hci_ref):
    t1 = SSM_CHUNK
    nc = SSM_ROWS // t1
    i = pl.program_id(1)

    @pl.when(i == 0)
    def _():
        hcr_ref[...] = jnp.zeros_like(hcr_ref)
        hci_ref[...] = jnp.zeros_like(hci_ref)

    for j in range(N_SLAB):
        for s in range(t1):
            ucb_ref[j, :, s * LANES:(s + 1) * LANES] = u_ref[j, pl.ds(s, nc, stride=t1), :].astype(BF16)

    for j in range(N_SLAB):
        st = jnp.dot(ucb_ref[j], wst_ref[j], preferred_element_type=F32)
        sre_ref[:, j * SLAB_STATES:(j + 1) * SLAB_STATES] = st[:, :SLAB_STATES]
        sim_ref[:, j * SLAB_STATES:(j + 1) * SLAB_STATES] = st[:, SLAB_STATES:]

    row = lax.broadcasted_iota(jnp.int32, (SUBLANES, SLAB_STATES), 0)
    for j in range(N_SLAB):
        cols = pl.ds(j * SLAB_STATES, SLAB_STATES)

        def body(gi, carry, cols=cols):
            hr, hi = carry
            r0 = pl.multiple_of(gi * SUBLANES, SUBLANES)
            xr = sre_ref[pl.ds(r0, SUBLANES), cols]
            xi = sim_ref[pl.ds(r0, SUBLANES), cols]
            for n, k in enumerate((1, 2, 4)):
                akr = ak_ref[2 * n, :, cols]
                aki = ak_ref[2 * n + 1, :, cols]
                sr = pltpu.roll(xr, k, 0)
                si = pltpu.roll(xi, k, 0)
                xr, xi = xr + akr * sr - aki * si, xi + akr * si + aki * sr
            tr = tr_ref[0, :, cols]
            ti = tr_ref[1, :, cols]
            er = xr + tr * hr - ti * hi
            ei = xi + tr * hi + ti * hr
            sre_ref[pl.ds(r0, SUBLANES), cols] = jnp.where(row >= 1, pltpu.roll(er, 1, 0), hr)
            sim_ref[pl.ds(r0, SUBLANES), cols] = jnp.where(row >= 1, pltpu.roll(ei, 1, 0), hi)
            last = SUBLANES - 1
            return (jnp.broadcast_to(er[last:last + 1], er.shape), jnp.broadcast_to(ei[last:last + 1], ei.shape))

        hr, hi = lax.fori_loop(0, nc // SUBLANES, body, (hcr_ref[:, cols], hci_ref[:, cols]))
        hcr_ref[:, cols] = hr
        hci_ref[:, cols] = hi

    hre_ref[0] = hcr_ref[...]
    him_ref[0] = hci_ref[...]

    ys = []
    for j in range(N_SLAB):
        cols = slice(j * SLAB_STATES, (j + 1) * SLAB_STATES)
        hprev = jnp.concatenate([sre_ref[:, cols], sim_ref[:, cols]], axis=1).astype(BF16)
        y = lax.dot_general(hprev, cat_ref[j], (((1,), (1,)), ((), ())), preferred_element_type=F32)
        parts = []
        for p2 in range(t1 // 2):
            kk = (2 * p2 + 2) * LANES
            yi = jnp.dot(ucb_ref[j, :, :kk], strip_ref[j, (t1 - 2 - 2 * p2) * LANES:, :],
                         preferred_element_type=F32)
            parts.append(y[:, 2 * p2 * LANES:(2 * p2 + 2) * LANES] + yi)
        ys.append(jnp.concatenate(parts, axis=1))

    for t in range(t1):
        y = jnp.concatenate([ys[j][:, t * LANES:(t + 1) * LANES] + d_ref[j] * u_ref[j, pl.ds(t, nc, stride=t1), :]
                             for j in range(N_SLAB)], axis=1)
        n = _glu_rms(y, gluw_ref, glub_ref, g_ref)
        for j in range(N_SLAB):
            o_ref[j, pl.ds(t, nc, stride=t1), :] = n[:, j * LANES:(j + 1) * LANES]


def _ssm_prompt_call(u, tab, d_t, gluw, glub, g, *, batch, seq, name):
    t1 = SSM_CHUNK
    nc = SSM_ROWS // t1
    steps = seq // SSM_ROWS
    blk = pl.BlockSpec((N_SLAB, SSM_ROWS, LANES), lambda b, i: (0, b * steps + i, 0))
    st = pl.BlockSpec((1, SUBLANES, N_STATES), lambda b, i: (b, 0, 0))
    consts = list(tab) + [d_t, gluw, glub, g]
    out, hre, him = pl.pallas_call(
        _ssm_kernel,
        grid=(batch, steps),
        in_specs=[blk] + [_resident(c.shape, 2) for c in consts],
        out_specs=[blk, st, st],
        out_shape=[jax.ShapeDtypeStruct(u.shape, F32),
                   jax.ShapeDtypeStruct((batch, SUBLANES, N_STATES), F32),
                   jax.ShapeDtypeStruct((batch, SUBLANES, N_STATES), F32)],
        scratch_shapes=[pltpu.VMEM((N_SLAB, nc, t1 * LANES), BF16),
                        pltpu.VMEM((nc, N_STATES), F32),
                        pltpu.VMEM((nc, N_STATES), F32),
                        pltpu.VMEM((SUBLANES, N_STATES), F32),
                        pltpu.VMEM((SUBLANES, N_STATES), F32)],
        compiler_params=pltpu.CompilerParams(dimension_semantics=("arbitrary", "arbitrary"),
                                             vmem_limit_bytes=VMEM_LIMIT_BYTES),
        name=name,
    )(u, *consts)
    return out, hre[:, 0], him[:, 0]


def _ssm_sample_kernel(u_ref, h0r_ref, h0i_ref, wst_ref, cat_ref, kd_ref, a_ref, d_ref, gluw_ref, glub_ref, g_ref,
                       o_ref, hr_ref, hi_ref):
    ys = []
    for j in range(N_SLAB):
        cols = slice(j * SLAB_STATES, (j + 1) * SLAB_STATES)
        uf = u_ref[j]
        ub = uf.astype(BF16)
        st = jnp.dot(ub, wst_ref[j], preferred_element_type=F32)
        h0r = h0r_ref[:, cols]
        h0i = h0i_ref[:, cols]
        ar = a_ref[0, :, cols]
        ai = a_ref[1, :, cols]
        hr_ref[:, cols] = ar * h0r - ai * h0i + st[:, :SLAB_STATES]
        hi_ref[:, cols] = ar * h0i + ai * h0r + st[:, SLAB_STATES:]
        hcat = jnp.concatenate([h0r, h0i], axis=1).astype(BF16)
        y = lax.dot_general(hcat, cat_ref[j], (((1,), (1,)), ((), ())), preferred_element_type=F32)
        y = y + jnp.dot(ub, kd_ref[j], preferred_element_type=F32)
        ys.append(y + d_ref[j] * uf)
    n = _glu_rms(jnp.concatenate(ys, axis=1), gluw_ref, glub_ref, g_ref)
    for j in range(N_SLAB):
        o_ref[j] = n[:, j * LANES:(j + 1) * LANES]


def _ssm_sample_call(u, h0r, h0i, tab, a_flat, d_t, gluw, glub, g, *, name):
    n_seq = u.shape[1]
    wst, cat, strip = tab[:3]
    last = SSM_CHUNK - 1
    whole = lambda a: pl.BlockSpec(a.shape, lambda i: (0,) * len(a.shape))
    outs = [jax.ShapeDtypeStruct(u.shape, F32),
            jax.ShapeDtypeStruct((n_seq, N_STATES), F32),
            jax.ShapeDtypeStruct((n_seq, N_STATES), F32)]
    return pl.pallas_call(
        _ssm_sample_kernel,
        grid=(1,),
        in_specs=[whole(u), whole(h0r), whole(h0i),
                  pl.BlockSpec((N_SLAB, LANES, 2 * SLAB_STATES), lambda i: (0, last, 0)),
                  pl.BlockSpec((N_SLAB, LANES, 2 * SLAB_STATES), lambda i: (0, 0, 0)),
                  pl.BlockSpec((N_SLAB, LANES, LANES), lambda i: (0, last, 1)),
                  whole(a_flat), whole(d_t), whole(gluw), whole(glub), whole(g)],
        out_specs=[whole(o) for o in outs],
        out_shape=outs,
        compiler_params=pltpu.CompilerParams(dimension_semantics=("arbitrary",),
                                             vmem_limit_bytes=VMEM_LIMIT_BYTES),
        name=name,
    )(u, h0r, h0i, wst, cat, strip, a_flat, d_t, gluw, glub, g)


def _attn_kernel(sink_ref, q_ref, kvc_ref, kvp_ref, g_ref, o_ref):
    i = pl.program_id(1)
    kj = lax.broadcasted_iota(jnp.int32, (2 * WINDOW, WINDOW), 0)
    qi = lax.broadcasted_iota(jnp.int32, (2 * WINDOW, WINDOW), 1)
    band = (kj >= qi) & (kj <= qi + WINDOW)
    for sb in range(ATTN_BLOCKS):
        blk = slice(sb * WINDOW, (sb + 1) * WINDOW)
        kc = kvc_ref[blk, :]
        if sb == 0:
            kp = kvp_ref[...]
            valid = band & ((i > 0) | (kj >= WINDOW))
        else:
            kp = kvc_ref[(sb - 1) * WINDOW:sb * WINDOW, :]
            valid = band
        kcat = (jnp.concatenate([kp[:, :KV_WIDTH], kc[:, :KV_WIDTH]], axis=0) * ATTN_SCALE).astype(BF16)
        v_t = jnp.concatenate([kp[:, KV_WIDTH:], kc[:, KV_WIDTH:]], axis=0).T.astype(BF16)
        outs = []
        for h in range(N_HEADS):
            kvh = h // Q_PER_KV
            s_t = lax.dot_general(kcat, q_ref[h, blk, :], (((1,), (1,)), ((), ())),
                                  preferred_element_type=F32)
            s_t = jnp.where(valid, s_t, MASKED)
            sink = sink_ref[h]
            m = jnp.maximum(jnp.max(s_t, axis=0, keepdims=True), sink)
            p_t = jnp.exp(s_t - m)
            den = jnp.sum(p_t, axis=0, keepdims=True) + jnp.exp(sink - m)
            o_t = jnp.dot(v_t[kvh * HEAD_DIM:(kvh + 1) * HEAD_DIM], p_t.astype(BF16), preferred_element_type=F32)
            outs.append(o_t / den)
        tiles = [jnp.concatenate([outs[pr], outs[pr + Q_PER_KV]], axis=0) for pr in range(Q_PER_KV)]
        ss = None
        for tile in tiles:
            t = jnp.sum(tile * tile, axis=0, keepdims=True)
            ss = t if ss is None else ss + t
        inv = lax.rsqrt(ss / ATTN_WIDTH + RMS_EPS)
        for pr, tile in enumerate(tiles):
            o_ref[pr, blk, :] = ((tile * inv).T * g_ref[pr]).astype(o_ref.dtype)


def _attn_prompt_call(q, kv, sinks, g_perm, *, batch, seq, name):
    rows = ATTN_BLOCKS * WINDOW
    nb = seq // rows
    M = kv.shape[0]
    prev = lambda b, i: ((b * nb + i) * ATTN_BLOCKS - jnp.minimum(i, 1), 0)
    return pl.pallas_call(
        _attn_kernel,
        grid=(batch, nb),
        in_specs=[pl.BlockSpec(memory_space=pltpu.SMEM),
                  pl.BlockSpec((N_HEADS, rows, LANES), lambda b, i: (0, b * nb + i, 0)),
                  pl.BlockSpec((rows, 2 * KV_WIDTH), lambda b, i: (b * nb + i, 0)),
                  pl.BlockSpec((WINDOW, 2 * KV_WIDTH), prev),
                  _resident(g_perm.shape, 2)],
        out_specs=pl.BlockSpec((Q_PER_KV, rows, LANES), lambda b, i: (0, b * nb + i, 0)),
        out_shape=jax.ShapeDtypeStruct((Q_PER_KV, M, LANES), BF16),
        compiler_params=pltpu.CompilerParams(dimension_semantics=("arbitrary", "arbitrary"),
                                             vmem_limit_bytes=VMEM_LIMIT_BYTES),
        name=name,
    )(sinks, q, kv, kv, g_perm)


SEQ_PER_STEP = 8


def _attn_sample_kernel(q_ref, kv_ref, ck_ref, cv_ref, sink_ref, g_ref, own_ref, o_ref, ko_ref, vo_ref, *, n_seq):
    i = pl.program_id(0)
    sink = sink_ref[...]
    for t in range(SEQ_PER_STEP):
        n = i * SEQ_PER_STEP + t
        qn = q_ref[pl.ds(n, N_HEADS, stride=n_seq), :].astype(BF16)
        kvn = kv_ref[pl.ds(n, 1), :]
        kn = kvn[:, :KV_WIDTH]
        vn = kvn[:, KV_WIDTH:]
        kc = ck_ref[t]
        vc = cv_ref[t]
        s = lax.dot_general(qn, kc.astype(BF16), (((1,), (1,)), ((), ())), preferred_element_type=F32) * ATTN_SCALE
        s_new = jnp.sum(qn.astype(F32) * kn.astype(BF16).astype(F32), axis=-1, keepdims=True) * ATTN_SCALE
        m = jnp.maximum(jnp.maximum(jnp.max(s, axis=-1, keepdims=True), s_new), sink)
        p = jnp.exp(s - m)
        p_new = jnp.exp(s_new - m)
        den = jnp.sum(p, axis=-1, keepdims=True) + p_new + jnp.exp(sink - m)
        o = jnp.dot(p.astype(BF16), vc.astype(BF16), preferred_element_type=F32)
        o = (o + p_new.astype(BF16).astype(F32) * vn.astype(BF16).astype(F32)) / den
        ss = jnp.sum(jnp.sum(o * o * own_ref[...], axis=-1, keepdims=True), axis=0, keepdims=True)
        nrm = o * lax.rsqrt(ss / ATTN_WIDTH + RMS_EPS) * g_ref[...]
        o_ref[pl.ds(n, Q_PER_KV, stride=n_seq), :] = nrm[:Q_PER_KV] + nrm[Q_PER_KV:]
        ko_ref[t, pl.ds(0, WINDOW - 1), :] = ck_ref[t, pl.ds(1, WINDOW - 1), :]
        ko_ref[t, pl.ds(WINDOW - 1, 1), :] = kn
        vo_ref[t, pl.ds(0, WINDOW - 1), :] = cv_ref[t, pl.ds(1, WINDOW - 1), :]
        vo_ref[t, pl.ds(WINDOW - 1, 1), :] = vn


def _attn_sample_call(q, kv, ck, cv, sink8, g8, own8, *, name):
    n_seq = kv.shape[0]
    assert n_seq % SEQ_PER_STEP == 0
    cblk = pl.BlockSpec((SEQ_PER_STEP, WINDOW, 2 * HEAD_DIM), lambda i: (i, 0, 0))
    whole = lambda a: pl.BlockSpec(a.shape, lambda i: (0,) * len(a.shape))
    o_shape = jax.ShapeDtypeStruct((Q_PER_KV * n_seq, LANES), F32)
    return pl.pallas_call(
        functools.partial(_attn_sample_kernel, n_seq=n_seq),
        grid=(n_seq // SEQ_PER_STEP,),
        in_specs=[whole(q), whole(kv), cblk, cblk, whole(sink8), whole(g8), whole(own8)],
        out_specs=[whole(o_shape), cblk, cblk],
        out_shape=[o_shape,
                   jax.ShapeDtypeStruct(ck.shape, F32),
                   jax.ShapeDtypeStruct(cv.shape, F32)],
        compiler_params=pltpu.CompilerParams(dimension_semantics=("arbitrary",),
                                             vmem_limit_bytes=VMEM_LIMIT_BYTES),
        name=name,
    )(q, kv, ck, cv, sink8, g8, own8)


def _expand_heads(a):
    lead = a.shape[:-1]
    a = a.reshape(lead + (N_KV_HEADS, Q_PER_KV, 1, HEAD_DIM))
    sel = jnp.eye(N_KV_HEADS, dtype=a.dtype).reshape(N_KV_HEADS, 1, N_KV_HEADS, 1)
    return (a * sel).reshape(lead + (Q_EXP,))


def _pair_heads(a):
    lead = a.shape[:-1]
    a = a.reshape(lead + (N_KV_HEADS, Q_PER_KV, HEAD_DIM))
    return jnp.swapaxes(a, -3, -2).reshape(lead + (ATTN_WIDTH,))


def _prep_weights(ln_g, ln_b, ffn1_w_in, ffn1_w_out, ffn2_w_in, ffn2_w_out, w_in, ssm_lam_re, ssm_lam_im, ssm_log_dt,
                  ssm_b_re, ssm_b_im, ssm_c_re, ssm_c_im, ssm_d, glu_w, glu_b, attn_sinks, g_ssm_out, g_attn_out,
                  w_out):
    w = dict(ffn1=[], ffn2=[], ln=[], w_in=[], wo_s=[], wo_a=[], tab=[], a_flat=[], d_1=[],
             gluw=[], glub=[], g_ssm=[], sinks=[], g_pair=[], sink8=[], g_att=[], own=None)
    own = _expand_heads(jnp.ones((ATTN_WIDTH,), F32)).reshape(N_HEADS, 1, LANES)
    w['own'] = own
    for l in range(DEPTH):
        w['ffn1'].append((ffn1_w_in[l].astype(BF16), ffn1_w_out[l].astype(BF16)))
        w['ffn2'].append((ffn2_w_in[l].astype(BF16), ffn2_w_out[l].astype(BF16)))
        w['ln'].append([(ln_g[l, i][None, :], ln_b[l, i][None, :]) for i in range(3)])
        o1, o2 = SSM_WIDTH, SSM_WIDTH + ATTN_WIDTH
        w['w_in'].append(jnp.concatenate([w_in[l][:, :o1], _expand_heads(w_in[l][:, o1:o2]), w_in[l][:, o2:]],
                                         axis=1).astype(BF16))
        w['wo_s'].append(w_out[l, :SSM_WIDTH].astype(BF16))
        w['wo_a'].append(_pair_heads(w_out[l, SSM_WIDTH:].T).T.astype(BF16))
        params, a_flat = _ssm_table_inputs(ssm_lam_re[l], ssm_lam_im[l], ssm_log_dt[l], ssm_b_re[l], ssm_b_im[l],
                                           ssm_c_re[l], ssm_c_im[l])
        w['tab'].append(_ssm_tables_call(params, a_flat, name=f"ssm_tables_{l}"))
        w['a_flat'].append(a_flat)
        d_slab = ssm_d[l].reshape(N_SLAB, 1, LANES)
        w['d_1'].append(d_slab)
        w['gluw'].append(glu_w[l].astype(BF16))
        w['glub'].append(glu_b[l][None, :])
        w['g_ssm'].append(g_ssm_out[l][None, :])
        w['sink8'].append(attn_sinks[l][:, None])
        w['sinks'].append(attn_sinks[l])
        w['g_pair'].append(_pair_heads(g_attn_out[l]).reshape(Q_PER_KV, 1, LANES))
        w['g_att'].append(_expand_heads(g_attn_out[l]).reshape(N_HEADS, 1, LANES))
    return w


def _prompt_mixer(u, q, kv, l, w, batch, seq):
    ssm_n, hre, him = _ssm_prompt_call(u, w['tab'][l], w['d_1'][l], w['gluw'][l],
                                       w['glub'][l], w['g_ssm'][l], batch=batch, seq=seq, name=f"p_ssm_{l}")
    att_n = _attn_prompt_call(q, kv, w['sinks'][l], w['g_pair'][l], batch=batch, seq=seq, name=f"p_attn_{l}")
    kvw = kv.reshape(batch, seq, 2, N_KV_HEADS, HEAD_DIM)[:, -WINDOW:]
    return (ssm_n, att_n, hre.reshape(batch, N_SSM_GROUPS, SSM_STATE), him.reshape(batch, N_SSM_GROUPS, SSM_STATE),
            kvw[:, :, 0], kvw[:, :, 1])


def _sample_mixer(u, q, kv, l, w, h0_re, h0_im, k_buf, v_buf):
    n_seq = kv.shape[0]
    ssm_n, hre, him = _ssm_sample_call(u, h0_re.reshape(n_seq, N_STATES), h0_im.reshape(n_seq, N_STATES),
                                       w['tab'][l], w['a_flat'][l], w['d_1'][l], w['gluw'][l], w['glub'][l], w['g_ssm'][l],
                                       name=f"s_ssm_{l}")
    att, kn, vn = _attn_sample_call(q.reshape(N_HEADS * n_seq, LANES), kv,
                                    k_buf.reshape(n_seq, WINDOW, 2 * HEAD_DIM), v_buf.reshape(n_seq, WINDOW, 2 * HEAD_DIM),
                                    w['sink8'][l], w['g_att'][l].reshape(N_HEADS, LANES),
                                    w['own'].reshape(N_HEADS, LANES), name=f"s_attn_{l}")
    return (ssm_n, att.reshape(Q_PER_KV, n_seq, LANES), hre.reshape(n_seq, N_SSM_GROUPS, SSM_STATE),
            him.reshape(n_seq, N_SSM_GROUPS, SSM_STATE), kn.reshape(k_buf.shape), vn.reshape(v_buf.shape))


def _trunk(x, h0_re, h0_im, k_buf, v_buf, w, *, tm, tag):
    Bn, L, _ = x.shape
    sample = k_buf is not None
    q_dtype = F32 if sample else BF16
    x = x.reshape(Bn * L, D_MODEL)
    ks, vs, hrs, his = [], [], [], []
    mix = None
    for l in range(DEPTH):
        if l > 0:
            x = _rows_call(x, w['ffn2'][l - 1], w['ln'][l - 1][2], mix=mix, tm=tm, name=f"{tag}_mix_ffn2_{l - 1}")[0]
        x, u, q, kv = _rows_call(x, w['ffn1'][l], w['ln'][l][0], proj=w['w_in'][l], tm=tm, q_dtype=q_dtype,
                                 name=f"{tag}_ffn1_{l}")
        if sample:
            ssm_n, att_n, hre, him, kn, vn = _sample_mixer(u, q, kv, l, w, h0_re[l], h0_im[l], k_buf[l], v_buf[l])
        else:
            ssm_n, att_n, hre, him, kn, vn = _prompt_mixer(u, q, kv, l, w, Bn, L)
        mix = (ssm_n, att_n, w['wo_s'][l], w['wo_a'][l], w['ln'][l][1][0], w['ln'][l][1][1])
        ks.append(kn); vs.append(vn); hrs.append(hre); his.append(him)
    x = _rows_call(x, w['ffn2'][DEPTH - 1], w['ln'][DEPTH - 1][2], mix=mix, tm=tm, name=f"{tag}_mix_ffn2_{DEPTH - 1}")[0]
    return x.reshape(Bn, L, D_MODEL), jnp.stack(ks), jnp.stack(vs), jnp.stack(hrs), jnp.stack(his)


def kernel(x_prompt, x_sample, cache_k_win, cache_v_win, state_ssm_re, state_ssm_im, ln_g, ln_b, ffn1_w_in, ffn1_w_out, ffn2_w_in, ffn2_w_out, w_in, ssm_lam_re, ssm_lam_im, ssm_log_dt, ssm_b_re, ssm_b_im, ssm_c_re, ssm_c_im, ssm_d, glu_w, glu_b, attn_sinks, g_ssm_out, g_attn_out, w_out):
    w = _prep_weights(ln_g, ln_b, ffn1_w_in, ffn1_w_out, ffn2_w_in, ffn2_w_out, w_in, ssm_lam_re, ssm_lam_im,
                      ssm_log_dt, ssm_b_re, ssm_b_im, ssm_c_re, ssm_c_im, ssm_d, glu_w, glu_b, attn_sinks,
                      g_ssm_out, g_attn_out, w_out)
    y_prompt, kp, vp, hrp, hip = _trunk(x_prompt, None, None, None, None, w, tm=512, tag="p")
    y_sample, ks_, vs_, hrs, his = _trunk(x_sample, state_ssm_re, state_ssm_im, cache_k_win, cache_v_win, w,
                                          tm=128, tag="s")
    return (y_prompt, y_sample, kp, vp, hrp, hip, ks_, vs_, hrs, his)
```

```python
import functools

import jax
import jax.numpy as jnp
from jax import lax
from jax.experimental import pallas as pl
from jax.experimental.pallas import tpu as pltpu

D_MODEL = 1024
DEPTH = 2
SSM_WIDTH = 512
SSM_GROUP = 16
N_SSM_GROUPS = 32
SSM_STATE = 64
N_STATES = N_SSM_GROUPS * SSM_STATE
ATTN_WIDTH = 512
HEAD_DIM = 64
N_HEADS = 8
N_KV_HEADS = 2
Q_PER_KV = 4
KV_WIDTH = 128
WINDOW = 128
ATTN_SCALE = HEAD_DIM ** -0.5
D_FF = 2816
ALPHA = (2.0 * DEPTH) ** 0.25
LN_EPS = 1e-5
RMS_EPS = 1e-6

LANES = 128
SUBLANES = 8
VMEM_LIMIT_BYTES = 56 * 1024 * 1024
N_SLAB = SSM_WIDTH // LANES
GROUPS_PER_SLAB = LANES // SSM_GROUP
SLAB_STATES = GROUPS_PER_SLAB * SSM_STATE
Q_EXP = N_HEADS * LANES
SSM_CHUNK = 8
SSM_ROWS = 2048
ATTN_BLOCKS = 4
MASKED = -1e30

F32 = jnp.float32
BF16 = jnp.bfloat16


def _layer_norm(r, g, b):
    mu = jnp.mean(r, axis=-1, keepdims=True)
    c = r - mu
    var = jnp.mean(c * c, axis=-1, keepdims=True)
    return c * lax.rsqrt(var + LN_EPS) * g + b


def _rms_norm(y, g):
    return y * lax.rsqrt(jnp.mean(y * y, axis=-1, keepdims=True) + RMS_EPS) * g


def _resident(shape, n_grid=1):
    zeros = (0,) * len(shape)
    if n_grid == 1:
        return pl.BlockSpec(shape, lambda i: zeros, pipeline_mode=pl.Buffered(1))
    return pl.BlockSpec(shape, lambda i, j: zeros, pipeline_mode=pl.Buffered(1))


def _layer_resident(stacked, layer):
    return pl.BlockSpec((None,) + stacked.shape[1:], lambda i: (layer, 0, 0), pipeline_mode=pl.Buffered(1))


def _rows_kernel(*refs, has_mix, has_proj):
    it = iter(refs)
    x_ref = next(it)
    if has_mix:
        ssm_ref, att_ref, wo_s_ref, wo_a_ref, gm_ref, bm_ref = (next(it) for _ in range(6))
    wgu_ref, wdn_ref, g_ref, b_ref = (next(it) for _ in range(4))
    if has_proj:
        wp_ref = next(it)
    o_ref = next(it)
    if has_proj:
        u_ref, q_ref, kv_ref = (next(it) for _ in range(3))

    x = x_ref[...]
    if has_mix:
        ssm = jnp.concatenate([ssm_ref[j] for j in range(N_SLAB)], axis=1).astype(BF16)
        att = jnp.concatenate([att_ref[j] for j in range(Q_PER_KV)], axis=1).astype(BF16)
        m = jnp.dot(ssm, wo_s_ref[...], preferred_element_type=F32)
        m = m + jnp.dot(att, wo_a_ref[...], preferred_element_type=F32)
        x = _layer_norm(ALPHA * x + m, gm_ref[...], bm_ref[...])
    xb = x.astype(BF16)
    gu = jnp.dot(xb, wgu_ref[...], preferred_element_type=F32)
    gate = gu[:, :D_FF]
    up = gu[:, D_FF:]
    h = (gate * jax.nn.sigmoid(gate) * up).astype(BF16)
    y = jnp.dot(h, wdn_ref[...], preferred_element_type=F32)
    x = _layer_norm(ALPHA * x + 0.5 * y, g_ref[...], b_ref[...])
    o_ref[...] = x
    if has_proj:
        z = jnp.dot(x.astype(BF16), wp_ref[...], preferred_element_type=F32)
        for j in range(N_SLAB):
            u_ref[j] = z[:, j * LANES:(j + 1) * LANES]
        n_q = q_ref.shape[0]
        for hd in range(n_q):
            q_ref[hd] = z[:, SSM_WIDTH + hd * LANES:SSM_WIDTH + (hd + 1) * LANES].astype(q_ref.dtype)
        kv_ref[...] = z[:, SSM_WIDTH + n_q * LANES:]


def _rows_call(x, ffn, ln, mix=None, proj=None, *, tm, q_dtype=BF16, name):
    M = x.shape[0]
    assert M % tm == 0
    row = lambda w: pl.BlockSpec((tm, w), lambda i: (i, 0))
    slab = lambda n: pl.BlockSpec((n, tm, LANES), lambda i: (0, i, 0))
    args = [x]
    specs = [row(D_MODEL)]
    if mix is not None:
        ssm_n, att_n, wo_s, wo_a, gm, bm = mix
        args += [ssm_n, att_n, wo_s, wo_a, gm, bm]
        specs += [slab(N_SLAB), slab(Q_PER_KV), _resident(wo_s.shape), _resident(wo_a.shape),
                  _resident(gm.shape), _resident(bm.shape)]
    wgu_all, wdn_all, layer = ffn
    args += [wgu_all, wdn_all, ln[0], ln[1]]
    specs += [_layer_resident(wgu_all, layer), _layer_resident(wdn_all, layer),
              _resident(ln[0].shape), _resident(ln[1].shape)]
    out_shape = [jax.ShapeDtypeStruct((M, D_MODEL), F32)]
    out_specs = [row(D_MODEL)]
    if proj is not None:
        args.append(proj)
        specs.append(_resident(proj.shape))
        n_q = (proj.shape[1] - SSM_WIDTH - 2 * KV_WIDTH) // LANES
        out_shape += [jax.ShapeDtypeStruct((N_SLAB, M, LANES), F32),
                      jax.ShapeDtypeStruct((n_q, M, LANES), q_dtype),
                      jax.ShapeDtypeStruct((M, 2 * KV_WIDTH), F32)]
        out_specs += [slab(N_SLAB), slab(n_q), row(2 * KV_WIDTH)]
    return pl.pallas_call(
        functools.partial(_rows_kernel, has_mix=mix is not None, has_proj=proj is not None),
        grid=(M // tm,),
        in_specs=specs,
        out_specs=out_specs,
        out_shape=out_shape,
        compiler_params=pltpu.CompilerParams(dimension_semantics=("arbitrary",),
                                             vmem_limit_bytes=VMEM_LIMIT_BYTES),
        name=name,
    )(*args)


def _cmul(xr, xi, yr, yi):
    return xr * yr - xi * yi, xr * yi + xi * yr


def _ssm_tables_kernel(p_ref, af_ref, wst_ref, cat_ref, strip_ref, ak_ref, tr_ref):
    t1 = SSM_CHUNK
    a_re, a_im, lam_re, lam_im, b_re, b_im, c_re, c_im = (p_ref[k] for k in range(8))
    num_re, num_im = a_re - 1.0, a_im
    den = lam_re * lam_re + lam_im * lam_im
    f_re = (num_re * lam_re + num_im * lam_im) / den
    f_im = (num_im * lam_re - num_re * lam_im) / den
    bb_re, bb_im = _cmul(f_re, f_im, b_re, b_im)

    def same_group(shape, row_div, col_div):
        r = lax.broadcasted_iota(jnp.int32, shape, 0) // row_div
        c = lax.broadcasted_iota(jnp.int32, shape, 1) // col_div
        return (r == c).astype(F32)

    m_state = same_group((LANES, SLAB_STATES), SSM_GROUP, SSM_STATE)
    m_chan = 0.5 * same_group((LANES, LANES), SSM_GROUP, SSM_GROUP)

    def expand(x):
        return (jnp.concatenate([x] * (SLAB_STATES // LANES), axis=1) * m_state).astype(BF16)

    def dot_nt(a, b):
        return lax.dot_general(a, b, (((1,), (1,)), ((), ())), precision=lax.Precision.HIGHEST,
                               preferred_element_type=F32)

    pw_re, pw_im = jnp.ones_like(a_re), jnp.zeros_like(a_re)
    kd = []
    for l in range(t1):
        ab_re, ab_im = _cmul(pw_re, pw_im, bb_re, bb_im)
        s = t1 - 1 - l
        kd_l = []
        for j in range(N_SLAB):
            rows = slice(j * LANES, (j + 1) * LANES)
            wst_ref[j, s * LANES:(s + 1) * LANES, :SLAB_STATES] = expand(ab_re[rows])
            wst_ref[j, s * LANES:(s + 1) * LANES, SLAB_STATES:] = expand(ab_im[rows])
            k = dot_nt(ab_re[rows], c_re[rows]) - dot_nt(ab_im[rows], c_im[rows])
            kd_l.append((k * m_chan).astype(BF16))
        kd.append(kd_l)
        pw_re, pw_im = _cmul(pw_re, pw_im, a_re, a_im)
        cf_re = c_re * pw_re - c_im * pw_im
        cf_im = -(c_re * pw_im + c_im * pw_re)
        for j in range(N_SLAB):
            rows = slice(j * LANES, (j + 1) * LANES)
            cat_ref[j, l * LANES:(l + 1) * LANES, :SLAB_STATES] = expand(cf_re[rows])
            cat_ref[j, l * LANES:(l + 1) * LANES, SLAB_STATES:] = expand(cf_im[rows])
    for j in range(N_SLAB):
        for rho in range(t1):
            for c in range(2):
                lag = t1 - 2 - rho + c
                blk = kd[lag][j] if lag >= 0 else jnp.zeros((LANES, LANES), BF16)
                strip_ref[j, rho * LANES:(rho + 1) * LANES, c * LANES:(c + 1) * LANES] = blk

    f_re, f_im = af_ref[0], af_ref[1]
    base_re, base_im = f_re, f_im
    for _ in range(t1 - 1):
        base_re, base_im = _cmul(base_re, base_im, f_re, f_im)
    row = lax.broadcasted_iota(jnp.int32, (SUBLANES, N_STATES), 0)
    pws = [(base_re, base_im)]
    for _ in range(SUBLANES - 1):
        pws.append(_cmul(pws[-1][0], pws[-1][1], base_re, base_im))
    for n, k in enumerate((1, 2, 4)):
        for part in range(2):
            ak_ref[2 * n + part] = jnp.where(row >= k, jnp.broadcast_to(pws[k - 1][part], row.shape), 0.0)
    for part in range(2):
        acc = jnp.zeros(row.shape, F32)
        for r in range(SUBLANES):
            acc = jnp.where(row == r, jnp.broadcast_to(pws[r][part], row.shape), acc)
        tr_ref[part] = acc


def _ssm_tables_call(params, a_flat, *, name):
    t1 = SSM_CHUNK
    return pl.pallas_call(
        _ssm_tables_kernel,
        out_shape=[jax.ShapeDtypeStruct((N_SLAB, t1 * LANES, 2 * SLAB_STATES), BF16),
                   jax.ShapeDtypeStruct((N_SLAB, t1 * LANES, 2 * SLAB_STATES), BF16),
                   jax.ShapeDtypeStruct((N_SLAB, t1 * LANES, 2 * LANES), BF16),
                   jax.ShapeDtypeStruct((6, SUBLANES, N_STATES), F32),
                   jax.ShapeDtypeStruct((2, SUBLANES, N_STATES), F32)],
        compiler_params=pltpu.CompilerParams(vmem_limit_bytes=VMEM_LIMIT_BYTES),
        name=name,
    )(params, a_flat)


def _ssm_table_inputs(lam_re, lam_im, log_dt, b_re, b_im, c_re, c_im):
    dt = jnp.exp(log_dt)[:, None]
    mag = jnp.exp(lam_re * dt)
    a_re = mag * jnp.cos(lam_im * dt)
    a_im = mag * jnp.sin(lam_im * dt)
    rep = lambda x: jnp.repeat(x, SSM_GROUP, axis=0)
    flat = lambda x: x.reshape(SSM_WIDTH, SSM_STATE)
    params = jnp.stack([rep(a_re), rep(a_im), rep(lam_re), rep(lam_im),
                        flat(jnp.swapaxes(b_re, 1, 2)), flat(jnp.swapaxes(b_im, 1, 2)), flat(c_re), flat(c_im)])
    params = jnp.concatenate([params, params], axis=-1)
    a_flat = jnp.stack([a_re.reshape(1, N_STATES), a_im.reshape(1, N_STATES)])
    return params, a_flat


def _glu_rms(y, gluw_ref, glub_ref, g_ref):
    g = jax.nn.gelu(y).astype(BF16)
    zz = jnp.dot(g, gluw_ref[...], preferred_element_type=F32) + glub_ref[...]
    o = zz[:, :SSM_WIDTH] * jax.nn.sigmoid(zz[:, SSM_WIDTH:])
    return _rms_norm(o, g_ref[...])


def _ssm_kernel(u_ref, wst_ref, cat_ref, strip_ref, ak_ref, tr_ref, d_ref, gluw_ref, glub_ref, g_ref,
                o_ref, hre_ref, him_ref,
                ucb_ref, sre_ref, sim_ref, hcr_ref, hci_ref):
    t1 = SSM_CHUNK
    nc = SSM_ROWS // t1
    i = pl.program_id(1)

    @pl.when(i == 0)
    def _():
        hcr_ref[...] = jnp.zeros_like(hcr_ref)
        hci_ref[...] = jnp.zeros_like(hci_ref)

    for j in range(N_SLAB):
        for s in range(t1):
            ucb_ref[j, :, s * LANES:(s + 1) * LANES] = u_ref[j, pl.ds(s, nc, stride=t1), :].astype(BF16)

    for j in range(N_SLAB):
        st = jnp.dot(ucb_ref[j], wst_ref[j], preferred_element_type=F32)
        sre_ref[:, j * SLAB_STATES:(j + 1) * SLAB_STATES] = st[:, :SLAB_STATES]
        sim_ref[:, j * SLAB_STATES:(j + 1) * SLAB_STATES] = st[:, SLAB_STATES:]

    row = lax.broadcasted_iota(jnp.int32, (SUBLANES, SLAB_STATES), 0)
    for j in range(N_SLAB):
        cols = pl.ds(j * SLAB_STATES, SLAB_STATES)

        def body(gi, carry, cols=cols):
            hr, hi = carry
            r0 = pl.multiple_of(gi * SUBLANES, SUBLANES)
            xr = sre_ref[pl.ds(r0, SUBLANES), cols]
            xi = sim_ref[pl.ds(r0, SUBLANES), cols]
            for n, k in enumerate((1, 2, 4)):
                akr = ak_ref[2 * n, :, cols]
                aki = ak_ref[2 * n + 1, :, cols]
                sr = pltpu.roll(xr, k, 0)
                si = pltpu.roll(xi, k, 0)
                xr, xi = xr + akr * sr - aki * si, xi + akr * si + aki * sr
            tr = tr_ref[0, :, cols]
            ti = tr_ref[1, :, cols]
            er = xr + tr * hr - ti * hi
            ei = xi + tr * hi + ti * hr
            sre_ref[pl.ds(r0, SUBLANES), cols] = jnp.where(row >= 1, pltpu.roll(er, 1, 0), hr)
            sim_ref[pl.ds(r0, SUBLANES), cols] = jnp.where(row >= 1, pltpu.roll(ei, 1, 0), hi)
            last = SUBLANES - 1
            return (jnp.broadcast_to(er[last:last + 1], er.shape), jnp.broadcast_to(ei[last:last + 1], ei.shape))

        hr, hi = lax.fori_loop(0, nc // SUBLANES, body, (hcr_ref[:, cols], hci_ref[:, cols]))
        hcr_ref[:, cols] = hr
        hci_ref[:, cols] = hi

    hre_ref[0] = hcr_ref[...]
    him_ref[0] = hci_ref[...]

    ys = []
    for j in range(N_SLAB):
        cols = slice(j * SLAB_STATES, (j + 1) * SLAB_STATES)
        hprev = jnp.concatenate([sre_ref[:, cols], sim_ref[:, cols]], axis=1).astype(BF16)
        y = lax.dot_general(hprev, cat_ref[j], (((1,), (1,)), ((), ())), preferred_element_type=F32)
        parts = []
        for p2 in range(t1 // 2):
            kk = (2 * p2 + 2) * LANES
            yi = jnp.dot(ucb_ref[j, :, :kk], strip_ref[j, (t1 - 2 - 2 * p2) * LANES:, :],
                         preferred_element_type=F32)
            parts.append(y[:, 2 * p2 * LANES:(2 * p2 + 2) * LANES] + yi)
        ys.append(jnp.concatenate(parts, axis=1))

    for t in range(t1):
        y = jnp.concatenate([ys[j][:, t * LANES:(t + 1) * LANES] + d_ref[j] * u_ref[j, pl.ds(t, nc, stride=t1), :]
                             for j in range(N_SLAB)], axis=1)
        n = _glu_rms(y, gluw_ref, glub_ref, g_ref)
        for j in range(N_SLAB):
            o_ref[j, pl.ds(t, nc, stride=t1), :] = n[:, j * LANES:(j + 1) * LANES]


def _ssm_prompt_call(u, tab, d_t, gluw, glub, g, *, batch, seq, name):
    t1 = SSM_CHUNK
    nc = SSM_ROWS // t1
    steps = seq // SSM_ROWS
    blk = pl.BlockSpec((N_SLAB, SSM_ROWS, LANES), lambda b, i: (0, b * steps + i, 0))
    st = pl.BlockSpec((1, SUBLANES, N_STATES), lambda b, i: (b, 0, 0))
    consts = list(tab) + [d_t, gluw, glub, g]
    out, hre, him = pl.pallas_call(
        _ssm_kernel,
        grid=(batch, steps),
        in_specs=[blk] + [_resident(c.shape, 2) for c in consts],
        out_specs=[blk, st, st],
        out_shape=[jax.ShapeDtypeStruct(u.shape, F32),
                   jax.ShapeDtypeStruct((batch, SUBLANES, N_STATES), F32),
                   jax.ShapeDtypeStruct((batch, SUBLANES, N_STATES), F32)],
        scratch_shapes=[pltpu.VMEM((N_SLAB, nc, t1 * LANES), BF16),
                        pltpu.VMEM((nc, N_STATES), F32),
                        pltpu.VMEM((nc, N_STATES), F32),
                        pltpu.VMEM((SUBLANES, N_STATES), F32),
                        pltpu.VMEM((SUBLANES, N_STATES), F32)],
        compiler_params=pltpu.CompilerParams(dimension_semantics=("arbitrary", "arbitrary"),
                                             vmem_limit_bytes=VMEM_LIMIT_BYTES),
        name=name,
    )(u, *consts)
    return out, hre[:, 0], him[:, 0]


def _ssm_sample_kernel(u_ref, h0r_ref, h0i_ref, wst_ref, cat_ref, kd_ref, a_ref, d_ref, gluw_ref, glub_ref, g_ref,
                       o_ref, hr_ref, hi_ref):
    ys = []
    for j in range(N_SLAB):
        cols = slice(j * SLAB_STATES, (j + 1) * SLAB_STATES)
        uf = u_ref[j]
        ub = uf.astype(BF16)
        st = jnp.dot(ub, wst_ref[j], preferred_element_type=F32)
        h0r = h0r_ref[:, cols]
        h0i = h0i_ref[:, cols]
        ar = a_ref[0, :, cols]
        ai = a_ref[1, :, cols]
        hr_ref[:, cols] = ar * h0r - ai * h0i + st[:, :SLAB_STATES]
        hi_ref[:, cols] = ar * h0i + ai * h0r + st[:, SLAB_STATES:]
        hcat = jnp.concatenate([h0r, h0i], axis=1).astype(BF16)
        y = lax.dot_general(hcat, cat_ref[j], (((1,), (1,)), ((), ())), preferred_element_type=F32)
        y = y + jnp.dot(ub, kd_ref[j], preferred_element_type=F32)
        ys.append(y + d_ref[j] * uf)
    n = _glu_rms(jnp.concatenate(ys, axis=1), gluw_ref, glub_ref, g_ref)
    for j in range(N_SLAB):
        o_ref[j] = n[:, j * LANES:(j + 1) * LANES]


def _ssm_sample_call(u, h0r, h0i, tab, a_flat, d_t, gluw, glub, g, *, name):
    n_seq = u.shape[1]
    wst, cat, strip = tab[:3]
    last = SSM_CHUNK - 1
    whole = lambda a: pl.BlockSpec(a.shape, lambda i: (0,) * len(a.shape))
    outs = [jax.ShapeDtypeStruct(u.shape, F32),
            jax.ShapeDtypeStruct((n_seq, N_STATES), F32),
            jax.ShapeDtypeStruct((n_seq, N_STATES), F32)]
    return pl.pallas_call(
        _ssm_sample_kernel,
        grid=(1,),
        in_specs=[whole(u), whole(h0r), whole(h0i),
                  pl.BlockSpec((N_SLAB, LANES, 2 * SLAB_STATES), lambda i: (0, last, 0)),
                  pl.BlockSpec((N_SLAB, LANES, 2 * SLAB_STATES), lambda i: (0, 0, 0)),
                  pl.BlockSpec((N_SLAB, LANES, LANES), lambda i: (0, last, 1)),
                  whole(a_flat), whole(d_t), whole(gluw), whole(glub), whole(g)],
        out_specs=[whole(o) for o in outs],
        out_shape=outs,
        compiler_params=pltpu.CompilerParams(dimension_semantics=("arbitrary",),
                                             vmem_limit_bytes=VMEM_LIMIT_BYTES),
        name=name,
    )(u, h0r, h0i, wst, cat, strip, a_flat, d_t, gluw, glub, g)


def _attn_kernel(sink_ref, q_ref, kvc_ref, kvp_ref, g_ref, o_ref):
    i = pl.program_id(1)
    kj = lax.broadcasted_iota(jnp.int32, (2 * WINDOW, WINDOW), 0)
    qi = lax.broadcasted_iota(jnp.int32, (2 * WINDOW, WINDOW), 1)
    band = (kj >= qi) & (kj <= qi + WINDOW)
    low = qi < HEAD_DIM
    for sb in range(ATTN_BLOCKS):
        blk = slice(sb * WINDOW, (sb + 1) * WINDOW)
        kc = kvc_ref[blk, :]
        if sb == 0:
            kp = kvp_ref[...]
            valid = band & ((i > 0) | (kj >= WINDOW))
        else:
            kp = kvc_ref[(sb - 1) * WINDOW:sb * WINDOW, :]
            valid = band
        kcat = jnp.concatenate([kp[:, :KV_WIDTH], kc[:, :KV_WIDTH]], axis=0) * ATTN_SCALE
        swap = pltpu.roll(kcat, HEAD_DIM, 1)
        zero = jnp.zeros_like(kcat)
        k_sel = [[jnp.where(low, kcat, zero).astype(BF16), jnp.where(low, zero, swap).astype(BF16)],
                 [jnp.where(low, swap, zero).astype(BF16), jnp.where(low, zero, kcat).astype(BF16)]]
        v_t = jnp.concatenate([kp[:, KV_WIDTH:], kc[:, KV_WIDTH:]], axis=0).T.astype(BF16)
        outs = []
        for h in range(N_HEADS):
            kvh = h // Q_PER_KV
            s_t = lax.dot_general(k_sel[kvh][h % 2], q_ref[h // 2, blk, :], (((1,), (1,)), ((), ())),
                                  preferred_element_type=F32)
            s_t = jnp.where(valid, s_t, MASKED)
            sink = sink_ref[h]
            m = jnp.maximum(jnp.max(s_t, axis=0, keepdims=True), sink)
            p_t = jnp.exp(s_t - m)
            den = jnp.sum(p_t, axis=0, keepdims=True) + jnp.exp(sink - m)
            o_t = jnp.dot(v_t[kvh * HEAD_DIM:(kvh + 1) * HEAD_DIM], p_t.astype(BF16), preferred_element_type=F32)
            outs.append(o_t / den)
        tiles = [jnp.concatenate([outs[pr], outs[pr + Q_PER_KV]], axis=0) for pr in range(Q_PER_KV)]
        ss = None
        for tile in tiles:
            t = jnp.sum(tile * tile, axis=0, keepdims=True)
            ss = t if ss is None else ss + t
        inv = lax.rsqrt(ss / ATTN_WIDTH + RMS_EPS)
        for pr, tile in enumerate(tiles):
            o_ref[pr, blk, :] = ((tile * inv).T * g_ref[pr]).astype(o_ref.dtype)


def _attn_prompt_call(q, kv, sinks, g_perm, *, batch, seq, name):
    rows = ATTN_BLOCKS * WINDOW
    nb = seq // rows
    M = kv.shape[0]
    prev = lambda b, i: ((b * nb + i) * ATTN_BLOCKS - jnp.minimum(i, 1), 0)
    return pl.pallas_call(
        _attn_kernel,
        grid=(batch, nb),
        in_specs=[pl.BlockSpec(memory_space=pltpu.SMEM),
                  pl.BlockSpec((Q_PER_KV, rows, LANES), lambda b, i: (0, b * nb + i, 0)),
                  pl.BlockSpec((rows, 2 * KV_WIDTH), lambda b, i: (b * nb + i, 0)),
                  pl.BlockSpec((WINDOW, 2 * KV_WIDTH), prev),
                  _resident(g_perm.shape, 2)],
        out_specs=pl.BlockSpec((Q_PER_KV, rows, LANES), lambda b, i: (0, b * nb + i, 0)),
        out_shape=jax.ShapeDtypeStruct((Q_PER_KV, M, LANES), BF16),
        compiler_params=pltpu.CompilerParams(dimension_semantics=("arbitrary", "arbitrary"),
                                             vmem_limit_bytes=VMEM_LIMIT_BYTES),
        name=name,
    )(sinks, q, kv, kv, g_perm)


SEQ_PER_STEP = 8


def _attn_sample_kernel(q_ref, kv_ref, ck_ref, cv_ref, sink_ref, g_ref, own_ref, o_ref, ko_ref, vo_ref, *, n_seq):
    i = pl.program_id(0)
    sink = sink_ref[...]
    for t in range(SEQ_PER_STEP):
        n = i * SEQ_PER_STEP + t
        qn = q_ref[pl.ds(n, N_HEADS, stride=n_seq), :].astype(BF16)
        kvn = kv_ref[pl.ds(n, 1), :]
        kn = kvn[:, :KV_WIDTH]
        vn = kvn[:, KV_WIDTH:]
        kc = ck_ref[t]
        vc = cv_ref[t]
        s = lax.dot_general(qn, kc.astype(BF16), (((1,), (1,)), ((), ())), preferred_element_type=F32) * ATTN_SCALE
        s_new = jnp.sum(qn.astype(F32) * kn.astype(BF16).astype(F32), axis=-1, keepdims=True) * ATTN_SCALE
        m = jnp.maximum(jnp.maximum(jnp.max(s, axis=-1, keepdims=True), s_new), sink)
        p = jnp.exp(s - m)
        p_new = jnp.exp(s_new - m)
        den = jnp.sum(p, axis=-1, keepdims=True) + p_new + jnp.exp(sink - m)
        o = jnp.dot(p.astype(BF16), vc.astype(BF16), preferred_element_type=F32)
        o = (o + p_new.astype(BF16).astype(F32) * vn.astype(BF16).astype(F32)) / den
        ss = jnp.sum(jnp.sum(o * o * own_ref[...], axis=-1, keepdims=True), axis=0, keepdims=True)
        nrm = o * lax.rsqrt(ss / ATTN_WIDTH + RMS_EPS) * g_ref[...]
        o_ref[pl.ds(n, Q_PER_KV, stride=n_seq), :] = nrm[:Q_PER_KV] + nrm[Q_PER_KV:]
        ko_ref[t, pl.ds(0, WINDOW - 1), :] = ck_ref[t, pl.ds(1, WINDOW - 1), :]
        ko_ref[t, pl.ds(WINDOW - 1, 1), :] = kn
        vo_ref[t, pl.ds(0, WINDOW - 1), :] = cv_ref[t, pl.ds(1, WINDOW - 1), :]
        vo_ref[t, pl.ds(WINDOW - 1, 1), :] = vn


def _attn_sample_call(q, kv, ck, cv, layer, sink8, g8, own8, *, name):
    n_seq = kv.shape[0]
    assert n_seq % SEQ_PER_STEP == 0
    cblk = pl.BlockSpec((SEQ_PER_STEP, WINDOW, 2 * HEAD_DIM), lambda i: (i, 0, 0))
    cin = pl.BlockSpec((None, SEQ_PER_STEP, WINDOW, 2 * HEAD_DIM), lambda i: (layer, i, 0, 0))
    whole = lambda a: pl.BlockSpec(a.shape, lambda i: (0,) * len(a.shape))
    o_shape = jax.ShapeDtypeStruct((Q_PER_KV * n_seq, LANES), F32)
    return pl.pallas_call(
        functools.partial(_attn_sample_kernel, n_seq=n_seq),
        grid=(n_seq // SEQ_PER_STEP,),
        in_specs=[whole(q), whole(kv), cin, cin, whole(sink8), whole(g8), whole(own8)],
        out_specs=[whole(o_shape), cblk, cblk],
        out_shape=[o_shape,
                   jax.ShapeDtypeStruct(ck.shape[1:], F32),
                   jax.ShapeDtypeStruct(cv.shape[1:], F32)],
        compiler_params=pltpu.CompilerParams(dimension_semantics=("arbitrary",),
                                             vmem_limit_bytes=VMEM_LIMIT_BYTES),
        name=name,
    )(q, kv, ck, cv, sink8, g8, own8)


def _expand_heads(a):
    lead = a.shape[:-1]
    a = a.reshape(lead + (N_KV_HEADS, Q_PER_KV, 1, HEAD_DIM))
    sel = jnp.eye(N_KV_HEADS, dtype=a.dtype).reshape(N_KV_HEADS, 1, N_KV_HEADS, 1)
    return (a * sel).reshape(lead + (Q_EXP,))


def _pair_heads(a):
    lead = a.shape[:-1]
    a = a.reshape(lead + (N_KV_HEADS, Q_PER_KV, HEAD_DIM))
    return jnp.swapaxes(a, -3, -2).reshape(lead + (ATTN_WIDTH,))


def _prep_weights(ln_g, ln_b, ffn1_w_in, ffn1_w_out, ffn2_w_in, ffn2_w_out, w_in, ssm_lam_re, ssm_lam_im, ssm_log_dt,
                  ssm_b_re, ssm_b_im, ssm_c_re, ssm_c_im, ssm_d, glu_w, glu_b, attn_sinks, g_ssm_out, g_attn_out,
                  w_out):
    w = dict(ffn1=[], ffn2=[], ln=[], w_in=[], w_in_exp=[], wo_s=[], wo_a=[], tab=[], a_flat=[], d_1=[],
             gluw=[], glub=[], g_ssm=[], sinks=[], g_pair=[], sink8=[], g_att=[], own=None)
    own = _expand_heads(jnp.ones((ATTN_WIDTH,), F32)).reshape(N_HEADS, 1, LANES)
    w['own'] = own
    ffn1_in_b, ffn1_out_b = ffn1_w_in.astype(BF16), ffn1_w_out.astype(BF16)
    ffn2_in_b, ffn2_out_b = ffn2_w_in.astype(BF16), ffn2_w_out.astype(BF16)
    for l in range(DEPTH):
        w['ffn1'].append((ffn1_in_b, ffn1_out_b, l))
        w['ffn2'].append((ffn2_in_b, ffn2_out_b, l))
        w['ln'].append([(ln_g[l, i][None, :], ln_b[l, i][None, :]) for i in range(3)])
        o1, o2 = SSM_WIDTH, SSM_WIDTH + ATTN_WIDTH
        w['w_in'].append(w_in[l].astype(BF16))
        w['w_in_exp'].append(jnp.concatenate([w_in[l][:, :o1], _expand_heads(w_in[l][:, o1:o2]), w_in[l][:, o2:]],
                                             axis=1).astype(BF16))
        w['wo_s'].append(w_out[l, :SSM_WIDTH].astype(BF16))
        w['wo_a'].append(_pair_heads(w_out[l, SSM_WIDTH:].T).T.astype(BF16))
        params, a_flat = _ssm_table_inputs(ssm_lam_re[l], ssm_lam_im[l], ssm_log_dt[l], ssm_b_re[l], ssm_b_im[l],
                                           ssm_c_re[l], ssm_c_im[l])
        w['tab'].append(_ssm_tables_call(params, a_flat, name=f"ssm_tables_{l}"))
        w['a_flat'].append(a_flat)
        d_slab = ssm_d[l].reshape(N_SLAB, 1, LANES)
        w['d_1'].append(d_slab)
        w['gluw'].append(glu_w[l].astype(BF16))
        w['glub'].append(glu_b[l][None, :])
        w['g_ssm'].append(g_ssm_out[l][None, :])
        w['sink8'].append(attn_sinks[l][:, None])
        w['sinks'].append(attn_sinks[l])
        w['g_pair'].append(_pair_heads(g_attn_out[l]).reshape(Q_PER_KV, 1, LANES))
        w['g_att'].append(_expand_heads(g_attn_out[l]).reshape(N_HEADS, 1, LANES))
    return w


def _prompt_mixer(u, q, kv, l, w, batch, seq):
    ssm_n, hre, him = _ssm_prompt_call(u, w['tab'][l], w['d_1'][l], w['gluw'][l],
                                       w['glub'][l], w['g_ssm'][l], batch=batch, seq=seq, name=f"p_ssm_{l}")
    att_n = _attn_prompt_call(q, kv, w['sinks'][l], w['g_pair'][l], batch=batch, seq=seq, name=f"p_attn_{l}")
    kvw = kv.reshape(batch, seq, 2 * KV_WIDTH)[:, -WINDOW:].reshape(batch, WINDOW, 2, N_KV_HEADS, HEAD_DIM)
    return (ssm_n, att_n, hre.reshape(batch, N_SSM_GROUPS, SSM_STATE), him.reshape(batch, N_SSM_GROUPS, SSM_STATE),
            kvw[:, :, 0], kvw[:, :, 1])


def _sample_mixer(u, q, kv, l, w, h0_re, h0_im, k_buf, v_buf):
    n_seq = kv.shape[0]
    ssm_n, hre, him = _ssm_sample_call(u, h0_re.reshape(n_seq, N_STATES), h0_im.reshape(n_seq, N_STATES),
                                       w['tab'][l], w['a_flat'][l], w['d_1'][l], w['gluw'][l], w['glub'][l], w['g_ssm'][l],
                                       name=f"s_ssm_{l}")
    att, kn, vn = _attn_sample_call(q.reshape(N_HEADS * n_seq, LANES), kv, k_buf, v_buf, l,
                                    w['sink8'][l], w['g_att'][l].reshape(N_HEADS, LANES),
                                    w['own'].reshape(N_HEADS, LANES), name=f"s_attn_{l}")
    win = (n_seq, WINDOW, N_KV_HEADS, HEAD_DIM)
    return (ssm_n, att.reshape(Q_PER_KV, n_seq, LANES), hre.reshape(n_seq, N_SSM_GROUPS, SSM_STATE),
            him.reshape(n_seq, N_SSM_GROUPS, SSM_STATE), kn.reshape(win), vn.reshape(win))


def _trunk(x, h0_re, h0_im, k_buf, v_buf, w, *, tm, tag):
    Bn, L, _ = x.shape
    sample = k_buf is not None
    q_dtype = F32 if sample else BF16
    x = x.reshape(Bn * L, D_MODEL)
    ks, vs, hrs, his = [], [], [], []
    mix = None
    for l in range(DEPTH):
        if l > 0:
            x = _rows_call(x, w['ffn2'][l - 1], w['ln'][l - 1][2], mix=mix, tm=tm, name=f"{tag}_mix_ffn2_{l - 1}")[0]
        proj = w['w_in_exp'][l] if sample else w['w_in'][l]
        x, u, q, kv = _rows_call(x, w['ffn1'][l], w['ln'][l][0], proj=proj, tm=tm, q_dtype=q_dtype,
                                 name=f"{tag}_ffn1_{l}")
        if sample:
            ssm_n, att_n, hre, him, kn, vn = _sample_mixer(u, q, kv, l, w, h0_re[l], h0_im[l], k_buf, v_buf)
        else:
            ssm_n, att_n, hre, him, kn, vn = _prompt_mixer(u, q, kv, l, w, Bn, L)
        mix = (ssm_n, att_n, w['wo_s'][l], w['wo_a'][l], w['ln'][l][1][0], w['ln'][l][1][1])
        ks.append(kn); vs.append(vn); hrs.append(hre); his.append(him)
    x = _rows_call(x, w['ffn2'][DEPTH - 1], w['ln'][DEPTH - 1][2], mix=mix, tm=tm, name=f"{tag}_mix_ffn2_{DEPTH - 1}")[0]
    return x.reshape(Bn, L, D_MODEL), jnp.stack(ks), jnp.stack(vs), jnp.stack(hrs), jnp.stack(his)


def kernel(x_prompt, x_sample, cache_k_win, cache_v_win, state_ssm_re, state_ssm_im, ln_g, ln_b, ffn1_w_in, ffn1_w_out, ffn2_w_in, ffn2_w_out, w_in, ssm_lam_re, ssm_lam_im, ssm_log_dt, ssm_b_re, ssm_b_im, ssm_c_re, ssm_c_im, ssm_d, glu_w, glu_b, attn_sinks, g_ssm_out, g_attn_out, w_out):
    w = _prep_weights(ln_g, ln_b, ffn1_w_in, ffn1_w_out, ffn2_w_in, ffn2_w_out, w_in, ssm_lam_re, ssm_lam_im,
                      ssm_log_dt, ssm_b_re, ssm_b_im, ssm_c_re, ssm_c_im, ssm_d, glu_w, glu_b, attn_sinks,
                      g_ssm_out, g_attn_out, w_out)
    y_prompt, kp, vp, hrp, hip = _trunk(x_prompt, None, None, None, None, w, tm=512, tag="p")
    n_seq = x_sample.shape[0]
    merged = (DEPTH, n_seq, WINDOW, N_KV_HEADS * HEAD_DIM)
    y_sample, ks_, vs_, hrs, his = _trunk(x_sample, state_ssm_re, state_ssm_im, cache_k_win.reshape(merged),
                                          cache_v_win.reshape(merged), w, tm=128, tag="s")
    return (y_prompt, y_sample, kp, vp, hrp, hip, ks_, vs_, hrs, his)
```

```python
import functools

import jax
import jax.numpy as jnp
from jax import lax
from jax.experimental import pallas as pl
from jax.experimental.pallas import tpu as pltpu

D_MODEL = 1024
DEPTH = 2
SSM_WIDTH = 512
SSM_GROUP = 16
N_SSM_GROUPS = 32
SSM_STATE = 64
N_STATES = N_SSM_GROUPS * SSM_STATE
ATTN_WIDTH = 512
HEAD_DIM = 64
N_HEADS = 8
N_KV_HEADS = 2
Q_PER_KV = 4
KV_WIDTH = 128
WINDOW = 128
ATTN_SCALE = HEAD_DIM ** -0.5
D_FF = 2816
ALPHA = (2.0 * DEPTH) ** 0.25
LN_EPS = 1e-5
RMS_EPS = 1e-6

LANES = 128
SUBLANES = 8
VMEM_LIMIT_BYTES = 56 * 1024 * 1024
N_SLAB = SSM_WIDTH // LANES
GROUPS_PER_SLAB = LANES // SSM_GROUP
SLAB_STATES = GROUPS_PER_SLAB * SSM_STATE
Q_EXP = N_HEADS * LANES
SSM_CHUNK = 8
SSM_ROWS = 2048
ATTN_BLOCKS = 4
ATTN_HEAD_ORDER = (0, 2, 1, 3)
MASKED = -1e30

F32 = jnp.float32
BF16 = jnp.bfloat16


def _layer_norm(r, g, b):
    mu = jnp.mean(r, axis=-1, keepdims=True)
    c = r - mu
    var = jnp.mean(c * c, axis=-1, keepdims=True)
    return c * lax.rsqrt(var + LN_EPS) * g + b


def _rms_norm(y, g):
    return y * lax.rsqrt(jnp.mean(y * y, axis=-1, keepdims=True) + RMS_EPS) * g


def _resident(shape, n_grid=1):
    zeros = (0,) * len(shape)
    if n_grid == 1:
        return pl.BlockSpec(shape, lambda i: zeros, pipeline_mode=pl.Buffered(1))
    return pl.BlockSpec(shape, lambda i, j: zeros, pipeline_mode=pl.Buffered(1))


def _layer_resident(stacked, layer):
    return pl.BlockSpec((None,) + stacked.shape[1:], lambda i: (layer, 0, 0), pipeline_mode=pl.Buffered(1))


def _rows_kernel(*refs, has_mix, has_proj):
    it = iter(refs)
    x_ref = next(it)
    if has_mix:
        ssm_ref, att_ref, wo_s_ref, wo_a_ref, gm_ref, bm_ref = (next(it) for _ in range(6))
    wgu_ref, wdn_ref, g_ref, b_ref = (next(it) for _ in range(4))
    if has_proj:
        wp_ref = next(it)
    o_ref = next(it)
    if has_proj:
        u_ref, q_ref, kv_ref = (next(it) for _ in range(3))

    x = x_ref[...]
    if has_mix:
        ssm = jnp.concatenate([ssm_ref[j] for j in range(N_SLAB)], axis=1).astype(BF16)
        att = att_ref[...].astype(BF16)
        m = jnp.dot(ssm, wo_s_ref[...], preferred_element_type=F32)
        m = m + jnp.dot(att, wo_a_ref[...], preferred_element_type=F32)
        x = _layer_norm(ALPHA * x + m, gm_ref[...], bm_ref[...])
    xb = x.astype(BF16)
    gu = jnp.dot(xb, wgu_ref[...], preferred_element_type=F32)
    gate = gu[:, :D_FF]
    up = gu[:, D_FF:]
    h = (gate * jax.nn.sigmoid(gate) * up).astype(BF16)
    y = jnp.dot(h, wdn_ref[...], preferred_element_type=F32)
    x = _layer_norm(ALPHA * x + 0.5 * y, g_ref[...], b_ref[...])
    o_ref[...] = x
    if has_proj:
        z = jnp.dot(x.astype(BF16), wp_ref[...], preferred_element_type=F32)
        for j in range(N_SLAB):
            u_ref[j] = z[:, j * LANES:(j + 1) * LANES]
        q_cols = q_ref.shape[1]
        q_ref[...] = z[:, SSM_WIDTH:SSM_WIDTH + q_cols].astype(q_ref.dtype)
        kv_ref[...] = z[:, SSM_WIDTH + q_cols:]


def _rows_call(x, ffn, ln, mix=None, proj=None, *, tm, q_dtype=BF16, name):
    M = x.shape[0]
    assert M % tm == 0
    row = lambda w: pl.BlockSpec((tm, w), lambda i: (i, 0))
    slab = lambda n: pl.BlockSpec((n, tm, LANES), lambda i: (0, i, 0))
    args = [x]
    specs = [row(D_MODEL)]
    if mix is not None:
        ssm_n, att_n, wo_s, wo_a, gm, bm = mix
        args += [ssm_n, att_n, wo_s, wo_a, gm, bm]
        specs += [slab(N_SLAB), row(att_n.shape[1]), _resident(wo_s.shape), _resident(wo_a.shape),
                  _resident(gm.shape), _resident(bm.shape)]
    wgu_all, wdn_all, layer = ffn
    args += [wgu_all, wdn_all, ln[0], ln[1]]
    specs += [_layer_resident(wgu_all, layer), _layer_resident(wdn_all, layer),
              _resident(ln[0].shape), _resident(ln[1].shape)]
    out_shape = [jax.ShapeDtypeStruct((M, D_MODEL), F32)]
    out_specs = [row(D_MODEL)]
    if proj is not None:
        args.append(proj)
        specs.append(_resident(proj.shape))
        q_cols = proj.shape[1] - SSM_WIDTH - 2 * KV_WIDTH
        out_shape += [jax.ShapeDtypeStruct((N_SLAB, M, LANES), F32),
                      jax.ShapeDtypeStruct((M, q_cols), q_dtype),
                      jax.ShapeDtypeStruct((M, 2 * KV_WIDTH), F32)]
        out_specs += [slab(N_SLAB), row(q_cols), row(2 * KV_WIDTH)]
    return pl.pallas_call(
        functools.partial(_rows_kernel, has_mix=mix is not None, has_proj=proj is not None),
        grid=(M // tm,),
        in_specs=specs,
        out_specs=out_specs,
        out_shape=out_shape,
        compiler_params=pltpu.CompilerParams(dimension_semantics=("arbitrary",),
                                             vmem_limit_bytes=VMEM_LIMIT_BYTES),
        name=name,
    )(*args)


def _cmul(xr, xi, yr, yi):
    return xr * yr - xi * yi, xr * yi + xi * yr


def _ssm_tables_kernel(p_ref, af_ref, wst_ref, cat_ref, strip_ref, ak_ref, tr_ref):
    t1 = SSM_CHUNK
    a_re, a_im, lam_re, lam_im, b_re, b_im, c_re, c_im = (p_ref[k] for k in range(8))
    num_re, num_im = a_re - 1.0, a_im
    den = lam_re * lam_re + lam_im * lam_im
    f_re = (num_re * lam_re + num_im * lam_im) / den
    f_im = (num_im * lam_re - num_re * lam_im) / den
    bb_re, bb_im = _cmul(f_re, f_im, b_re, b_im)

    def same_group(shape, row_div, col_div):
        r = lax.broadcasted_iota(jnp.int32, shape, 0) // row_div
        c = lax.broadcasted_iota(jnp.int32, shape, 1) // col_div
        return (r == c).astype(F32)

    m_state = same_group((LANES, SLAB_STATES), SSM_GROUP, SSM_STATE)
    m_chan = 0.5 * same_group((LANES, LANES), SSM_GROUP, SSM_GROUP)

    def expand(x):
        return (jnp.concatenate([x] * (SLAB_STATES // LANES), axis=1) * m_state).astype(BF16)

    def dot_nt(a, b):
        return lax.dot_general(a, b, (((1,), (1,)), ((), ())), precision=lax.Precision.HIGHEST,
                               preferred_element_type=F32)

    pw_re, pw_im = jnp.ones_like(a_re), jnp.zeros_like(a_re)
    kd = []
    for l in range(t1):
        ab_re, ab_im = _cmul(pw_re, pw_im, bb_re, bb_im)
        s = t1 - 1 - l
        kd_l = []
        for j in range(N_SLAB):
            rows = slice(j * LANES, (j + 1) * LANES)
            wst_ref[j, s * LANES:(s + 1) * LANES, :SLAB_STATES] = expand(ab_re[rows])
            wst_ref[j, s * LANES:(s + 1) * LANES, SLAB_STATES:] = expand(ab_im[rows])
            k = dot_nt(ab_re[rows], c_re[rows]) - dot_nt(ab_im[rows], c_im[rows])
            kd_l.append((k * m_chan).astype(BF16))
        kd.append(kd_l)
        pw_re, pw_im = _cmul(pw_re, pw_im, a_re, a_im)
        cf_re = c_re * pw_re - c_im * pw_im
        cf_im = -(c_re * pw_im + c_im * pw_re)
        for j in range(N_SLAB):
            rows = slice(j * LANES, (j + 1) * LANES)
            cat_ref[j, l * LANES:(l + 1) * LANES, :SLAB_STATES] = expand(cf_re[rows])
            cat_ref[j, l * LANES:(l + 1) * LANES, SLAB_STATES:] = expand(cf_im[rows])
    for j in range(N_SLAB):
        for rho in range(t1):
            for c in range(2):
                lag = t1 - 2 - rho + c
                blk = kd[lag][j] if lag >= 0 else jnp.zeros((LANES, LANES), BF16)
                strip_ref[j, rho * LANES:(rho + 1) * LANES, c * LANES:(c + 1) * LANES] = blk

    f_re, f_im = af_ref[0], af_ref[1]
    base_re, base_im = f_re, f_im
    for _ in range(t1 - 1):
        base_re, base_im = _cmul(base_re, base_im, f_re, f_im)
    row = lax.broadcasted_iota(jnp.int32, (SUBLANES, N_STATES), 0)
    pws = [(base_re, base_im)]
    for _ in range(SUBLANES - 1):
        pws.append(_cmul(pws[-1][0], pws[-1][1], base_re, base_im))
    for n, k in enumerate((1, 2, 4)):
        for part in range(2):
            ak_ref[2 * n + part] = jnp.where(row >= k, jnp.broadcast_to(pws[k - 1][part], row.shape), 0.0)
    for part in range(2):
        acc = jnp.zeros(row.shape, F32)
        for r in range(SUBLANES):
            acc = jnp.where(row == r, jnp.broadcast_to(pws[r][part], row.shape), acc)
        tr_ref[part] = acc


def _ssm_tables_call(params, a_flat, *, name):
    t1 = SSM_CHUNK
    return pl.pallas_call(
        _ssm_tables_kernel,
        out_shape=[jax.ShapeDtypeStruct((N_SLAB, t1 * LANES, 2 * SLAB_STATES), BF16),
                   jax.ShapeDtypeStruct((N_SLAB, t1 * LANES, 2 * SLAB_STATES), BF16),
                   jax.ShapeDtypeStruct((N_SLAB, t1 * LANES, 2 * LANES), BF16),
                   jax.ShapeDtypeStruct((6, SUBLANES, N_STATES), F32),
                   jax.ShapeDtypeStruct((2, SUBLANES, N_STATES), F32)],
        compiler_params=pltpu.CompilerParams(vmem_limit_bytes=VMEM_LIMIT_BYTES),
        name=name,
    )(params, a_flat)


def _ssm_table_inputs(lam_re, lam_im, log_dt, b_re, b_im, c_re, c_im):
    dt = jnp.exp(log_dt)[:, None]
    mag = jnp.exp(lam_re * dt)
    a_re = mag * jnp.cos(lam_im * dt)
    a_im = mag * jnp.sin(lam_im * dt)
    rep = lambda x: jnp.repeat(x, SSM_GROUP, axis=0)
    flat = lambda x: x.reshape(SSM_WIDTH, SSM_STATE)
    params = jnp.stack([rep(a_re), rep(a_im), rep(lam_re), rep(lam_im),
                        flat(jnp.swapaxes(b_re, 1, 2)), flat(jnp.swapaxes(b_im, 1, 2)), flat(c_re), flat(c_im)])
    params = jnp.concatenate([params, params], axis=-1)
    a_flat = jnp.stack([a_re.reshape(1, N_STATES), a_im.reshape(1, N_STATES)])
    return params, a_flat


def _glu_rms(y, gluw_ref, glub_ref, g_ref):
    g = jax.nn.gelu(y).astype(BF16)
    zz = jnp.dot(g, gluw_ref[...], preferred_element_type=F32) + glub_ref[...]
    o = zz[:, :SSM_WIDTH] * jax.nn.sigmoid(zz[:, SSM_WIDTH:])
    return _rms_norm(o, g_ref[...])


def _ssm_kernel(u_ref, wst_ref, cat_ref, strip_ref, ak_ref, tr_ref, d_ref, gluw_ref, glub_ref, g_ref,
                o_ref, hre_ref, him_ref,
                ucb_ref, sre_ref, sim_ref, hcr_ref, hci_ref):
    t1 = SSM_CHUNK
    nc = SSM_ROWS // t1
    i = pl.program_id(1)

    @pl.when(i == 0)
    def _():
        hcr_ref[...] = jnp.zeros_like(hcr_ref)
        hci_ref[...] = jnp.zeros_like(hci_ref)

    for j in range(N_SLAB):
        for s in range(t1):
            ucb_ref[j, :, s * LANES:(s + 1) * LANES] = u_ref[j, pl.ds(s, nc, stride=t1), :].astype(BF16)

    for j in range(N_SLAB):
        st = jnp.dot(ucb_ref[j], wst_ref[j], preferred_element_type=F32)
        sre_ref[:, j * SLAB_STATES:(j + 1) * SLAB_STATES] = st[:, :SLAB_STATES]
        sim_ref[:, j * SLAB_STATES:(j + 1) * SLAB_STATES] = st[:, SLAB_STATES:]

    row = lax.broadcasted_iota(jnp.int32, (SUBLANES, SLAB_STATES), 0)
    for j in range(N_SLAB):
        cols = pl.ds(j * SLAB_STATES, SLAB_STATES)

        def body(gi, carry, cols=cols):
            hr, hi = carry
            r0 = pl.multiple_of(gi * SUBLANES, SUBLANES)
            xr = sre_ref[pl.ds(r0, SUBLANES), cols]
            xi = sim_ref[pl.ds(r0, SUBLANES), cols]
            for n, k in enumerate((1, 2, 4)):
                akr = ak_ref[2 * n, :, cols]
                aki = ak_ref[2 * n + 1, :, cols]
                sr = pltpu.roll(xr, k, 0)
                si = pltpu.roll(xi, k, 0)
                xr, xi = xr + akr * sr - aki * si, xi + akr * si + aki * sr
            tr = tr_ref[0, :, cols]
            ti = tr_ref[1, :, cols]
            er = xr + tr * hr - ti * hi
            ei = xi + tr * hi + ti * hr
            sre_ref[pl.ds(r0, SUBLANES), cols] = jnp.where(row >= 1, pltpu.roll(er, 1, 0), hr)
            sim_ref[pl.ds(r0, SUBLANES), cols] = jnp.where(row >= 1, pltpu.roll(ei, 1, 0), hi)
            last = SUBLANES - 1
            return (jnp.broadcast_to(er[last:last + 1], er.shape), jnp.broadcast_to(ei[last:last + 1], ei.shape))

        hr, hi = lax.fori_loop(0, nc // SUBLANES, body, (hcr_ref[:, cols], hci_ref[:, cols]))
        hcr_ref[:, cols] = hr
        hci_ref[:, cols] = hi

    hre_ref[0] = hcr_ref[...]
    him_ref[0] = hci_ref[...]

    ys = []
    for j in range(N_SLAB):
        cols = slice(j * SLAB_STATES, (j + 1) * SLAB_STATES)
        hprev = jnp.concatenate([sre_ref[:, cols], sim_ref[:, cols]], axis=1).astype(BF16)
        y = lax.dot_general(hprev, cat_ref[j], (((1,), (1,)), ((), ())), preferred_element_type=F32)
        parts = []
        for p2 in range(t1 // 2):
            kk = (2 * p2 + 2) * LANES
            yi = jnp.dot(ucb_ref[j, :, :kk], strip_ref[j, (t1 - 2 - 2 * p2) * LANES:, :],
                         preferred_element_type=F32)
            parts.append(y[:, 2 * p2 * LANES:(2 * p2 + 2) * LANES] + yi)
        ys.append(jnp.concatenate(parts, axis=1))

    for t in range(t1):
        y = jnp.concatenate([ys[j][:, t * LANES:(t + 1) * LANES] + d_ref[j] * u_ref[j, pl.ds(t, nc, stride=t1), :]
                             for j in range(N_SLAB)], axis=1)
        n = _glu_rms(y, gluw_ref, glub_ref, g_ref)
        for j in range(N_SLAB):
            o_ref[j, pl.ds(t, nc, stride=t1), :] = n[:, j * LANES:(j + 1) * LANES]


def _ssm_prompt_call(u, tab, d_t, gluw, glub, g, *, batch, seq, name):
    t1 = SSM_CHUNK
    nc = SSM_ROWS // t1
    steps = seq // SSM_ROWS
    blk = pl.BlockSpec((N_SLAB, SSM_ROWS, LANES), lambda b, i: (0, b * steps + i, 0))
    st = pl.BlockSpec((1, SUBLANES, N_STATES), lambda b, i: (b, 0, 0))
    consts = list(tab) + [d_t, gluw, glub, g]
    out, hre, him = pl.pallas_call(
        _ssm_kernel,
        grid=(batch, steps),
        in_specs=[blk] + [_resident(c.shape, 2) for c in consts],
        out_specs=[blk, st, st],
        out_shape=[jax.ShapeDtypeStruct(u.shape, F32),
                   jax.ShapeDtypeStruct((batch, SUBLANES, N_STATES), F32),
                   jax.ShapeDtypeStruct((batch, SUBLANES, N_STATES), F32)],
        scratch_shapes=[pltpu.VMEM((N_SLAB, nc, t1 * LANES), BF16),
                        pltpu.VMEM((nc, N_STATES), F32),
                        pltpu.VMEM((nc, N_STATES), F32),
                        pltpu.VMEM((SUBLANES, N_STATES), F32),
                        pltpu.VMEM((SUBLANES, N_STATES), F32)],
        compiler_params=pltpu.CompilerParams(dimension_semantics=("arbitrary", "arbitrary"),
                                             vmem_limit_bytes=VMEM_LIMIT_BYTES),
        name=name,
    )(u, *consts)
    return out, hre[:, 0], him[:, 0]


def _ssm_sample_kernel(u_ref, h0r_ref, h0i_ref, wst_ref, cat_ref, kd_ref, a_ref, d_ref, gluw_ref, glub_ref, g_ref,
                       o_ref, hr_ref, hi_ref):
    ys = []
    for j in range(N_SLAB):
        cols = slice(j * SLAB_STATES, (j + 1) * SLAB_STATES)
        uf = u_ref[j]
        ub = uf.astype(BF16)
        st = jnp.dot(ub, wst_ref[j], preferred_element_type=F32)
        h0r = h0r_ref[:, cols]
        h0i = h0i_ref[:, cols]
        ar = a_ref[0, :, cols]
        ai = a_ref[1, :, cols]
        hr_ref[:, cols] = ar * h0r - ai * h0i + st[:, :SLAB_STATES]
        hi_ref[:, cols] = ar * h0i + ai * h0r + st[:, SLAB_STATES:]
        hcat = jnp.concatenate([h0r, h0i], axis=1).astype(BF16)
        y = lax.dot_general(hcat, cat_ref[j], (((1,), (1,)), ((), ())), preferred_element_type=F32)
        y = y + jnp.dot(ub, kd_ref[j], preferred_element_type=F32)
        ys.append(y + d_ref[j] * uf)
    n = _glu_rms(jnp.concatenate(ys, axis=1), gluw_ref, glub_ref, g_ref)
    for j in range(N_SLAB):
        o_ref[j] = n[:, j * LANES:(j + 1) * LANES]


def _ssm_sample_call(u, h0r, h0i, tab, a_flat, d_t, gluw, glub, g, *, name):
    n_seq = u.shape[1]
    wst, cat, strip = tab[:3]
    last = SSM_CHUNK - 1
    whole = lambda a: pl.BlockSpec(a.shape, lambda i: (0,) * len(a.shape))
    outs = [jax.ShapeDtypeStruct(u.shape, F32),
            jax.ShapeDtypeStruct((n_seq, N_STATES), F32),
            jax.ShapeDtypeStruct((n_seq, N_STATES), F32)]
    return pl.pallas_call(
        _ssm_sample_kernel,
        grid=(1,),
        in_specs=[whole(u), whole(h0r), whole(h0i),
                  pl.BlockSpec((N_SLAB, LANES, 2 * SLAB_STATES), lambda i: (0, last, 0)),
                  pl.BlockSpec((N_SLAB, LANES, 2 * SLAB_STATES), lambda i: (0, 0, 0)),
                  pl.BlockSpec((N_SLAB, LANES, LANES), lambda i: (0, last, 1)),
                  whole(a_flat), whole(d_t), whole(gluw), whole(glub), whole(g)],
        out_specs=[whole(o) for o in outs],
        out_shape=outs,
        compiler_params=pltpu.CompilerParams(dimension_semantics=("arbitrary",),
                                             vmem_limit_bytes=VMEM_LIMIT_BYTES),
        name=name,
    )(u, h0r, h0i, wst, cat, strip, a_flat, d_t, gluw, glub, g)


def _attn_kernel(sink_ref, q_ref, kvc_ref, kvp_ref, g_ref, o_ref):
    i = pl.program_id(1)
    kj = lax.broadcasted_iota(jnp.int32, (2 * WINDOW, WINDOW), 0)
    qi = lax.broadcasted_iota(jnp.int32, (2 * WINDOW, WINDOW), 1)
    band = (kj >= qi) & (kj <= qi + WINDOW)
    low = qi < HEAD_DIM
    for sb in range(ATTN_BLOCKS):
        blk = slice(sb * WINDOW, (sb + 1) * WINDOW)
        kc = kvc_ref[blk, :]
        if sb == 0:
            kp = kvp_ref[...]
            valid = band & ((i > 0) | (kj >= WINDOW))
        else:
            kp = kvc_ref[(sb - 1) * WINDOW:sb * WINDOW, :]
            valid = band
        valid2 = jnp.concatenate([valid, valid], axis=1)
        kcat = jnp.concatenate([kp[:, :KV_WIDTH], kc[:, :KV_WIDTH]], axis=0) * ATTN_SCALE
        swap = pltpu.roll(kcat, HEAD_DIM, 1)
        zero = jnp.zeros_like(kcat)
        k_sel = [[jnp.where(low, kcat, zero).astype(BF16), jnp.where(low, zero, swap).astype(BF16)],
                 [jnp.where(low, swap, zero).astype(BF16), jnp.where(low, zero, kcat).astype(BF16)]]
        v_t = jnp.concatenate([kp[:, KV_WIDTH:], kc[:, KV_WIDTH:]], axis=0).T.astype(BF16)
        parts = []
        for kvh in range(N_KV_HEADS):
            q_pair = jnp.concatenate([q_ref[blk, (2 * kvh) * LANES:(2 * kvh + 1) * LANES],
                                      q_ref[blk, (2 * kvh + 1) * LANES:(2 * kvh + 2) * LANES]], axis=0)
            for parity in range(2):
                s_t = lax.dot_general(k_sel[kvh][parity], q_pair, (((1,), (1,)), ((), ())),
                                      preferred_element_type=F32)
                parts.append(jnp.where(valid2, s_t, MASKED))
        s_all = jnp.concatenate(parts, axis=1)
        sink = sink_ref[...]
        m = jnp.maximum(jnp.max(s_all, axis=0, keepdims=True), sink)
        p_all = jnp.exp(s_all - m)
        inv_den = 1.0 / (jnp.sum(p_all, axis=0, keepdims=True) + jnp.exp(sink - m))
        p_all = p_all.astype(BF16)
        half = Q_PER_KV * WINDOW
        o_kv = [jnp.dot(v_t[kvh * HEAD_DIM:(kvh + 1) * HEAD_DIM], p_all[:, kvh * half:(kvh + 1) * half],
                        preferred_element_type=F32) * inv_den[:, kvh * half:(kvh + 1) * half]
                for kvh in range(N_KV_HEADS)]
        tiles = []
        for pr in range(Q_PER_KV):
            lanes = slice(ATTN_HEAD_ORDER.index(pr) * WINDOW, (ATTN_HEAD_ORDER.index(pr) + 1) * WINDOW)
            tiles.append(jnp.concatenate([o_kv[0][:, lanes], o_kv[1][:, lanes]], axis=0))
        ss = None
        for tile in tiles:
            t = jnp.sum(tile * tile, axis=0, keepdims=True)
            ss = t if ss is None else ss + t
        inv = lax.rsqrt(ss / ATTN_WIDTH + RMS_EPS)
        for pr, tile in enumerate(tiles):
            o_ref[blk, pr * LANES:(pr + 1) * LANES] = ((tile * inv).T * g_ref[pr]).astype(o_ref.dtype)


def _attn_prompt_call(q, kv, sinks, g_perm, *, batch, seq, name):
    rows = ATTN_BLOCKS * WINDOW
    nb = seq // rows
    M = kv.shape[0]
    cur = lambda b, i: (b * nb + i, 0)
    prev = lambda b, i: ((b * nb + i) * ATTN_BLOCKS - jnp.minimum(i, 1), 0)
    return pl.pallas_call(
        _attn_kernel,
        grid=(batch, nb),
        in_specs=[_resident(sinks.shape, 2),
                  pl.BlockSpec((rows, ATTN_WIDTH), cur),
                  pl.BlockSpec((rows, 2 * KV_WIDTH), cur),
                  pl.BlockSpec((WINDOW, 2 * KV_WIDTH), prev),
                  _resident(g_perm.shape, 2)],
        out_specs=pl.BlockSpec((rows, ATTN_WIDTH), cur),
        out_shape=jax.ShapeDtypeStruct((M, ATTN_WIDTH), BF16),
        compiler_params=pltpu.CompilerParams(dimension_semantics=("arbitrary", "arbitrary"),
                                             vmem_limit_bytes=VMEM_LIMIT_BYTES),
        name=name,
    )(sinks, q, kv, kv, g_perm)


SEQ_PER_STEP = LANES // N_HEADS


def _attn_sample_kernel(q_ref, kv_ref, ck_ref, cv_ref, sink_ref, g_ref, own_ref, o_ref, ko_ref, vo_ref):
    nrow = SEQ_PER_STEP * N_HEADS
    dims_nt = (((1,), (1,)), ((), ()))
    qb = q_ref[...].astype(BF16)
    k_new = kv_ref[:, :KV_WIDTH]
    v_new = kv_ref[:, KV_WIDTH:]
    row_seq = lax.broadcasted_iota(jnp.int32, (nrow, LANES), 0) // N_HEADS
    lane_seq = lax.broadcasted_iota(jnp.int32, (nrow, LANES), 1) // N_HEADS
    row_id = lax.broadcasted_iota(jnp.int32, (nrow, LANES), 0)
    lane_id = lax.broadcasted_iota(jnp.int32, (nrow, LANES), 1)

    s_t = None
    for n in range(SEQ_PER_STEP):
        qn = jnp.where(row_seq == n, qb, jnp.zeros_like(qb))
        part = lax.dot_general(ck_ref[n].astype(BF16), qn, dims_nt, preferred_element_type=F32)
        s_t = part if s_t is None else s_t + part
    s_t = s_t * ATTN_SCALE
    s_cross = lax.dot_general(k_new.astype(BF16), qb, dims_nt, preferred_element_type=F32)
    own_seq = (lax.broadcasted_iota(jnp.int32, s_cross.shape, 0)
               == lax.broadcasted_iota(jnp.int32, s_cross.shape, 1) // N_HEADS)
    s_new = jnp.sum(jnp.where(own_seq, s_cross, 0.0), axis=0, keepdims=True) * ATTN_SCALE
    sink = sink_ref[...]
    m = jnp.maximum(jnp.maximum(jnp.max(s_t, axis=0, keepdims=True), s_new), sink)
    p_t = jnp.exp(s_t - m)
    p_new = jnp.exp(s_new - m)
    inv = 1.0 / (jnp.sum(p_t, axis=0, keepdims=True) + p_new + jnp.exp(sink - m))
    p = (p_t * inv).T.astype(BF16)

    o = None
    for n in range(SEQ_PER_STEP):
        pn = jnp.where(row_seq == n, p, jnp.zeros_like(p))
        part = jnp.dot(pn, cv_ref[n].astype(BF16), preferred_element_type=F32)
        o = part if o is None else o + part
    p_diag = jnp.where(row_id == lane_id, jnp.broadcast_to(p_new * inv, (nrow, LANES)), 0.0).astype(BF16)
    p_col = jnp.dot(p_diag, jnp.ones((LANES, LANES), BF16), preferred_element_type=F32)
    pick = (row_seq[:, :SEQ_PER_STEP] == lax.broadcasted_iota(jnp.int32, (nrow, SEQ_PER_STEP), 1)).astype(BF16)
    v_rows = jnp.dot(pick, v_new.astype(BF16), preferred_element_type=F32)
    o = o + p_col * v_rows

    hi = lax.Precision.HIGHEST
    row_sums = jnp.dot(o * o * own_ref[...], jnp.ones((LANES, LANES), F32), precision=hi, preferred_element_type=F32)
    ss = jnp.dot((row_seq == lane_seq).astype(F32), row_sums, precision=hi, preferred_element_type=F32)
    o_ref[...] = o * lax.rsqrt(ss / ATTN_WIDTH + RMS_EPS) * g_ref[...]

    for n in range(SEQ_PER_STEP):
        ko_ref[n, pl.ds(0, WINDOW - 1), :] = ck_ref[n, pl.ds(1, WINDOW - 1), :]
        ko_ref[n, pl.ds(WINDOW - 1, 1), :] = k_new[n:n + 1]
        vo_ref[n, pl.ds(0, WINDOW - 1), :] = cv_ref[n, pl.ds(1, WINDOW - 1), :]
        vo_ref[n, pl.ds(WINDOW - 1, 1), :] = v_new[n:n + 1]


def _attn_sample_call(q, kv, ck, cv, layer, sink_row, g_rows, own_rows, *, name):
    n_seq = kv.shape[0]
    assert n_seq % SEQ_PER_STEP == 0
    nrow = SEQ_PER_STEP * N_HEADS
    cblk = pl.BlockSpec((SEQ_PER_STEP, WINDOW, 2 * HEAD_DIM), lambda i: (i, 0, 0))
    cin = pl.BlockSpec((None, SEQ_PER_STEP, WINDOW, 2 * HEAD_DIM), lambda i: (layer, i, 0, 0))
    qblk = pl.BlockSpec((nrow, LANES), lambda i: (i, 0))
    whole = lambda a: pl.BlockSpec(a.shape, lambda i: (0,) * len(a.shape))
    return pl.pallas_call(
        _attn_sample_kernel,
        grid=(n_seq // SEQ_PER_STEP,),
        in_specs=[qblk, pl.BlockSpec((SEQ_PER_STEP, 2 * KV_WIDTH), lambda i: (i, 0)), cin, cin,
                  whole(sink_row), whole(g_rows), whole(own_rows)],
        out_specs=[qblk, cblk, cblk],
        out_shape=[jax.ShapeDtypeStruct(q.shape, F32),
                   jax.ShapeDtypeStruct(ck.shape[1:], F32),
                   jax.ShapeDtypeStruct(cv.shape[1:], F32)],
        compiler_params=pltpu.CompilerParams(dimension_semantics=("arbitrary",),
                                             vmem_limit_bytes=VMEM_LIMIT_BYTES),
        name=name,
    )(q, kv, ck, cv, sink_row, g_rows, own_rows)


def _expand_heads(a):
    lead = a.shape[:-1]
    a = a.reshape(lead + (N_KV_HEADS, Q_PER_KV, 1, HEAD_DIM))
    sel = jnp.eye(N_KV_HEADS, dtype=a.dtype).reshape(N_KV_HEADS, 1, N_KV_HEADS, 1)
    return (a * sel).reshape(lead + (Q_EXP,))


def _pair_heads(a):
    lead = a.shape[:-1]
    a = a.reshape(lead + (N_KV_HEADS, Q_PER_KV, HEAD_DIM))
    return jnp.swapaxes(a, -3, -2).reshape(lead + (ATTN_WIDTH,))


def _prep_weights(ln_g, ln_b, ffn1_w_in, ffn1_w_out, ffn2_w_in, ffn2_w_out, w_in, ssm_lam_re, ssm_lam_im, ssm_log_dt,
                  ssm_b_re, ssm_b_im, ssm_c_re, ssm_c_im, ssm_d, glu_w, glu_b, attn_sinks, g_ssm_out, g_attn_out,
                  w_out):
    w = dict(ffn1=[], ffn2=[], ln=[], w_in=[], w_in_exp=[], wo_s=[], wo_a=[], tab=[], a_flat=[], d_1=[],
             gluw=[], glub=[], g_ssm=[], sinks=[], g_pair=[], sink_rows=[], g_rows=[], wo_a_exp=[], own_rows=None)
    w['own_rows'] = jnp.tile(_expand_heads(jnp.ones((ATTN_WIDTH,), F32)).reshape(N_HEADS, LANES), (SEQ_PER_STEP, 1))
    ffn1_in_b, ffn1_out_b = ffn1_w_in.astype(BF16), ffn1_w_out.astype(BF16)
    ffn2_in_b, ffn2_out_b = ffn2_w_in.astype(BF16), ffn2_w_out.astype(BF16)
    for l in range(DEPTH):
        w['ffn1'].append((ffn1_in_b, ffn1_out_b, l))
        w['ffn2'].append((ffn2_in_b, ffn2_out_b, l))
        w['ln'].append([(ln_g[l, i][None, :], ln_b[l, i][None, :]) for i in range(3)])
        o1, o2 = SSM_WIDTH, SSM_WIDTH + ATTN_WIDTH
        w['w_in'].append(w_in[l].astype(BF16))
        w['w_in_exp'].append(jnp.concatenate([w_in[l][:, :o1], _expand_heads(w_in[l][:, o1:o2]), w_in[l][:, o2:]],
                                             axis=1).astype(BF16))
        w['wo_s'].append(w_out[l, :SSM_WIDTH].astype(BF16))
        w['wo_a'].append(_pair_heads(w_out[l, SSM_WIDTH:].T).T.astype(BF16))
        params, a_flat = _ssm_table_inputs(ssm_lam_re[l], ssm_lam_im[l], ssm_log_dt[l], ssm_b_re[l], ssm_b_im[l],
                                           ssm_c_re[l], ssm_c_im[l])
        w['tab'].append(_ssm_tables_call(params, a_flat, name=f"ssm_tables_{l}"))
        w['a_flat'].append(a_flat)
        d_slab = ssm_d[l].reshape(N_SLAB, 1, LANES)
        w['d_1'].append(d_slab)
        w['gluw'].append(glu_w[l].astype(BF16))
        w['glub'].append(glu_b[l][None, :])
        w['g_ssm'].append(g_ssm_out[l][None, :])
        w['sink_rows'].append(jnp.tile(attn_sinks[l], SEQ_PER_STEP)[None, :])
        w['g_rows'].append(jnp.tile(_expand_heads(g_attn_out[l]).reshape(N_HEADS, LANES), (SEQ_PER_STEP, 1)))
        w['wo_a_exp'].append(_expand_heads(w_out[l, SSM_WIDTH:].T).T.astype(BF16))
        order = jnp.array([Q_PER_KV * kvh + o for kvh in range(N_KV_HEADS) for o in ATTN_HEAD_ORDER])
        w['sinks'].append(jnp.repeat(attn_sinks[l][order], WINDOW)[None, :])
        w['g_pair'].append(_pair_heads(g_attn_out[l]).reshape(Q_PER_KV, 1, LANES))
    return w


def _prompt_mixer(u, q, kv, l, w, batch, seq):
    ssm_n, hre, him = _ssm_prompt_call(u, w['tab'][l], w['d_1'][l], w['gluw'][l],
                                       w['glub'][l], w['g_ssm'][l], batch=batch, seq=seq, name=f"p_ssm_{l}")
    att_n = _attn_prompt_call(q, kv, w['sinks'][l], w['g_pair'][l], batch=batch, seq=seq, name=f"p_attn_{l}")
    kvw = kv.reshape(batch, seq, 2 * KV_WIDTH)[:, -WINDOW:].reshape(batch, WINDOW, 2, N_KV_HEADS, HEAD_DIM)
    return (ssm_n, att_n, hre.reshape(batch, N_SSM_GROUPS, SSM_STATE), him.reshape(batch, N_SSM_GROUPS, SSM_STATE),
            kvw[:, :, 0], kvw[:, :, 1])


def _sample_mixer(u, q, kv, l, w, h0_re, h0_im, k_buf, v_buf):
    n_seq = kv.shape[0]
    ssm_n, hre, him = _ssm_sample_call(u, h0_re.reshape(n_seq, N_STATES), h0_im.reshape(n_seq, N_STATES),
                                       w['tab'][l], w['a_flat'][l], w['d_1'][l], w['gluw'][l], w['glub'][l], w['g_ssm'][l],
                                       name=f"s_ssm_{l}")
    att, kn, vn = _attn_sample_call(q.reshape(n_seq * N_HEADS, LANES), kv, k_buf, v_buf, l,
                                    w['sink_rows'][l], w['g_rows'][l], w['own_rows'], name=f"s_attn_{l}")
    win = (n_seq, WINDOW, N_KV_HEADS, HEAD_DIM)
    return (ssm_n, att.reshape(n_seq, Q_EXP), hre.reshape(n_seq, N_SSM_GROUPS, SSM_STATE),
            him.reshape(n_seq, N_SSM_GROUPS, SSM_STATE), kn.reshape(win), vn.reshape(win))


def _trunk(x, h0_re, h0_im, k_buf, v_buf, w, *, tm, tag):
    Bn, L, _ = x.shape
    sample = k_buf is not None
    q_dtype = F32 if sample else BF16
    x = x.reshape(Bn * L, D_MODEL)
    ks, vs, hrs, his = [], [], [], []
    mix = None
    for l in range(DEPTH):
        if l > 0:
            x = _rows_call(x, w['ffn2'][l - 1], w['ln'][l - 1][2], mix=mix, tm=tm, name=f"{tag}_mix_ffn2_{l - 1}")[0]
        proj = w['w_in_exp'][l] if sample else w['w_in'][l]
        x, u, q, kv = _rows_call(x, w['ffn1'][l], w['ln'][l][0], proj=proj, tm=tm, q_dtype=q_dtype,
                                 name=f"{tag}_ffn1_{l}")
        if sample:
            ssm_n, att_n, hre, him, kn, vn = _sample_mixer(u, q, kv, l, w, h0_re[l], h0_im[l], k_buf, v_buf)
        else:
            ssm_n, att_n, hre, him, kn, vn = _prompt_mixer(u, q, kv, l, w, Bn, L)
        wo_a = w['wo_a_exp'][l] if sample else w['wo_a'][l]
        mix = (ssm_n, att_n, w['wo_s'][l], wo_a, w['ln'][l][1][0], w['ln'][l][1][1])
        ks.append(kn); vs.append(vn); hrs.append(hre); his.append(him)
    x = _rows_call(x, w['ffn2'][DEPTH - 1], w['ln'][DEPTH - 1][2], mix=mix, tm=tm, name=f"{tag}_mix_ffn2_{DEPTH - 1}")[0]
    return x.reshape(Bn, L, D_MODEL), jnp.stack(ks), jnp.stack(vs), jnp.stack(hrs), jnp.stack(his)


def kernel(x_prompt, x_sample, cache_k_win, cache_v_win, state_ssm_re, state_ssm_im, ln_g, ln_b, ffn1_w_in, ffn1_w_out, ffn2_w_in, ffn2_w_out, w_in, ssm_lam_re, ssm_lam_im, ssm_log_dt, ssm_b_re, ssm_b_im, ssm_c_re, ssm_c_im, ssm_d, glu_w, glu_b, attn_sinks, g_ssm_out, g_attn_out, w_out):
    w = _prep_weights(ln_g, ln_b, ffn1_w_in, ffn1_w_out, ffn2_w_in, ffn2_w_out, w_in, ssm_lam_re, ssm_lam_im,
                      ssm_log_dt, ssm_b_re, ssm_b_im, ssm_c_re, ssm_c_im, ssm_d, glu_w, glu_b, attn_sinks,
                      g_ssm_out, g_attn_out, w_out)
    y_prompt, kp, vp, hrp, hip = _trunk(x_prompt, None, None, None, None, w, tm=512, tag="p")
    n_seq = x_sample.shape[0]
    merged = (DEPTH, n_seq, WINDOW, N_KV_HEADS * HEAD_DIM)
    y_sample, ks_, vs_, hrs, his = _trunk(x_sample, state_ssm_re, state_ssm_im, cache_k_win.reshape(merged),
                                          cache_v_win.reshape(merged), w, tm=128, tag="s")
    return (y_prompt, y_sample, kp, vp, hrp, hip, ks_, vs_, hrs, his)
```

```python
import functools

import jax
import jax.numpy as jnp
from jax import lax
from jax.experimental import pallas as pl
from jax.experimental.pallas import tpu as pltpu

D_MODEL = 1024
DEPTH = 2
SSM_WIDTH = 512
SSM_GROUP = 16
N_SSM_GROUPS = 32
SSM_STATE = 64
N_STATES = N_SSM_GROUPS * SSM_STATE
ATTN_WIDTH = 512
HEAD_DIM = 64
N_HEADS = 8
N_KV_HEADS = 2
Q_PER_KV = 4
KV_WIDTH = 128
WINDOW = 128
ATTN_SCALE = HEAD_DIM ** -0.5
D_FF = 2816
ALPHA = (2.0 * DEPTH) ** 0.25
LN_EPS = 1e-5
RMS_EPS = 1e-6

ROW_GROUP = 256
LANES = 128
SUBLANES = 8
VMEM_LIMIT_BYTES = 56 * 1024 * 1024
N_SLAB = SSM_WIDTH // LANES
GROUPS_PER_SLAB = LANES // SSM_GROUP
SLAB_STATES = GROUPS_PER_SLAB * SSM_STATE
Q_EXP = N_HEADS * LANES
SSM_CHUNK = 8
SSM_ROWS = 2048
ATTN_BLOCKS = 4
ATTN_HEAD_ORDER = (0, 2, 1, 3)
MASKED = -1e30

F32 = jnp.float32
BF16 = jnp.bfloat16


def _layer_norm(r, g, b):
    mu = jnp.mean(r, axis=-1, keepdims=True)
    c = r - mu
    var = jnp.mean(c * c, axis=-1, keepdims=True)
    return c * lax.rsqrt(var + LN_EPS) * g + b


def _rms_norm(y, g):
    return y * lax.rsqrt(jnp.mean(y * y, axis=-1, keepdims=True) + RMS_EPS) * g


def _resident(shape, n_grid=1):
    zeros = (0,) * len(shape)
    if n_grid == 1:
        return pl.BlockSpec(shape, lambda i: zeros, pipeline_mode=pl.Buffered(1))
    return pl.BlockSpec(shape, lambda i, j: zeros, pipeline_mode=pl.Buffered(1))


def _layer_resident(stacked, layer):
    return pl.BlockSpec((None,) + stacked.shape[1:], lambda i: (layer, 0, 0), pipeline_mode=pl.Buffered(1))


def _rows_kernel(*refs, has_mix, has_proj, n_sub):
    it = iter(refs)
    x_ref = next(it)
    if has_mix:
        ssm_ref, att_ref, wo_s_ref, wo_a_ref, gm_ref, bm_ref = (next(it) for _ in range(6))
    wgu_ref, wdn_ref, g_ref, b_ref = (next(it) for _ in range(4))
    if has_proj:
        wp_ref = next(it)
    o_ref = next(it)
    if has_proj:
        u_ref, q_ref, kv_ref = (next(it) for _ in range(3))

    tm = x_ref.shape[0]
    sub = tm // n_sub
    groups = [slice(r * sub, (r + 1) * sub) for r in range(n_sub)]
    xs = [x_ref[rows, :] for rows in groups]
    if has_mix:
        ms = []
        for rows in groups:
            ssm = jnp.concatenate([ssm_ref[j, rows, :] for j in range(N_SLAB)], axis=1).astype(BF16)
            att = att_ref[rows, :].astype(BF16)
            ms.append(jnp.dot(ssm, wo_s_ref[...], preferred_element_type=F32)
                      + jnp.dot(att, wo_a_ref[...], preferred_element_type=F32))
        xs = [_layer_norm(ALPHA * x + m, gm_ref[...], bm_ref[...]) for x, m in zip(xs, ms)]
    gus = [jnp.dot(x.astype(BF16), wgu_ref[...], preferred_element_type=F32) for x in xs]
    hs = [(gu[:, :D_FF] * jax.nn.sigmoid(gu[:, :D_FF]) * gu[:, D_FF:]).astype(BF16) for gu in gus]
    ys = [jnp.dot(h, wdn_ref[...], preferred_element_type=F32) for h in hs]
    xs = [_layer_norm(ALPHA * x + 0.5 * y, g_ref[...], b_ref[...]) for x, y in zip(xs, ys)]
    for rows, x in zip(groups, xs):
        o_ref[rows, :] = x
    if has_proj:
        q_cols = q_ref.shape[1]
        zs = [jnp.dot(x.astype(BF16), wp_ref[...], preferred_element_type=F32) for x in xs]
        for rows, z in zip(groups, zs):
            for j in range(N_SLAB):
                u_ref[j, rows, :] = z[:, j * LANES:(j + 1) * LANES]
            q_ref[rows, :] = z[:, SSM_WIDTH:SSM_WIDTH + q_cols].astype(q_ref.dtype)
            kv_ref[rows, :] = z[:, SSM_WIDTH + q_cols:]


def _rows_call(x, ffn, ln, mix=None, proj=None, *, tm, q_dtype=BF16, name):
    M = x.shape[0]
    assert M % tm == 0
    row = lambda w: pl.BlockSpec((tm, w), lambda i: (i, 0))
    slab = lambda n: pl.BlockSpec((n, tm, LANES), lambda i: (0, i, 0))
    args = [x]
    specs = [row(D_MODEL)]
    if mix is not None:
        ssm_n, att_n, wo_s, wo_a, gm, bm = mix
        args += [ssm_n, att_n, wo_s, wo_a, gm, bm]
        specs += [slab(N_SLAB), row(att_n.shape[1]), _resident(wo_s.shape), _resident(wo_a.shape),
                  _resident(gm.shape), _resident(bm.shape)]
    wgu_all, wdn_all, layer = ffn
    args += [wgu_all, wdn_all, ln[0], ln[1]]
    specs += [_layer_resident(wgu_all, layer), _layer_resident(wdn_all, layer),
              _resident(ln[0].shape), _resident(ln[1].shape)]
    out_shape = [jax.ShapeDtypeStruct((M, D_MODEL), F32)]
    out_specs = [row(D_MODEL)]
    if proj is not None:
        args.append(proj)
        specs.append(_resident(proj.shape))
        q_cols = proj.shape[1] - SSM_WIDTH - 2 * KV_WIDTH
        out_shape += [jax.ShapeDtypeStruct((N_SLAB, M, LANES), F32),
                      jax.ShapeDtypeStruct((M, q_cols), q_dtype),
                      jax.ShapeDtypeStruct((M, 2 * KV_WIDTH), F32)]
        out_specs += [slab(N_SLAB), row(q_cols), row(2 * KV_WIDTH)]
    return pl.pallas_call(
        functools.partial(_rows_kernel, has_mix=mix is not None, has_proj=proj is not None,
                          n_sub=max(1, tm // ROW_GROUP)),
        grid=(M // tm,),
        in_specs=specs,
        out_specs=out_specs,
        out_shape=out_shape,
        compiler_params=pltpu.CompilerParams(dimension_semantics=("arbitrary",),
                                             vmem_limit_bytes=VMEM_LIMIT_BYTES),
        name=name,
    )(*args)


def _cmul(xr, xi, yr, yi):
    return xr * yr - xi * yi, xr * yi + xi * yr


def _ssm_tables_kernel(p_ref, af_ref, wst_ref, cat_ref, strip_ref, ak_ref, tr_ref):
    t1 = SSM_CHUNK
    a_re, a_im, lam_re, lam_im, b_re, b_im, c_re, c_im = (p_ref[k] for k in range(8))
    num_re, num_im = a_re - 1.0, a_im
    den = lam_re * lam_re + lam_im * lam_im
    f_re = (num_re * lam_re + num_im * lam_im) / den
    f_im = (num_im * lam_re - num_re * lam_im) / den
    bb_re, bb_im = _cmul(f_re, f_im, b_re, b_im)

    def same_group(shape, row_div, col_div):
        r = lax.broadcasted_iota(jnp.int32, shape, 0) // row_div
        c = lax.broadcasted_iota(jnp.int32, shape, 1) // col_div
        return (r == c).astype(F32)

    m_state = same_group((LANES, SLAB_STATES), SSM_GROUP, SSM_STATE)
    m_chan = 0.5 * same_group((LANES, LANES), SSM_GROUP, SSM_GROUP)

    def expand(x):
        return (jnp.concatenate([x] * (SLAB_STATES // LANES), axis=1) * m_state).astype(BF16)

    def dot_nt(a, b):
        return lax.dot_general(a, b, (((1,), (1,)), ((), ())), precision=lax.Precision.HIGHEST,
                               preferred_element_type=F32)

    pw_re, pw_im = jnp.ones_like(a_re), jnp.zeros_like(a_re)
    kd = []
    for l in range(t1):
        ab_re, ab_im = _cmul(pw_re, pw_im, bb_re, bb_im)
        s = t1 - 1 - l
        kd_l = []
        for j in range(N_SLAB):
            rows = slice(j * LANES, (j + 1) * LANES)
            wst_ref[j, s * LANES:(s + 1) * LANES, :SLAB_STATES] = expand(ab_re[rows])
            wst_ref[j, s * LANES:(s + 1) * LANES, SLAB_STATES:] = expand(ab_im[rows])
            k = dot_nt(ab_re[rows], c_re[rows]) - dot_nt(ab_im[rows], c_im[rows])
            kd_l.append((k * m_chan).astype(BF16))
        kd.append(kd_l)
        pw_re, pw_im = _cmul(pw_re, pw_im, a_re, a_im)
        cf_re = c_re * pw_re - c_im * pw_im
        cf_im = -(c_re * pw_im + c_im * pw_re)
        for j in range(N_SLAB):
            rows = slice(j * LANES, (j + 1) * LANES)
            cat_ref[j, l * LANES:(l + 1) * LANES, :SLAB_STATES] = expand(cf_re[rows])
            cat_ref[j, l * LANES:(l + 1) * LANES, SLAB_STATES:] = expand(cf_im[rows])
    for j in range(N_SLAB):
        for rho in range(t1):
            for c in range(2):
                lag = t1 - 2 - rho + c
                blk = kd[lag][j] if lag >= 0 else jnp.zeros((LANES, LANES), BF16)
                strip_ref[j, rho * LANES:(rho + 1) * LANES, c * LANES:(c + 1) * LANES] = blk

    f_re, f_im = af_ref[0], af_ref[1]
    base_re, base_im = f_re, f_im
    for _ in range(t1 - 1):
        base_re, base_im = _cmul(base_re, base_im, f_re, f_im)
    row = lax.broadcasted_iota(jnp.int32, (SUBLANES, N_STATES), 0)
    pws = [(base_re, base_im)]
    for _ in range(SUBLANES - 1):
        pws.append(_cmul(pws[-1][0], pws[-1][1], base_re, base_im))
    for n, k in enumerate((1, 2, 4)):
        for part in range(2):
            ak_ref[2 * n + part] = jnp.where(row >= k, jnp.broadcast_to(pws[k - 1][part], row.shape), 0.0)
    for part in range(2):
        acc = jnp.zeros(row.shape, F32)
        for r in range(SUBLANES):
            acc = jnp.where(row == r, jnp.broadcast_to(pws[r][part], row.shape), acc)
        tr_ref[part] = acc


def _ssm_tables_call(params, a_flat, *, name):
    t1 = SSM_CHUNK
    return pl.pallas_call(
        _ssm_tables_kernel,
        out_shape=[jax.ShapeDtypeStruct((N_SLAB, t1 * LANES, 2 * SLAB_STATES), BF16),
                   jax.ShapeDtypeStruct((N_SLAB, t1 * LANES, 2 * SLAB_STATES), BF16),
                   jax.ShapeDtypeStruct((N_SLAB, t1 * LANES, 2 * LANES), BF16),
                   jax.ShapeDtypeStruct((6, SUBLANES, N_STATES), F32),
                   jax.ShapeDtypeStruct((2, SUBLANES, N_STATES), F32)],
        compiler_params=pltpu.CompilerParams(vmem_limit_bytes=VMEM_LIMIT_BYTES),
        name=name,
    )(params, a_flat)


def _ssm_table_inputs(lam_re, lam_im, log_dt, b_re, b_im, c_re, c_im):
    dt = jnp.exp(log_dt)[:, None]
    mag = jnp.exp(lam_re * dt)
    a_re = mag * jnp.cos(lam_im * dt)
    a_im = mag * jnp.sin(lam_im * dt)
    rep = lambda x: jnp.repeat(x, SSM_GROUP, axis=0)
    flat = lambda x: x.reshape(SSM_WIDTH, SSM_STATE)
    params = jnp.stack([rep(a_re), rep(a_im), rep(lam_re), rep(lam_im),
                        flat(jnp.swapaxes(b_re, 1, 2)), flat(jnp.swapaxes(b_im, 1, 2)), flat(c_re), flat(c_im)])
    params = jnp.concatenate([params, params], axis=-1)
    a_flat = jnp.stack([a_re.reshape(1, N_STATES), a_im.reshape(1, N_STATES)])
    return params, a_flat


def _glu_rms(y, gluw_ref, glub_ref, g_ref):
    g = jax.nn.gelu(y).astype(BF16)
    zz = jnp.dot(g, gluw_ref[...], preferred_element_type=F32) + glub_ref[...]
    o = zz[:, :SSM_WIDTH] * jax.nn.sigmoid(zz[:, SSM_WIDTH:])
    return _rms_norm(o, g_ref[...])


def _ssm_kernel(u_ref, wst_ref, cat_ref, strip_ref, ak_ref, tr_ref, d_ref, gluw_ref, glub_ref, g_ref,
                o_ref, hre_ref, him_ref,
                ucb_ref, sre_ref, sim_ref, hcr_ref, hci_ref):
    t1 = SSM_CHUNK
    nc = SSM_ROWS // t1
    i = pl.program_id(1)

    @pl.when(i == 0)
    def _():
        hcr_ref[...] = jnp.zeros_like(hcr_ref)
        hci_ref[...] = jnp.zeros_like(hci_ref)

    for j in range(N_SLAB):
        for s in range(t1):
            ucb_ref[j, :, s * LANES:(s + 1) * LANES] = u_ref[j, pl.ds(s, nc, stride=t1), :].astype(BF16)

    for j in range(N_SLAB):
        st = jnp.dot(ucb_ref[j], wst_ref[j], preferred_element_type=F32)
        sre_ref[:, j * SLAB_STATES:(j + 1) * SLAB_STATES] = st[:, :SLAB_STATES]
        sim_ref[:, j * SLAB_STATES:(j + 1) * SLAB_STATES] = st[:, SLAB_STATES:]

    row = lax.broadcasted_iota(jnp.int32, (SUBLANES, SLAB_STATES), 0)
    for j in range(N_SLAB):
        cols = pl.ds(j * SLAB_STATES, SLAB_STATES)

        def body(gi, carry, cols=cols):
            hr, hi = carry
            r0 = pl.multiple_of(gi * SUBLANES, SUBLANES)
            xr = sre_ref[pl.ds(r0, SUBLANES), cols]
            xi = sim_ref[pl.ds(r0, SUBLANES), cols]
            for n, k in enumerate((1, 2, 4)):
                akr = ak_ref[2 * n, :, cols]
                aki = ak_ref[2 * n + 1, :, cols]
                sr = pltpu.roll(xr, k, 0)
                si = pltpu.roll(xi, k, 0)
                xr, xi = xr + akr * sr - aki * si, xi + akr * si + aki * sr
            tr = tr_ref[0, :, cols]
            ti = tr_ref[1, :, cols]
            er = xr + tr * hr - ti * hi
            ei = xi + tr * hi + ti * hr
            sre_ref[pl.ds(r0, SUBLANES), cols] = jnp.where(row >= 1, pltpu.roll(er, 1, 0), hr)
            sim_ref[pl.ds(r0, SUBLANES), cols] = jnp.where(row >= 1, pltpu.roll(ei, 1, 0), hi)
            last = SUBLANES - 1
            return (jnp.broadcast_to(er[last:last + 1], er.shape), jnp.broadcast_to(ei[last:last + 1], ei.shape))

        hr, hi = lax.fori_loop(0, nc // SUBLANES, body, (hcr_ref[:, cols], hci_ref[:, cols]), unroll=True)
        hcr_ref[:, cols] = hr
        hci_ref[:, cols] = hi

    hre_ref[0] = hcr_ref[...]
    him_ref[0] = hci_ref[...]

    ys = []
    for j in range(N_SLAB):
        cols = slice(j * SLAB_STATES, (j + 1) * SLAB_STATES)
        hprev = jnp.concatenate([sre_ref[:, cols], sim_ref[:, cols]], axis=1).astype(BF16)
        y = lax.dot_general(hprev, cat_ref[j], (((1,), (1,)), ((), ())), preferred_element_type=F32)
        parts = []
        for p2 in range(t1 // 2):
            kk = (2 * p2 + 2) * LANES
            yi = jnp.dot(ucb_ref[j, :, :kk], strip_ref[j, (t1 - 2 - 2 * p2) * LANES:, :],
                         preferred_element_type=F32)
            parts.append(y[:, 2 * p2 * LANES:(2 * p2 + 2) * LANES] + yi)
        ys.append(jnp.concatenate(parts, axis=1))

    for t in range(t1):
        y = jnp.concatenate([ys[j][:, t * LANES:(t + 1) * LANES] + d_ref[j] * u_ref[j, pl.ds(t, nc, stride=t1), :]
                             for j in range(N_SLAB)], axis=1)
        n = _glu_rms(y, gluw_ref, glub_ref, g_ref)
        for j in range(N_SLAB):
            o_ref[j, pl.ds(t, nc, stride=t1), :] = n[:, j * LANES:(j + 1) * LANES]


def _ssm_prompt_call(u, tab, d_t, gluw, glub, g, *, batch, seq, name):
    t1 = SSM_CHUNK
    nc = SSM_ROWS // t1
    steps = seq // SSM_ROWS
    blk = pl.BlockSpec((N_SLAB, SSM_ROWS, LANES), lambda b, i: (0, b * steps + i, 0))
    st = pl.BlockSpec((1, SUBLANES, N_STATES), lambda b, i: (b, 0, 0))
    consts = list(tab) + [d_t, gluw, glub, g]
    out, hre, him = pl.pallas_call(
        _ssm_kernel,
        grid=(batch, steps),
        in_specs=[blk] + [_resident(c.shape, 2) for c in consts],
        out_specs=[blk, st, st],
        out_shape=[jax.ShapeDtypeStruct(u.shape, F32),
                   jax.ShapeDtypeStruct((batch, SUBLANES, N_STATES), F32),
                   jax.ShapeDtypeStruct((batch, SUBLANES, N_STATES), F32)],
        scratch_shapes=[pltpu.VMEM((N_SLAB, nc, t1 * LANES), BF16),
                        pltpu.VMEM((nc, N_STATES), F32),
                        pltpu.VMEM((nc, N_STATES), F32),
                        pltpu.VMEM((SUBLANES, N_STATES), F32),
                        pltpu.VMEM((SUBLANES, N_STATES), F32)],
        compiler_params=pltpu.CompilerParams(dimension_semantics=("arbitrary", "arbitrary"),
                                             vmem_limit_bytes=VMEM_LIMIT_BYTES),
        name=name,
    )(u, *consts)
    return out, hre[:, 0], him[:, 0]


def _ssm_sample_kernel(u_ref, h0r_ref, h0i_ref, wst_ref, cat_ref, kd_ref, a_ref, d_ref, gluw_ref, glub_ref, g_ref,
                       o_ref, hr_ref, hi_ref):
    ys = []
    for j in range(N_SLAB):
        cols = slice(j * SLAB_STATES, (j + 1) * SLAB_STATES)
        uf = u_ref[j]
        ub = uf.astype(BF16)
        st = jnp.dot(ub, wst_ref[j], preferred_element_type=F32)
        h0r = h0r_ref[:, cols]
        h0i = h0i_ref[:, cols]
        ar = a_ref[0, :, cols]
        ai = a_ref[1, :, cols]
        hr_ref[:, cols] = ar * h0r - ai * h0i + st[:, :SLAB_STATES]
        hi_ref[:, cols] = ar * h0i + ai * h0r + st[:, SLAB_STATES:]
        hcat = jnp.concatenate([h0r, h0i], axis=1).astype(BF16)
        y = lax.dot_general(hcat, cat_ref[j], (((1,), (1,)), ((), ())), preferred_element_type=F32)
        y = y + jnp.dot(ub, kd_ref[j], preferred_element_type=F32)
        ys.append(y + d_ref[j] * uf)
    n = _glu_rms(jnp.concatenate(ys, axis=1), gluw_ref, glub_ref, g_ref)
    for j in range(N_SLAB):
        o_ref[j] = n[:, j * LANES:(j + 1) * LANES]


def _ssm_sample_call(u, h0r, h0i, tab, a_flat, d_t, gluw, glub, g, *, name):
    n_seq = u.shape[1]
    wst, cat, strip = tab[:3]
    last = SSM_CHUNK - 1
    whole = lambda a: pl.BlockSpec(a.shape, lambda i: (0,) * len(a.shape))
    outs = [jax.ShapeDtypeStruct(u.shape, F32),
            jax.ShapeDtypeStruct((n_seq, N_STATES), F32),
            jax.ShapeDtypeStruct((n_seq, N_STATES), F32)]
    return pl.pallas_call(
        _ssm_sample_kernel,
        grid=(1,),
        in_specs=[whole(u), whole(h0r), whole(h0i),
                  pl.BlockSpec((N_SLAB, LANES, 2 * SLAB_STATES), lambda i: (0, last, 0)),
                  pl.BlockSpec((N_SLAB, LANES, 2 * SLAB_STATES), lambda i: (0, 0, 0)),
                  pl.BlockSpec((N_SLAB, LANES, LANES), lambda i: (0, last, 1)),
                  whole(a_flat), whole(d_t), whole(gluw), whole(glub), whole(g)],
        out_specs=[whole(o) for o in outs],
        out_shape=outs,
        compiler_params=pltpu.CompilerParams(dimension_semantics=("arbitrary",),
                                             vmem_limit_bytes=VMEM_LIMIT_BYTES),
        name=name,
    )(u, h0r, h0i, wst, cat, strip, a_flat, d_t, gluw, glub, g)


def _attn_kernel(sink_ref, q_ref, kvc_ref, kvp_ref, g_ref, o_ref):
    i = pl.program_id(1)
    kj = lax.broadcasted_iota(jnp.int32, (2 * WINDOW, WINDOW), 0)
    qi = lax.broadcasted_iota(jnp.int32, (2 * WINDOW, WINDOW), 1)
    band = (kj >= qi) & (kj <= qi + WINDOW)
    low = qi < HEAD_DIM
    for sb in range(ATTN_BLOCKS):
        blk = slice(sb * WINDOW, (sb + 1) * WINDOW)
        kc = kvc_ref[blk, :]
        if sb == 0:
            kp = kvp_ref[...]
            valid = band & ((i > 0) | (kj >= WINDOW))
        else:
            kp = kvc_ref[(sb - 1) * WINDOW:sb * WINDOW, :]
            valid = band
        valid2 = jnp.concatenate([valid, valid], axis=1)
        kcat = jnp.concatenate([kp[:, :KV_WIDTH], kc[:, :KV_WIDTH]], axis=0) * ATTN_SCALE
        swap = pltpu.roll(kcat, HEAD_DIM, 1)
        zero = jnp.zeros_like(kcat)
        k_sel = [[jnp.where(low, kcat, zero).astype(BF16), jnp.where(low, zero, swap).astype(BF16)],
                 [jnp.where(low, swap, zero).astype(BF16), jnp.where(low, zero, kcat).astype(BF16)]]
        v_t = jnp.concatenate([kp[:, KV_WIDTH:], kc[:, KV_WIDTH:]], axis=0).T.astype(BF16)
        parts = []
        for kvh in range(N_KV_HEADS):
            q_pair = jnp.concatenate([q_ref[blk, (2 * kvh) * LANES:(2 * kvh + 1) * LANES],
                                      q_ref[blk, (2 * kvh + 1) * LANES:(2 * kvh + 2) * LANES]], axis=0)
            for parity in range(2):
                s_t = lax.dot_general(k_sel[kvh][parity], q_pair, (((1,), (1,)), ((), ())),
                                      preferred_element_type=F32)
                parts.append(jnp.where(valid2, s_t, MASKED))
        s_all = jnp.concatenate(parts, axis=1)
        sink = sink_ref[...]
        m = jnp.maximum(jnp.max(s_all, axis=0, keepdims=True), sink)
        p_all = jnp.exp(s_all - m)
        inv_den = 1.0 / (jnp.sum(p_all, axis=0, keepdims=True) + jnp.exp(sink - m))
        p_all = p_all.astype(BF16)
        half = Q_PER_KV * WINDOW
        o_kv = [jnp.dot(v_t[kvh * HEAD_DIM:(kvh + 1) * HEAD_DIM], p_all[:, kvh * half:(kvh + 1) * half],
                        preferred_element_type=F32) * inv_den[:, kvh * half:(kvh + 1) * half]
                for kvh in range(N_KV_HEADS)]
        tiles = []
        for pr in range(Q_PER_KV):
            lanes = slice(ATTN_HEAD_ORDER.index(pr) * WINDOW, (ATTN_HEAD_ORDER.index(pr) + 1) * WINDOW)
            tiles.append(jnp.concatenate([o_kv[0][:, lanes], o_kv[1][:, lanes]], axis=0))
        ss = None
        for tile in tiles:
            t = jnp.sum(tile * tile, axis=0, keepdims=True)
            ss = t if ss is None else ss + t
        inv = lax.rsqrt(ss / ATTN_WIDTH + RMS_EPS)
        for pr, tile in enumerate(tiles):
            o_ref[blk, pr * LANES:(pr + 1) * LANES] = ((tile * inv).T * g_ref[pr]).astype(o_ref.dtype)


def _attn_prompt_call(q, kv, sinks, g_perm, *, batch, seq, name):
    rows = ATTN_BLOCKS * WINDOW
    nb = seq // rows
    M = kv.shape[0]
    cur = lambda b, i: (b * nb + i, 0)
    prev = lambda b, i: ((b * nb + i) * ATTN_BLOCKS - jnp.minimum(i, 1), 0)
    return pl.pallas_call(
        _attn_kernel,
        grid=(batch, nb),
        in_specs=[_resident(sinks.shape, 2),
                  pl.BlockSpec((rows, ATTN_WIDTH), cur),
                  pl.BlockSpec((rows, 2 * KV_WIDTH), cur),
                  pl.BlockSpec((WINDOW, 2 * KV_WIDTH), prev),
                  _resident(g_perm.shape, 2)],
        out_specs=pl.BlockSpec((rows, ATTN_WIDTH), cur),
        out_shape=jax.ShapeDtypeStruct((M, ATTN_WIDTH), BF16),
        compiler_params=pltpu.CompilerParams(dimension_semantics=("arbitrary", "arbitrary"),
                                             vmem_limit_bytes=VMEM_LIMIT_BYTES),
        name=name,
    )(sinks, q, kv, kv, g_perm)


SEQ_PER_STEP = LANES // N_HEADS


def _attn_sample_kernel(q_ref, kv_ref, ck_ref, cv_ref, sink_ref, g_ref, own_ref, o_ref, ko_ref, vo_ref):
    nrow = SEQ_PER_STEP * N_HEADS
    dims_nt = (((1,), (1,)), ((), ()))
    qb = q_ref[...].astype(BF16)
    k_new = kv_ref[:, :KV_WIDTH]
    v_new = kv_ref[:, KV_WIDTH:]
    row_seq = lax.broadcasted_iota(jnp.int32, (nrow, LANES), 0) // N_HEADS
    lane_seq = lax.broadcasted_iota(jnp.int32, (nrow, LANES), 1) // N_HEADS
    row_id = lax.broadcasted_iota(jnp.int32, (nrow, LANES), 0)
    lane_id = lax.broadcasted_iota(jnp.int32, (nrow, LANES), 1)

    s_t = None
    for n in range(SEQ_PER_STEP):
        qn = jnp.where(row_seq == n, qb, jnp.zeros_like(qb))
        part = lax.dot_general(ck_ref[n].astype(BF16), qn, dims_nt, preferred_element_type=F32)
        s_t = part if s_t is None else s_t + part
    s_t = s_t * ATTN_SCALE
    s_cross = lax.dot_general(k_new.astype(BF16), qb, dims_nt, preferred_element_type=F32)
    own_seq = (lax.broadcasted_iota(jnp.int32, s_cross.shape, 0)
               == lax.broadcasted_iota(jnp.int32, s_cross.shape, 1) // N_HEADS)
    s_new = jnp.sum(jnp.where(own_seq, s_cross, 0.0), axis=0, keepdims=True) * ATTN_SCALE
    sink = sink_ref[...]
    m = jnp.maximum(jnp.maximum(jnp.max(s_t, axis=0, keepdims=True), s_new), sink)
    p_t = jnp.exp(s_t - m)
    p_new = jnp.exp(s_new - m)
    inv = 1.0 / (jnp.sum(p_t, axis=0, keepdims=True) + p_new + jnp.exp(sink - m))
    p = (p_t * inv).T.astype(BF16)

    o = None
    for n in range(SEQ_PER_STEP):
        pn = jnp.where(row_seq == n, p, jnp.zeros_like(p))
        part = jnp.dot(pn, cv_ref[n].astype(BF16), preferred_element_type=F32)
        o = part if o is None else o + part
    p_diag = jnp.where(row_id == lane_id, jnp.broadcast_to(p_new * inv, (nrow, LANES)), 0.0).astype(BF16)
    p_col = jnp.dot(p_diag, jnp.ones((LANES, LANES), BF16), preferred_element_type=F32)
    pick = (row_seq[:, :SEQ_PER_STEP] == lax.broadcasted_iota(jnp.int32, (nrow, SEQ_PER_STEP), 1)).astype(BF16)
    v_rows = jnp.dot(pick, v_new.astype(BF16), preferred_element_type=F32)
    o = o + p_col * v_rows

    hi = lax.Precision.HIGHEST
    row_sums = jnp.dot(o * o * own_ref[...], jnp.ones((LANES, LANES), F32), precision=hi, preferred_element_type=F32)
    ss = jnp.dot((row_seq == lane_seq).astype(F32), row_sums, precision=hi, preferred_element_type=F32)
    o_ref[...] = o * lax.rsqrt(ss / ATTN_WIDTH + RMS_EPS) * g_ref[...]

    for n in range(SEQ_PER_STEP):
        ko_ref[n, pl.ds(0, WINDOW - 1), :] = ck_ref[n, pl.ds(1, WINDOW - 1), :]
        ko_ref[n, pl.ds(WINDOW - 1, 1), :] = k_new[n:n + 1]
        vo_ref[n, pl.ds(0, WINDOW - 1), :] = cv_ref[n, pl.ds(1, WINDOW - 1), :]
        vo_ref[n, pl.ds(WINDOW - 1, 1), :] = v_new[n:n + 1]


def _attn_sample_call(q, kv, ck, cv, layer, sink_row, g_rows, own_rows, *, name):
    n_seq = kv.shape[0]
    assert n_seq % SEQ_PER_STEP == 0
    nrow = SEQ_PER_STEP * N_HEADS
    cblk = pl.BlockSpec((SEQ_PER_STEP, WINDOW, 2 * HEAD_DIM), lambda i: (i, 0, 0))
    cin = pl.BlockSpec((None, SEQ_PER_STEP, WINDOW, 2 * HEAD_DIM), lambda i: (layer, i, 0, 0))
    qblk = pl.BlockSpec((nrow, LANES), lambda i: (i, 0))
    whole = lambda a: pl.BlockSpec(a.shape, lambda i: (0,) * len(a.shape))
    return pl.pallas_call(
        _attn_sample_kernel,
        grid=(n_seq // SEQ_PER_STEP,),
        in_specs=[qblk, pl.BlockSpec((SEQ_PER_STEP, 2 * KV_WIDTH), lambda i: (i, 0)), cin, cin,
                  whole(sink_row), whole(g_rows), whole(own_rows)],
        out_specs=[qblk, cblk, cblk],
        out_shape=[jax.ShapeDtypeStruct(q.shape, F32),
                   jax.ShapeDtypeStruct(ck.shape[1:], F32),
                   jax.ShapeDtypeStruct(cv.shape[1:], F32)],
        compiler_params=pltpu.CompilerParams(dimension_semantics=("arbitrary",),
                                             vmem_limit_bytes=VMEM_LIMIT_BYTES),
        name=name,
    )(q, kv, ck, cv, sink_row, g_rows, own_rows)


def _expand_heads(a):
    lead = a.shape[:-1]
    a = a.reshape(lead + (N_KV_HEADS, Q_PER_KV, 1, HEAD_DIM))
    sel = jnp.eye(N_KV_HEADS, dtype=a.dtype).reshape(N_KV_HEADS, 1, N_KV_HEADS, 1)
    return (a * sel).reshape(lead + (Q_EXP,))


def _pair_heads(a):
    lead = a.shape[:-1]
    a = a.reshape(lead + (N_KV_HEADS, Q_PER_KV, HEAD_DIM))
    return jnp.swapaxes(a, -3, -2).reshape(lead + (ATTN_WIDTH,))


def _prep_weights(ln_g, ln_b, ffn1_w_in, ffn1_w_out, ffn2_w_in, ffn2_w_out, w_in, ssm_lam_re, ssm_lam_im, ssm_log_dt,
                  ssm_b_re, ssm_b_im, ssm_c_re, ssm_c_im, ssm_d, glu_w, glu_b, attn_sinks, g_ssm_out, g_attn_out,
                  w_out):
    w = dict(ffn1=[], ffn2=[], ln=[], w_in=[], w_in_exp=[], wo_s=[], wo_a=[], tab=[], a_flat=[], d_1=[],
             gluw=[], glub=[], g_ssm=[], sinks=[], g_pair=[], sink_rows=[], g_rows=[], wo_a_exp=[], own_rows=None)
    w['own_rows'] = jnp.tile(_expand_heads(jnp.ones((ATTN_WIDTH,), F32)).reshape(N_HEADS, LANES), (SEQ_PER_STEP, 1))
    ffn1_in_b, ffn1_out_b = ffn1_w_in.astype(BF16), ffn1_w_out.astype(BF16)
    ffn2_in_b, ffn2_out_b = ffn2_w_in.astype(BF16), ffn2_w_out.astype(BF16)
    for l in range(DEPTH):
        w['ffn1'].append((ffn1_in_b, ffn1_out_b, l))
        w['ffn2'].append((ffn2_in_b, ffn2_out_b, l))
        w['ln'].append([(ln_g[l, i][None, :], ln_b[l, i][None, :]) for i in range(3)])
        o1, o2 = SSM_WIDTH, SSM_WIDTH + ATTN_WIDTH
        w['w_in'].append(w_in[l].astype(BF16))
        w['w_in_exp'].append(jnp.concatenate([w_in[l][:, :o1], _expand_heads(w_in[l][:, o1:o2]), w_in[l][:, o2:]],
                                             axis=1).astype(BF16))
        w['wo_s'].append(w_out[l, :SSM_WIDTH].astype(BF16))
        w['wo_a'].append(_pair_heads(w_out[l, SSM_WIDTH:].T).T.astype(BF16))
        params, a_flat = _ssm_table_inputs(ssm_lam_re[l], ssm_lam_im[l], ssm_log_dt[l], ssm_b_re[l], ssm_b_im[l],
                                           ssm_c_re[l], ssm_c_im[l])
        w['tab'].append(_ssm_tables_call(params, a_flat, name=f"ssm_tables_{l}"))
        w['a_flat'].append(a_flat)
        d_slab = ssm_d[l].reshape(N_SLAB, 1, LANES)
        w['d_1'].append(d_slab)
        w['gluw'].append(glu_w[l].astype(BF16))
        w['glub'].append(glu_b[l][None, :])
        w['g_ssm'].append(g_ssm_out[l][None, :])
        w['sink_rows'].append(jnp.tile(attn_sinks[l], SEQ_PER_STEP)[None, :])
        w['g_rows'].append(jnp.tile(_expand_heads(g_attn_out[l]).reshape(N_HEADS, LANES), (SEQ_PER_STEP, 1)))
        w['wo_a_exp'].append(_expand_heads(w_out[l, SSM_WIDTH:].T).T.astype(BF16))
        order = jnp.array([Q_PER_KV * kvh + o for kvh in range(N_KV_HEADS) for o in ATTN_HEAD_ORDER])
        w['sinks'].append(jnp.repeat(attn_sinks[l][order], WINDOW)[None, :])
        w['g_pair'].append(_pair_heads(g_attn_out[l]).reshape(Q_PER_KV, 1, LANES))
    return w


def _prompt_mixer(u, q, kv, l, w, batch, seq):
    ssm_n, hre, him = _ssm_prompt_call(u, w['tab'][l], w['d_1'][l], w['gluw'][l],
                                       w['glub'][l], w['g_ssm'][l], batch=batch, seq=seq, name=f"p_ssm_{l}")
    att_n = _attn_prompt_call(q, kv, w['sinks'][l], w['g_pair'][l], batch=batch, seq=seq, name=f"p_attn_{l}")
    kvw = kv.reshape(batch, seq, 2 * KV_WIDTH)[:, -WINDOW:].reshape(batch, WINDOW, 2, N_KV_HEADS, HEAD_DIM)
    return (ssm_n, att_n, hre.reshape(batch, N_SSM_GROUPS, SSM_STATE), him.reshape(batch, N_SSM_GROUPS, SSM_STATE),
            kvw[:, :, 0], kvw[:, :, 1])


def _sample_mixer(u, q, kv, l, w, h0_re, h0_im, k_buf, v_buf):
    n_seq = kv.shape[0]
    ssm_n, hre, him = _ssm_sample_call(u, h0_re.reshape(n_seq, N_STATES), h0_im.reshape(n_seq, N_STATES),
                                       w['tab'][l], w['a_flat'][l], w['d_1'][l], w['gluw'][l], w['glub'][l], w['g_ssm'][l],
                                       name=f"s_ssm_{l}")
    att, kn, vn = _attn_sample_call(q.reshape(n_seq * N_HEADS, LANES), kv, k_buf, v_buf, l,
                                    w['sink_rows'][l], w['g_rows'][l], w['own_rows'], name=f"s_attn_{l}")
    win = (n_seq, WINDOW, N_KV_HEADS, HEAD_DIM)
    return (ssm_n, att.reshape(n_seq, Q_EXP), hre.reshape(n_seq, N_SSM_GROUPS, SSM_STATE),
            him.reshape(n_seq, N_SSM_GROUPS, SSM_STATE), kn.reshape(win), vn.reshape(win))


def _trunk(x, h0_re, h0_im, k_buf, v_buf, w, *, tm, tag):
    Bn, L, _ = x.shape
    sample = k_buf is not None
    q_dtype = F32 if sample else BF16
    x = x.reshape(Bn * L, D_MODEL)
    ks, vs, hrs, his = [], [], [], []
    mix = None
    for l in range(DEPTH):
        if l > 0:
            x = _rows_call(x, w['ffn2'][l - 1], w['ln'][l - 1][2], mix=mix, tm=tm, name=f"{tag}_mix_ffn2_{l - 1}")[0]
        proj = w['w_in_exp'][l] if sample else w['w_in'][l]
        x, u, q, kv = _rows_call(x, w['ffn1'][l], w['ln'][l][0], proj=proj, tm=tm, q_dtype=q_dtype,
                                 name=f"{tag}_ffn1_{l}")
        if sample:
            ssm_n, att_n, hre, him, kn, vn = _sample_mixer(u, q, kv, l, w, h0_re[l], h0_im[l], k_buf, v_buf)
        else:
            ssm_n, att_n, hre, him, kn, vn = _prompt_mixer(u, q, kv, l, w, Bn, L)
        wo_a = w['wo_a_exp'][l] if sample else w['wo_a'][l]
        mix = (ssm_n, att_n, w['wo_s'][l], wo_a, w['ln'][l][1][0], w['ln'][l][1][1])
        ks.append(kn); vs.append(vn); hrs.append(hre); his.append(him)
    x = _rows_call(x, w['ffn2'][DEPTH - 1], w['ln'][DEPTH - 1][2], mix=mix, tm=tm, name=f"{tag}_mix_ffn2_{DEPTH - 1}")[0]
    return x.reshape(Bn, L, D_MODEL), jnp.stack(ks), jnp.stack(vs), jnp.stack(hrs), jnp.stack(his)


def kernel(x_prompt, x_sample, cache_k_win, cache_v_win, state_ssm_re, state_ssm_im, ln_g, ln_b, ffn1_w_in, ffn1_w_out, ffn2_w_in, ffn2_w_out, w_in, ssm_lam_re, ssm_lam_im, ssm_log_dt, ssm_b_re, ssm_b_im, ssm_c_re, ssm_c_im, ssm_d, glu_w, glu_b, attn_sinks, g_ssm_out, g_attn_out, w_out):
    w = _prep_weights(ln_g, ln_b, ffn1_w_in, ffn1_w_out, ffn2_w_in, ffn2_w_out, w_in, ssm_lam_re, ssm_lam_im,
                      ssm_log_dt, ssm_b_re, ssm_b_im, ssm_c_re, ssm_c_im, ssm_d, glu_w, glu_b, attn_sinks,
                      g_ssm_out, g_attn_out, w_out)
    y_prompt, kp, vp, hrp, hip = _trunk(x_prompt, None, None, None, None, w, tm=512, tag="p")
    n_seq = x_sample.shape[0]
    merged = (DEPTH, n_seq, WINDOW, N_KV_HEADS * HEAD_DIM)
    y_sample, ks_, vs_, hrs, his = _trunk(x_sample, state_ssm_re, state_ssm_im, cache_k_win.reshape(merged),
                                          cache_v_win.reshape(merged), w, tm=128, tag="s")
    return (y_prompt, y_sample, kp, vp, hrp, hip, ks_, vs_, hrs, his)
```

```python
import functools

import jax
import jax.numpy as jnp
from jax import lax
from jax.experimental import pallas as pl
from jax.experimental.pallas import tpu as pltpu

D_MODEL = 1024
DEPTH = 2
SSM_WIDTH = 512
SSM_GROUP = 16
N_SSM_GROUPS = 32
SSM_STATE = 64
N_STATES = N_SSM_GROUPS * SSM_STATE
ATTN_WIDTH = 512
HEAD_DIM = 64
N_HEADS = 8
N_KV_HEADS = 2
Q_PER_KV = 4
KV_WIDTH = 128
WINDOW = 128
ATTN_SCALE = HEAD_DIM ** -0.5
D_FF = 2816
ALPHA = (2.0 * DEPTH) ** 0.25
LN_EPS = 1e-5
RMS_EPS = 1e-6

ROW_GROUP = 256
LANES = 128
SUBLANES = 8
VMEM_LIMIT_BYTES = 56 * 1024 * 1024
N_SLAB = SSM_WIDTH // LANES
GROUPS_PER_SLAB = LANES // SSM_GROUP
SLAB_STATES = GROUPS_PER_SLAB * SSM_STATE
Q_EXP = N_HEADS * LANES
SSM_CHUNK = 8
SSM_ROWS = 2048
ATTN_BLOCKS = 4
ATTN_HEAD_ORDER = (0, 2, 1, 3)
MASKED = -1e30

F32 = jnp.float32
BF16 = jnp.bfloat16


def _layer_norm(r, g, b):
    mu = jnp.mean(r, axis=-1, keepdims=True)
    c = r - mu
    var = jnp.mean(c * c, axis=-1, keepdims=True)
    return c * lax.rsqrt(var + LN_EPS) * g + b


def _rms_norm(y, g):
    return y * lax.rsqrt(jnp.mean(y * y, axis=-1, keepdims=True) + RMS_EPS) * g


def _const(c, n_grid=1):
    if isinstance(c, tuple):
        arr, idx = c
        shape = (None,) + arr.shape[1:]
        index = (idx,) + (0,) * (arr.ndim - 1)
    else:
        arr, shape, index = c, c.shape, (0,) * c.ndim
    imap = (lambda i: index) if n_grid == 1 else (lambda i, j: index)
    return arr, pl.BlockSpec(shape, imap, pipeline_mode=pl.Buffered(1))


def _consts(cs, n_grid=1):
    arrs, specs = zip(*[_const(c, n_grid) for c in cs])
    return list(arrs), list(specs)


def _rows_kernel(*refs, has_mix, has_proj, n_sub):
    it = iter(refs)
    x_ref = next(it)
    if has_mix:
        ssm_ref, att_ref, wo_s_ref, wo_a_ref, gm_ref, bm_ref = (next(it) for _ in range(6))
    wgu_ref, wdn_ref, g_ref, b_ref = (next(it) for _ in range(4))
    if has_proj:
        wp_ref = next(it)
    o_ref = next(it)
    if has_proj:
        u_ref, q_ref, kv_ref = (next(it) for _ in range(3))

    tm = x_ref.shape[0]
    sub = tm // n_sub
    groups = [slice(r * sub, (r + 1) * sub) for r in range(n_sub)]
    xs = [x_ref[rows, :] for rows in groups]
    if has_mix:
        ms = []
        for rows in groups:
            ssm = jnp.concatenate([ssm_ref[j, rows, :] for j in range(N_SLAB)], axis=1).astype(BF16)
            att = att_ref[rows, :].astype(BF16)
            ms.append(jnp.dot(ssm, wo_s_ref[...], preferred_element_type=F32)
                      + jnp.dot(att, wo_a_ref[...], preferred_element_type=F32))
        xs = [_layer_norm(ALPHA * x + m, gm_ref[...], bm_ref[...]) for x, m in zip(xs, ms)]
    gus = [jnp.dot(x.astype(BF16), wgu_ref[...], preferred_element_type=F32) for x in xs]
    hs = [(gu[:, :D_FF] * jax.nn.sigmoid(gu[:, :D_FF]) * gu[:, D_FF:]).astype(BF16) for gu in gus]
    ys = [jnp.dot(h, wdn_ref[...], preferred_element_type=F32) for h in hs]
    xs = [_layer_norm(ALPHA * x + 0.5 * y, g_ref[...], b_ref[...]) for x, y in zip(xs, ys)]
    for rows, x in zip(groups, xs):
        o_ref[rows, :] = x
    if has_proj:
        q_cols = q_ref.shape[1]
        zs = [jnp.dot(x.astype(BF16), wp_ref[...], preferred_element_type=F32) for x in xs]
        for rows, z in zip(groups, zs):
            for j in range(N_SLAB):
                u_ref[j, rows, :] = z[:, j * LANES:(j + 1) * LANES]
            q_ref[rows, :] = z[:, SSM_WIDTH:SSM_WIDTH + q_cols].astype(q_ref.dtype)
            kv_ref[rows, :] = z[:, SSM_WIDTH + q_cols:]


def _rows_call(x, ffn, ln, mix=None, proj=None, *, tm, q_dtype=BF16, name):
    M = x.shape[0]
    assert M % tm == 0
    row = lambda w: pl.BlockSpec((tm, w), lambda i: (i, 0))
    slab = lambda n: pl.BlockSpec((n, tm, LANES), lambda i: (0, i, 0))
    args = [x]
    specs = [row(D_MODEL)]
    if mix is not None:
        ssm_n, att_n = mix[:2]
        arrs, cspecs = _consts(mix[2:])
        args += [ssm_n, att_n] + arrs
        specs += [slab(N_SLAB), row(att_n.shape[1])] + cspecs
    arrs, cspecs = _consts(list(ffn) + list(ln))
    args += arrs
    specs += cspecs
    out_shape = [jax.ShapeDtypeStruct((M, D_MODEL), F32)]
    out_specs = [row(D_MODEL)]
    if proj is not None:
        arr, cspec = _const(proj)
        args.append(arr)
        specs.append(cspec)
        q_cols = arr.shape[-1] - SSM_WIDTH - 2 * KV_WIDTH
        out_shape += [jax.ShapeDtypeStruct((N_SLAB, M, LANES), F32),
                      jax.ShapeDtypeStruct((M, q_cols), q_dtype),
                      jax.ShapeDtypeStruct((M, 2 * KV_WIDTH), F32)]
        out_specs += [slab(N_SLAB), row(q_cols), row(2 * KV_WIDTH)]
    return pl.pallas_call(
        functools.partial(_rows_kernel, has_mix=mix is not None, has_proj=proj is not None,
                          n_sub=max(1, tm // ROW_GROUP)),
        grid=(M // tm,),
        in_specs=specs,
        out_specs=out_specs,
        out_shape=out_shape,
        compiler_params=pltpu.CompilerParams(dimension_semantics=("arbitrary",),
                                             vmem_limit_bytes=VMEM_LIMIT_BYTES),
        name=name,
    )(*args)


def _cmul(xr, xi, yr, yi):
    return xr * yr - xi * yi, xr * yi + xi * yr


def _ssm_tables_kernel(p_ref, af_ref, wst_ref, cat_ref, strip_ref, ak_ref, tr_ref):
    t1 = SSM_CHUNK
    a_re, a_im, lam_re, lam_im, b_re, b_im, c_re, c_im = (p_ref[k] for k in range(8))
    num_re, num_im = a_re - 1.0, a_im
    den = lam_re * lam_re + lam_im * lam_im
    f_re = (num_re * lam_re + num_im * lam_im) / den
    f_im = (num_im * lam_re - num_re * lam_im) / den
    bb_re, bb_im = _cmul(f_re, f_im, b_re, b_im)

    def same_group(shape, row_div, col_div):
        r = lax.broadcasted_iota(jnp.int32, shape, 0) // row_div
        c = lax.broadcasted_iota(jnp.int32, shape, 1) // col_div
        return (r == c).astype(F32)

    m_state = same_group((LANES, SLAB_STATES), SSM_GROUP, SSM_STATE)
    m_chan = 0.5 * same_group((LANES, LANES), SSM_GROUP, SSM_GROUP)

    def expand(x):
        return (jnp.concatenate([x] * (SLAB_STATES // LANES), axis=1) * m_state).astype(BF16)

    def dot_nt(a, b):
        return lax.dot_general(a, b, (((1,), (1,)), ((), ())), precision=lax.Precision.HIGHEST,
                               preferred_element_type=F32)

    pw_re, pw_im = jnp.ones_like(a_re), jnp.zeros_like(a_re)
    kd = []
    for l in range(t1):
        ab_re, ab_im = _cmul(pw_re, pw_im, bb_re, bb_im)
        s = t1 - 1 - l
        kd_l = []
        for j in range(N_SLAB):
            rows = slice(j * LANES, (j + 1) * LANES)
            wst_ref[j, s * LANES:(s + 1) * LANES, :SLAB_STATES] = expand(ab_re[rows])
            wst_ref[j, s * LANES:(s + 1) * LANES, SLAB_STATES:] = expand(ab_im[rows])
            k = dot_nt(ab_re[rows], c_re[rows]) - dot_nt(ab_im[rows], c_im[rows])
            kd_l.append((k * m_chan).astype(BF16))
        kd.append(kd_l)
        pw_re, pw_im = _cmul(pw_re, pw_im, a_re, a_im)
        cf_re = c_re * pw_re - c_im * pw_im
        cf_im = -(c_re * pw_im + c_im * pw_re)
        for j in range(N_SLAB):
            rows = slice(j * LANES, (j + 1) * LANES)
            cat_ref[j, l * LANES:(l + 1) * LANES, :SLAB_STATES] = expand(cf_re[rows])
            cat_ref[j, l * LANES:(l + 1) * LANES, SLAB_STATES:] = expand(cf_im[rows])
    for j in range(N_SLAB):
        for rho in range(t1):
            for c in range(2):
                lag = t1 - 2 - rho + c
                blk = kd[lag][j] if lag >= 0 else jnp.zeros((LANES, LANES), BF16)
                strip_ref[j, rho * LANES:(rho + 1) * LANES, c * LANES:(c + 1) * LANES] = blk

    f_re, f_im = af_ref[0], af_ref[1]
    base_re, base_im = f_re, f_im
    for _ in range(t1 - 1):
        base_re, base_im = _cmul(base_re, base_im, f_re, f_im)
    row = lax.broadcasted_iota(jnp.int32, (SUBLANES, N_STATES), 0)
    pws = [(base_re, base_im)]
    for _ in range(SUBLANES - 1):
        pws.append(_cmul(pws[-1][0], pws[-1][1], base_re, base_im))
    for n, k in enumerate((1, 2, 4)):
        for part in range(2):
            ak_ref[2 * n + part] = jnp.where(row >= k, jnp.broadcast_to(pws[k - 1][part], row.shape), 0.0)
    for part in range(2):
        acc = jnp.zeros(row.shape, F32)
        for r in range(SUBLANES):
            acc = jnp.where(row == r, jnp.broadcast_to(pws[r][part], row.shape), acc)
        tr_ref[part] = acc


def _ssm_tables_call(params, a_flat):
    t1 = SSM_CHUNK
    shapes = [((N_SLAB, t1 * LANES, 2 * SLAB_STATES), BF16),
              ((N_SLAB, t1 * LANES, 2 * SLAB_STATES), BF16),
              ((N_SLAB, t1 * LANES, 2 * LANES), BF16),
              ((6, SUBLANES, N_STATES), F32),
              ((2, SUBLANES, N_STATES), F32)]
    per_layer = lambda shape: pl.BlockSpec((None,) + tuple(shape), lambda l: (l,) + (0,) * len(shape))
    return pl.pallas_call(
        _ssm_tables_kernel,
        grid=(DEPTH,),
        in_specs=[per_layer(params.shape[1:]), per_layer(a_flat.shape[1:])],
        out_specs=[per_layer(s) for s, _ in shapes],
        out_shape=[jax.ShapeDtypeStruct((DEPTH,) + s, d) for s, d in shapes],
        compiler_params=pltpu.CompilerParams(dimension_semantics=("arbitrary",),
                                             vmem_limit_bytes=VMEM_LIMIT_BYTES),
        name="ssm_tables",
    )(params, a_flat)


def _ssm_table_inputs(lam_re, lam_im, log_dt, b_re, b_im, c_re, c_im):
    dt = jnp.exp(log_dt)[..., None]
    mag = jnp.exp(lam_re * dt)
    a_re = mag * jnp.cos(lam_im * dt)
    a_im = mag * jnp.sin(lam_im * dt)
    rep = lambda x: jnp.repeat(x, SSM_GROUP, axis=1)
    flat = lambda x: x.reshape(DEPTH, SSM_WIDTH, SSM_STATE)
    params = jnp.stack([rep(a_re), rep(a_im), rep(lam_re), rep(lam_im),
                        flat(jnp.swapaxes(b_re, 2, 3)), flat(jnp.swapaxes(b_im, 2, 3)), flat(c_re), flat(c_im)],
                       axis=1)
    params = jnp.concatenate([params, params], axis=-1)
    a_flat = jnp.stack([a_re.reshape(DEPTH, 1, N_STATES), a_im.reshape(DEPTH, 1, N_STATES)], axis=1)
    return params, a_flat


def _glu_rms(y, gluw_ref, glub_ref, g_ref):
    g = jax.nn.gelu(y).astype(BF16)
    zz = jnp.dot(g, gluw_ref[...], preferred_element_type=F32) + glub_ref[...]
    o = zz[:, :SSM_WIDTH] * jax.nn.sigmoid(zz[:, SSM_WIDTH:])
    return _rms_norm(o, g_ref[...])


def _ssm_kernel(u_ref, wst_ref, cat_ref, strip_ref, ak_ref, tr_ref, d_ref, gluw_ref, glub_ref, g_ref,
                o_ref, hre_ref, him_ref,
                ucb_ref, sre_ref, sim_ref, hcr_ref, hci_ref):
    t1 = SSM_CHUNK
    nc = SSM_ROWS // t1
    i = pl.program_id(1)

    @pl.when(i == 0)
    def _():
        hcr_ref[...] = jnp.zeros_like(hcr_ref)
        hci_ref[...] = jnp.zeros_like(hci_ref)

    for j in range(N_SLAB):
        for s in range(t1):
            ucb_ref[j, :, s * LANES:(s + 1) * LANES] = u_ref[j, pl.ds(s, nc, stride=t1), :].astype(BF16)

    for j in range(N_SLAB):
        st = jnp.dot(ucb_ref[j], wst_ref[j], preferred_element_type=F32)
        sre_ref[:, j * SLAB_STATES:(j + 1) * SLAB_STATES] = st[:, :SLAB_STATES]
        sim_ref[:, j * SLAB_STATES:(j + 1) * SLAB_STATES] = st[:, SLAB_STATES:]

    row = lax.broadcasted_iota(jnp.int32, (SUBLANES, SLAB_STATES), 0)
    for j in range(N_SLAB):
        cols = pl.ds(j * SLAB_STATES, SLAB_STATES)

        def body(gi, carry, cols=cols):
            hr, hi = carry
            r0 = pl.multiple_of(gi * SUBLANES, SUBLANES)
            xr = sre_ref[pl.ds(r0, SUBLANES), cols]
            xi = sim_ref[pl.ds(r0, SUBLANES), cols]
            for n, k in enumerate((1, 2, 4)):
                akr = ak_ref[2 * n, :, cols]
                aki = ak_ref[2 * n + 1, :, cols]
                sr = pltpu.roll(xr, k, 0)
                si = pltpu.roll(xi, k, 0)
                xr, xi = xr + akr * sr - aki * si, xi + akr * si + aki * sr
            tr = tr_ref[0, :, cols]
            ti = tr_ref[1, :, cols]
            er = xr + tr * hr - ti * hi
            ei = xi + tr * hi + ti * hr
            sre_ref[pl.ds(r0, SUBLANES), cols] = jnp.where(row >= 1, pltpu.roll(er, 1, 0), hr)
            sim_ref[pl.ds(r0, SUBLANES), cols] = jnp.where(row >= 1, pltpu.roll(ei, 1, 0), hi)
            last = SUBLANES - 1
            return (jnp.broadcast_to(er[last:last + 1], er.shape), jnp.broadcast_to(ei[last:last + 1], ei.shape))

        hr, hi = lax.fori_loop(0, nc // SUBLANES, body, (hcr_ref[:, cols], hci_ref[:, cols]), unroll=True)
        hcr_ref[:, cols] = hr
        hci_ref[:, cols] = hi

    hre_ref[0] = hcr_ref[...]
    him_ref[0] = hci_ref[...]

    ys = []
    for j in range(N_SLAB):
        cols = slice(j * SLAB_STATES, (j + 1) * SLAB_STATES)
        hprev = jnp.concatenate([sre_ref[:, cols], sim_ref[:, cols]], axis=1).astype(BF16)
        y = lax.dot_general(hprev, cat_ref[j], (((1,), (1,)), ((), ())), preferred_element_type=F32)
        parts = []
        for p2 in range(t1 // 2):
            kk = (2 * p2 + 2) * LANES
            yi = jnp.dot(ucb_ref[j, :, :kk], strip_ref[j, (t1 - 2 - 2 * p2) * LANES:, :],
                         preferred_element_type=F32)
            parts.append(y[:, 2 * p2 * LANES:(2 * p2 + 2) * LANES] + yi)
        ys.append(jnp.concatenate(parts, axis=1))

    for t in range(t1):
        y = jnp.concatenate([ys[j][:, t * LANES:(t + 1) * LANES] + d_ref[j] * u_ref[j, pl.ds(t, nc, stride=t1), :]
                             for j in range(N_SLAB)], axis=1)
        n = _glu_rms(y, gluw_ref, glub_ref, g_ref)
        for j in range(N_SLAB):
            o_ref[j, pl.ds(t, nc, stride=t1), :] = n[:, j * LANES:(j + 1) * LANES]


def _ssm_prompt_call(u, tab, d_t, gluw, glub, g, *, batch, seq, name):
    t1 = SSM_CHUNK
    nc = SSM_ROWS // t1
    steps = seq // SSM_ROWS
    blk = pl.BlockSpec((N_SLAB, SSM_ROWS, LANES), lambda b, i: (0, b * steps + i, 0))
    st = pl.BlockSpec((1, SUBLANES, N_STATES), lambda b, i: (b, 0, 0))
    consts, cspecs = _consts(list(tab) + [d_t, gluw, glub, g], 2)
    out, hre, him = pl.pallas_call(
        _ssm_kernel,
        grid=(batch, steps),
        in_specs=[blk] + cspecs,
        out_specs=[blk, st, st],
        out_shape=[jax.ShapeDtypeStruct(u.shape, F32),
                   jax.ShapeDtypeStruct((batch, SUBLANES, N_STATES), F32),
                   jax.ShapeDtypeStruct((batch, SUBLANES, N_STATES), F32)],
        scratch_shapes=[pltpu.VMEM((N_SLAB, nc, t1 * LANES), BF16),
                        pltpu.VMEM((nc, N_STATES), F32),
                        pltpu.VMEM((nc, N_STATES), F32),
                        pltpu.VMEM((SUBLANES, N_STATES), F32),
                        pltpu.VMEM((SUBLANES, N_STATES), F32)],
        compiler_params=pltpu.CompilerParams(dimension_semantics=("arbitrary", "arbitrary"),
                                             vmem_limit_bytes=VMEM_LIMIT_BYTES),
        name=name,
    )(u, *consts)
    return out, hre[:, 0], him[:, 0]


def _ssm_sample_kernel(u_ref, h0r_ref, h0i_ref, wst_ref, cat_ref, kd_ref, a_ref, d_ref, gluw_ref, glub_ref, g_ref,
                       o_ref, hr_ref, hi_ref):
    ys = []
    for j in range(N_SLAB):
        cols = slice(j * SLAB_STATES, (j + 1) * SLAB_STATES)
        uf = u_ref[j]
        ub = uf.astype(BF16)
        st = jnp.dot(ub, wst_ref[j], preferred_element_type=F32)
        h0r = h0r_ref[:, cols]
        h0i = h0i_ref[:, cols]
        ar = a_ref[0, :, cols]
        ai = a_ref[1, :, cols]
        hr_ref[:, cols] = ar * h0r - ai * h0i + st[:, :SLAB_STATES]
        hi_ref[:, cols] = ar * h0i + ai * h0r + st[:, SLAB_STATES:]
        hcat = jnp.concatenate([h0r, h0i], axis=1).astype(BF16)
        y = lax.dot_general(hcat, cat_ref[j], (((1,), (1,)), ((), ())), preferred_element_type=F32)
        y = y + jnp.dot(ub, kd_ref[j], preferred_element_type=F32)
        ys.append(y + d_ref[j] * uf)
    n = _glu_rms(jnp.concatenate(ys, axis=1), gluw_ref, glub_ref, g_ref)
    for j in range(N_SLAB):
        o_ref[j] = n[:, j * LANES:(j + 1) * LANES]


def _ssm_sample_call(u, h0r, h0i, tab, a_flat, d_t, gluw, glub, g, *, name):
    n_seq = u.shape[1]
    (wst, layer), (cat, _), (strip, _) = tab[:3]
    last = SSM_CHUNK - 1
    whole = lambda a: pl.BlockSpec(a.shape, lambda i: (0,) * len(a.shape))
    outs = [jax.ShapeDtypeStruct(u.shape, F32),
            jax.ShapeDtypeStruct((n_seq, N_STATES), F32),
            jax.ShapeDtypeStruct((n_seq, N_STATES), F32)]
    consts, cspecs = _consts([a_flat, d_t, gluw, glub, g])
    return pl.pallas_call(
        _ssm_sample_kernel,
        grid=(1,),
        in_specs=[whole(u), whole(h0r), whole(h0i),
                  pl.BlockSpec((None, N_SLAB, LANES, 2 * SLAB_STATES), lambda i: (layer, 0, last, 0)),
                  pl.BlockSpec((None, N_SLAB, LANES, 2 * SLAB_STATES), lambda i: (layer, 0, 0, 0)),
                  pl.BlockSpec((None, N_SLAB, LANES, LANES), lambda i: (layer, 0, last, 1))] + cspecs,
        out_specs=[whole(o) for o in outs],
        out_shape=outs,
        compiler_params=pltpu.CompilerParams(dimension_semantics=("arbitrary",),
                                             vmem_limit_bytes=VMEM_LIMIT_BYTES),
        name=name,
    )(u, h0r, h0i, wst, cat, strip, *consts)


def _attn_kernel(sink_ref, q_ref, kvc_ref, kvp_ref, g_ref, o_ref):
    i = pl.program_id(1)
    kj = lax.broadcasted_iota(jnp.int32, (2 * WINDOW, WINDOW), 0)
    qi = lax.broadcasted_iota(jnp.int32, (2 * WINDOW, WINDOW), 1)
    band = (kj >= qi) & (kj <= qi + WINDOW)
    low = qi < HEAD_DIM
    for sb in range(ATTN_BLOCKS):
        blk = slice(sb * WINDOW, (sb + 1) * WINDOW)
        kc = kvc_ref[blk, :]
        if sb == 0:
            kp = kvp_ref[...]
            valid = band & ((i > 0) | (kj >= WINDOW))
        else:
            kp = kvc_ref[(sb - 1) * WINDOW:sb * WINDOW, :]
            valid = band
        valid2 = jnp.concatenate([valid, valid], axis=1)
        kcat = jnp.concatenate([kp[:, :KV_WIDTH], kc[:, :KV_WIDTH]], axis=0) * ATTN_SCALE
        swap = pltpu.roll(kcat, HEAD_DIM, 1)
        zero = jnp.zeros_like(kcat)
        k_sel = [[jnp.where(low, kcat, zero).astype(BF16), jnp.where(low, zero, swap).astype(BF16)],
                 [jnp.where(low, swap, zero).astype(BF16), jnp.where(low, zero, kcat).astype(BF16)]]
        v_t = jnp.concatenate([kp[:, KV_WIDTH:], kc[:, KV_WIDTH:]], axis=0).T.astype(BF16)
        parts = []
        for kvh in range(N_KV_HEADS):
            q_pair = jnp.concatenate([q_ref[blk, (2 * kvh) * LANES:(2 * kvh + 1) * LANES],
                                      q_ref[blk, (2 * kvh + 1) * LANES:(2 * kvh + 2) * LANES]], axis=0)
            for parity in range(2):
                s_t = lax.dot_general(k_sel[kvh][parity], q_pair, (((1,), (1,)), ((), ())),
                                      preferred_element_type=F32)
                parts.append(jnp.where(valid2, s_t, MASKED))
        s_all = jnp.concatenate(parts, axis=1)
        sink = sink_ref[...]
        m = jnp.maximum(jnp.max(s_all, axis=0, keepdims=True), sink)
        p_all = jnp.exp(s_all - m)
        inv_den = 1.0 / (jnp.sum(p_all, axis=0, keepdims=True) + jnp.exp(sink - m))
        p_all = p_all.astype(BF16)
        half = Q_PER_KV * WINDOW
        o_kv = [jnp.dot(v_t[kvh * HEAD_DIM:(kvh + 1) * HEAD_DIM], p_all[:, kvh * half:(kvh + 1) * half],
                        preferred_element_type=F32) * inv_den[:, kvh * half:(kvh + 1) * half]
                for kvh in range(N_KV_HEADS)]
        tiles = []
        for pr in range(Q_PER_KV):
            lanes = slice(ATTN_HEAD_ORDER.index(pr) * WINDOW, (ATTN_HEAD_ORDER.index(pr) + 1) * WINDOW)
            tiles.append(jnp.concatenate([o_kv[0][:, lanes], o_kv[1][:, lanes]], axis=0))
        ss = None
        for tile in tiles:
            t = jnp.sum(tile * tile, axis=0, keepdims=True)
            ss = t if ss is None else ss + t
        inv = lax.rsqrt(ss / ATTN_WIDTH + RMS_EPS)
        for pr, tile in enumerate(tiles):
            o_ref[blk, pr * LANES:(pr + 1) * LANES] = ((tile * inv).T * g_ref[pr]).astype(o_ref.dtype)


def _attn_prompt_call(q, kv, sinks, g_perm, *, batch, seq, name):
    rows = ATTN_BLOCKS * WINDOW
    nb = seq // rows
    M = kv.shape[0]
    cur = lambda b, i: (b * nb + i, 0)
    prev = lambda b, i: ((b * nb + i) * ATTN_BLOCKS - jnp.minimum(i, 1), 0)
    (sinks, g_perm), (sink_spec, g_spec) = _consts([sinks, g_perm], 2)
    return pl.pallas_call(
        _attn_kernel,
        grid=(batch, nb),
        in_specs=[sink_spec,
                  pl.BlockSpec((rows, ATTN_WIDTH), cur),
                  pl.BlockSpec((rows, 2 * KV_WIDTH), cur),
                  pl.BlockSpec((WINDOW, 2 * KV_WIDTH), prev),
                  g_spec],
        out_specs=pl.BlockSpec((rows, ATTN_WIDTH), cur),
        out_shape=jax.ShapeDtypeStruct((M, ATTN_WIDTH), BF16),
        compiler_params=pltpu.CompilerParams(dimension_semantics=("arbitrary", "arbitrary"),
                                             vmem_limit_bytes=VMEM_LIMIT_BYTES),
        name=name,
    )(sinks, q, kv, kv, g_perm)


SEQ_PER_STEP = LANES // N_HEADS


def _attn_sample_kernel(q_ref, kv_ref, ck_ref, cv_ref, sink_ref, g_ref, own_ref, o_ref, ko_ref, vo_ref):
    nrow = SEQ_PER_STEP * N_HEADS
    dims_nt = (((1,), (1,)), ((), ()))
    qb = q_ref[...].astype(BF16)
    k_new = kv_ref[:, :KV_WIDTH]
    v_new = kv_ref[:, KV_WIDTH:]
    row_seq = lax.broadcasted_iota(jnp.int32, (nrow, LANES), 0) // N_HEADS
    lane_seq = lax.broadcasted_iota(jnp.int32, (nrow, LANES), 1) // N_HEADS
    row_id = lax.broadcasted_iota(jnp.int32, (nrow, LANES), 0)
    lane_id = lax.broadcasted_iota(jnp.int32, (nrow, LANES), 1)

    s_t = None
    for n in range(SEQ_PER_STEP):
        qn = jnp.where(row_seq == n, qb, jnp.zeros_like(qb))
        part = lax.dot_general(ck_ref[n].astype(BF16), qn, dims_nt, preferred_element_type=F32)
        s_t = part if s_t is None else s_t + part
    s_t = s_t * ATTN_SCALE
    s_cross = lax.dot_general(k_new.astype(BF16), qb, dims_nt, preferred_element_type=F32)
    own_seq = (lax.broadcasted_iota(jnp.int32, s_cross.shape, 0)
               == lax.broadcasted_iota(jnp.int32, s_cross.shape, 1) // N_HEADS)
    s_new = jnp.sum(jnp.where(own_seq, s_cross, 0.0), axis=0, keepdims=True) * ATTN_SCALE
    sink = sink_ref[...]
    m = jnp.maximum(jnp.maximum(jnp.max(s_t, axis=0, keepdims=True), s_new), sink)
    p_t = jnp.exp(s_t - m)
    p_new = jnp.exp(s_new - m)
    inv = 1.0 / (jnp.sum(p_t, axis=0, keepdims=True) + p_new + jnp.exp(sink - m))
    p = (p_t * inv).T.astype(BF16)

    o = None
    for n in range(SEQ_PER_STEP):
        pn = jnp.where(row_seq == n, p, jnp.zeros_like(p))
        part = jnp.dot(pn, cv_ref[n].astype(BF16), preferred_element_type=F32)
        o = part if o is None else o + part
    p_diag = jnp.where(row_id == lane_id, jnp.broadcast_to(p_new * inv, (nrow, LANES)), 0.0).astype(BF16)
    p_col = jnp.dot(p_diag, jnp.ones((LANES, LANES), BF16), preferred_element_type=F32)
    pick = (row_seq[:, :SEQ_PER_STEP] == lax.broadcasted_iota(jnp.int32, (nrow, SEQ_PER_STEP), 1)).astype(BF16)
    v_rows = jnp.dot(pick, v_new.astype(BF16), preferred_element_type=F32)
    o = o + p_col * v_rows

    hi = lax.Precision.HIGHEST
    row_sums = jnp.dot(o * o * own_ref[...], jnp.ones((LANES, LANES), F32), precision=hi, preferred_element_type=F32)
    ss = jnp.dot((row_seq == lane_seq).astype(F32), row_sums, precision=hi, preferred_element_type=F32)
    o_ref[...] = o * lax.rsqrt(ss / ATTN_WIDTH + RMS_EPS) * g_ref[...]

    for n in range(SEQ_PER_STEP):
        ko_ref[n, pl.ds(0, WINDOW - 1), :] = ck_ref[n, pl.ds(1, WINDOW - 1), :]
        ko_ref[n, pl.ds(WINDOW - 1, 1), :] = k_new[n:n + 1]
        vo_ref[n, pl.ds(0, WINDOW - 1), :] = cv_ref[n, pl.ds(1, WINDOW - 1), :]
        vo_ref[n, pl.ds(WINDOW - 1, 1), :] = v_new[n:n + 1]


def _attn_sample_call(q, kv, ck, cv, layer, sink_row, g_rows, own_rows, *, name):
    n_seq = kv.shape[0]
    assert n_seq % SEQ_PER_STEP == 0
    nrow = SEQ_PER_STEP * N_HEADS
    cblk = pl.BlockSpec((SEQ_PER_STEP, WINDOW, 2 * HEAD_DIM), lambda i: (i, 0, 0))
    cin = pl.BlockSpec((None, SEQ_PER_STEP, WINDOW, 2 * HEAD_DIM), lambda i: (layer, i, 0, 0))
    qblk = pl.BlockSpec((nrow, LANES), lambda i: (i, 0))
    consts, cspecs = _consts([sink_row, g_rows, own_rows])
    return pl.pallas_call(
        _attn_sample_kernel,
        grid=(n_seq // SEQ_PER_STEP,),
        in_specs=[qblk, pl.BlockSpec((SEQ_PER_STEP, 2 * KV_WIDTH), lambda i: (i, 0)), cin, cin] + cspecs,
        out_specs=[qblk, cblk, cblk],
        out_shape=[jax.ShapeDtypeStruct(q.shape, F32),
                   jax.ShapeDtypeStruct(ck.shape[1:], F32),
                   jax.ShapeDtypeStruct(cv.shape[1:], F32)],
        compiler_params=pltpu.CompilerParams(dimension_semantics=("arbitrary",),
                                             vmem_limit_bytes=VMEM_LIMIT_BYTES),
        name=name,
    )(q, kv, ck, cv, *consts)


def _expand_heads(a):
    lead = a.shape[:-1]
    a = a.reshape(lead + (N_KV_HEADS, Q_PER_KV, 1, HEAD_DIM))
    sel = jnp.eye(N_KV_HEADS, dtype=a.dtype).reshape(N_KV_HEADS, 1, N_KV_HEADS, 1)
    return (a * sel).reshape(lead + (Q_EXP,))


def _pair_heads(a):
    lead = a.shape[:-1]
    a = a.reshape(lead + (N_KV_HEADS, Q_PER_KV, HEAD_DIM))
    return jnp.swapaxes(a, -3, -2).reshape(lead + (ATTN_WIDTH,))


def _prep_weights(ln_g, ln_b, ffn1_w_in, ffn1_w_out, ffn2_w_in, ffn2_w_out, w_in, ssm_lam_re, ssm_lam_im, ssm_log_dt,
                  ssm_b_re, ssm_b_im, ssm_c_re, ssm_c_im, ssm_d, glu_w, glu_b, attn_sinks, g_ssm_out, g_attn_out,
                  w_out):
    o1, o2 = SSM_WIDTH, SSM_WIDTH + ATTN_WIDTH
    rows_t = lambda a: jnp.swapaxes(a, 1, 2)
    params, a_flat = _ssm_table_inputs(ssm_lam_re, ssm_lam_im, ssm_log_dt, ssm_b_re, ssm_b_im, ssm_c_re, ssm_c_im)
    tables = _ssm_tables_call(params, a_flat)
    head_order = jnp.array([Q_PER_KV * kvh + o for kvh in range(N_KV_HEADS) for o in ATTN_HEAD_ORDER])
    g_exp = _expand_heads(g_attn_out).reshape(DEPTH, N_HEADS, LANES)
    stacked = dict(
        ffn1_in=ffn1_w_in.astype(BF16), ffn1_out=ffn1_w_out.astype(BF16),
        ffn2_in=ffn2_w_in.astype(BF16), ffn2_out=ffn2_w_out.astype(BF16),
        w_in=w_in.astype(BF16),
        w_in_exp=jnp.concatenate([w_in[:, :, :o1], _expand_heads(w_in[:, :, o1:o2]), w_in[:, :, o2:]],
                                 axis=2).astype(BF16),
        wo_s=w_out[:, :SSM_WIDTH].astype(BF16),
        wo_a=rows_t(_pair_heads(rows_t(w_out[:, SSM_WIDTH:]))).astype(BF16),
        wo_a_exp=rows_t(_expand_heads(rows_t(w_out[:, SSM_WIDTH:]))).astype(BF16),
        a_flat=a_flat,
        d_1=ssm_d.reshape(DEPTH, N_SLAB, 1, LANES),
        gluw=glu_w.astype(BF16),
        glub=glu_b.reshape(DEPTH, 1, 2 * SSM_WIDTH),
        g_ssm=g_ssm_out.reshape(DEPTH, 1, SSM_WIDTH),
        sinks=jnp.repeat(attn_sinks[:, head_order], WINDOW, axis=1)[:, None, :],
        g_pair=_pair_heads(g_attn_out).reshape(DEPTH, Q_PER_KV, 1, LANES),
        sink_rows=jnp.tile(attn_sinks, (1, SEQ_PER_STEP))[:, None, :],
        g_rows=jnp.tile(g_exp, (1, SEQ_PER_STEP, 1)),
    )
    w = {name: [(arr, l) for l in range(DEPTH)] for name, arr in stacked.items()}
    w['ffn1'] = [(w['ffn1_in'][l], w['ffn1_out'][l]) for l in range(DEPTH)]
    w['ffn2'] = [(w['ffn2_in'][l], w['ffn2_out'][l]) for l in range(DEPTH)]
    w['tab'] = [tuple((t, l) for t in tables) for l in range(DEPTH)]
    n_ln = ln_g.shape[1]
    ln_g3 = ln_g.reshape(DEPTH * n_ln, 1, D_MODEL)
    ln_b3 = ln_b.reshape(DEPTH * n_ln, 1, D_MODEL)
    w['ln'] = [[((ln_g3, l * n_ln + i), (ln_b3, l * n_ln + i)) for i in range(n_ln)] for l in range(DEPTH)]
    w['own_rows'] = jnp.tile(_expand_heads(jnp.ones((ATTN_WIDTH,), F32)).reshape(N_HEADS, LANES), (SEQ_PER_STEP, 1))
    return w


def _prompt_mixer(u, q, kv, l, w, batch, seq):
    ssm_n, hre, him = _ssm_prompt_call(u, w['tab'][l], w['d_1'][l], w['gluw'][l],
                                       w['glub'][l], w['g_ssm'][l], batch=batch, seq=seq, name=f"p_ssm_{l}")
    att_n = _attn_prompt_call(q, kv, w['sinks'][l], w['g_pair'][l], batch=batch, seq=seq, name=f"p_attn_{l}")
    kvw = kv.reshape(batch, seq, 2 * KV_WIDTH)[:, -WINDOW:].reshape(batch, WINDOW, 2, N_KV_HEADS, HEAD_DIM)
    return (ssm_n, att_n, hre.reshape(batch, N_SSM_GROUPS, SSM_STATE), him.reshape(batch, N_SSM_GROUPS, SSM_STATE),
            kvw[:, :, 0], kvw[:, :, 1])


def _sample_mixer(u, q, kv, l, w, h0_re, h0_im, k_buf, v_buf):
    n_seq = kv.shape[0]
    ssm_n, hre, him = _ssm_sample_call(u, h0_re.reshape(n_seq, N_STATES), h0_im.reshape(n_seq, N_STATES),
                                       w['tab'][l], w['a_flat'][l], w['d_1'][l], w['gluw'][l], w['glub'][l], w['g_ssm'][l],
                                       name=f"s_ssm_{l}")
    att, kn, vn = _attn_sample_call(q.reshape(n_seq * N_HEADS, LANES), kv, k_buf, v_buf, l,
                                    w['sink_rows'][l], w['g_rows'][l], w['own_rows'], name=f"s_attn_{l}")
    win = (n_seq, WINDOW, N_KV_HEADS, HEAD_DIM)
    return (ssm_n, att.reshape(n_seq, Q_EXP), hre.reshape(n_seq, N_SSM_GROUPS, SSM_STATE),
            him.reshape(n_seq, N_SSM_GROUPS, SSM_STATE), kn.reshape(win), vn.reshape(win))


def _trunk(x, h0_re, h0_im, k_buf, v_buf, w, *, tm, tag):
    Bn, L, _ = x.shape
    sample = k_buf is not None
    q_dtype = F32 if sample else BF16
    x = x.reshape(Bn * L, D_MODEL)
    ks, vs, hrs, his = [], [], [], []
    mix = None
    for l in range(DEPTH):
        if l > 0:
            x = _rows_call(x, w['ffn2'][l - 1], w['ln'][l - 1][2], mix=mix, tm=tm, name=f"{tag}_mix_ffn2_{l - 1}")[0]
        proj = w['w_in_exp'][l] if sample else w['w_in'][l]
        x, u, q, kv = _rows_call(x, w['ffn1'][l], w['ln'][l][0], proj=proj, tm=tm, q_dtype=q_dtype,
                                 name=f"{tag}_ffn1_{l}")
        if sample:
            ssm_n, att_n, hre, him, kn, vn = _sample_mixer(u, q, kv, l, w, h0_re[l], h0_im[l], k_buf, v_buf)
        else:
            ssm_n, att_n, hre, him, kn, vn = _prompt_mixer(u, q, kv, l, w, Bn, L)
        wo_a = w['wo_a_exp'][l] if sample else w['wo_a'][l]
        mix = (ssm_n, att_n, w['wo_s'][l], wo_a, w['ln'][l][1][0], w['ln'][l][1][1])
        ks.append(kn); vs.append(vn); hrs.append(hre); his.append(him)
    x = _rows_call(x, w['ffn2'][DEPTH - 1], w['ln'][DEPTH - 1][2], mix=mix, tm=tm, name=f"{tag}_mix_ffn2_{DEPTH - 1}")[0]
    return x.reshape(Bn, L, D_MODEL), jnp.stack(ks), jnp.stack(vs), jnp.stack(hrs), jnp.stack(his)


def kernel(x_prompt, x_sample, cache_k_win, cache_v_win, state_ssm_re, state_ssm_im, ln_g, ln_b, ffn1_w_in, ffn1_w_out, ffn2_w_in, ffn2_w_out, w_in, ssm_lam_re, ssm_lam_im, ssm_log_dt, ssm_b_re, ssm_b_im, ssm_c_re, ssm_c_im, ssm_d, glu_w, glu_b, attn_sinks, g_ssm_out, g_attn_out, w_out):
    w = _prep_weights(ln_g, ln_b, ffn1_w_in, ffn1_w_out, ffn2_w_in, ffn2_w_out, w_in, ssm_lam_re, ssm_lam_im,
                      ssm_log_dt, ssm_b_re, ssm_b_im, ssm_c_re, ssm_c_im, ssm_d, glu_w, glu_b, attn_sinks,
                      g_ssm_out, g_attn_out, w_out)
    y_prompt, kp, vp, hrp, hip = _trunk(x_prompt, None, None, None, None, w, tm=512, tag="p")
    n_seq = x_sample.shape[0]
    merged = (DEPTH, n_seq, WINDOW, N_KV_HEADS * HEAD_DIM)
    y_sample, ks_, vs_, hrs, his = _trunk(x_sample, state_ssm_re, state_ssm_im, cache_k_win.reshape(merged),
                                          cache_v_win.reshape(merged), w, tm=128, tag="s")
    return (y_prompt, y_sample, kp, vp, hrp, hip, ks_, vs_, hrs, his)
```

```python
import functools

import jax
import jax.numpy as jnp
from jax import lax
from jax.experimental import pallas as pl
from jax.experimental.pallas import tpu as pltpu

D_MODEL = 1024
DEPTH = 2
SSM_WIDTH = 512
SSM_GROUP = 16
N_SSM_GROUPS = 32
SSM_STATE = 64
N_STATES = N_SSM_GROUPS * SSM_STATE
ATTN_WIDTH = 512
HEAD_DIM = 64
N_HEADS = 8
N_KV_HEADS = 2
Q_PER_KV = 4
KV_WIDTH = 128
WINDOW = 128
ATTN_SCALE = HEAD_DIM ** -0.5
D_FF = 2816
ALPHA = (2.0 * DEPTH) ** 0.25
LN_EPS = 1e-5
RMS_EPS = 1e-6

ROW_GROUP = 256
LANES = 128
SUBLANES = 8
VMEM_LIMIT_BYTES = 56 * 1024 * 1024
N_SLAB = SSM_WIDTH // LANES
GROUPS_PER_SLAB = LANES // SSM_GROUP
SLAB_STATES = GROUPS_PER_SLAB * SSM_STATE
Q_EXP = N_HEADS * LANES
SSM_CHUNK = 8
SSM_ROWS = 2048
ATTN_BLOCKS = 4
ATTN_HEAD_ORDER = (0, 2, 1, 3)
MASKED = -1e30

F32 = jnp.float32
BF16 = jnp.bfloat16


def _layer_norm(r, g, b):
    mu = jnp.mean(r, axis=-1, keepdims=True)
    c = r - mu
    var = jnp.mean(c * c, axis=-1, keepdims=True)
    return c * lax.rsqrt(var + LN_EPS) * g + b


def _rms_norm(y, g):
    return y * lax.rsqrt(jnp.mean(y * y, axis=-1, keepdims=True) + RMS_EPS) * g


def _const(c, n_grid=1):
    if isinstance(c, tuple):
        arr, idx = c
        shape = (None,) + arr.shape[1:]
        index = (idx,) + (0,) * (arr.ndim - 1)
    else:
        arr, shape, index = c, c.shape, (0,) * c.ndim
    imap = (lambda i: index) if n_grid == 1 else (lambda i, j: index)
    return arr, pl.BlockSpec(shape, imap, pipeline_mode=pl.Buffered(1))


def _consts(cs, n_grid=1):
    arrs, specs = zip(*[_const(c, n_grid) for c in cs])
    return list(arrs), list(specs)


def _rows_kernel(*refs, has_mix, has_proj, n_sub):
    it = iter(refs)
    x_ref = next(it)
    if has_mix:
        ssm_ref, att_ref, wo_s_ref, wo_a_ref, gm_ref, bm_ref = (next(it) for _ in range(6))
    wgu_ref, wdn_ref, g_ref, b_ref = (next(it) for _ in range(4))
    if has_proj:
        wp_ref = next(it)
    o_ref = next(it)
    if has_proj:
        u_ref, q_ref, kv_ref = (next(it) for _ in range(3))

    tm = x_ref.shape[0]
    sub = tm // n_sub
    groups = [slice(r * sub, (r + 1) * sub) for r in range(n_sub)]
    xs = [x_ref[rows, :] for rows in groups]
    if has_mix:
        ms = []
        for rows in groups:
            ssm = jnp.concatenate([ssm_ref[j, rows, :] for j in range(N_SLAB)], axis=1).astype(BF16)
            att = att_ref[rows, :].astype(BF16)
            ms.append(jnp.dot(ssm, wo_s_ref[...], preferred_element_type=F32)
                      + jnp.dot(att, wo_a_ref[...], preferred_element_type=F32))
        xs = [_layer_norm(ALPHA * x + m, gm_ref[...], bm_ref[...]) for x, m in zip(xs, ms)]
    gus = [jnp.dot(x.astype(BF16), wgu_ref[...], preferred_element_type=F32) for x in xs]
    hs = [(gu[:, :D_FF] * jax.nn.sigmoid(gu[:, :D_FF]) * gu[:, D_FF:]).astype(BF16) for gu in gus]
    ys = [jnp.dot(h, wdn_ref[...], preferred_element_type=F32) for h in hs]
    xs = [_layer_norm(ALPHA * x + 0.5 * y, g_ref[...], b_ref[...]) for x, y in zip(xs, ys)]
    for rows, x in zip(groups, xs):
        o_ref[rows, :] = x
    if has_proj:
        q_cols = q_ref.shape[1]
        zs = [jnp.dot(x.astype(BF16), wp_ref[...], preferred_element_type=F32) for x in xs]
        for rows, z in zip(groups, zs):
            for j in range(N_SLAB):
                u_ref[j, rows, :] = z[:, j * LANES:(j + 1) * LANES]
            q_ref[rows, :] = z[:, SSM_WIDTH:SSM_WIDTH + q_cols].astype(q_ref.dtype)
            kv_ref[rows, :] = z[:, SSM_WIDTH + q_cols:]


def _rows_call(x, ffn, ln, mix=None, proj=None, *, tm, q_dtype=BF16, name):
    M = x.shape[0]
    assert M % tm == 0
    row = lambda w: pl.BlockSpec((tm, w), lambda i: (i, 0))
    slab = lambda n: pl.BlockSpec((n, tm, LANES), lambda i: (0, i, 0))
    args = [x]
    specs = [row(D_MODEL)]
    if mix is not None:
        ssm_n, att_n = mix[:2]
        arrs, cspecs = _consts(mix[2:])
        args += [ssm_n, att_n] + arrs
        specs += [slab(N_SLAB), row(att_n.shape[1])] + cspecs
    arrs, cspecs = _consts(list(ffn) + list(ln))
    args += arrs
    specs += cspecs
    out_shape = [jax.ShapeDtypeStruct((M, D_MODEL), F32)]
    out_specs = [row(D_MODEL)]
    if proj is not None:
        arr, cspec = _const(proj)
        args.append(arr)
        specs.append(cspec)
        q_cols = arr.shape[-1] - SSM_WIDTH - 2 * KV_WIDTH
        out_shape += [jax.ShapeDtypeStruct((N_SLAB, M, LANES), F32),
                      jax.ShapeDtypeStruct((M, q_cols), q_dtype),
                      jax.ShapeDtypeStruct((M, 2 * KV_WIDTH), F32)]
        out_specs += [slab(N_SLAB), row(q_cols), row(2 * KV_WIDTH)]
    return pl.pallas_call(
        functools.partial(_rows_kernel, has_mix=mix is not None, has_proj=proj is not None,
                          n_sub=max(1, tm // ROW_GROUP)),
        grid=(M // tm,),
        in_specs=specs,
        out_specs=out_specs,
        out_shape=out_shape,
        compiler_params=pltpu.CompilerParams(dimension_semantics=("arbitrary",),
                                             vmem_limit_bytes=VMEM_LIMIT_BYTES),
        name=name,
    )(*args)


def _cmul(xr, xi, yr, yi):
    return xr * yr - xi * yi, xr * yi + xi * yr


def _ssm_tables_kernel(p_ref, af_ref, wst_ref, cat_ref, strip_ref, ak_ref, tr_ref):
    t1 = SSM_CHUNK
    a_re, a_im, lam_re, lam_im, b_re, b_im, c_re, c_im = (p_ref[k] for k in range(8))
    num_re, num_im = a_re - 1.0, a_im
    den = lam_re * lam_re + lam_im * lam_im
    f_re = (num_re * lam_re + num_im * lam_im) / den
    f_im = (num_im * lam_re - num_re * lam_im) / den
    bb_re, bb_im = _cmul(f_re, f_im, b_re, b_im)

    def same_group(shape, row_div, col_div):
        r = lax.broadcasted_iota(jnp.int32, shape, 0) // row_div
        c = lax.broadcasted_iota(jnp.int32, shape, 1) // col_div
        return (r == c).astype(F32)

    m_state = same_group((LANES, SLAB_STATES), SSM_GROUP, SSM_STATE)
    m_chan = 0.5 * same_group((LANES, LANES), SSM_GROUP, SSM_GROUP)

    def expand(x):
        return (jnp.concatenate([x] * (SLAB_STATES // LANES), axis=1) * m_state).astype(BF16)

    def dot_nt(a, b):
        return lax.dot_general(a, b, (((1,), (1,)), ((), ())), precision=lax.Precision.HIGHEST,
                               preferred_element_type=F32)

    pw_re, pw_im = jnp.ones_like(a_re), jnp.zeros_like(a_re)
    kd = []
    for l in range(t1):
        ab_re, ab_im = _cmul(pw_re, pw_im, bb_re, bb_im)
        s = t1 - 1 - l
        kd_l = []
        for j in range(N_SLAB):
            rows = slice(j * LANES, (j + 1) * LANES)
            wst_ref[j, s * LANES:(s + 1) * LANES, :SLAB_STATES] = expand(ab_re[rows])
            wst_ref[j, s * LANES:(s + 1) * LANES, SLAB_STATES:] = expand(ab_im[rows])
            k = dot_nt(ab_re[rows], c_re[rows]) - dot_nt(ab_im[rows], c_im[rows])
            kd_l.append((k * m_chan).astype(BF16))
        kd.append(kd_l)
        pw_re, pw_im = _cmul(pw_re, pw_im, a_re, a_im)
        cf_re = c_re * pw_re - c_im * pw_im
        cf_im = -(c_re * pw_im + c_im * pw_re)
        for j in range(N_SLAB):
            rows = slice(j * LANES, (j + 1) * LANES)
            cat_ref[j, l * LANES:(l + 1) * LANES, :SLAB_STATES] = expand(cf_re[rows])
            cat_ref[j, l * LANES:(l + 1) * LANES, SLAB_STATES:] = expand(cf_im[rows])
    for j in range(N_SLAB):
        for rho in range(t1):
            for c in range(2):
                lag = t1 - 2 - rho + c
                blk = kd[lag][j] if lag >= 0 else jnp.zeros((LANES, LANES), BF16)
                strip_ref[j, rho * LANES:(rho + 1) * LANES, c * LANES:(c + 1) * LANES] = blk

    f_re, f_im = af_ref[0], af_ref[1]
    base_re, base_im = f_re, f_im
    for _ in range(t1 - 1):
        base_re, base_im = _cmul(base_re, base_im, f_re, f_im)
    row = lax.broadcasted_iota(jnp.int32, (SUBLANES, N_STATES), 0)
    pws = [(base_re, base_im)]
    for _ in range(SUBLANES - 1):
        pws.append(_cmul(pws[-1][0], pws[-1][1], base_re, base_im))
    for n, k in enumerate((1, 2, 4)):
        for part in range(2):
            ak_ref[2 * n + part] = jnp.where(row >= k, jnp.broadcast_to(pws[k - 1][part], row.shape), 0.0)
    for part in range(2):
        acc = jnp.zeros(row.shape, F32)
        for r in range(SUBLANES):
            acc = jnp.where(row == r, jnp.broadcast_to(pws[r][part], row.shape), acc)
        tr_ref[part] = acc


def _ssm_tables_call(params, a_flat):
    t1 = SSM_CHUNK
    shapes = [((N_SLAB, t1 * LANES, 2 * SLAB_STATES), BF16),
              ((N_SLAB, t1 * LANES, 2 * SLAB_STATES), BF16),
              ((N_SLAB, t1 * LANES, 2 * LANES), BF16),
              ((6, SUBLANES, N_STATES), F32),
              ((2, SUBLANES, N_STATES), F32)]
    per_layer = lambda shape: pl.BlockSpec((None,) + tuple(shape), lambda l: (l,) + (0,) * len(shape))
    return pl.pallas_call(
        _ssm_tables_kernel,
        grid=(DEPTH,),
        in_specs=[per_layer(params.shape[1:]), per_layer(a_flat.shape[1:])],
        out_specs=[per_layer(s) for s, _ in shapes],
        out_shape=[jax.ShapeDtypeStruct((DEPTH,) + s, d) for s, d in shapes],
        compiler_params=pltpu.CompilerParams(dimension_semantics=("arbitrary",),
                                             vmem_limit_bytes=VMEM_LIMIT_BYTES),
        name="ssm_tables",
    )(params, a_flat)


def _ssm_table_inputs(lam_re, lam_im, log_dt, b_re, b_im, c_re, c_im):
    dt = jnp.exp(log_dt)[..., None]
    mag = jnp.exp(lam_re * dt)
    a_re = mag * jnp.cos(lam_im * dt)
    a_im = mag * jnp.sin(lam_im * dt)
    rep = lambda x: jnp.repeat(x, SSM_GROUP, axis=1)
    flat = lambda x: x.reshape(DEPTH, SSM_WIDTH, SSM_STATE)
    params = jnp.stack([rep(a_re), rep(a_im), rep(lam_re), rep(lam_im),
                        flat(jnp.swapaxes(b_re, 2, 3)), flat(jnp.swapaxes(b_im, 2, 3)), flat(c_re), flat(c_im)],
                       axis=1)
    params = jnp.concatenate([params, params], axis=-1)
    a_flat = jnp.stack([a_re.reshape(DEPTH, 1, N_STATES), a_im.reshape(DEPTH, 1, N_STATES)], axis=1)
    return params, a_flat


def _glu_rms(y, gluw_ref, glub_ref, g_ref):
    g = jax.nn.gelu(y).astype(BF16)
    zz = jnp.dot(g, gluw_ref[...], preferred_element_type=F32) + glub_ref[...]
    o = zz[:, :SSM_WIDTH] * jax.nn.sigmoid(zz[:, SSM_WIDTH:])
    return _rms_norm(o, g_ref[...])


def _ssm_kernel(u_ref, wst_ref, cat_ref, strip_ref, ak_ref, tr_ref, d_ref, gluw_ref, glub_ref, g_ref,
                o_ref, hre_ref, him_ref,
                ucb_ref, sre_ref, sim_ref, hcr_ref, hci_ref):
    t1 = SSM_CHUNK
    nc = SSM_ROWS // t1
    i = pl.program_id(1)

    @pl.when(i == 0)
    def _():
        hcr_ref[...] = jnp.zeros_like(hcr_ref)
        hci_ref[...] = jnp.zeros_like(hci_ref)

    for j in range(N_SLAB):
        for s in range(t1):
            ucb_ref[j, :, s * LANES:(s + 1) * LANES] = u_ref[j, pl.ds(s, nc, stride=t1), :].astype(BF16)

    for j in range(N_SLAB):
        st = jnp.dot(ucb_ref[j], wst_ref[j], preferred_element_type=F32)
        sre_ref[:, j * SLAB_STATES:(j + 1) * SLAB_STATES] = st[:, :SLAB_STATES]
        sim_ref[:, j * SLAB_STATES:(j + 1) * SLAB_STATES] = st[:, SLAB_STATES:]

    row = lax.broadcasted_iota(jnp.int32, (SUBLANES, SLAB_STATES), 0)
    for j in range(N_SLAB):
        cols = pl.ds(j * SLAB_STATES, SLAB_STATES)

        def body(gi, carry, cols=cols):
            hr, hi = carry
            r0 = pl.multiple_of(gi * SUBLANES, SUBLANES)
            xr = sre_ref[pl.ds(r0, SUBLANES), cols]
            xi = sim_ref[pl.ds(r0, SUBLANES), cols]
            for n, k in enumerate((1, 2, 4)):
                akr = ak_ref[2 * n, :, cols]
                aki = ak_ref[2 * n + 1, :, cols]
                sr = pltpu.roll(xr, k, 0)
                si = pltpu.roll(xi, k, 0)
                xr, xi = xr + akr * sr - aki * si, xi + akr * si + aki * sr
            tr = tr_ref[0, :, cols]
            ti = tr_ref[1, :, cols]
            er = xr + tr * hr - ti * hi
            ei = xi + tr * hi + ti * hr
            sre_ref[pl.ds(r0, SUBLANES), cols] = jnp.where(row >= 1, pltpu.roll(er, 1, 0), hr)
            sim_ref[pl.ds(r0, SUBLANES), cols] = jnp.where(row >= 1, pltpu.roll(ei, 1, 0), hi)
            last = SUBLANES - 1
            return (jnp.broadcast_to(er[last:last + 1], er.shape), jnp.broadcast_to(ei[last:last + 1], ei.shape))

        hr, hi = lax.fori_loop(0, nc // SUBLANES, body, (hcr_ref[:, cols], hci_ref[:, cols]), unroll=True)
        hcr_ref[:, cols] = hr
        hci_ref[:, cols] = hi

    hre_ref[0] = hcr_ref[...]
    him_ref[0] = hci_ref[...]

    ys = []
    for j in range(N_SLAB):
        cols = slice(j * SLAB_STATES, (j + 1) * SLAB_STATES)
        hprev = jnp.concatenate([sre_ref[:, cols], sim_ref[:, cols]], axis=1).astype(BF16)
        y = lax.dot_general(hprev, cat_ref[j], (((1,), (1,)), ((), ())), preferred_element_type=F32)
        parts = []
        for p2 in range(t1 // 2):
            kk = (2 * p2 + 2) * LANES
            yi = jnp.dot(ucb_ref[j, :, :kk], strip_ref[j, (t1 - 2 - 2 * p2) * LANES:, :],
                         preferred_element_type=F32)
            parts.append(y[:, 2 * p2 * LANES:(2 * p2 + 2) * LANES] + yi)
        ys.append(jnp.concatenate(parts, axis=1))

    for t in range(t1):
        y = jnp.concatenate([ys[j][:, t * LANES:(t + 1) * LANES] + d_ref[j] * u_ref[j, pl.ds(t, nc, stride=t1), :]
                             for j in range(N_SLAB)], axis=1)
        n = _glu_rms(y, gluw_ref, glub_ref, g_ref)
        for j in range(N_SLAB):
            o_ref[j, pl.ds(t, nc, stride=t1), :] = n[:, j * LANES:(j + 1) * LANES]


def _ssm_prompt_call(u, tab, d_t, gluw, glub, g, *, batch, seq, name):
    t1 = SSM_CHUNK
    nc = SSM_ROWS // t1
    steps = seq // SSM_ROWS
    blk = pl.BlockSpec((N_SLAB, SSM_ROWS, LANES), lambda b, i: (0, b * steps + i, 0))
    st = pl.BlockSpec((1, SUBLANES, N_STATES), lambda b, i: (b, 0, 0))
    consts, cspecs = _consts(list(tab) + [d_t, gluw, glub, g], 2)
    out, hre, him = pl.pallas_call(
        _ssm_kernel,
        grid=(batch, steps),
        in_specs=[blk] + cspecs,
        out_specs=[blk, st, st],
        out_shape=[jax.ShapeDtypeStruct(u.shape, F32),
                   jax.ShapeDtypeStruct((batch, SUBLANES, N_STATES), F32),
                   jax.ShapeDtypeStruct((batch, SUBLANES, N_STATES), F32)],
        scratch_shapes=[pltpu.VMEM((N_SLAB, nc, t1 * LANES), BF16),
                        pltpu.VMEM((nc, N_STATES), F32),
                        pltpu.VMEM((nc, N_STATES), F32),
                        pltpu.VMEM((SUBLANES, N_STATES), F32),
                        pltpu.VMEM((SUBLANES, N_STATES), F32)],
        compiler_params=pltpu.CompilerParams(dimension_semantics=("arbitrary", "arbitrary"),
                                             vmem_limit_bytes=VMEM_LIMIT_BYTES),
        name=name,
    )(u, *consts)
    return out, hre[:, 0], him[:, 0]


def _ssm_sample_kernel(u_ref, h0r_ref, h0i_ref, wst_ref, cat_ref, kd_ref, a_ref, d_ref, gluw_ref, glub_ref, g_ref,
                       o_ref, hr_ref, hi_ref):
    ys = []
    for j in range(N_SLAB):
        cols = slice(j * SLAB_STATES, (j + 1) * SLAB_STATES)
        uf = u_ref[j]
        ub = uf.astype(BF16)
        st = jnp.dot(ub, wst_ref[j], preferred_element_type=F32)
        h0r = h0r_ref[:, cols]
        h0i = h0i_ref[:, cols]
        ar = a_ref[0, :, cols]
        ai = a_ref[1, :, cols]
        hr_ref[:, cols] = ar * h0r - ai * h0i + st[:, :SLAB_STATES]
        hi_ref[:, cols] = ar * h0i + ai * h0r + st[:, SLAB_STATES:]
        hcat = jnp.concatenate([h0r, h0i], axis=1).astype(BF16)
        y = lax.dot_general(hcat, cat_ref[j], (((1,), (1,)), ((), ())), preferred_element_type=F32)
        y = y + jnp.dot(ub, kd_ref[j], preferred_element_type=F32)
        ys.append(y + d_ref[j] * uf)
    n = _glu_rms(jnp.concatenate(ys, axis=1), gluw_ref, glub_ref, g_ref)
    for j in range(N_SLAB):
        o_ref[j] = n[:, j * LANES:(j + 1) * LANES]


def _ssm_sample_call(u, h0r, h0i, tab, a_flat, d_t, gluw, glub, g, *, name):
    n_seq = u.shape[1]
    (wst, layer), (cat, _), (strip, _) = tab[:3]
    last = SSM_CHUNK - 1
    whole = lambda a: pl.BlockSpec(a.shape, lambda i: (0,) * len(a.shape))
    outs = [jax.ShapeDtypeStruct(u.shape, F32),
            jax.ShapeDtypeStruct((n_seq, N_STATES), F32),
            jax.ShapeDtypeStruct((n_seq, N_STATES), F32)]
    consts, cspecs = _consts([a_flat, d_t, gluw, glub, g])
    return pl.pallas_call(
        _ssm_sample_kernel,
        grid=(1,),
        in_specs=[whole(u), whole(h0r), whole(h0i),
                  pl.BlockSpec((None, N_SLAB, LANES, 2 * SLAB_STATES), lambda i: (layer, 0, last, 0)),
                  pl.BlockSpec((None, N_SLAB, LANES, 2 * SLAB_STATES), lambda i: (layer, 0, 0, 0)),
                  pl.BlockSpec((None, N_SLAB, LANES, LANES), lambda i: (layer, 0, last, 1))] + cspecs,
        out_specs=[whole(o) for o in outs],
        out_shape=outs,
        compiler_params=pltpu.CompilerParams(dimension_semantics=("arbitrary",),
                                             vmem_limit_bytes=VMEM_LIMIT_BYTES),
        name=name,
    )(u, h0r, h0i, wst, cat, strip, *consts)


def _attn_kernel(sink_ref, q_ref, kvc_ref, kvp_ref, g_ref, o_ref):
    i = pl.program_id(1)
    kj = lax.broadcasted_iota(jnp.int32, (2 * WINDOW, WINDOW), 0)
    qi = lax.broadcasted_iota(jnp.int32, (2 * WINDOW, WINDOW), 1)
    band = (kj >= qi) & (kj <= qi + WINDOW)
    low = qi < HEAD_DIM
    for sb in range(ATTN_BLOCKS):
        blk = slice(sb * WINDOW, (sb + 1) * WINDOW)
        kc = kvc_ref[blk, :]
        if sb == 0:
            kp = kvp_ref[...]
            valid = band & ((i > 0) | (kj >= WINDOW))
        else:
            kp = kvc_ref[(sb - 1) * WINDOW:sb * WINDOW, :]
            valid = band
        valid2 = jnp.concatenate([valid, valid], axis=1)
        kcat = jnp.concatenate([kp[:, :KV_WIDTH], kc[:, :KV_WIDTH]], axis=0) * ATTN_SCALE
        swap = pltpu.roll(kcat, HEAD_DIM, 1)
        zero = jnp.zeros_like(kcat)
        k_sel = [[jnp.where(low, kcat, zero).astype(BF16), jnp.where(low, zero, swap).astype(BF16)],
                 [jnp.where(low, swap, zero).astype(BF16), jnp.where(low, zero, kcat).astype(BF16)]]
        v_t = jnp.concatenate([kp[:, KV_WIDTH:], kc[:, KV_WIDTH:]], axis=0).T.astype(BF16)
        parts = []
        for kvh in range(N_KV_HEADS):
            q_pair = jnp.concatenate([q_ref[blk, (2 * kvh) * LANES:(2 * kvh + 1) * LANES],
                                      q_ref[blk, (2 * kvh + 1) * LANES:(2 * kvh + 2) * LANES]], axis=0)
            for parity in range(2):
                s_t = lax.dot_general(k_sel[kvh][parity], q_pair, (((1,), (1,)), ((), ())),
                                      preferred_element_type=F32)
                parts.append(jnp.where(valid2, s_t, MASKED))
        s_all = jnp.concatenate(parts, axis=1)
        sink = sink_ref[...]
        m = jnp.maximum(jnp.max(s_all, axis=0, keepdims=True), sink)
        p_all = jnp.exp(s_all - m)
        inv_den = 1.0 / (jnp.sum(p_all, axis=0, keepdims=True) + jnp.exp(sink - m))
        p_all = p_all.astype(BF16)
        half = Q_PER_KV * WINDOW
        o_kv = [jnp.dot(v_t[kvh * HEAD_DIM:(kvh + 1) * HEAD_DIM], p_all[:, kvh * half:(kvh + 1) * half],
                        preferred_element_type=F32) * inv_den[:, kvh * half:(kvh + 1) * half]
                for kvh in range(N_KV_HEADS)]
        tiles = []
        for pr in range(Q_PER_KV):
            lanes = slice(ATTN_HEAD_ORDER.index(pr) * WINDOW, (ATTN_HEAD_ORDER.index(pr) + 1) * WINDOW)
            tiles.append(jnp.concatenate([o_kv[0][:, lanes], o_kv[1][:, lanes]], axis=0))
        ss = None
        for tile in tiles:
            t = jnp.sum(tile * tile, axis=0, keepdims=True)
            ss = t if ss is None else ss + t
        inv = lax.rsqrt(ss / ATTN_WIDTH + RMS_EPS)
        for pr, tile in enumerate(tiles):
            o_ref[blk, pr * LANES:(pr + 1) * LANES] = ((tile * inv).T * g_ref[pr]).astype(o_ref.dtype)


def _attn_prompt_call(q, kv, sinks, g_perm, *, batch, seq, name):
    rows = ATTN_BLOCKS * WINDOW
    nb = seq // rows
    M = kv.shape[0]
    cur = lambda b, i: (b * nb + i, 0)
    prev = lambda b, i: ((b * nb + i) * ATTN_BLOCKS - jnp.minimum(i, 1), 0)
    (sinks, g_perm), (sink_spec, g_spec) = _consts([sinks, g_perm], 2)
    return pl.pallas_call(
        _attn_kernel,
        grid=(batch, nb),
        in_specs=[sink_spec,
                  pl.BlockSpec((rows, ATTN_WIDTH), cur),
                  pl.BlockSpec((rows, 2 * KV_WIDTH), cur),
                  pl.BlockSpec((WINDOW, 2 * KV_WIDTH), prev),
                  g_spec],
        out_specs=pl.BlockSpec((rows, ATTN_WIDTH), cur),
        out_shape=jax.ShapeDtypeStruct((M, ATTN_WIDTH), BF16),
        compiler_params=pltpu.CompilerParams(dimension_semantics=("arbitrary", "arbitrary"),
                                             vmem_limit_bytes=VMEM_LIMIT_BYTES),
        name=name,
    )(sinks, q, kv, kv, g_perm)


SEQ_PER_STEP = LANES // N_HEADS


def _attn_sample_kernel(q_ref, kv_ref, ckt_ref, cvt_ref, sink_ref, g_ref, own_ref, o_ref, kot_ref, vot_ref):
    nrow = SEQ_PER_STEP * N_HEADS
    qb = q_ref[...].astype(BF16)
    k_new = kv_ref[:, :KV_WIDTH]
    v_new = kv_ref[:, KV_WIDTH:]
    row_seq = lax.broadcasted_iota(jnp.int32, (nrow, LANES), 0) // N_HEADS
    lane_seq = lax.broadcasted_iota(jnp.int32, (nrow, LANES), 1) // N_HEADS
    lane = lax.broadcasted_iota(jnp.int32, (nrow, LANES), 1)
    pick = (lax.broadcasted_iota(jnp.int32, (nrow, SEQ_PER_STEP), 0) // N_HEADS
            == lax.broadcasted_iota(jnp.int32, (nrow, SEQ_PER_STEP), 1)).astype(BF16)

    s = None
    for n in range(SEQ_PER_STEP):
        qn = jnp.where(row_seq == n, qb, jnp.zeros_like(qb))
        part = jnp.dot(qn, ckt_ref[n].astype(BF16), preferred_element_type=F32)
        s = part if s is None else s + part
    s = s * ATTN_SCALE
    k_rows = jnp.dot(pick, k_new.astype(BF16), preferred_element_type=F32)
    s_new = jnp.sum(qb.astype(F32) * k_rows, axis=-1, keepdims=True) * ATTN_SCALE
    sink = sink_ref[...]
    m = jnp.maximum(jnp.maximum(jnp.max(s, axis=-1, keepdims=True), s_new), sink)
    p = jnp.exp(s - m)
    p_new = jnp.exp(s_new - m)
    inv = 1.0 / (jnp.sum(p, axis=-1, keepdims=True) + p_new + jnp.exp(sink - m))
    pb = (p * inv).astype(BF16)

    o = None
    for n in range(SEQ_PER_STEP):
        pn = jnp.where(row_seq == n, pb, jnp.zeros_like(pb))
        part = lax.dot_general(pn, cvt_ref[n].astype(BF16), (((1,), (1,)), ((), ())),
                               preferred_element_type=F32)
        o = part if o is None else o + part
    v_rows = jnp.dot(pick, v_new.astype(BF16), preferred_element_type=F32)
    o = o + (p_new * inv).astype(BF16).astype(F32) * v_rows

    hi = lax.Precision.HIGHEST
    row_sums = jnp.dot(o * o * own_ref[...], jnp.ones((LANES, LANES), F32), precision=hi, preferred_element_type=F32)
    ss = jnp.dot((row_seq == lane_seq).astype(F32), row_sums, precision=hi, preferred_element_type=F32)
    o_ref[...] = o * lax.rsqrt(ss / ATTN_WIDTH + RMS_EPS) * g_ref[...]

    pad = jnp.zeros((LANES - SEQ_PER_STEP, KV_WIDTH), F32)
    k_cols = jnp.concatenate([k_new, pad], axis=0).T
    v_cols = jnp.concatenate([v_new, pad], axis=0).T
    last = lane == WINDOW - 1
    for n in range(SEQ_PER_STEP):
        kot_ref[n] = jnp.where(last, pltpu.roll(k_cols, WINDOW - 1 - n, 1), pltpu.roll(ckt_ref[n], WINDOW - 1, 1))
        vot_ref[n] = jnp.where(last, pltpu.roll(v_cols, WINDOW - 1 - n, 1), pltpu.roll(cvt_ref[n], WINDOW - 1, 1))


def _attn_sample_call(q, kv, ck, cv, layer, sink_row, g_rows, own_rows, *, name):
    n_seq = kv.shape[0]
    assert n_seq % SEQ_PER_STEP == 0
    nrow = SEQ_PER_STEP * N_HEADS
    cblk = pl.BlockSpec((SEQ_PER_STEP, WINDOW, 2 * HEAD_DIM), lambda i: (i, 0, 0))
    cin = pl.BlockSpec((None, SEQ_PER_STEP, WINDOW, 2 * HEAD_DIM), lambda i: (layer, i, 0, 0))
    qblk = pl.BlockSpec((nrow, LANES), lambda i: (i, 0))
    consts, cspecs = _consts([sink_row, g_rows, own_rows])
    return pl.pallas_call(
        _attn_sample_kernel,
        grid=(n_seq // SEQ_PER_STEP,),
        in_specs=[qblk, pl.BlockSpec((SEQ_PER_STEP, 2 * KV_WIDTH), lambda i: (i, 0)), cin, cin] + cspecs,
        out_specs=[qblk, cblk, cblk],
        out_shape=[jax.ShapeDtypeStruct(q.shape, F32),
                   jax.ShapeDtypeStruct(ck.shape[1:], F32),
                   jax.ShapeDtypeStruct(cv.shape[1:], F32)],
        compiler_params=pltpu.CompilerParams(dimension_semantics=("arbitrary",),
                                             vmem_limit_bytes=VMEM_LIMIT_BYTES),
        name=name,
    )(q, kv, ck, cv, *consts)


def _expand_heads(a):
    lead = a.shape[:-1]
    a = a.reshape(lead + (N_KV_HEADS, Q_PER_KV, 1, HEAD_DIM))
    sel = jnp.eye(N_KV_HEADS, dtype=a.dtype).reshape(N_KV_HEADS, 1, N_KV_HEADS, 1)
    return (a * sel).reshape(lead + (Q_EXP,))


def _pair_heads(a):
    lead = a.shape[:-1]
    a = a.reshape(lead + (N_KV_HEADS, Q_PER_KV, HEAD_DIM))
    return jnp.swapaxes(a, -3, -2).reshape(lead + (ATTN_WIDTH,))


def _prep_weights(ln_g, ln_b, ffn1_w_in, ffn1_w_out, ffn2_w_in, ffn2_w_out, w_in, ssm_lam_re, ssm_lam_im, ssm_log_dt,
                  ssm_b_re, ssm_b_im, ssm_c_re, ssm_c_im, ssm_d, glu_w, glu_b, attn_sinks, g_ssm_out, g_attn_out,
                  w_out):
    o1, o2 = SSM_WIDTH, SSM_WIDTH + ATTN_WIDTH
    rows_t = lambda a: jnp.swapaxes(a, 1, 2)
    params, a_flat = _ssm_table_inputs(ssm_lam_re, ssm_lam_im, ssm_log_dt, ssm_b_re, ssm_b_im, ssm_c_re, ssm_c_im)
    tables = _ssm_tables_call(params, a_flat)
    head_order = jnp.array([Q_PER_KV * kvh + o for kvh in range(N_KV_HEADS) for o in ATTN_HEAD_ORDER])
    g_exp = _expand_heads(g_attn_out).reshape(DEPTH, N_HEADS, LANES)
    w_in_b = w_in.astype(BF16)
    w_out_b = w_out.astype(BF16)
    stacked = dict(
        ffn1_in=ffn1_w_in.astype(BF16), ffn1_out=ffn1_w_out.astype(BF16),
        ffn2_in=ffn2_w_in.astype(BF16), ffn2_out=ffn2_w_out.astype(BF16),
        w_in=w_in_b,
        w_in_exp=jnp.concatenate([w_in_b[:, :, :o1], _expand_heads(w_in_b[:, :, o1:o2]), w_in_b[:, :, o2:]], axis=2),
        wo_s=w_out_b[:, :SSM_WIDTH],
        wo_a=rows_t(_pair_heads(rows_t(w_out_b[:, SSM_WIDTH:]))),
        wo_a_exp=rows_t(_expand_heads(rows_t(w_out_b[:, SSM_WIDTH:]))),
        a_flat=a_flat,
        d_1=ssm_d.reshape(DEPTH, N_SLAB, 1, LANES),
        gluw=glu_w.astype(BF16),
        glub=glu_b.reshape(DEPTH, 1, 2 * SSM_WIDTH),
        g_ssm=g_ssm_out.reshape(DEPTH, 1, SSM_WIDTH),
        sinks=jnp.repeat(attn_sinks[:, head_order], WINDOW, axis=1)[:, None, :],
        g_pair=_pair_heads(g_attn_out).reshape(DEPTH, Q_PER_KV, 1, LANES),
        sink_rows=jnp.tile(attn_sinks, (1, SEQ_PER_STEP))[:, :, None],
        g_rows=jnp.tile(g_exp, (1, SEQ_PER_STEP, 1)),
    )
    w = {name: [(arr, l) for l in range(DEPTH)] for name, arr in stacked.items()}
    w['ffn1'] = [(w['ffn1_in'][l], w['ffn1_out'][l]) for l in range(DEPTH)]
    w['ffn2'] = [(w['ffn2_in'][l], w['ffn2_out'][l]) for l in range(DEPTH)]
    w['tab'] = [tuple((t, l) for t in tables) for l in range(DEPTH)]
    n_ln = ln_g.shape[1]
    ln_g3 = ln_g.reshape(DEPTH * n_ln, 1, D_MODEL)
    ln_b3 = ln_b.reshape(DEPTH * n_ln, 1, D_MODEL)
    w['ln'] = [[((ln_g3, l * n_ln + i), (ln_b3, l * n_ln + i)) for i in range(n_ln)] for l in range(DEPTH)]
    w['own_rows'] = jnp.tile(_expand_heads(jnp.ones((ATTN_WIDTH,), F32)).reshape(N_HEADS, LANES), (SEQ_PER_STEP, 1))
    return w


def _prompt_mixer(u, q, kv, l, w, batch, seq):
    ssm_n, hre, him = _ssm_prompt_call(u, w['tab'][l], w['d_1'][l], w['gluw'][l],
                                       w['glub'][l], w['g_ssm'][l], batch=batch, seq=seq, name=f"p_ssm_{l}")
    att_n = _attn_prompt_call(q, kv, w['sinks'][l], w['g_pair'][l], batch=batch, seq=seq, name=f"p_attn_{l}")
    kvw = kv.reshape(batch, seq, 2 * KV_WIDTH)[:, -WINDOW:].reshape(batch, WINDOW, 2, N_KV_HEADS, HEAD_DIM)
    return (ssm_n, att_n, hre.reshape(batch, N_SSM_GROUPS, SSM_STATE), him.reshape(batch, N_SSM_GROUPS, SSM_STATE),
            kvw[:, :, 0], kvw[:, :, 1])


def _sample_mixer(u, q, kv, l, w, h0_re, h0_im, k_buf, v_buf):
    n_seq = kv.shape[0]
    ssm_n, hre, him = _ssm_sample_call(u, h0_re.reshape(n_seq, N_STATES), h0_im.reshape(n_seq, N_STATES),
                                       w['tab'][l], w['a_flat'][l], w['d_1'][l], w['gluw'][l], w['glub'][l], w['g_ssm'][l],
                                       name=f"s_ssm_{l}")
    att, kn, vn = _attn_sample_call(q.reshape(n_seq * N_HEADS, LANES), kv, k_buf, v_buf, l,
                                    w['sink_rows'][l], w['g_rows'][l], w['own_rows'], name=f"s_attn_{l}")
    untranspose = lambda t: jnp.transpose(t.reshape(n_seq, N_KV_HEADS, HEAD_DIM, WINDOW), (0, 3, 1, 2))
    return (ssm_n, att.reshape(n_seq, Q_EXP), hre.reshape(n_seq, N_SSM_GROUPS, SSM_STATE),
            him.reshape(n_seq, N_SSM_GROUPS, SSM_STATE), untranspose(kn), untranspose(vn))


def _trunk(x, h0_re, h0_im, k_buf, v_buf, w, *, tm, tag):
    Bn, L, _ = x.shape
    sample = k_buf is not None
    q_dtype = F32 if sample else BF16
    x = x.reshape(Bn * L, D_MODEL)
    ks, vs, hrs, his = [], [], [], []
    mix = None
    for l in range(DEPTH):
        if l > 0:
            x = _rows_call(x, w['ffn2'][l - 1], w['ln'][l - 1][2], mix=mix, tm=tm, name=f"{tag}_mix_ffn2_{l - 1}")[0]
        proj = w['w_in_exp'][l] if sample else w['w_in'][l]
        x, u, q, kv = _rows_call(x, w['ffn1'][l], w['ln'][l][0], proj=proj, tm=tm, q_dtype=q_dtype,
                                 name=f"{tag}_ffn1_{l}")
        if sample:
            ssm_n, att_n, hre, him, kn, vn = _sample_mixer(u, q, kv, l, w, h0_re[l], h0_im[l], k_buf, v_buf)
        else:
            ssm_n, att_n, hre, him, kn, vn = _prompt_mixer(u, q, kv, l, w, Bn, L)
        wo_a = w['wo_a_exp'][l] if sample else w['wo_a'][l]
        mix = (ssm_n, att_n, w['wo_s'][l], wo_a, w['ln'][l][1][0], w['ln'][l][1][1])
        ks.append(kn); vs.append(vn); hrs.append(hre); his.append(him)
    x = _rows_call(x, w['ffn2'][DEPTH - 1], w['ln'][DEPTH - 1][2], mix=mix, tm=tm, name=f"{tag}_mix_ffn2_{DEPTH - 1}")[0]
    return x.reshape(Bn, L, D_MODEL), jnp.stack(ks), jnp.stack(vs), jnp.stack(hrs), jnp.stack(his)


def kernel(x_prompt, x_sample, cache_k_win, cache_v_win, state_ssm_re, state_ssm_im, ln_g, ln_b, ffn1_w_in, ffn1_w_out, ffn2_w_in, ffn2_w_out, w_in, ssm_lam_re, ssm_lam_im, ssm_log_dt, ssm_b_re, ssm_b_im, ssm_c_re, ssm_c_im, ssm_d, glu_w, glu_b, attn_sinks, g_ssm_out, g_attn_out, w_out):
    w = _prep_weights(ln_g, ln_b, ffn1_w_in, ffn1_w_out, ffn2_w_in, ffn2_w_out, w_in, ssm_lam_re, ssm_lam_im,
                      ssm_log_dt, ssm_b_re, ssm_b_im, ssm_c_re, ssm_c_im, ssm_d, glu_w, glu_b, attn_sinks,
                      g_ssm_out, g_attn_out, w_out)
    y_prompt, kp, vp, hrp, hip = _trunk(x_prompt, None, None, None, None, w, tm=512, tag="p")
    n_seq = x_sample.shape[0]
    transposed = lambda c: jnp.transpose(c, (0, 1, 3, 4, 2)).reshape(DEPTH, n_seq, N_KV_HEADS * HEAD_DIM, WINDOW)
    y_sample, ks_, vs_, hrs, his = _trunk(x_sample, state_ssm_re, state_ssm_im, transposed(cache_k_win),
                                          transposed(cache_v_win), w, tm=128, tag="s")
    return (y_prompt, y_sample, kp, vp, hrp, hip, ks_, vs_, hrs, his)
```

```python
import functools

import jax
import jax.numpy as jnp
from jax import lax
from jax.experimental import pallas as pl
from jax.experimental.pallas import tpu as pltpu

D_MODEL = 1024
DEPTH = 2
SSM_WIDTH = 512
SSM_GROUP = 16
N_SSM_GROUPS = 32
SSM_STATE = 64
N_STATES = N_SSM_GROUPS * SSM_STATE
ATTN_WIDTH = 512
HEAD_DIM = 64
N_HEADS = 8
N_KV_HEADS = 2
Q_PER_KV = 4
KV_WIDTH = 128
WINDOW = 128
ATTN_SCALE = HEAD_DIM ** -0.5
D_FF = 2816
ALPHA = (2.0 * DEPTH) ** 0.25
LN_EPS = 1e-5
RMS_EPS = 1e-6

ROW_GROUP = 256
W_GU_STAGE_ROWS = 64
W_DN_STAGE_ROWS = 352
LANES = 128
SUBLANES = 8
VMEM_LIMIT_BYTES = 56 * 1024 * 1024
N_SLAB = SSM_WIDTH // LANES
GROUPS_PER_SLAB = LANES // SSM_GROUP
SLAB_STATES = GROUPS_PER_SLAB * SSM_STATE
Q_EXP = N_HEADS * LANES
SSM_CHUNK = 8
SSM_ROWS = 2048
ATTN_BLOCKS = 4
ATTN_HEAD_ORDER = (0, 2, 1, 3)
MASKED = -1e30

F32 = jnp.float32
BF16 = jnp.bfloat16


def _layer_norm(r, g, b):
    mu = jnp.mean(r, axis=-1, keepdims=True)
    c = r - mu
    var = jnp.mean(c * c, axis=-1, keepdims=True)
    return c * lax.rsqrt(var + LN_EPS) * g + b


def _rms_norm(y, g):
    return y * lax.rsqrt(jnp.mean(y * y, axis=-1, keepdims=True) + RMS_EPS) * g


def _const(c, n_grid=1):
    if isinstance(c, tuple):
        arr, idx = c
        shape = (None,) + arr.shape[1:]
        index = (idx,) + (0,) * (arr.ndim - 1)
    else:
        arr, shape, index = c, c.shape, (0,) * c.ndim
    imap = (lambda i: index) if n_grid == 1 else (lambda i, j: index)
    return arr, pl.BlockSpec(shape, imap, pipeline_mode=pl.Buffered(1))


def _consts(cs, n_grid=1):
    arrs, specs = zip(*[_const(c, n_grid) for c in cs])
    return list(arrs), list(specs)


def _load_cast(src_hbm, layer, dst_ref, stage_ref, sem_ref):
    rows = stage_ref.shape[1]
    n_chunks = dst_ref.shape[0] // rows

    def copy(c):
        return pltpu.make_async_copy(src_hbm.at[layer, pl.ds(c * rows, rows), :], stage_ref.at[c % 2],
                                     sem_ref.at[c % 2])

    copy(0).start()
    for c in range(n_chunks):
        copy(c).wait()
        if c + 1 < n_chunks:
            copy(c + 1).start()
        dst_ref[pl.ds(c * rows, rows), :] = stage_ref[c % 2].astype(BF16)


def _rows_kernel(*refs, has_mix, has_proj, n_sub, layer):
    it = iter(refs)
    x_ref = next(it)
    if has_mix:
        ssm_ref, att_ref, wo_s_ref, wo_a_ref, gm_ref, bm_ref = (next(it) for _ in range(6))
    wgu_hbm, wdn_hbm, g_ref, b_ref = (next(it) for _ in range(4))
    if has_proj:
        wp_ref = next(it)
    o_ref = next(it)
    if has_proj:
        u_ref, q_ref, kv_ref = (next(it) for _ in range(3))
    wgu_ref, wdn_ref, stage_gu, stage_dn, sem_gu, sem_dn = (next(it) for _ in range(6))

    @pl.when(pl.program_id(0) == 0)
    def _():
        _load_cast(wgu_hbm, layer, wgu_ref, stage_gu, sem_gu)
        _load_cast(wdn_hbm, layer, wdn_ref, stage_dn, sem_dn)

    tm = x_ref.shape[0]
    sub = tm // n_sub
    groups = [slice(r * sub, (r + 1) * sub) for r in range(n_sub)]
    xs = [x_ref[rows, :] for rows in groups]
    if has_mix:
        ms = []
        for rows in groups:
            ssm = jnp.concatenate([ssm_ref[j, rows, :] for j in range(N_SLAB)], axis=1).astype(BF16)
            att = att_ref[rows, :].astype(BF16)
            ms.append(jnp.dot(ssm, wo_s_ref[...], preferred_element_type=F32)
                      + jnp.dot(att, wo_a_ref[...], preferred_element_type=F32))
        xs = [_layer_norm(ALPHA * x + m, gm_ref[...], bm_ref[...]) for x, m in zip(xs, ms)]
    gus = [jnp.dot(x.astype(BF16), wgu_ref[...], preferred_element_type=F32) for x in xs]
    hs = [(gu[:, :D_FF] * jax.nn.sigmoid(gu[:, :D_FF]) * gu[:, D_FF:]).astype(BF16) for gu in gus]
    ys = [jnp.dot(h, wdn_ref[...], preferred_element_type=F32) for h in hs]
    xs = [_layer_norm(ALPHA * x + 0.5 * y, g_ref[...], b_ref[...]) for x, y in zip(xs, ys)]
    for rows, x in zip(groups, xs):
        o_ref[rows, :] = x
    if has_proj:
        q_cols = q_ref.shape[1]
        zs = [jnp.dot(x.astype(BF16), wp_ref[...], preferred_element_type=F32) for x in xs]
        for rows, z in zip(groups, zs):
            for j in range(N_SLAB):
                u_ref[j, rows, :] = z[:, j * LANES:(j + 1) * LANES]
            q_ref[rows, :] = z[:, SSM_WIDTH:SSM_WIDTH + q_cols].astype(q_ref.dtype)
            kv_ref[rows, :] = z[:, SSM_WIDTH + q_cols:]


def _rows_call(x, ffn, ln, mix=None, proj=None, *, tm, q_dtype=BF16, name):
    M = x.shape[0]
    assert M % tm == 0
    row = lambda w: pl.BlockSpec((tm, w), lambda i: (i, 0))
    slab = lambda n: pl.BlockSpec((n, tm, LANES), lambda i: (0, i, 0))
    args = [x]
    specs = [row(D_MODEL)]
    if mix is not None:
        ssm_n, att_n = mix[:2]
        arrs, cspecs = _consts(mix[2:])
        args += [ssm_n, att_n] + arrs
        specs += [slab(N_SLAB), row(att_n.shape[1])] + cspecs
    w_gu, w_dn, layer = ffn
    arrs, cspecs = _consts(list(ln))
    args += [w_gu, w_dn] + arrs
    specs += [pl.BlockSpec(memory_space=pl.ANY), pl.BlockSpec(memory_space=pl.ANY)] + cspecs
    out_shape = [jax.ShapeDtypeStruct((M, D_MODEL), F32)]
    out_specs = [row(D_MODEL)]
    if proj is not None:
        arr, cspec = _const(proj)
        args.append(arr)
        specs.append(cspec)
        q_cols = arr.shape[-1] - SSM_WIDTH - 2 * KV_WIDTH
        out_shape += [jax.ShapeDtypeStruct((N_SLAB, M, LANES), F32),
                      jax.ShapeDtypeStruct((M, q_cols), q_dtype),
                      jax.ShapeDtypeStruct((M, 2 * KV_WIDTH), F32)]
        out_specs += [slab(N_SLAB), row(q_cols), row(2 * KV_WIDTH)]
    return pl.pallas_call(
        functools.partial(_rows_kernel, has_mix=mix is not None, has_proj=proj is not None,
                          n_sub=max(1, tm // ROW_GROUP), layer=layer),
        grid=(M // tm,),
        in_specs=specs,
        out_specs=out_specs,
        out_shape=out_shape,
        scratch_shapes=[pltpu.VMEM(w_gu.shape[1:], BF16), pltpu.VMEM(w_dn.shape[1:], BF16),
                        pltpu.VMEM((2, W_GU_STAGE_ROWS, w_gu.shape[2]), F32),
                        pltpu.VMEM((2, W_DN_STAGE_ROWS, w_dn.shape[2]), F32),
                        pltpu.SemaphoreType.DMA((2,)), pltpu.SemaphoreType.DMA((2,))],
        compiler_params=pltpu.CompilerParams(dimension_semantics=("arbitrary",),
                                             vmem_limit_bytes=VMEM_LIMIT_BYTES),
        name=name,
    )(*args)


def _cmul(xr, xi, yr, yi):
    return xr * yr - xi * yi, xr * yi + xi * yr


def _ssm_tables_kernel(p_ref, af_ref, wst_ref, cat_ref, strip_ref, ak_ref, tr_ref):
    t1 = SSM_CHUNK
    a_re, a_im, lam_re, lam_im, b_re, b_im, c_re, c_im = (p_ref[k] for k in range(8))
    num_re, num_im = a_re - 1.0, a_im
    den = lam_re * lam_re + lam_im * lam_im
    f_re = (num_re * lam_re + num_im * lam_im) / den
    f_im = (num_im * lam_re - num_re * lam_im) / den
    bb_re, bb_im = _cmul(f_re, f_im, b_re, b_im)

    def same_group(shape, row_div, col_div):
        r = lax.broadcasted_iota(jnp.int32, shape, 0) // row_div
        c = lax.broadcasted_iota(jnp.int32, shape, 1) // col_div
        return (r == c).astype(F32)

    m_state = same_group((LANES, SLAB_STATES), SSM_GROUP, SSM_STATE)
    m_chan = 0.5 * same_group((LANES, LANES), SSM_GROUP, SSM_GROUP)

    def expand(x):
        return (jnp.concatenate([x] * (SLAB_STATES // LANES), axis=1) * m_state).astype(BF16)

    def dot_nt(a, b):
        return lax.dot_general(a, b, (((1,), (1,)), ((), ())), precision=lax.Precision.HIGHEST,
                               preferred_element_type=F32)

    pw_re, pw_im = jnp.ones_like(a_re), jnp.zeros_like(a_re)
    kd = []
    for l in range(t1):
        ab_re, ab_im = _cmul(pw_re, pw_im, bb_re, bb_im)
        s = t1 - 1 - l
        kd_l = []
        for j in range(N_SLAB):
            rows = slice(j * LANES, (j + 1) * LANES)
            wst_ref[j, s * LANES:(s + 1) * LANES, :SLAB_STATES] = expand(ab_re[rows])
            wst_ref[j, s * LANES:(s + 1) * LANES, SLAB_STATES:] = expand(ab_im[rows])
            k = dot_nt(ab_re[rows], c_re[rows]) - dot_nt(ab_im[rows], c_im[rows])
            kd_l.append((k * m_chan).astype(BF16))
        kd.append(kd_l)
        pw_re, pw_im = _cmul(pw_re, pw_im, a_re, a_im)
        cf_re = c_re * pw_re - c_im * pw_im
        cf_im = -(c_re * pw_im + c_im * pw_re)
        for j in range(N_SLAB):
            rows = slice(j * LANES, (j + 1) * LANES)
            cat_ref[j, l * LANES:(l + 1) * LANES, :SLAB_STATES] = expand(cf_re[rows])
            cat_ref[j, l * LANES:(l + 1) * LANES, SLAB_STATES:] = expand(cf_im[rows])
    for j in range(N_SLAB):
        for rho in range(t1):
            for c in range(2):
                lag = t1 - 2 - rho + c
                blk = kd[lag][j] if lag >= 0 else jnp.zeros((LANES, LANES), BF16)
                strip_ref[j, rho * LANES:(rho + 1) * LANES, c * LANES:(c + 1) * LANES] = blk

    f_re, f_im = af_ref[0], af_ref[1]
    base_re, base_im = f_re, f_im
    for _ in range(t1 - 1):
        base_re, base_im = _cmul(base_re, base_im, f_re, f_im)
    row = lax.broadcasted_iota(jnp.int32, (SUBLANES, N_STATES), 0)
    pws = [(base_re, base_im)]
    for _ in range(SUBLANES - 1):
        pws.append(_cmul(pws[-1][0], pws[-1][1], base_re, base_im))
    for n, k in enumerate((1, 2, 4)):
        for part in range(2):
            ak_ref[2 * n + part] = jnp.where(row >= k, jnp.broadcast_to(pws[k - 1][part], row.shape), 0.0)
    for part in range(2):
        acc = jnp.zeros(row.shape, F32)
        for r in range(SUBLANES):
            acc = jnp.where(row == r, jnp.broadcast_to(pws[r][part], row.shape), acc)
        tr_ref[part] = acc


def _ssm_tables_call(params, a_flat):
    t1 = SSM_CHUNK
    shapes = [((N_SLAB, t1 * LANES, 2 * SLAB_STATES), BF16),
              ((N_SLAB, t1 * LANES, 2 * SLAB_STATES), BF16),
              ((N_SLAB, t1 * LANES, 2 * LANES), BF16),
              ((6, SUBLANES, N_STATES), F32),
              ((2, SUBLANES, N_STATES), F32)]
    per_layer = lambda shape: pl.BlockSpec((None,) + tuple(shape), lambda l: (l,) + (0,) * len(shape))
    return pl.pallas_call(
        _ssm_tables_kernel,
        grid=(DEPTH,),
        in_specs=[per_layer(params.shape[1:]), per_layer(a_flat.shape[1:])],
        out_specs=[per_layer(s) for s, _ in shapes],
        out_shape=[jax.ShapeDtypeStruct((DEPTH,) + s, d) for s, d in shapes],
        compiler_params=pltpu.CompilerParams(dimension_semantics=("arbitrary",),
                                             vmem_limit_bytes=VMEM_LIMIT_BYTES),
        name="ssm_tables",
    )(params, a_flat)


def _ssm_table_inputs(lam_re, lam_im, log_dt, b_re, b_im, c_re, c_im):
    dt = jnp.exp(log_dt)[..., None]
    mag = jnp.exp(lam_re * dt)
    a_re = mag * jnp.cos(lam_im * dt)
    a_im = mag * jnp.sin(lam_im * dt)
    rep = lambda x: jnp.repeat(x, SSM_GROUP, axis=1)
    flat = lambda x: x.reshape(DEPTH, SSM_WIDTH, SSM_STATE)
    params = jnp.stack([rep(a_re), rep(a_im), rep(lam_re), rep(lam_im),
                        flat(jnp.swapaxes(b_re, 2, 3)), flat(jnp.swapaxes(b_im, 2, 3)), flat(c_re), flat(c_im)],
                       axis=1)
    params = jnp.concatenate([params, params], axis=-1)
    a_flat = jnp.stack([a_re.reshape(DEPTH, 1, N_STATES), a_im.reshape(DEPTH, 1, N_STATES)], axis=1)
    return params, a_flat


def _glu_rms(y, gluw_ref, glub_ref, g_ref):
    g = jax.nn.gelu(y).astype(BF16)
    zz = jnp.dot(g, gluw_ref[...], preferred_element_type=F32) + glub_ref[...]
    o = zz[:, :SSM_WIDTH] * jax.nn.sigmoid(zz[:, SSM_WIDTH:])
    return _rms_norm(o, g_ref[...])


def _ssm_kernel(u_ref, wst_ref, cat_ref, strip_ref, ak_ref, tr_ref, d_ref, gluw_ref, glub_ref, g_ref,
                o_ref, hre_ref, him_ref,
                ucb_ref, sre_ref, sim_ref, hcr_ref, hci_ref):
    t1 = SSM_CHUNK
    nc = SSM_ROWS // t1
    i = pl.program_id(1)

    @pl.when(i == 0)
    def _():
        hcr_ref[...] = jnp.zeros_like(hcr_ref)
        hci_ref[...] = jnp.zeros_like(hci_ref)

    for j in range(N_SLAB):
        for s in range(t1):
            ucb_ref[j, :, s * LANES:(s + 1) * LANES] = u_ref[j, pl.ds(s, nc, stride=t1), :].astype(BF16)

    for j in range(N_SLAB):
        st = jnp.dot(ucb_ref[j], wst_ref[j], preferred_element_type=F32)
        sre_ref[:, j * SLAB_STATES:(j + 1) * SLAB_STATES] = st[:, :SLAB_STATES]
        sim_ref[:, j * SLAB_STATES:(j + 1) * SLAB_STATES] = st[:, SLAB_STATES:]

    row = lax.broadcasted_iota(jnp.int32, (SUBLANES, SLAB_STATES), 0)
    for j in range(N_SLAB):
        cols = pl.ds(j * SLAB_STATES, SLAB_STATES)

        def body(gi, carry, cols=cols):
            hr, hi = carry
            r0 = pl.multiple_of(gi * SUBLANES, SUBLANES)
            xr = sre_ref[pl.ds(r0, SUBLANES), cols]
            xi = sim_ref[pl.ds(r0, SUBLANES), cols]
            for n, k in enumerate((1, 2, 4)):
                akr = ak_ref[2 * n, :, cols]
                aki = ak_ref[2 * n + 1, :, cols]
                sr = pltpu.roll(xr, k, 0)
                si = pltpu.roll(xi, k, 0)
                xr, xi = xr + akr * sr - aki * si, xi + akr * si + aki * sr
            tr = tr_ref[0, :, cols]
            ti = tr_ref[1, :, cols]
            er = xr + tr * hr - ti * hi
            ei = xi + tr * hi + ti * hr
            sre_ref[pl.ds(r0, SUBLANES), cols] = jnp.where(row >= 1, pltpu.roll(er, 1, 0), hr)
            sim_ref[pl.ds(r0, SUBLANES), cols] = jnp.where(row >= 1, pltpu.roll(ei, 1, 0), hi)
            last = SUBLANES - 1
            return (jnp.broadcast_to(er[last:last + 1], er.shape), jnp.broadcast_to(ei[last:last + 1], ei.shape))

        hr, hi = lax.fori_loop(0, nc // SUBLANES, body, (hcr_ref[:, cols], hci_ref[:, cols]), unroll=True)
        hcr_ref[:, cols] = hr
        hci_ref[:, cols] = hi

    hre_ref[0] = hcr_ref[...]
    him_ref[0] = hci_ref[...]

    ys = []
    for j in range(N_SLAB):
        cols = slice(j * SLAB_STATES, (j + 1) * SLAB_STATES)
        hprev = jnp.concatenate([sre_ref[:, cols], sim_ref[:, cols]], axis=1).astype(BF16)
        y = lax.dot_general(hprev, cat_ref[j], (((1,), (1,)), ((), ())), preferred_element_type=F32)
        parts = []
        for p2 in range(t1 // 2):
            kk = (2 * p2 + 2) * LANES
            yi = jnp.dot(ucb_ref[j, :, :kk], strip_ref[j, (t1 - 2 - 2 * p2) * LANES:, :],
                         preferred_element_type=F32)
            parts.append(y[:, 2 * p2 * LANES:(2 * p2 + 2) * LANES] + yi)
        ys.append(jnp.concatenate(parts, axis=1))

    for t in range(t1):
        y = jnp.concatenate([ys[j][:, t * LANES:(t + 1) * LANES] + d_ref[j] * u_ref[j, pl.ds(t, nc, stride=t1), :]
                             for j in range(N_SLAB)], axis=1)
        n = _glu_rms(y, gluw_ref, glub_ref, g_ref)
        for j in range(N_SLAB):
            o_ref[j, pl.ds(t, nc, stride=t1), :] = n[:, j * LANES:(j + 1) * LANES]


def _ssm_prompt_call(u, tab, d_t, gluw, glub, g, *, batch, seq, name):
    t1 = SSM_CHUNK
    nc = SSM_ROWS // t1
    steps = seq // SSM_ROWS
    blk = pl.BlockSpec((N_SLAB, SSM_ROWS, LANES), lambda b, i: (0, b * steps + i, 0))
    st = pl.BlockSpec((1, SUBLANES, N_STATES), lambda b, i: (b, 0, 0))
    consts, cspecs = _consts(list(tab) + [d_t, gluw, glub, g], 2)
    out, hre, him = pl.pallas_call(
        _ssm_kernel,
        grid=(batch, steps),
        in_specs=[blk] + cspecs,
        out_specs=[blk, st, st],
        out_shape=[jax.ShapeDtypeStruct(u.shape, F32),
                   jax.ShapeDtypeStruct((batch, SUBLANES, N_STATES), F32),
                   jax.ShapeDtypeStruct((batch, SUBLANES, N_STATES), F32)],
        scratch_shapes=[pltpu.VMEM((N_SLAB, nc, t1 * LANES), BF16),
                        pltpu.VMEM((nc, N_STATES), F32),
                        pltpu.VMEM((nc, N_STATES), F32),
                        pltpu.VMEM((SUBLANES, N_STATES), F32),
                        pltpu.VMEM((SUBLANES, N_STATES), F32)],
        compiler_params=pltpu.CompilerParams(dimension_semantics=("arbitrary", "arbitrary"),
                                             vmem_limit_bytes=VMEM_LIMIT_BYTES),
        name=name,
    )(u, *consts)
    return out, hre[:, 0], him[:, 0]


def _ssm_sample_kernel(u_ref, h0r_ref, h0i_ref, wst_ref, cat_ref, kd_ref, a_ref, d_ref, gluw_ref, glub_ref, g_ref,
                       o_ref, hr_ref, hi_ref):
    ys = []
    for j in range(N_SLAB):
        cols = slice(j * SLAB_STATES, (j + 1) * SLAB_STATES)
        uf = u_ref[j]
        ub = uf.astype(BF16)
        st = jnp.dot(ub, wst_ref[j], preferred_element_type=F32)
        h0r = h0r_ref[:, cols]
        h0i = h0i_ref[:, cols]
        ar = a_ref[0, :, cols]
        ai = a_ref[1, :, cols]
        hr_ref[:, cols] = ar * h0r - ai * h0i + st[:, :SLAB_STATES]
        hi_ref[:, cols] = ar * h0i + ai * h0r + st[:, SLAB_STATES:]
        hcat = jnp.concatenate([h0r, h0i], axis=1).astype(BF16)
        y = lax.dot_general(hcat, cat_ref[j], (((1,), (1,)), ((), ())), preferred_element_type=F32)
        y = y + jnp.dot(ub, kd_ref[j], preferred_element_type=F32)
        ys.append(y + d_ref[j] * uf)
    n = _glu_rms(jnp.concatenate(ys, axis=1), gluw_ref, glub_ref, g_ref)
    for j in range(N_SLAB):
        o_ref[j] = n[:, j * LANES:(j + 1) * LANES]


def _ssm_sample_call(u, h0r, h0i, tab, a_flat, d_t, gluw, glub, g, *, name):
    n_seq = u.shape[1]
    (wst, layer), (cat, _), (strip, _) = tab[:3]
    last = SSM_CHUNK - 1
    whole = lambda a: pl.BlockSpec(a.shape, lambda i: (0,) * len(a.shape))
    outs = [jax.ShapeDtypeStruct(u.shape, F32),
            jax.ShapeDtypeStruct((n_seq, N_STATES), F32),
            jax.ShapeDtypeStruct((n_seq, N_STATES), F32)]
    consts, cspecs = _consts([a_flat, d_t, gluw, glub, g])
    return pl.pallas_call(
        _ssm_sample_kernel,
        grid=(1,),
        in_specs=[whole(u), whole(h0r), whole(h0i),
                  pl.BlockSpec((None, N_SLAB, LANES, 2 * SLAB_STATES), lambda i: (layer, 0, last, 0)),
                  pl.BlockSpec((None, N_SLAB, LANES, 2 * SLAB_STATES), lambda i: (layer, 0, 0, 0)),
                  pl.BlockSpec((None, N_SLAB, LANES, LANES), lambda i: (layer, 0, last, 1))] + cspecs,
        out_specs=[whole(o) for o in outs],
        out_shape=outs,
        compiler_params=pltpu.CompilerParams(dimension_semantics=("arbitrary",),
                                             vmem_limit_bytes=VMEM_LIMIT_BYTES),
        name=name,
    )(u, h0r, h0i, wst, cat, strip, *consts)


def _attn_kernel(sink_ref, q_ref, kvc_ref, kvp_ref, g_ref, o_ref):
    i = pl.program_id(1)
    kj = lax.broadcasted_iota(jnp.int32, (2 * WINDOW, WINDOW), 0)
    qi = lax.broadcasted_iota(jnp.int32, (2 * WINDOW, WINDOW), 1)
    band = (kj >= qi) & (kj <= qi + WINDOW)
    low = qi < HEAD_DIM
    for sb in range(ATTN_BLOCKS):
        blk = slice(sb * WINDOW, (sb + 1) * WINDOW)
        kc = kvc_ref[blk, :]
        if sb == 0:
            kp = kvp_ref[...]
            valid = band & ((i > 0) | (kj >= WINDOW))
        else:
            kp = kvc_ref[(sb - 1) * WINDOW:sb * WINDOW, :]
            valid = band
        valid2 = jnp.concatenate([valid, valid], axis=1)
        kcat = jnp.concatenate([kp[:, :KV_WIDTH], kc[:, :KV_WIDTH]], axis=0) * ATTN_SCALE
        swap = pltpu.roll(kcat, HEAD_DIM, 1)
        zero = jnp.zeros_like(kcat)
        k_sel = [[jnp.where(low, kcat, zero).astype(BF16), jnp.where(low, zero, swap).astype(BF16)],
                 [jnp.where(low, swap, zero).astype(BF16), jnp.where(low, zero, kcat).astype(BF16)]]
        v_t = jnp.concatenate([kp[:, KV_WIDTH:], kc[:, KV_WIDTH:]], axis=0).T.astype(BF16)
        parts = []
        for kvh in range(N_KV_HEADS):
            q_pair = jnp.concatenate([q_ref[blk, (2 * kvh) * LANES:(2 * kvh + 1) * LANES],
                                      q_ref[blk, (2 * kvh + 1) * LANES:(2 * kvh + 2) * LANES]], axis=0)
            for parity in range(2):
                s_t = lax.dot_general(k_sel[kvh][parity], q_pair, (((1,), (1,)), ((), ())),
                                      preferred_element_type=F32)
                parts.append(jnp.where(valid2, s_t, MASKED))
        s_all = jnp.concatenate(parts, axis=1)
        sink = sink_ref[...]
        m = jnp.maximum(jnp.max(s_all, axis=0, keepdims=True), sink)
        p_all = jnp.exp(s_all - m)
        inv_den = 1.0 / (jnp.sum(p_all, axis=0, keepdims=True) + jnp.exp(sink - m))
        p_all = p_all.astype(BF16)
        half = Q_PER_KV * WINDOW
        o_kv = [jnp.dot(v_t[kvh * HEAD_DIM:(kvh + 1) * HEAD_DIM], p_all[:, kvh * half:(kvh + 1) * half],
                        preferred_element_type=F32) * inv_den[:, kvh * half:(kvh + 1) * half]
                for kvh in range(N_KV_HEADS)]
        tiles = []
        for pr in range(Q_PER_KV):
            lanes = slice(ATTN_HEAD_ORDER.index(pr) * WINDOW, (ATTN_HEAD_ORDER.index(pr) + 1) * WINDOW)
            tiles.append(jnp.concatenate([o_kv[0][:, lanes], o_kv[1][:, lanes]], axis=0))
        ss = None
        for tile in tiles:
            t = jnp.sum(tile * tile, axis=0, keepdims=True)
            ss = t if ss is None else ss + t
        inv = lax.rsqrt(ss / ATTN_WIDTH + RMS_EPS)
        for pr, tile in enumerate(tiles):
            o_ref[blk, pr * LANES:(pr + 1) * LANES] = ((tile * inv).T * g_ref[pr]).astype(o_ref.dtype)


def _attn_prompt_call(q, kv, sinks, g_perm, *, batch, seq, name):
    rows = ATTN_BLOCKS * WINDOW
    nb = seq // rows
    M = kv.shape[0]
    cur = lambda b, i: (b * nb + i, 0)
    prev = lambda b, i: ((b * nb + i) * ATTN_BLOCKS - jnp.minimum(i, 1), 0)
    (sinks, g_perm), (sink_spec, g_spec) = _consts([sinks, g_perm], 2)
    return pl.pallas_call(
        _attn_kernel,
        grid=(batch, nb),
        in_specs=[sink_spec,
                  pl.BlockSpec((rows, ATTN_WIDTH), cur),
                  pl.BlockSpec((rows, 2 * KV_WIDTH), cur),
                  pl.BlockSpec((WINDOW, 2 * KV_WIDTH), prev),
                  g_spec],
        out_specs=pl.BlockSpec((rows, ATTN_WIDTH), cur),
        out_shape=jax.ShapeDtypeStruct((M, ATTN_WIDTH), BF16),
        compiler_params=pltpu.CompilerParams(dimension_semantics=("arbitrary", "arbitrary"),
                                             vmem_limit_bytes=VMEM_LIMIT_BYTES),
        name=name,
    )(sinks, q, kv, kv, g_perm)


SEQ_PER_STEP = LANES // N_HEADS


def _attn_sample_kernel(q_ref, kv_ref, ckt_ref, cvt_ref, sink_ref, g_ref, own_ref, o_ref, kot_ref, vot_ref):
    nrow = SEQ_PER_STEP * N_HEADS
    qb = q_ref[...].astype(BF16)
    k_new = kv_ref[:, :KV_WIDTH]
    v_new = kv_ref[:, KV_WIDTH:]
    row_seq = lax.broadcasted_iota(jnp.int32, (nrow, LANES), 0) // N_HEADS
    lane_seq = lax.broadcasted_iota(jnp.int32, (nrow, LANES), 1) // N_HEADS
    lane = lax.broadcasted_iota(jnp.int32, (nrow, LANES), 1)
    pick = (lax.broadcasted_iota(jnp.int32, (nrow, SEQ_PER_STEP), 0) // N_HEADS
            == lax.broadcasted_iota(jnp.int32, (nrow, SEQ_PER_STEP), 1)).astype(BF16)

    qf = q_ref[...]
    heads_of = lambda a, n: a[n * N_HEADS:(n + 1) * N_HEADS]
    s = jnp.concatenate([jnp.dot(heads_of(qf, n).astype(BF16), ckt_ref[n].astype(BF16), preferred_element_type=F32)
                         for n in range(SEQ_PER_STEP)], axis=0) * ATTN_SCALE
    k_rows = jnp.dot(pick, k_new.astype(BF16), preferred_element_type=F32)
    s_new = jnp.sum(qb.astype(F32) * k_rows, axis=-1, keepdims=True) * ATTN_SCALE
    sink = sink_ref[...]
    m = jnp.maximum(jnp.maximum(jnp.max(s, axis=-1, keepdims=True), s_new), sink)
    p = jnp.exp(s - m)
    p_new = jnp.exp(s_new - m)
    inv = 1.0 / (jnp.sum(p, axis=-1, keepdims=True) + p_new + jnp.exp(sink - m))
    pn = p * inv
    o = jnp.concatenate([lax.dot_general(heads_of(pn, n).astype(BF16), cvt_ref[n].astype(BF16),
                                         (((1,), (1,)), ((), ())), preferred_element_type=F32)
                         for n in range(SEQ_PER_STEP)], axis=0)
    v_rows = jnp.dot(pick, v_new.astype(BF16), preferred_element_type=F32)
    o = o + (p_new * inv).astype(BF16).astype(F32) * v_rows

    hi = lax.Precision.HIGHEST
    row_sums = jnp.dot(o * o * own_ref[...], jnp.ones((LANES, LANES), F32), precision=hi, preferred_element_type=F32)
    ss = jnp.dot((row_seq == lane_seq).astype(F32), row_sums, precision=hi, preferred_element_type=F32)
    o_ref[...] = o * lax.rsqrt(ss / ATTN_WIDTH + RMS_EPS) * g_ref[...]

    pad = jnp.zeros((LANES - SEQ_PER_STEP, KV_WIDTH), F32)
    k_cols = jnp.concatenate([k_new, pad], axis=0).T
    v_cols = jnp.concatenate([v_new, pad], axis=0).T
    last = lane == WINDOW - 1
    for n in range(SEQ_PER_STEP):
        kot_ref[n] = jnp.where(last, pltpu.roll(k_cols, WINDOW - 1 - n, 1), pltpu.roll(ckt_ref[n], WINDOW - 1, 1))
        vot_ref[n] = jnp.where(last, pltpu.roll(v_cols, WINDOW - 1 - n, 1), pltpu.roll(cvt_ref[n], WINDOW - 1, 1))


def _attn_sample_call(q, kv, ck, cv, layer, sink_row, g_rows, own_rows, *, name):
    n_seq = kv.shape[0]
    assert n_seq % SEQ_PER_STEP == 0
    nrow = SEQ_PER_STEP * N_HEADS
    cblk = pl.BlockSpec((SEQ_PER_STEP, WINDOW, 2 * HEAD_DIM), lambda i: (i, 0, 0))
    cin = pl.BlockSpec((None, SEQ_PER_STEP, WINDOW, 2 * HEAD_DIM), lambda i: (layer, i, 0, 0))
    qblk = pl.BlockSpec((nrow, LANES), lambda i: (i, 0))
    consts, cspecs = _consts([sink_row, g_rows, own_rows])
    return pl.pallas_call(
        _attn_sample_kernel,
        grid=(n_seq // SEQ_PER_STEP,),
        in_specs=[qblk, pl.BlockSpec((SEQ_PER_STEP, 2 * KV_WIDTH), lambda i: (i, 0)), cin, cin] + cspecs,
        out_specs=[qblk, cblk, cblk],
        out_shape=[jax.ShapeDtypeStruct(q.shape, F32),
                   jax.ShapeDtypeStruct(ck.shape[1:], F32),
                   jax.ShapeDtypeStruct(cv.shape[1:], F32)],
        compiler_params=pltpu.CompilerParams(dimension_semantics=("arbitrary",),
                                             vmem_limit_bytes=VMEM_LIMIT_BYTES),
        name=name,
    )(q, kv, ck, cv, *consts)


def _expand_heads(a):
    lead = a.shape[:-1]
    a = a.reshape(lead + (N_KV_HEADS, Q_PER_KV, 1, HEAD_DIM))
    sel = jnp.eye(N_KV_HEADS, dtype=a.dtype).reshape(N_KV_HEADS, 1, N_KV_HEADS, 1)
    return (a * sel).reshape(lead + (Q_EXP,))


def _pair_heads(a):
    lead = a.shape[:-1]
    a = a.reshape(lead + (N_KV_HEADS, Q_PER_KV, HEAD_DIM))
    return jnp.swapaxes(a, -3, -2).reshape(lead + (ATTN_WIDTH,))


def _prep_weights(ln_g, ln_b, ffn1_w_in, ffn1_w_out, ffn2_w_in, ffn2_w_out, w_in, ssm_lam_re, ssm_lam_im, ssm_log_dt,
                  ssm_b_re, ssm_b_im, ssm_c_re, ssm_c_im, ssm_d, glu_w, glu_b, attn_sinks, g_ssm_out, g_attn_out,
                  w_out):
    o1, o2 = SSM_WIDTH, SSM_WIDTH + ATTN_WIDTH
    rows_t = lambda a: jnp.swapaxes(a, 1, 2)
    params, a_flat = _ssm_table_inputs(ssm_lam_re, ssm_lam_im, ssm_log_dt, ssm_b_re, ssm_b_im, ssm_c_re, ssm_c_im)
    tables = _ssm_tables_call(params, a_flat)
    head_order = jnp.array([Q_PER_KV * kvh + o for kvh in range(N_KV_HEADS) for o in ATTN_HEAD_ORDER])
    g_exp = _expand_heads(g_attn_out).reshape(DEPTH, N_HEADS, LANES)
    w_in_b = w_in.astype(BF16)
    w_out_b = w_out.astype(BF16)
    stacked = dict(
        w_in=w_in_b,
        w_in_exp=jnp.concatenate([w_in_b[:, :, :o1], _expand_heads(w_in_b[:, :, o1:o2]), w_in_b[:, :, o2:]], axis=2),
        wo_s=w_out_b[:, :SSM_WIDTH],
        wo_a=rows_t(_pair_heads(rows_t(w_out_b[:, SSM_WIDTH:]))),
        wo_a_exp=rows_t(_expand_heads(rows_t(w_out_b[:, SSM_WIDTH:]))),
        a_flat=a_flat,
        d_1=ssm_d.reshape(DEPTH, N_SLAB, 1, LANES),
        gluw=glu_w.astype(BF16),
        glub=glu_b.reshape(DEPTH, 1, 2 * SSM_WIDTH),
        g_ssm=g_ssm_out.reshape(DEPTH, 1, SSM_WIDTH),
        sinks=jnp.repeat(attn_sinks[:, head_order], WINDOW, axis=1)[:, None, :],
        g_pair=_pair_heads(g_attn_out).reshape(DEPTH, Q_PER_KV, 1, LANES),
        sink_rows=jnp.tile(attn_sinks, (1, SEQ_PER_STEP))[:, :, None],
        g_rows=jnp.tile(g_exp, (1, SEQ_PER_STEP, 1)),
    )
    w = {name: [(arr, l) for l in range(DEPTH)] for name, arr in stacked.items()}
    w['ffn1'] = [(ffn1_w_in, ffn1_w_out, l) for l in range(DEPTH)]
    w['ffn2'] = [(ffn2_w_in, ffn2_w_out, l) for l in range(DEPTH)]
    w['tab'] = [tuple((t, l) for t in tables) for l in range(DEPTH)]
    n_ln = ln_g.shape[1]
    ln_g3 = ln_g.reshape(DEPTH * n_ln, 1, D_MODEL)
    ln_b3 = ln_b.reshape(DEPTH * n_ln, 1, D_MODEL)
    w['ln'] = [[((ln_g3, l * n_ln + i), (ln_b3, l * n_ln + i)) for i in range(n_ln)] for l in range(DEPTH)]
    w['own_rows'] = jnp.tile(_expand_heads(jnp.ones((ATTN_WIDTH,), F32)).reshape(N_HEADS, LANES), (SEQ_PER_STEP, 1))
    return w


def _prompt_mixer(u, q, kv, l, w, batch, seq):
    ssm_n, hre, him = _ssm_prompt_call(u, w['tab'][l], w['d_1'][l], w['gluw'][l],
                                       w['glub'][l], w['g_ssm'][l], batch=batch, seq=seq, name=f"p_ssm_{l}")
    att_n = _attn_prompt_call(q, kv, w['sinks'][l], w['g_pair'][l], batch=batch, seq=seq, name=f"p_attn_{l}")
    kvw = kv.reshape(batch, seq, 2 * KV_WIDTH)[:, -WINDOW:].reshape(batch, WINDOW, 2, N_KV_HEADS, HEAD_DIM)
    return (ssm_n, att_n, hre.reshape(batch, N_SSM_GROUPS, SSM_STATE), him.reshape(batch, N_SSM_GROUPS, SSM_STATE),
            kvw[:, :, 0], kvw[:, :, 1])


def _sample_mixer(u, q, kv, l, w, h0_re, h0_im, k_buf, v_buf):
    n_seq = kv.shape[0]
    ssm_n, hre, him = _ssm_sample_call(u, h0_re.reshape(n_seq, N_STATES), h0_im.reshape(n_seq, N_STATES),
                                       w['tab'][l], w['a_flat'][l], w['d_1'][l], w['gluw'][l], w['glub'][l], w['g_ssm'][l],
                                       name=f"s_ssm_{l}")
    att, kn, vn = _attn_sample_call(q.reshape(n_seq * N_HEADS, LANES), kv, k_buf, v_buf, l,
                                    w['sink_rows'][l], w['g_rows'][l], w['own_rows'], name=f"s_attn_{l}")
    untranspose = lambda t: jnp.transpose(t.reshape(n_seq, N_KV_HEADS, HEAD_DIM, WINDOW), (0, 3, 1, 2))
    return (ssm_n, att.reshape(n_seq, Q_EXP), hre.reshape(n_seq, N_SSM_GROUPS, SSM_STATE),
            him.reshape(n_seq, N_SSM_GROUPS, SSM_STATE), untranspose(kn), untranspose(vn))


def _trunk(x, h0_re, h0_im, k_buf, v_buf, w, *, tm, tag):
    Bn, L, _ = x.shape
    sample = k_buf is not None
    q_dtype = F32 if sample else BF16
    x = x.reshape(Bn * L, D_MODEL)
    ks, vs, hrs, his = [], [], [], []
    mix = None
    for l in range(DEPTH):
        if l > 0:
            x = _rows_call(x, w['ffn2'][l - 1], w['ln'][l - 1][2], mix=mix, tm=tm, name=f"{tag}_mix_ffn2_{l - 1}")[0]
        proj = w['w_in_exp'][l] if sample else w['w_in'][l]
        x, u, q, kv = _rows_call(x, w['ffn1'][l], w['ln'][l][0], proj=proj, tm=tm, q_dtype=q_dtype,
                                 name=f"{tag}_ffn1_{l}")
        if sample:
            ssm_n, att_n, hre, him, kn, vn = _sample_mixer(u, q, kv, l, w, h0_re[l], h0_im[l], k_buf, v_buf)
        else:
            ssm_n, att_n, hre, him, kn, vn = _prompt_mixer(u, q, kv, l, w, Bn, L)
        wo_a = w['wo_a_exp'][l] if sample else w['wo_a'][l]
        mix = (ssm_n, att_n, w['wo_s'][l], wo_a, w['ln'][l][1][0], w['ln'][l][1][1])
        ks.append(kn); vs.append(vn); hrs.append(hre); his.append(him)
    x = _rows_call(x, w['ffn2'][DEPTH - 1], w['ln'][DEPTH - 1][2], mix=mix, tm=tm, name=f"{tag}_mix_ffn2_{DEPTH - 1}")[0]
    return x.reshape(Bn, L, D_MODEL), jnp.stack(ks), jnp.stack(vs), jnp.stack(hrs), jnp.stack(his)


def kernel(x_prompt, x_sample, cache_k_win, cache_v_win, state_ssm_re, state_ssm_im, ln_g, ln_b, ffn1_w_in, ffn1_w_out, ffn2_w_in, ffn2_w_out, w_in, ssm_lam_re, ssm_lam_im, ssm_log_dt, ssm_b_re, ssm_b_im, ssm_c_re, ssm_c_im, ssm_d, glu_w, glu_b, attn_sinks, g_ssm_out, g_attn_out, w_out):
    w = _prep_weights(ln_g, ln_b, ffn1_w_in, ffn1_w_out, ffn2_w_in, ffn2_w_out, w_in, ssm_lam_re, ssm_lam_im,
                      ssm_log_dt, ssm_b_re, ssm_b_im, ssm_c_re, ssm_c_im, ssm_d, glu_w, glu_b, attn_sinks,
                      g_ssm_out, g_attn_out, w_out)
    y_prompt, kp, vp, hrp, hip = _trunk(x_prompt, None, None, None, None, w, tm=512, tag="p")
    n_seq = x_sample.shape[0]
    transposed = lambda c: jnp.transpose(c, (0, 1, 3, 4, 2)).reshape(DEPTH, n_seq, N_KV_HEADS * HEAD_DIM, WINDOW)
    y_sample, ks_, vs_, hrs, his = _trunk(x_sample, state_ssm_re, state_ssm_im, transposed(cache_k_win),
                                          transposed(cache_v_win), w, tm=128, tag="s")
    return (y_prompt, y_sample, kp, vp, hrp, hip, ks_, vs_, hrs, his)
```

```python
import functools

import jax
import jax.numpy as jnp
from jax import lax
from jax.experimental import pallas as pl
from jax.experimental.pallas import tpu as pltpu

D_MODEL = 1024
DEPTH = 2
SSM_WIDTH = 512
SSM_GROUP = 16
N_SSM_GROUPS = 32
SSM_STATE = 64
N_STATES = N_SSM_GROUPS * SSM_STATE
ATTN_WIDTH = 512
HEAD_DIM = 64
N_HEADS = 8
N_KV_HEADS = 2
Q_PER_KV = 4
KV_WIDTH = 128
WINDOW = 128
ATTN_SCALE = HEAD_DIM ** -0.5
D_FF = 2816
ALPHA = (2.0 * DEPTH) ** 0.25
LN_EPS = 1e-5
RMS_EPS = 1e-6

ROW_GROUP = 256
W_STAGE_SLOTS = 4
W_GU_STAGE_ROWS = 64
W_DN_STAGE_ROWS = 352
LANES = 128
SUBLANES = 8
VMEM_LIMIT_BYTES = 56 * 1024 * 1024
N_SLAB = SSM_WIDTH // LANES
GROUPS_PER_SLAB = LANES // SSM_GROUP
SLAB_STATES = GROUPS_PER_SLAB * SSM_STATE
Q_EXP = N_HEADS * LANES
SSM_CHUNK = 8
SSM_ROWS = 2048
ATTN_BLOCKS = 4
ATTN_HEAD_ORDER = (0, 2, 1, 3)
MASKED = -1e30

F32 = jnp.float32
BF16 = jnp.bfloat16


def _layer_norm(r, g, b):
    mu = jnp.mean(r, axis=-1, keepdims=True)
    c = r - mu
    var = jnp.mean(c * c, axis=-1, keepdims=True)
    return c * lax.rsqrt(var + LN_EPS) * g + b


def _rms_norm(y, g):
    return y * lax.rsqrt(jnp.mean(y * y, axis=-1, keepdims=True) + RMS_EPS) * g


def _const(c, n_grid=1):
    if isinstance(c, tuple):
        arr, idx = c
        shape = (None,) + arr.shape[1:]
        index = (idx,) + (0,) * (arr.ndim - 1)
    else:
        arr, shape, index = c, c.shape, (0,) * c.ndim
    imap = (lambda i: index) if n_grid == 1 else (lambda i, j: index)
    return arr, pl.BlockSpec(shape, imap, pipeline_mode=pl.Buffered(1))


def _consts(cs, n_grid=1):
    arrs, specs = zip(*[_const(c, n_grid) for c in cs])
    return list(arrs), list(specs)


def _load_cast(src_hbm, layer, dst_ref, stage_ref, sem_ref):
    n_slots, rows = stage_ref.shape[:2]
    n_chunks = dst_ref.shape[0] // rows

    def copy(c):
        return pltpu.make_async_copy(src_hbm.at[layer, pl.ds(c * rows, rows), :], stage_ref.at[c % n_slots],
                                     sem_ref.at[c % n_slots])

    for c in range(min(n_slots - 1, n_chunks)):
        copy(c).start()
    for c in range(n_chunks):
        copy(c).wait()
        dst_ref[pl.ds(c * rows, rows), :] = stage_ref[c % n_slots].astype(BF16)
        if c + n_slots - 1 < n_chunks:
            copy(c + n_slots - 1).start()


def _rows_kernel(*refs, has_mix, has_proj, n_sub, layer):
    it = iter(refs)
    x_ref = next(it)
    if has_mix:
        ssm_ref, att_ref, wo_s_ref, wo_a_ref, gm_ref, bm_ref = (next(it) for _ in range(6))
    wgu_hbm, wdn_hbm, g_ref, b_ref = (next(it) for _ in range(4))
    if has_proj:
        wp_ref = next(it)
    o_ref = next(it)
    if has_proj:
        u_ref, q_ref, kv_ref = (next(it) for _ in range(3))
    wgu_ref, wdn_ref, stage_gu, stage_dn, sem_gu, sem_dn = (next(it) for _ in range(6))

    @pl.when(pl.program_id(0) == 0)
    def _():
        _load_cast(wgu_hbm, layer, wgu_ref, stage_gu, sem_gu)
        _load_cast(wdn_hbm, layer, wdn_ref, stage_dn, sem_dn)

    tm = x_ref.shape[0]
    sub = tm // n_sub
    groups = [slice(r * sub, (r + 1) * sub) for r in range(n_sub)]
    xs = [x_ref[rows, :] for rows in groups]
    if has_mix:
        ms = []
        for rows in groups:
            ssm = jnp.concatenate([ssm_ref[j, rows, :] for j in range(N_SLAB)], axis=1).astype(BF16)
            att = att_ref[rows, :].astype(BF16)
            ms.append(jnp.dot(ssm, wo_s_ref[...], preferred_element_type=F32)
                      + jnp.dot(att, wo_a_ref[...], preferred_element_type=F32))
        xs = [_layer_norm(ALPHA * x + m, gm_ref[...], bm_ref[...]) for x, m in zip(xs, ms)]
    gus = [jnp.dot(x.astype(BF16), wgu_ref[...], preferred_element_type=F32) for x in xs]
    hs = [(gu[:, :D_FF] * jax.nn.sigmoid(gu[:, :D_FF]) * gu[:, D_FF:]).astype(BF16) for gu in gus]
    ys = [jnp.dot(h, wdn_ref[...], preferred_element_type=F32) for h in hs]
    xs = [_layer_norm(ALPHA * x + 0.5 * y, g_ref[...], b_ref[...]) for x, y in zip(xs, ys)]
    for rows, x in zip(groups, xs):
        o_ref[rows, :] = x
    if has_proj:
        q_cols = q_ref.shape[1]
        zs = [jnp.dot(x.astype(BF16), wp_ref[...], preferred_element_type=F32) for x in xs]
        for rows, z in zip(groups, zs):
            for j in range(N_SLAB):
                u_ref[j, rows, :] = z[:, j * LANES:(j + 1) * LANES]
            q_ref[rows, :] = z[:, SSM_WIDTH:SSM_WIDTH + q_cols].astype(q_ref.dtype)
            kv_ref[rows, :] = z[:, SSM_WIDTH + q_cols:]


def _rows_call(x, ffn, ln, mix=None, proj=None, *, tm, q_dtype=BF16, name):
    M = x.shape[0]
    assert M % tm == 0
    row = lambda w: pl.BlockSpec((tm, w), lambda i: (i, 0))
    slab = lambda n: pl.BlockSpec((n, tm, LANES), lambda i: (0, i, 0))
    args = [x]
    specs = [row(D_MODEL)]
    if mix is not None:
        ssm_n, att_n = mix[:2]
        arrs, cspecs = _consts(mix[2:])
        args += [ssm_n, att_n] + arrs
        specs += [slab(N_SLAB), row(att_n.shape[1])] + cspecs
    w_gu, w_dn, layer = ffn
    arrs, cspecs = _consts(list(ln))
    args += [w_gu, w_dn] + arrs
    specs += [pl.BlockSpec(memory_space=pl.ANY), pl.BlockSpec(memory_space=pl.ANY)] + cspecs
    out_shape = [jax.ShapeDtypeStruct((M, D_MODEL), F32)]
    out_specs = [row(D_MODEL)]
    if proj is not None:
        arr, cspec = _const(proj)
        args.append(arr)
        specs.append(cspec)
        q_cols = arr.shape[-1] - SSM_WIDTH - 2 * KV_WIDTH
        out_shape += [jax.ShapeDtypeStruct((N_SLAB, M, LANES), F32),
                      jax.ShapeDtypeStruct((M, q_cols), q_dtype),
                      jax.ShapeDtypeStruct((M, 2 * KV_WIDTH), F32)]
        out_specs += [slab(N_SLAB), row(q_cols), row(2 * KV_WIDTH)]
    return pl.pallas_call(
        functools.partial(_rows_kernel, has_mix=mix is not None, has_proj=proj is not None,
                          n_sub=max(1, tm // ROW_GROUP), layer=layer),
        grid=(M // tm,),
        in_specs=specs,
        out_specs=out_specs,
        out_shape=out_shape,
        scratch_shapes=[pltpu.VMEM(w_gu.shape[1:], BF16), pltpu.VMEM(w_dn.shape[1:], BF16),
                        pltpu.VMEM((W_STAGE_SLOTS, W_GU_STAGE_ROWS, w_gu.shape[2]), F32),
                        pltpu.VMEM((W_STAGE_SLOTS, W_DN_STAGE_ROWS, w_dn.shape[2]), F32),
                        pltpu.SemaphoreType.DMA((W_STAGE_SLOTS,)), pltpu.SemaphoreType.DMA((W_STAGE_SLOTS,))],
        compiler_params=pltpu.CompilerParams(dimension_semantics=("arbitrary",),
                                             vmem_limit_bytes=VMEM_LIMIT_BYTES),
        name=name,
    )(*args)


def _cmul(xr, xi, yr, yi):
    return xr * yr - xi * yi, xr * yi + xi * yr


def _ssm_tables_kernel(p_ref, af_ref, wst_ref, cat_ref, strip_ref, ak_ref, tr_ref):
    t1 = SSM_CHUNK
    a_re, a_im, lam_re, lam_im, b_re, b_im, c_re, c_im = (p_ref[k] for k in range(8))
    num_re, num_im = a_re - 1.0, a_im
    den = lam_re * lam_re + lam_im * lam_im
    f_re = (num_re * lam_re + num_im * lam_im) / den
    f_im = (num_im * lam_re - num_re * lam_im) / den
    bb_re, bb_im = _cmul(f_re, f_im, b_re, b_im)

    def same_group(shape, row_div, col_div):
        r = lax.broadcasted_iota(jnp.int32, shape, 0) // row_div
        c = lax.broadcasted_iota(jnp.int32, shape, 1) // col_div
        return (r == c).astype(F32)

    m_state = same_group((LANES, SLAB_STATES), SSM_GROUP, SSM_STATE)
    m_chan = 0.5 * same_group((LANES, LANES), SSM_GROUP, SSM_GROUP)

    def expand(x):
        return (jnp.concatenate([x] * (SLAB_STATES // LANES), axis=1) * m_state).astype(BF16)

    def dot_nt(a, b):
        return lax.dot_general(a, b, (((1,), (1,)), ((), ())), precision=lax.Precision.HIGHEST,
                               preferred_element_type=F32)

    pw_re, pw_im = jnp.ones_like(a_re), jnp.zeros_like(a_re)
    kd = []
    for l in range(t1):
        ab_re, ab_im = _cmul(pw_re, pw_im, bb_re, bb_im)
        s = t1 - 1 - l
        kd_l = []
        for j in range(N_SLAB):
            rows = slice(j * LANES, (j + 1) * LANES)
            wst_ref[j, s * LANES:(s + 1) * LANES, :SLAB_STATES] = expand(ab_re[rows])
            wst_ref[j, s * LANES:(s + 1) * LANES, SLAB_STATES:] = expand(ab_im[rows])
            k = dot_nt(ab_re[rows], c_re[rows]) - dot_nt(ab_im[rows], c_im[rows])
            kd_l.append((k * m_chan).astype(BF16))
        kd.append(kd_l)
        pw_re, pw_im = _cmul(pw_re, pw_im, a_re, a_im)
        cf_re = c_re * pw_re - c_im * pw_im
        cf_im = -(c_re * pw_im + c_im * pw_re)
        for j in range(N_SLAB):
            rows = slice(j * LANES, (j + 1) * LANES)
            cat_ref[j, l * LANES:(l + 1) * LANES, :SLAB_STATES] = expand(cf_re[rows])
            cat_ref[j, l * LANES:(l + 1) * LANES, SLAB_STATES:] = expand(cf_im[rows])
    for j in range(N_SLAB):
        for rho in range(t1):
            for c in range(2):
                lag = t1 - 2 - rho + c
                blk = kd[lag][j] if lag >= 0 else jnp.zeros((LANES, LANES), BF16)
                strip_ref[j, rho * LANES:(rho + 1) * LANES, c * LANES:(c + 1) * LANES] = blk

    f_re, f_im = af_ref[0], af_ref[1]
    base_re, base_im = f_re, f_im
    for _ in range(t1 - 1):
        base_re, base_im = _cmul(base_re, base_im, f_re, f_im)
    row = lax.broadcasted_iota(jnp.int32, (SUBLANES, N_STATES), 0)
    pws = [(base_re, base_im)]
    for _ in range(SUBLANES - 1):
        pws.append(_cmul(pws[-1][0], pws[-1][1], base_re, base_im))
    for n, k in enumerate((1, 2, 4)):
        for part in range(2):
            ak_ref[2 * n + part] = jnp.where(row >= k, jnp.broadcast_to(pws[k - 1][part], row.shape), 0.0)
    for part in range(2):
        acc = jnp.zeros(row.shape, F32)
        for r in range(SUBLANES):
            acc = jnp.where(row == r, jnp.broadcast_to(pws[r][part], row.shape), acc)
        tr_ref[part] = acc


def _ssm_tables_call(params, a_flat):
    t1 = SSM_CHUNK
    shapes = [((N_SLAB, t1 * LANES, 2 * SLAB_STATES), BF16),
              ((N_SLAB, t1 * LANES, 2 * SLAB_STATES), BF16),
              ((N_SLAB, t1 * LANES, 2 * LANES), BF16),
              ((6, SUBLANES, N_STATES), F32),
              ((2, SUBLANES, N_STATES), F32)]
    per_layer = lambda shape: pl.BlockSpec((None,) + tuple(shape), lambda l: (l,) + (0,) * len(shape))
    return pl.pallas_call(
        _ssm_tables_kernel,
        grid=(DEPTH,),
        in_specs=[per_layer(params.shape[1:]), per_layer(a_flat.shape[1:])],
        out_specs=[per_layer(s) for s, _ in shapes],
        out_shape=[jax.ShapeDtypeStruct((DEPTH,) + s, d) for s, d in shapes],
        compiler_params=pltpu.CompilerParams(dimension_semantics=("arbitrary",),
                                             vmem_limit_bytes=VMEM_LIMIT_BYTES),
        name="ssm_tables",
    )(params, a_flat)


def _ssm_table_inputs(lam_re, lam_im, log_dt, b_re, b_im, c_re, c_im):
    dt = jnp.exp(log_dt)[..., None]
    mag = jnp.exp(lam_re * dt)
    a_re = mag * jnp.cos(lam_im * dt)
    a_im = mag * jnp.sin(lam_im * dt)
    rep = lambda x: jnp.repeat(x, SSM_GROUP, axis=1)
    flat = lambda x: x.reshape(DEPTH, SSM_WIDTH, SSM_STATE)
    params = jnp.stack([rep(a_re), rep(a_im), rep(lam_re), rep(lam_im),
                        flat(jnp.swapaxes(b_re, 2, 3)), flat(jnp.swapaxes(b_im, 2, 3)), flat(c_re), flat(c_im)],
                       axis=1)
    params = jnp.concatenate([params, params], axis=-1)
    a_flat = jnp.stack([a_re.reshape(DEPTH, 1, N_STATES), a_im.reshape(DEPTH, 1, N_STATES)], axis=1)
    return params, a_flat


def _glu_rms(y, gluw_ref, glub_ref, g_ref):
    g = jax.nn.gelu(y).astype(BF16)
    zz = jnp.dot(g, gluw_ref[...], preferred_element_type=F32) + glub_ref[...]
    o = zz[:, :SSM_WIDTH] * jax.nn.sigmoid(zz[:, SSM_WIDTH:])
    return _rms_norm(o, g_ref[...])


def _ssm_kernel(u_ref, wst_ref, cat_ref, strip_ref, ak_ref, tr_ref, d_ref, gluw_ref, glub_ref, g_ref,
                o_ref, hre_ref, him_ref,
                ucb_ref, sre_ref, sim_ref, hcr_ref, hci_ref):
    t1 = SSM_CHUNK
    nc = SSM_ROWS // t1
    i = pl.program_id(1)

    @pl.when(i == 0)
    def _():
        hcr_ref[...] = jnp.zeros_like(hcr_ref)
        hci_ref[...] = jnp.zeros_like(hci_ref)

    for j in range(N_SLAB):
        for s in range(t1):
            ucb_ref[j, :, s * LANES:(s + 1) * LANES] = u_ref[j, pl.ds(s, nc, stride=t1), :].astype(BF16)

    for j in range(N_SLAB):
        st = jnp.dot(ucb_ref[j], wst_ref[j], preferred_element_type=F32)
        sre_ref[:, j * SLAB_STATES:(j + 1) * SLAB_STATES] = st[:, :SLAB_STATES]
        sim_ref[:, j * SLAB_STATES:(j + 1) * SLAB_STATES] = st[:, SLAB_STATES:]

    row = lax.broadcasted_iota(jnp.int32, (SUBLANES, SLAB_STATES), 0)
    for j in range(N_SLAB):
        cols = pl.ds(j * SLAB_STATES, SLAB_STATES)

        def body(gi, carry, cols=cols):
            hr, hi = carry
            r0 = pl.multiple_of(gi * SUBLANES, SUBLANES)
            xr = sre_ref[pl.ds(r0, SUBLANES), cols]
            xi = sim_ref[pl.ds(r0, SUBLANES), cols]
            for n, k in enumerate((1, 2, 4)):
                akr = ak_ref[2 * n, :, cols]
                aki = ak_ref[2 * n + 1, :, cols]
                sr = pltpu.roll(xr, k, 0)
                si = pltpu.roll(xi, k, 0)
                xr, xi = xr + akr * sr - aki * si, xi + akr * si + aki * sr
            tr = tr_ref[0, :, cols]
            ti = tr_ref[1, :, cols]
            er = xr + tr * hr - ti * hi
            ei = xi + tr * hi + ti * hr
            sre_ref[pl.ds(r0, SUBLANES), cols] = jnp.where(row >= 1, pltpu.roll(er, 1, 0), hr)
            sim_ref[pl.ds(r0, SUBLANES), cols] = jnp.where(row >= 1, pltpu.roll(ei, 1, 0), hi)
            last = SUBLANES - 1
            return (jnp.broadcast_to(er[last:last + 1], er.shape), jnp.broadcast_to(ei[last:last + 1], ei.shape))

        hr, hi = lax.fori_loop(0, nc // SUBLANES, body, (hcr_ref[:, cols], hci_ref[:, cols]), unroll=True)
        hcr_ref[:, cols] = hr
        hci_ref[:, cols] = hi

    hre_ref[0] = hcr_ref[...]
    him_ref[0] = hci_ref[...]

    ys = []
    for j in range(N_SLAB):
        cols = slice(j * SLAB_STATES, (j + 1) * SLAB_STATES)
        hprev = jnp.concatenate([sre_ref[:, cols], sim_ref[:, cols]], axis=1).astype(BF16)
        y = lax.dot_general(hprev, cat_ref[j], (((1,), (1,)), ((), ())), preferred_element_type=F32)
        parts = []
        for p2 in range(t1 // 2):
            kk = (2 * p2 + 2) * LANES
            yi = jnp.dot(ucb_ref[j, :, :kk], strip_ref[j, (t1 - 2 - 2 * p2) * LANES:, :],
                         preferred_element_type=F32)
            parts.append(y[:, 2 * p2 * LANES:(2 * p2 + 2) * LANES] + yi)
        ys.append(jnp.concatenate(parts, axis=1))

    for t in range(t1):
        y = jnp.concatenate([ys[j][:, t * LANES:(t + 1) * LANES] + d_ref[j] * u_ref[j, pl.ds(t, nc, stride=t1), :]
                             for j in range(N_SLAB)], axis=1)
        n = _glu_rms(y, gluw_ref, glub_ref, g_ref)
        for j in range(N_SLAB):
            o_ref[j, pl.ds(t, nc, stride=t1), :] = n[:, j * LANES:(j + 1) * LANES]


def _ssm_prompt_call(u, tab, d_t, gluw, glub, g, *, batch, seq, name):
    t1 = SSM_CHUNK
    nc = SSM_ROWS // t1
    steps = seq // SSM_ROWS
    blk = pl.BlockSpec((N_SLAB, SSM_ROWS, LANES), lambda b, i: (0, b * steps + i, 0))
    st = pl.BlockSpec((1, SUBLANES, N_STATES), lambda b, i: (b, 0, 0))
    consts, cspecs = _consts(list(tab) + [d_t, gluw, glub, g], 2)
    out, hre, him = pl.pallas_call(
        _ssm_kernel,
        grid=(batch, steps),
        in_specs=[blk] + cspecs,
        out_specs=[blk, st, st],
        out_shape=[jax.ShapeDtypeStruct(u.shape, F32),
                   jax.ShapeDtypeStruct((batch, SUBLANES, N_STATES), F32),
                   jax.ShapeDtypeStruct((batch, SUBLANES, N_STATES), F32)],
        scratch_shapes=[pltpu.VMEM((N_SLAB, nc, t1 * LANES), BF16),
                        pltpu.VMEM((nc, N_STATES), F32),
                        pltpu.VMEM((nc, N_STATES), F32),
                        pltpu.VMEM((SUBLANES, N_STATES), F32),
                        pltpu.VMEM((SUBLANES, N_STATES), F32)],
        compiler_params=pltpu.CompilerParams(dimension_semantics=("arbitrary", "arbitrary"),
                                             vmem_limit_bytes=VMEM_LIMIT_BYTES),
        name=name,
    )(u, *consts)
    return out, hre[:, 0], him[:, 0]


def _ssm_sample_kernel(u_ref, h0r_ref, h0i_ref, wst_ref, cat_ref, kd_ref, a_ref, d_ref, gluw_ref, glub_ref, g_ref,
                       o_ref, hr_ref, hi_ref):
    ys = []
    for j in range(N_SLAB):
        cols = slice(j * SLAB_STATES, (j + 1) * SLAB_STATES)
        uf = u_ref[j]
        ub = uf.astype(BF16)
        st = jnp.dot(ub, wst_ref[j], preferred_element_type=F32)
        h0r = h0r_ref[:, cols]
        h0i = h0i_ref[:, cols]
        ar = a_ref[0, :, cols]
        ai = a_ref[1, :, cols]
        hr_ref[:, cols] = ar * h0r - ai * h0i + st[:, :SLAB_STATES]
        hi_ref[:, cols] = ar * h0i + ai * h0r + st[:, SLAB_STATES:]
        hcat = jnp.concatenate([h0r, h0i], axis=1).astype(BF16)
        y = lax.dot_general(hcat, cat_ref[j], (((1,), (1,)), ((), ())), preferred_element_type=F32)
        y = y + jnp.dot(ub, kd_ref[j], preferred_element_type=F32)
        ys.append(y + d_ref[j] * uf)
    n = _glu_rms(jnp.concatenate(ys, axis=1), gluw_ref, glub_ref, g_ref)
    for j in range(N_SLAB):
        o_ref[j] = n[:, j * LANES:(j + 1) * LANES]


def _ssm_sample_call(u, h0r, h0i, tab, a_flat, d_t, gluw, glub, g, *, name):
    n_seq = u.shape[1]
    (wst, layer), (cat, _), (strip, _) = tab[:3]
    last = SSM_CHUNK - 1
    whole = lambda a: pl.BlockSpec(a.shape, lambda i: (0,) * len(a.shape))
    outs = [jax.ShapeDtypeStruct(u.shape, F32),
            jax.ShapeDtypeStruct((n_seq, N_STATES), F32),
            jax.ShapeDtypeStruct((n_seq, N_STATES), F32)]
    consts, cspecs = _consts([a_flat, d_t, gluw, glub, g])
    return pl.pallas_call(
        _ssm_sample_kernel,
        grid=(1,),
        in_specs=[whole(u), whole(h0r), whole(h0i),
                  pl.BlockSpec((None, N_SLAB, LANES, 2 * SLAB_STATES), lambda i: (layer, 0, last, 0)),
                  pl.BlockSpec((None, N_SLAB, LANES, 2 * SLAB_STATES), lambda i: (layer, 0, 0, 0)),
                  pl.BlockSpec((None, N_SLAB, LANES, LANES), lambda i: (layer, 0, last, 1))] + cspecs,
        out_specs=[whole(o) for o in outs],
        out_shape=outs,
        compiler_params=pltpu.CompilerParams(dimension_semantics=("arbitrary",),
                                             vmem_limit_bytes=VMEM_LIMIT_BYTES),
        name=name,
    )(u, h0r, h0i, wst, cat, strip, *consts)


def _attn_kernel(sink_ref, q_ref, kvc_ref, kvp_ref, g_ref, o_ref):
    i = pl.program_id(1)
    kj = lax.broadcasted_iota(jnp.int32, (2 * WINDOW, WINDOW), 0)
    qi = lax.broadcasted_iota(jnp.int32, (2 * WINDOW, WINDOW), 1)
    band = (kj >= qi) & (kj <= qi + WINDOW)
    low = qi < HEAD_DIM
    for sb in range(ATTN_BLOCKS):
        blk = slice(sb * WINDOW, (sb + 1) * WINDOW)
        kc = kvc_ref[blk, :]
        if sb == 0:
            kp = kvp_ref[...]
            valid = band & ((i > 0) | (kj >= WINDOW))
        else:
            kp = kvc_ref[(sb - 1) * WINDOW:sb * WINDOW, :]
            valid = band
        valid2 = jnp.concatenate([valid, valid], axis=1)
        kcat = jnp.concatenate([kp[:, :KV_WIDTH], kc[:, :KV_WIDTH]], axis=0) * ATTN_SCALE
        swap = pltpu.roll(kcat, HEAD_DIM, 1)
        zero = jnp.zeros_like(kcat)
        k_sel = [[jnp.where(low, kcat, zero).astype(BF16), jnp.where(low, zero, swap).astype(BF16)],
                 [jnp.where(low, swap, zero).astype(BF16), jnp.where(low, zero, kcat).astype(BF16)]]
        v_t = jnp.concatenate([kp[:, KV_WIDTH:], kc[:, KV_WIDTH:]], axis=0).T.astype(BF16)
        parts = []
        for kvh in range(N_KV_HEADS):
            q_pair = jnp.concatenate([q_ref[blk, (2 * kvh) * LANES:(2 * kvh + 1) * LANES],
                                      q_ref[blk, (2 * kvh + 1) * LANES:(2 * kvh + 2) * LANES]], axis=0)
            for parity in range(2):
                s_t = lax.dot_general(k_sel[kvh][parity], q_pair, (((1,), (1,)), ((), ())),
                                      preferred_element_type=F32)
                parts.append(jnp.where(valid2, s_t, MASKED))
        s_all = jnp.concatenate(parts, axis=1)
        sink = sink_ref[...]
        m = jnp.maximum(jnp.max(s_all, axis=0, keepdims=True), sink)
        p_all = jnp.exp(s_all - m)
        inv_den = 1.0 / (jnp.sum(p_all, axis=0, keepdims=True) + jnp.exp(sink - m))
        p_all = p_all.astype(BF16)
        half = Q_PER_KV * WINDOW
        o_kv = [jnp.dot(v_t[kvh * HEAD_DIM:(kvh + 1) * HEAD_DIM], p_all[:, kvh * half:(kvh + 1) * half],
                        preferred_element_type=F32) * inv_den[:, kvh * half:(kvh + 1) * half]
                for kvh in range(N_KV_HEADS)]
        tiles = []
        for pr in range(Q_PER_KV):
            lanes = slice(ATTN_HEAD_ORDER.index(pr) * WINDOW, (ATTN_HEAD_ORDER.index(pr) + 1) * WINDOW)
            tiles.append(jnp.concatenate([o_kv[0][:, lanes], o_kv[1][:, lanes]], axis=0))
        ss = None
        for tile in tiles:
            t = jnp.sum(tile * tile, axis=0, keepdims=True)
            ss = t if ss is None else ss + t
        inv = lax.rsqrt(ss / ATTN_WIDTH + RMS_EPS)
        for pr, tile in enumerate(tiles):
            o_ref[blk, pr * LANES:(pr + 1) * LANES] = ((tile * inv).T * g_ref[pr]).astype(o_ref.dtype)


def _attn_prompt_call(q, kv, sinks, g_perm, *, batch, seq, name):
    rows = ATTN_BLOCKS * WINDOW
    nb = seq // rows
    M = kv.shape[0]
    cur = lambda b, i: (b * nb + i, 0)
    prev = lambda b, i: ((b * nb + i) * ATTN_BLOCKS - jnp.minimum(i, 1), 0)
    (sinks, g_perm), (sink_spec, g_spec) = _consts([sinks, g_perm], 2)
    return pl.pallas_call(
        _attn_kernel,
        grid=(batch, nb),
        in_specs=[sink_spec,
                  pl.BlockSpec((rows, ATTN_WIDTH), cur),
                  pl.BlockSpec((rows, 2 * KV_WIDTH), cur),
                  pl.BlockSpec((WINDOW, 2 * KV_WIDTH), prev),
                  g_spec],
        out_specs=pl.BlockSpec((rows, ATTN_WIDTH), cur),
        out_shape=jax.ShapeDtypeStruct((M, ATTN_WIDTH), BF16),
        compiler_params=pltpu.CompilerParams(dimension_semantics=("arbitrary", "arbitrary"),
                                             vmem_limit_bytes=VMEM_LIMIT_BYTES),
        name=name,
    )(sinks, q, kv, kv, g_perm)


SEQ_PER_STEP = LANES // N_HEADS


def _attn_sample_kernel(q_ref, kv_ref, ckt_ref, cvt_ref, sink_ref, g_ref, own_ref, o_ref, kot_ref, vot_ref):
    nrow = SEQ_PER_STEP * N_HEADS
    qb = q_ref[...].astype(BF16)
    k_new = kv_ref[:, :KV_WIDTH]
    v_new = kv_ref[:, KV_WIDTH:]
    row_seq = lax.broadcasted_iota(jnp.int32, (nrow, LANES), 0) // N_HEADS
    lane_seq = lax.broadcasted_iota(jnp.int32, (nrow, LANES), 1) // N_HEADS
    lane = lax.broadcasted_iota(jnp.int32, (nrow, LANES), 1)
    pick = (lax.broadcasted_iota(jnp.int32, (nrow, SEQ_PER_STEP), 0) // N_HEADS
            == lax.broadcasted_iota(jnp.int32, (nrow, SEQ_PER_STEP), 1)).astype(BF16)

    qf = q_ref[...]
    heads_of = lambda a, n: a[n * N_HEADS:(n + 1) * N_HEADS]
    s = jnp.concatenate([jnp.dot(heads_of(qf, n).astype(BF16), ckt_ref[n].astype(BF16), preferred_element_type=F32)
                         for n in range(SEQ_PER_STEP)], axis=0) * ATTN_SCALE
    k_rows = jnp.dot(pick, k_new.astype(BF16), preferred_element_type=F32)
    s_new = jnp.sum(qb.astype(F32) * k_rows, axis=-1, keepdims=True) * ATTN_SCALE
    sink = sink_ref[...]
    m = jnp.maximum(jnp.maximum(jnp.max(s, axis=-1, keepdims=True), s_new), sink)
    p = jnp.exp(s - m)
    p_new = jnp.exp(s_new - m)
    inv = 1.0 / (jnp.sum(p, axis=-1, keepdims=True) + p_new + jnp.exp(sink - m))
    pn = p * inv
    o = jnp.concatenate([lax.dot_general(heads_of(pn, n).astype(BF16), cvt_ref[n].astype(BF16),
                                         (((1,), (1,)), ((), ())), preferred_element_type=F32)
                         for n in range(SEQ_PER_STEP)], axis=0)
    v_rows = jnp.dot(pick, v_new.astype(BF16), preferred_element_type=F32)
    o = o + (p_new * inv).astype(BF16).astype(F32) * v_rows

    hi = lax.Precision.HIGHEST
    row_sums = jnp.dot(o * o * own_ref[...], jnp.ones((LANES, LANES), F32), precision=hi, preferred_element_type=F32)
    ss = jnp.dot((row_seq == lane_seq).astype(F32), row_sums, precision=hi, preferred_element_type=F32)
    o_ref[...] = o * lax.rsqrt(ss / ATTN_WIDTH + RMS_EPS) * g_ref[...]

    pad = jnp.zeros((LANES - SEQ_PER_STEP, KV_WIDTH), F32)
    k_cols = jnp.concatenate([k_new, pad], axis=0).T
    v_cols = jnp.concatenate([v_new, pad], axis=0).T
    last = lane == WINDOW - 1
    for n in range(SEQ_PER_STEP):
        kot_ref[n] = jnp.where(last, pltpu.roll(k_cols, WINDOW - 1 - n, 1), pltpu.roll(ckt_ref[n], WINDOW - 1, 1))
        vot_ref[n] = jnp.where(last, pltpu.roll(v_cols, WINDOW - 1 - n, 1), pltpu.roll(cvt_ref[n], WINDOW - 1, 1))


def _attn_sample_call(q, kv, ck, cv, layer, sink_row, g_rows, own_rows, *, name):
    n_seq = kv.shape[0]
    assert n_seq % SEQ_PER_STEP == 0
    nrow = SEQ_PER_STEP * N_HEADS
    cblk = pl.BlockSpec((SEQ_PER_STEP, WINDOW, 2 * HEAD_DIM), lambda i: (i, 0, 0))
    cin = pl.BlockSpec((None, SEQ_PER_STEP, WINDOW, 2 * HEAD_DIM), lambda i: (layer, i, 0, 0))
    qblk = pl.BlockSpec((nrow, LANES), lambda i: (i, 0))
    consts, cspecs = _consts([sink_row, g_rows, own_rows])
    return pl.pallas_call(
        _attn_sample_kernel,
        grid=(n_seq // SEQ_PER_STEP,),
        in_specs=[qblk, pl.BlockSpec((SEQ_PER_STEP, 2 * KV_WIDTH), lambda i: (i, 0)), cin, cin] + cspecs,
        out_specs=[qblk, cblk, cblk],
        out_shape=[jax.ShapeDtypeStruct(q.shape, F32),
                   jax.ShapeDtypeStruct(ck.shape[1:], F32),
                   jax.ShapeDtypeStruct(cv.shape[1:], F32)],
        compiler_params=pltpu.CompilerParams(dimension_semantics=("arbitrary",),
                                             vmem_limit_bytes=VMEM_LIMIT_BYTES),
        name=name,
    )(q, kv, ck, cv, *consts)


def _expand_heads(a):
    lead = a.shape[:-1]
    a = a.reshape(lead + (N_KV_HEADS, Q_PER_KV, 1, HEAD_DIM))
    sel = jnp.eye(N_KV_HEADS, dtype=a.dtype).reshape(N_KV_HEADS, 1, N_KV_HEADS, 1)
    return (a * sel).reshape(lead + (Q_EXP,))


def _pair_heads(a):
    lead = a.shape[:-1]
    a = a.reshape(lead + (N_KV_HEADS, Q_PER_KV, HEAD_DIM))
    return jnp.swapaxes(a, -3, -2).reshape(lead + (ATTN_WIDTH,))


def _prep_weights(ln_g, ln_b, ffn1_w_in, ffn1_w_out, ffn2_w_in, ffn2_w_out, w_in, ssm_lam_re, ssm_lam_im, ssm_log_dt,
                  ssm_b_re, ssm_b_im, ssm_c_re, ssm_c_im, ssm_d, glu_w, glu_b, attn_sinks, g_ssm_out, g_attn_out,
                  w_out):
    o1, o2 = SSM_WIDTH, SSM_WIDTH + ATTN_WIDTH
    rows_t = lambda a: jnp.swapaxes(a, 1, 2)
    params, a_flat = _ssm_table_inputs(ssm_lam_re, ssm_lam_im, ssm_log_dt, ssm_b_re, ssm_b_im, ssm_c_re, ssm_c_im)
    tables = _ssm_tables_call(params, a_flat)
    head_order = jnp.array([Q_PER_KV * kvh + o for kvh in range(N_KV_HEADS) for o in ATTN_HEAD_ORDER])
    g_exp = _expand_heads(g_attn_out).reshape(DEPTH, N_HEADS, LANES)
    w_in_b = w_in.astype(BF16)
    w_out_b = w_out.astype(BF16)
    stacked = dict(
        w_in=w_in_b,
        w_in_exp=jnp.concatenate([w_in_b[:, :, :o1], _expand_heads(w_in_b[:, :, o1:o2]), w_in_b[:, :, o2:]], axis=2),
        wo_s=w_out_b[:, :SSM_WIDTH],
        wo_a=rows_t(_pair_heads(rows_t(w_out_b[:, SSM_WIDTH:]))),
        wo_a_exp=rows_t(_expand_heads(rows_t(w_out_b[:, SSM_WIDTH:]))),
        a_flat=a_flat,
        d_1=ssm_d.reshape(DEPTH, N_SLAB, 1, LANES),
        gluw=glu_w.astype(BF16),
        glub=glu_b.reshape(DEPTH, 1, 2 * SSM_WIDTH),
        g_ssm=g_ssm_out.reshape(DEPTH, 1, SSM_WIDTH),
        sinks=jnp.repeat(attn_sinks[:, head_order], WINDOW, axis=1)[:, None, :],
        g_pair=_pair_heads(g_attn_out).reshape(DEPTH, Q_PER_KV, 1, LANES),
        sink_rows=jnp.tile(attn_sinks, (1, SEQ_PER_STEP))[:, :, None],
        g_rows=jnp.tile(g_exp, (1, SEQ_PER_STEP, 1)),
    )
    w = {name: [(arr, l) for l in range(DEPTH)] for name, arr in stacked.items()}
    w['ffn1'] = [(ffn1_w_in, ffn1_w_out, l) for l in range(DEPTH)]
    w['ffn2'] = [(ffn2_w_in, ffn2_w_out, l) for l in range(DEPTH)]
    w['tab'] = [tuple((t, l) for t in tables) for l in range(DEPTH)]
    n_ln = ln_g.shape[1]
    ln_g3 = ln_g.reshape(DEPTH * n_ln, 1, D_MODEL)
    ln_b3 = ln_b.reshape(DEPTH * n_ln, 1, D_MODEL)
    w['ln'] = [[((ln_g3, l * n_ln + i), (ln_b3, l * n_ln + i)) for i in range(n_ln)] for l in range(DEPTH)]
    w['own_rows'] = jnp.tile(_expand_heads(jnp.ones((ATTN_WIDTH,), F32)).reshape(N_HEADS, LANES), (SEQ_PER_STEP, 1))
    return w


def _prompt_mixer(u, q, kv, l, w, batch, seq):
    ssm_n, hre, him = _ssm_prompt_call(u, w['tab'][l], w['d_1'][l], w['gluw'][l],
                                       w['glub'][l], w['g_ssm'][l], batch=batch, seq=seq, name=f"p_ssm_{l}")
    att_n = _attn_prompt_call(q, kv, w['sinks'][l], w['g_pair'][l], batch=batch, seq=seq, name=f"p_attn_{l}")
    kvw = kv.reshape(batch, seq, 2 * KV_WIDTH)[:, -WINDOW:].reshape(batch, WINDOW, 2, N_KV_HEADS, HEAD_DIM)
    return (ssm_n, att_n, hre.reshape(batch, N_SSM_GROUPS, SSM_STATE), him.reshape(batch, N_SSM_GROUPS, SSM_STATE),
            kvw[:, :, 0], kvw[:, :, 1])


def _sample_mixer(u, q, kv, l, w, h0_re, h0_im, k_buf, v_buf):
    n_seq = kv.shape[0]
    ssm_n, hre, him = _ssm_sample_call(u, h0_re.reshape(n_seq, N_STATES), h0_im.reshape(n_seq, N_STATES),
                                       w['tab'][l], w['a_flat'][l], w['d_1'][l], w['gluw'][l], w['glub'][l], w['g_ssm'][l],
                                       name=f"s_ssm_{l}")
    att, kn, vn = _attn_sample_call(q.reshape(n_seq * N_HEADS, LANES), kv, k_buf, v_buf, l,
                                    w['sink_rows'][l], w['g_rows'][l], w['own_rows'], name=f"s_attn_{l}")
    untranspose = lambda t: jnp.transpose(t.reshape(n_seq, N_KV_HEADS, HEAD_DIM, WINDOW), (0, 3, 1, 2))
    return (ssm_n, att.reshape(n_seq, Q_EXP), hre.reshape(n_seq, N_SSM_GROUPS, SSM_STATE),
            him.reshape(n_seq, N_SSM_GROUPS, SSM_STATE), untranspose(kn), untranspose(vn))


def _trunk(x, h0_re, h0_im, k_buf, v_buf, w, *, tm, tag):
    Bn, L, _ = x.shape
    sample = k_buf is not None
    q_dtype = F32 if sample else BF16
    x = x.reshape(Bn * L, D_MODEL)
    ks, vs, hrs, his = [], [], [], []
    mix = None
    for l in range(DEPTH):
        if l > 0:
            x = _rows_call(x, w['ffn2'][l - 1], w['ln'][l - 1][2], mix=mix, tm=tm, name=f"{tag}_mix_ffn2_{l - 1}")[0]
        proj = w['w_in_exp'][l] if sample else w['w_in'][l]
        x, u, q, kv = _rows_call(x, w['ffn1'][l], w['ln'][l][0], proj=proj, tm=tm, q_dtype=q_dtype,
                                 name=f"{tag}_ffn1_{l}")
        if sample:
            ssm_n, att_n, hre, him, kn, vn = _sample_mixer(u, q, kv, l, w, h0_re[l], h0_im[l], k_buf, v_buf)
        else:
            ssm_n, att_n, hre, him, kn, vn = _prompt_mixer(u, q, kv, l, w, Bn, L)
        wo_a = w['wo_a_exp'][l] if sample else w['wo_a'][l]
        mix = (ssm_n, att_n, w['wo_s'][l], wo_a, w['ln'][l][1][0], w['ln'][l][1][1])
        ks.append(kn); vs.append(vn); hrs.append(hre); his.append(him)
    x = _rows_call(x, w['ffn2'][DEPTH - 1], w['ln'][DEPTH - 1][2], mix=mix, tm=tm, name=f"{tag}_mix_ffn2_{DEPTH - 1}")[0]
    return x.reshape(Bn, L, D_MODEL), jnp.stack(ks), jnp.stack(vs), jnp.stack(hrs), jnp.stack(his)


def kernel(x_prompt, x_sample, cache_k_win, cache_v_win, state_ssm_re, state_ssm_im, ln_g, ln_b, ffn1_w_in, ffn1_w_out, ffn2_w_in, ffn2_w_out, w_in, ssm_lam_re, ssm_lam_im, ssm_log_dt, ssm_b_re, ssm_b_im, ssm_c_re, ssm_c_im, ssm_d, glu_w, glu_b, attn_sinks, g_ssm_out, g_attn_out, w_out):
    w = _prep_weights(ln_g, ln_b, ffn1_w_in, ffn1_w_out, ffn2_w_in, ffn2_w_out, w_in, ssm_lam_re, ssm_lam_im,
                      ssm_log_dt, ssm_b_re, ssm_b_im, ssm_c_re, ssm_c_im, ssm_d, glu_w, glu_b, attn_sinks,
                      g_ssm_out, g_attn_out, w_out)
    y_prompt, kp, vp, hrp, hip = _trunk(x_prompt, None, None, None, None, w, tm=512, tag="p")
    n_seq = x_sample.shape[0]
    transposed = lambda c: jnp.transpose(c, (0, 1, 3, 4, 2)).reshape(DEPTH, n_seq, N_KV_HEADS * HEAD_DIM, WINDOW)
    y_sample, ks_, vs_, hrs, his = _trunk(x_sample, state_ssm_re, state_ssm_im, transposed(cache_k_win),
                                          transposed(cache_v_win), w, tm=128, tag="s")
    return (y_prompt, y_sample, kp, vp, hrp, hip, ks_, vs_, hrs, his)
```

```python
import functools

import jax
import jax.numpy as jnp
from jax import lax
from jax.experimental import pallas as pl
from jax.experimental.pallas import tpu as pltpu

D_MODEL = 1024
DEPTH = 2
SSM_WIDTH = 512
SSM_GROUP = 16
N_SSM_GROUPS = 32
SSM_STATE = 64
N_STATES = N_SSM_GROUPS * SSM_STATE
ATTN_WIDTH = 512
HEAD_DIM = 64
N_HEADS = 8
N_KV_HEADS = 2
Q_PER_KV = 4
KV_WIDTH = 128
WINDOW = 128
ATTN_SCALE = HEAD_DIM ** -0.5
D_FF = 2816
ALPHA = (2.0 * DEPTH) ** 0.25
LN_EPS = 1e-5
RMS_EPS = 1e-6

ROW_GROUP = 256
W_STAGE_SLOTS = 3
W_GU_STAGE_ROWS = 64
W_DN_STAGE_ROWS = 352
LANES = 128
SUBLANES = 8
VMEM_LIMIT_BYTES = 56 * 1024 * 1024
N_SLAB = SSM_WIDTH // LANES
GROUPS_PER_SLAB = LANES // SSM_GROUP
SLAB_STATES = GROUPS_PER_SLAB * SSM_STATE
Q_EXP = N_HEADS * LANES
SSM_CHUNK = 8
SSM_ROWS = 2048
ATTN_BLOCKS = 4
ATTN_HEAD_ORDER = (0, 2, 1, 3)
MASKED = -1e30

F32 = jnp.float32
BF16 = jnp.bfloat16


def _layer_norm(r, g, b):
    mu = jnp.mean(r, axis=-1, keepdims=True)
    c = r - mu
    var = jnp.mean(c * c, axis=-1, keepdims=True)
    return c * lax.rsqrt(var + LN_EPS) * g + b


def _rms_norm(y, g):
    return y * lax.rsqrt(jnp.mean(y * y, axis=-1, keepdims=True) + RMS_EPS) * g


def _const(c, n_grid=1):
    if isinstance(c, tuple):
        arr, idx = c
        shape = (None,) + arr.shape[1:]
        index = (idx,) + (0,) * (arr.ndim - 1)
    else:
        arr, shape, index = c, c.shape, (0,) * c.ndim
    imap = (lambda i: index) if n_grid == 1 else (lambda i, j: index)
    return arr, pl.BlockSpec(shape, imap, pipeline_mode=pl.Buffered(1))


def _consts(cs, n_grid=1):
    arrs, specs = zip(*[_const(c, n_grid) for c in cs])
    return list(arrs), list(specs)


def _load_cast(src_hbm, layer, dst_ref, stage_ref, sem_ref):
    n_slots, rows = stage_ref.shape[:2]
    n_chunks = dst_ref.shape[0] // rows

    def copy(c):
        return pltpu.make_async_copy(src_hbm.at[layer, pl.ds(c * rows, rows), :], stage_ref.at[c % n_slots],
                                     sem_ref.at[c % n_slots])

    for c in range(min(n_slots - 1, n_chunks)):
        copy(c).start()
    for c in range(n_chunks):
        copy(c).wait()
        dst_ref[pl.ds(c * rows, rows), :] = stage_ref[c % n_slots].astype(BF16)
        if c + n_slots - 1 < n_chunks:
            copy(c + n_slots - 1).start()


def _rows_kernel(*refs, has_mix, has_proj, has_sample, n_sub, layer):
    it = iter(refs)
    take = lambda n: tuple(next(it) for _ in range(n))
    (x_ref,) = take(1)
    if has_mix:
        ssm_ref, att_ref, wo_s_ref, wo_a_ref, gm_ref, bm_ref = take(6)
    wgu_hbm, wdn_hbm, g_ref, b_ref = take(4)
    wp_ref = take(1)[0] if has_proj else None
    if has_sample:
        (xs_ref,) = take(1)
        if has_mix:
            ssm_s_ref, att_s_ref, wo_a_s_ref = take(3)
        wp_s_ref = take(1)[0] if has_proj else None
    n_out = 4 if has_proj else 1
    outs = take(n_out)
    outs_s = take(n_out) if has_sample else None
    wgu_ref, wdn_ref, stage_gu, stage_dn, sem_gu, sem_dn = take(6)

    @pl.when(pl.program_id(0) == 0)
    def _():
        _load_cast(wgu_hbm, layer, wgu_ref, stage_gu, sem_gu)
        _load_cast(wdn_hbm, layer, wdn_ref, stage_dn, sem_dn)

    def mix_inputs(ssm_r, att_r, rows):
        ssm = jnp.concatenate([ssm_r[j, rows, :] for j in range(N_SLAB)], axis=1).astype(BF16)
        return ssm, att_r[rows, :].astype(BF16)

    def run(xs, mixes, wo_a, wp):
        if has_mix:
            ms = [jnp.dot(ssm, wo_s_ref[...], preferred_element_type=F32)
                  + jnp.dot(att, wo_a[...], preferred_element_type=F32) for ssm, att in mixes]
            xs = [_layer_norm(ALPHA * x + m, gm_ref[...], bm_ref[...]) for x, m in zip(xs, ms)]
        gus = [jnp.dot(x.astype(BF16), wgu_ref[...], preferred_element_type=F32) for x in xs]
        hs = [(gu[:, :D_FF] * jax.nn.sigmoid(gu[:, :D_FF]) * gu[:, D_FF:]).astype(BF16) for gu in gus]
        ys = [jnp.dot(h, wdn_ref[...], preferred_element_type=F32) for h in hs]
        xs = [_layer_norm(ALPHA * x + 0.5 * y, g_ref[...], b_ref[...]) for x, y in zip(xs, ys)]
        if has_proj:
            zs = [jnp.dot(x.astype(BF16), wp[...], preferred_element_type=F32) for x in xs]
        else:
            zs = [None] * len(xs)
        return xs, zs

    def store(out_refs, rows, x, z):
        out_refs[0][rows, :] = x
        if has_proj:
            u_ref, q_ref, kv_ref = out_refs[1:]
            q_cols = q_ref.shape[1]
            for j in range(N_SLAB):
                u_ref[j, rows, :] = z[:, j * LANES:(j + 1) * LANES]
            q_ref[rows, :] = z[:, SSM_WIDTH:SSM_WIDTH + q_cols].astype(q_ref.dtype)
            kv_ref[rows, :] = z[:, SSM_WIDTH + q_cols:]

    sub = x_ref.shape[0] // n_sub
    groups = [slice(r * sub, (r + 1) * sub) for r in range(n_sub)]
    mixes = [mix_inputs(ssm_ref, att_ref, rows) for rows in groups] if has_mix else None
    xs, zs = run([x_ref[rows, :] for rows in groups], mixes, wo_a_ref if has_mix else None, wp_ref)
    for rows, x, z in zip(groups, xs, zs):
        store(outs, rows, x, z)

    if has_sample:
        @pl.when(pl.program_id(0) == pl.num_programs(0) - 1)
        def _():
            every = slice(None)
            mixes_s = [mix_inputs(ssm_s_ref, att_s_ref, every)] if has_mix else None
            xs_s, zs_s = run([xs_ref[...]], mixes_s, wo_a_s_ref if has_mix else None, wp_s_ref)
            store(outs_s, every, xs_s[0], zs_s[0])


def _rows_call(x, ffn, ln, mix=None, proj=None, sample=None, *, tm, name):
    M = x.shape[0]
    assert M % tm == 0
    row = lambda w: pl.BlockSpec((tm, w), lambda i: (i, 0))
    slab = lambda n: pl.BlockSpec((n, tm, LANES), lambda i: (0, i, 0))
    whole = lambda a: pl.BlockSpec(a.shape, lambda i: (0,) * len(a.shape))
    args = [x]
    specs = [row(D_MODEL)]
    if mix is not None:
        ssm_n, att_n = mix[:2]
        arrs, cspecs = _consts(mix[2:])
        args += [ssm_n, att_n] + arrs
        specs += [slab(N_SLAB), row(att_n.shape[1])] + cspecs
    w_gu, w_dn, layer = ffn
    arrs, cspecs = _consts(list(ln))
    args += [w_gu, w_dn] + arrs
    specs += [pl.BlockSpec(memory_space=pl.ANY), pl.BlockSpec(memory_space=pl.ANY)] + cspecs

    def proj_outputs(rows, q_cols, q_dtype):
        return [jax.ShapeDtypeStruct((N_SLAB, rows, LANES), F32), jax.ShapeDtypeStruct((rows, q_cols), q_dtype),
                jax.ShapeDtypeStruct((rows, 2 * KV_WIDTH), F32)]

    out_shape = [jax.ShapeDtypeStruct((M, D_MODEL), F32)]
    out_specs = [row(D_MODEL)]
    if proj is not None:
        arr, cspec = _const(proj)
        args.append(arr)
        specs.append(cspec)
        q_cols = arr.shape[-1] - SSM_WIDTH - 2 * KV_WIDTH
        out_shape += proj_outputs(M, q_cols, BF16)
        out_specs += [slab(N_SLAB), row(q_cols), row(2 * KV_WIDTH)]
    n_main = len(out_shape)
    if sample is not None:
        xs = sample['x']
        args.append(xs)
        specs.append(whole(xs))
        if mix is not None:
            ssm_s, att_s, wo_a_s = sample['mix']
            arr, cspec = _const(wo_a_s)
            args += [ssm_s, att_s, arr]
            specs += [whole(ssm_s), whole(att_s), cspec]
        sample_out = [jax.ShapeDtypeStruct(xs.shape, F32)]
        if proj is not None:
            arr, cspec = _const(sample['proj'])
            args.append(arr)
            specs.append(cspec)
            sample_out += proj_outputs(xs.shape[0], arr.shape[-1] - SSM_WIDTH - 2 * KV_WIDTH, F32)
        out_shape += sample_out
        out_specs += [whole(o) for o in sample_out]
    outs = pl.pallas_call(
        functools.partial(_rows_kernel, has_mix=mix is not None, has_proj=proj is not None,
                          has_sample=sample is not None, n_sub=max(1, tm // ROW_GROUP), layer=layer),
        grid=(M // tm,),
        in_specs=specs,
        out_specs=out_specs,
        out_shape=out_shape,
        scratch_shapes=[pltpu.VMEM(w_gu.shape[1:], BF16), pltpu.VMEM(w_dn.shape[1:], BF16),
                        pltpu.VMEM((W_STAGE_SLOTS, W_GU_STAGE_ROWS, w_gu.shape[2]), F32),
                        pltpu.VMEM((W_STAGE_SLOTS, W_DN_STAGE_ROWS, w_dn.shape[2]), F32),
                        pltpu.SemaphoreType.DMA((W_STAGE_SLOTS,)), pltpu.SemaphoreType.DMA((W_STAGE_SLOTS,))],
        compiler_params=pltpu.CompilerParams(dimension_semantics=("arbitrary",),
                                             vmem_limit_bytes=VMEM_LIMIT_BYTES),
        name=name,
    )(*args)
    return outs[:n_main], (outs[n_main:] if sample is not None else None)


def _cmul(xr, xi, yr, yi):
    return xr * yr - xi * yi, xr * yi + xi * yr


def _ssm_tables_kernel(p_ref, af_ref, wst_ref, cat_ref, strip_ref, ak_ref, tr_ref):
    t1 = SSM_CHUNK
    a_re, a_im, lam_re, lam_im, b_re, b_im, c_re, c_im = (p_ref[k] for k in range(8))
    num_re, num_im = a_re - 1.0, a_im
    den = lam_re * lam_re + lam_im * lam_im
    f_re = (num_re * lam_re + num_im * lam_im) / den
    f_im = (num_im * lam_re - num_re * lam_im) / den
    bb_re, bb_im = _cmul(f_re, f_im, b_re, b_im)

    def same_group(shape, row_div, col_div):
        r = lax.broadcasted_iota(jnp.int32, shape, 0) // row_div
        c = lax.broadcasted_iota(jnp.int32, shape, 1) // col_div
        return (r == c).astype(F32)

    m_state = same_group((LANES, SLAB_STATES), SSM_GROUP, SSM_STATE)
    m_chan = 0.5 * same_group((LANES, LANES), SSM_GROUP, SSM_GROUP)

    def expand(x):
        return (jnp.concatenate([x] * (SLAB_STATES // LANES), axis=1) * m_state).astype(BF16)

    def dot_nt(a, b):
        return lax.dot_general(a, b, (((1,), (1,)), ((), ())), precision=lax.Precision.HIGHEST,
                               preferred_element_type=F32)

    pw_re, pw_im = jnp.ones_like(a_re), jnp.zeros_like(a_re)
    kd = []
    for l in range(t1):
        ab_re, ab_im = _cmul(pw_re, pw_im, bb_re, bb_im)
        s = t1 - 1 - l
        kd_l = []
        for j in range(N_SLAB):
            rows = slice(j * LANES, (j + 1) * LANES)
            wst_ref[j, s * LANES:(s + 1) * LANES, :SLAB_STATES] = expand(ab_re[rows])
            wst_ref[j, s * LANES:(s + 1) * LANES, SLAB_STATES:] = expand(ab_im[rows])
            k = dot_nt(ab_re[rows], c_re[rows]) - dot_nt(ab_im[rows], c_im[rows])
            kd_l.append((k * m_chan).astype(BF16))
        kd.append(kd_l)
        pw_re, pw_im = _cmul(pw_re, pw_im, a_re, a_im)
        cf_re = c_re * pw_re - c_im * pw_im
        cf_im = -(c_re * pw_im + c_im * pw_re)
        for j in range(N_SLAB):
            rows = slice(j * LANES, (j + 1) * LANES)
            cat_ref[j, l * LANES:(l + 1) * LANES, :SLAB_STATES] = expand(cf_re[rows])
            cat_ref[j, l * LANES:(l + 1) * LANES, SLAB_STATES:] = expand(cf_im[rows])
    for j in range(N_SLAB):
        for rho in range(t1):
            for c in range(2):
                lag = t1 - 2 - rho + c
                blk = kd[lag][j] if lag >= 0 else jnp.zeros((LANES, LANES), BF16)
                strip_ref[j, rho * LANES:(rho + 1) * LANES, c * LANES:(c + 1) * LANES] = blk

    f_re, f_im = af_ref[0], af_ref[1]
    base_re, base_im = f_re, f_im
    for _ in range(t1 - 1):
        base_re, base_im = _cmul(base_re, base_im, f_re, f_im)
    row = lax.broadcasted_iota(jnp.int32, (SUBLANES, N_STATES), 0)
    pws = [(base_re, base_im)]
    for _ in range(SUBLANES - 1):
        pws.append(_cmul(pws[-1][0], pws[-1][1], base_re, base_im))
    for n, k in enumerate((1, 2, 4)):
        for part in range(2):
            ak_ref[2 * n + part] = jnp.where(row >= k, jnp.broadcast_to(pws[k - 1][part], row.shape), 0.0)
    for part in range(2):
        acc = jnp.zeros(row.shape, F32)
        for r in range(SUBLANES):
            acc = jnp.where(row == r, jnp.broadcast_to(pws[r][part], row.shape), acc)
        tr_ref[part] = acc


def _ssm_tables_call(params, a_flat):
    t1 = SSM_CHUNK
    shapes = [((N_SLAB, t1 * LANES, 2 * SLAB_STATES), BF16),
              ((N_SLAB, t1 * LANES, 2 * SLAB_STATES), BF16),
              ((N_SLAB, t1 * LANES, 2 * LANES), BF16),
              ((6, SUBLANES, N_STATES), F32),
              ((2, SUBLANES, N_STATES), F32)]
    per_layer = lambda shape: pl.BlockSpec((None,) + tuple(shape), lambda l: (l,) + (0,) * len(shape))
    return pl.pallas_call(
        _ssm_tables_kernel,
        grid=(DEPTH,),
        in_specs=[per_layer(params.shape[1:]), per_layer(a_flat.shape[1:])],
        out_specs=[per_layer(s) for s, _ in shapes],
        out_shape=[jax.ShapeDtypeStruct((DEPTH,) + s, d) for s, d in shapes],
        compiler_params=pltpu.CompilerParams(dimension_semantics=("arbitrary",),
                                             vmem_limit_bytes=VMEM_LIMIT_BYTES),
        name="ssm_tables",
    )(params, a_flat)


def _ssm_table_inputs(lam_re, lam_im, log_dt, b_re, b_im, c_re, c_im):
    dt = jnp.exp(log_dt)[..., None]
    mag = jnp.exp(lam_re * dt)
    a_re = mag * jnp.cos(lam_im * dt)
    a_im = mag * jnp.sin(lam_im * dt)
    rep = lambda x: jnp.repeat(x, SSM_GROUP, axis=1)
    flat = lambda x: x.reshape(DEPTH, SSM_WIDTH, SSM_STATE)
    params = jnp.stack([rep(a_re), rep(a_im), rep(lam_re), rep(lam_im),
                        flat(jnp.swapaxes(b_re, 2, 3)), flat(jnp.swapaxes(b_im, 2, 3)), flat(c_re), flat(c_im)],
                       axis=1)
    params = jnp.concatenate([params, params], axis=-1)
    a_flat = jnp.stack([a_re.reshape(DEPTH, 1, N_STATES), a_im.reshape(DEPTH, 1, N_STATES)], axis=1)
    return params, a_flat


def _glu_rms(y, gluw_ref, glub_ref, g_ref):
    g = jax.nn.gelu(y).astype(BF16)
    zz = jnp.dot(g, gluw_ref[...], preferred_element_type=F32) + glub_ref[...]
    o = zz[:, :SSM_WIDTH] * jax.nn.sigmoid(zz[:, SSM_WIDTH:])
    return _rms_norm(o, g_ref[...])


def _ssm_kernel(u_ref, wst_ref, cat_ref, strip_ref, ak_ref, tr_ref, d_ref, gluw_ref, glub_ref, g_ref,
                o_ref, hre_ref, him_ref,
                ucb_ref, sre_ref, sim_ref, hcr_ref, hci_ref):
    t1 = SSM_CHUNK
    nc = SSM_ROWS // t1
    i = pl.program_id(1)

    @pl.when(i == 0)
    def _():
        hcr_ref[...] = jnp.zeros_like(hcr_ref)
        hci_ref[...] = jnp.zeros_like(hci_ref)

    for j in range(N_SLAB):
        for s in range(t1):
            ucb_ref[j, :, s * LANES:(s + 1) * LANES] = u_ref[j, pl.ds(s, nc, stride=t1), :].astype(BF16)

    for j in range(N_SLAB):
        st = jnp.dot(ucb_ref[j], wst_ref[j], preferred_element_type=F32)
        sre_ref[:, j * SLAB_STATES:(j + 1) * SLAB_STATES] = st[:, :SLAB_STATES]
        sim_ref[:, j * SLAB_STATES:(j + 1) * SLAB_STATES] = st[:, SLAB_STATES:]

    row = lax.broadcasted_iota(jnp.int32, (SUBLANES, SLAB_STATES), 0)
    for j in range(N_SLAB):
        cols = pl.ds(j * SLAB_STATES, SLAB_STATES)

        def body(gi, carry, cols=cols):
            hr, hi = carry
            r0 = pl.multiple_of(gi * SUBLANES, SUBLANES)
            xr = sre_ref[pl.ds(r0, SUBLANES), cols]
            xi = sim_ref[pl.ds(r0, SUBLANES), cols]
            for n, k in enumerate((1, 2, 4)):
                akr = ak_ref[2 * n, :, cols]
                aki = ak_ref[2 * n + 1, :, cols]
                sr = pltpu.roll(xr, k, 0)
                si = pltpu.roll(xi, k, 0)
                xr, xi = xr + akr * sr - aki * si, xi + akr * si + aki * sr
            tr = tr_ref[0, :, cols]
            ti = tr_ref[1, :, cols]
            er = xr + tr * hr - ti * hi
            ei = xi + tr * hi + ti * hr
            sre_ref[pl.ds(r0, SUBLANES), cols] = jnp.where(row >= 1, pltpu.roll(er, 1, 0), hr)
            sim_ref[pl.ds(r0, SUBLANES), cols] = jnp.where(row >= 1, pltpu.roll(ei, 1, 0), hi)
            last = SUBLANES - 1
            return (jnp.broadcast_to(er[last:last + 1], er.shape), jnp.broadcast_to(ei[last:last + 1], ei.shape))

        hr, hi = lax.fori_loop(0, nc // SUBLANES, body, (hcr_ref[:, cols], hci_ref[:, cols]), unroll=True)
        hcr_ref[:, cols] = hr
        hci_ref[:, cols] = hi

    hre_ref[0] = hcr_ref[...]
    him_ref[0] = hci_ref[...]

    ys = []
    for j in range(N_SLAB):
        cols = slice(j * SLAB_STATES, (j + 1) * SLAB_STATES)
        hprev = jnp.concatenate([sre_ref[:, cols], sim_ref[:, cols]], axis=1).astype(BF16)
        y = lax.dot_general(hprev, cat_ref[j], (((1,), (1,)), ((), ())), preferred_element_type=F32)
        parts = []
        for p2 in range(t1 // 2):
            kk = (2 * p2 + 2) * LANES
            yi = jnp.dot(ucb_ref[j, :, :kk], strip_ref[j, (t1 - 2 - 2 * p2) * LANES:, :],
                         preferred_element_type=F32)
            parts.append(y[:, 2 * p2 * LANES:(2 * p2 + 2) * LANES] + yi)
        ys.append(jnp.concatenate(parts, axis=1))

    for t in range(t1):
        y = jnp.concatenate([ys[j][:, t * LANES:(t + 1) * LANES] + d_ref[j] * u_ref[j, pl.ds(t, nc, stride=t1), :]
                             for j in range(N_SLAB)], axis=1)
        n = _glu_rms(y, gluw_ref, glub_ref, g_ref)
        for j in range(N_SLAB):
            o_ref[j, pl.ds(t, nc, stride=t1), :] = n[:, j * LANES:(j + 1) * LANES]


def _ssm_prompt_call(u, tab, d_t, gluw, glub, g, *, batch, seq, name):
    t1 = SSM_CHUNK
    nc = SSM_ROWS // t1
    steps = seq // SSM_ROWS
    blk = pl.BlockSpec((N_SLAB, SSM_ROWS, LANES), lambda b, i: (0, b * steps + i, 0))
    st = pl.BlockSpec((1, SUBLANES, N_STATES), lambda b, i: (b, 0, 0))
    consts, cspecs = _consts(list(tab) + [d_t, gluw, glub, g], 2)
    out, hre, him = pl.pallas_call(
        _ssm_kernel,
        grid=(batch, steps),
        in_specs=[blk] + cspecs,
        out_specs=[blk, st, st],
        out_shape=[jax.ShapeDtypeStruct(u.shape, F32),
                   jax.ShapeDtypeStruct((batch, SUBLANES, N_STATES), F32),
                   jax.ShapeDtypeStruct((batch, SUBLANES, N_STATES), F32)],
        scratch_shapes=[pltpu.VMEM((N_SLAB, nc, t1 * LANES), BF16),
                        pltpu.VMEM((nc, N_STATES), F32),
                        pltpu.VMEM((nc, N_STATES), F32),
                        pltpu.VMEM((SUBLANES, N_STATES), F32),
                        pltpu.VMEM((SUBLANES, N_STATES), F32)],
        compiler_params=pltpu.CompilerParams(dimension_semantics=("arbitrary", "arbitrary"),
                                             vmem_limit_bytes=VMEM_LIMIT_BYTES),
        name=name,
    )(u, *consts)
    return out, hre[:, 0], him[:, 0]


def _ssm_sample_kernel(u_ref, h0r_ref, h0i_ref, wst_ref, cat_ref, kd_ref, a_ref, d_ref, gluw_ref, glub_ref, g_ref,
                       o_ref, hr_ref, hi_ref):
    ys = []
    for j in range(N_SLAB):
        cols = slice(j * SLAB_STATES, (j + 1) * SLAB_STATES)
        uf = u_ref[j]
        ub = uf.astype(BF16)
        st = jnp.dot(ub, wst_ref[j], preferred_element_type=F32)
        h0r = h0r_ref[:, cols]
        h0i = h0i_ref[:, cols]
        ar = a_ref[0, :, cols]
        ai = a_ref[1, :, cols]
        hr_ref[:, cols] = ar * h0r - ai * h0i + st[:, :SLAB_STATES]
        hi_ref[:, cols] = ar * h0i + ai * h0r + st[:, SLAB_STATES:]
        hcat = jnp.concatenate([h0r, h0i], axis=1).astype(BF16)
        y = lax.dot_general(hcat, cat_ref[j], (((1,), (1,)), ((), ())), preferred_element_type=F32)
        y = y + jnp.dot(ub, kd_ref[j], preferred_element_type=F32)
        ys.append(y + d_ref[j] * uf)
    n = _glu_rms(jnp.concatenate(ys, axis=1), gluw_ref, glub_ref, g_ref)
    for j in range(N_SLAB):
        o_ref[j] = n[:, j * LANES:(j + 1) * LANES]


def _ssm_sample_call(u, h0r, h0i, tab, a_flat, d_t, gluw, glub, g, *, name):
    n_seq = u.shape[1]
    (wst, layer), (cat, _), (strip, _) = tab[:3]
    last = SSM_CHUNK - 1
    whole = lambda a: pl.BlockSpec(a.shape, lambda i: (0,) * len(a.shape))
    outs = [jax.ShapeDtypeStruct(u.shape, F32),
            jax.ShapeDtypeStruct((n_seq, N_STATES), F32),
            jax.ShapeDtypeStruct((n_seq, N_STATES), F32)]
    consts, cspecs = _consts([a_flat, d_t, gluw, glub, g])
    return pl.pallas_call(
        _ssm_sample_kernel,
        grid=(1,),
        in_specs=[whole(u), whole(h0r), whole(h0i),
                  pl.BlockSpec((None, N_SLAB, LANES, 2 * SLAB_STATES), lambda i: (layer, 0, last, 0)),
                  pl.BlockSpec((None, N_SLAB, LANES, 2 * SLAB_STATES), lambda i: (layer, 0, 0, 0)),
                  pl.BlockSpec((None, N_SLAB, LANES, LANES), lambda i: (layer, 0, last, 1))] + cspecs,
        out_specs=[whole(o) for o in outs],
        out_shape=outs,
        compiler_params=pltpu.CompilerParams(dimension_semantics=("arbitrary",),
                                             vmem_limit_bytes=VMEM_LIMIT_BYTES),
        name=name,
    )(u, h0r, h0i, wst, cat, strip, *consts)


def _attn_kernel(sink_ref, q_ref, kvc_ref, kvp_ref, g_ref, o_ref):
    i = pl.program_id(1)
    kj = lax.broadcasted_iota(jnp.int32, (2 * WINDOW, WINDOW), 0)
    qi = lax.broadcasted_iota(jnp.int32, (2 * WINDOW, WINDOW), 1)
    band = (kj >= qi) & (kj <= qi + WINDOW)
    low = qi < HEAD_DIM
    for sb in range(ATTN_BLOCKS):
        blk = slice(sb * WINDOW, (sb + 1) * WINDOW)
        kc = kvc_ref[blk, :]
        if sb == 0:
            kp = kvp_ref[...]
            valid = band & ((i > 0) | (kj >= WINDOW))
        else:
            kp = kvc_ref[(sb - 1) * WINDOW:sb * WINDOW, :]
            valid = band
        valid2 = jnp.concatenate([valid, valid], axis=1)
        kcat = jnp.concatenate([kp[:, :KV_WIDTH], kc[:, :KV_WIDTH]], axis=0) * ATTN_SCALE
        swap = pltpu.roll(kcat, HEAD_DIM, 1)
        zero = jnp.zeros_like(kcat)
        k_sel = [[jnp.where(low, kcat, zero).astype(BF16), jnp.where(low, zero, swap).astype(BF16)],
                 [jnp.where(low, swap, zero).astype(BF16), jnp.where(low, zero, kcat).astype(BF16)]]
        v_t = jnp.concatenate([kp[:, KV_WIDTH:], kc[:, KV_WIDTH:]], axis=0).T.astype(BF16)
        parts = []
        for kvh in range(N_KV_HEADS):
            q_pair = jnp.concatenate([q_ref[blk, (2 * kvh) * LANES:(2 * kvh + 1) * LANES],
                                      q_ref[blk, (2 * kvh + 1) * LANES:(2 * kvh + 2) * LANES]], axis=0)
            for parity in range(2):
                s_t = lax.dot_general(k_sel[kvh][parity], q_pair, (((1,), (1,)), ((), ())),
                                      preferred_element_type=F32)
                parts.append(jnp.where(valid2, s_t, MASKED))
        s_all = jnp.concatenate(parts, axis=1)
        sink = sink_ref[...]
        m = jnp.maximum(jnp.max(s_all, axis=0, keepdims=True), sink)
        p_all = jnp.exp(s_all - m)
        inv_den = 1.0 / (jnp.sum(p_all, axis=0, keepdims=True) + jnp.exp(sink - m))
        p_all = p_all.astype(BF16)
        half = Q_PER_KV * WINDOW
        o_kv = [jnp.dot(v_t[kvh * HEAD_DIM:(kvh + 1) * HEAD_DIM], p_all[:, kvh * half:(kvh + 1) * half],
                        preferred_element_type=F32) * inv_den[:, kvh * half:(kvh + 1) * half]
                for kvh in range(N_KV_HEADS)]
        tiles = []
        for pr in range(Q_PER_KV):
            lanes = slice(ATTN_HEAD_ORDER.index(pr) * WINDOW, (ATTN_HEAD_ORDER.index(pr) + 1) * WINDOW)
            tiles.append(jnp.concatenate([o_kv[0][:, lanes], o_kv[1][:, lanes]], axis=0))
        ss = None
        for tile in tiles:
            t = jnp.sum(tile * tile, axis=0, keepdims=True)
            ss = t if ss is None else ss + t
        inv = lax.rsqrt(ss / ATTN_WIDTH + RMS_EPS)
        for pr, tile in enumerate(tiles):
            o_ref[blk, pr * LANES:(pr + 1) * LANES] = ((tile * inv).T * g_ref[pr]).astype(o_ref.dtype)


def _attn_prompt_call(q, kv, sinks, g_perm, *, batch, seq, name):
    rows = ATTN_BLOCKS * WINDOW
    nb = seq // rows
    M = kv.shape[0]
    cur = lambda b, i: (b * nb + i, 0)
    prev = lambda b, i: ((b * nb + i) * ATTN_BLOCKS - jnp.minimum(i, 1), 0)
    (sinks, g_perm), (sink_spec, g_spec) = _consts([sinks, g_perm], 2)
    return pl.pallas_call(
        _attn_kernel,
        grid=(batch, nb),
        in_specs=[sink_spec,
                  pl.BlockSpec((rows, ATTN_WIDTH), cur),
                  pl.BlockSpec((rows, 2 * KV_WIDTH), cur),
                  pl.BlockSpec((WINDOW, 2 * KV_WIDTH), prev),
                  g_spec],
        out_specs=pl.BlockSpec((rows, ATTN_WIDTH), cur),
        out_shape=jax.ShapeDtypeStruct((M, ATTN_WIDTH), BF16),
        compiler_params=pltpu.CompilerParams(dimension_semantics=("arbitrary", "arbitrary"),
                                             vmem_limit_bytes=VMEM_LIMIT_BYTES),
        name=name,
    )(sinks, q, kv, kv, g_perm)


SEQ_PER_STEP = LANES // N_HEADS


def _attn_sample_kernel(q_ref, kv_ref, ckt_ref, cvt_ref, sink_ref, g_ref, own_ref, o_ref, kot_ref, vot_ref):
    nrow = SEQ_PER_STEP * N_HEADS
    qb = q_ref[...].astype(BF16)
    k_new = kv_ref[:, :KV_WIDTH]
    v_new = kv_ref[:, KV_WIDTH:]
    row_seq = lax.broadcasted_iota(jnp.int32, (nrow, LANES), 0) // N_HEADS
    lane_seq = lax.broadcasted_iota(jnp.int32, (nrow, LANES), 1) // N_HEADS
    lane = lax.broadcasted_iota(jnp.int32, (nrow, LANES), 1)
    pick = (lax.broadcasted_iota(jnp.int32, (nrow, SEQ_PER_STEP), 0) // N_HEADS
            == lax.broadcasted_iota(jnp.int32, (nrow, SEQ_PER_STEP), 1)).astype(BF16)

    qf = q_ref[...]
    heads_of = lambda a, n: a[n * N_HEADS:(n + 1) * N_HEADS]
    s = jnp.concatenate([jnp.dot(heads_of(qf, n).astype(BF16), ckt_ref[n].astype(BF16), preferred_element_type=F32)
                         for n in range(SEQ_PER_STEP)], axis=0) * ATTN_SCALE
    k_rows = jnp.dot(pick, k_new.astype(BF16), preferred_element_type=F32)
    s_new = jnp.sum(qb.astype(F32) * k_rows, axis=-1, keepdims=True) * ATTN_SCALE
    sink = sink_ref[...]
    m = jnp.maximum(jnp.maximum(jnp.max(s, axis=-1, keepdims=True), s_new), sink)
    p = jnp.exp(s - m)
    p_new = jnp.exp(s_new - m)
    inv = 1.0 / (jnp.sum(p, axis=-1, keepdims=True) + p_new + jnp.exp(sink - m))
    pn = p * inv
    o = jnp.concatenate([lax.dot_general(heads_of(pn, n).astype(BF16), cvt_ref[n].astype(BF16),
                                         (((1,), (1,)), ((), ())), preferred_element_type=F32)
                         for n in range(SEQ_PER_STEP)], axis=0)
    v_rows = jnp.dot(pick, v_new.astype(BF16), preferred_element_type=F32)
    o = o + (p_new * inv).astype(BF16).astype(F32) * v_rows

    hi = lax.Precision.HIGHEST
    row_sums = jnp.dot(o * o * own_ref[...], jnp.ones((LANES, LANES), F32), precision=hi, preferred_element_type=F32)
    ss = jnp.dot((row_seq == lane_seq).astype(F32), row_sums, precision=hi, preferred_element_type=F32)
    o_ref[...] = o * lax.rsqrt(ss / ATTN_WIDTH + RMS_EPS) * g_ref[...]

    pad = jnp.zeros((LANES - SEQ_PER_STEP, KV_WIDTH), F32)
    k_cols = jnp.concatenate([k_new, pad], axis=0).T
    v_cols = jnp.concatenate([v_new, pad], axis=0).T
    last = lane == WINDOW - 1
    for n in range(SEQ_PER_STEP):
        kot_ref[n] = jnp.where(last, pltpu.roll(k_cols, WINDOW - 1 - n, 1), pltpu.roll(ckt_ref[n], WINDOW - 1, 1))
        vot_ref[n] = jnp.where(last, pltpu.roll(v_cols, WINDOW - 1 - n, 1), pltpu.roll(cvt_ref[n], WINDOW - 1, 1))


def _attn_sample_call(q, kv, ck, cv, layer, sink_row, g_rows, own_rows, *, name):
    n_seq = kv.shape[0]
    assert n_seq % SEQ_PER_STEP == 0
    nrow = SEQ_PER_STEP * N_HEADS
    cblk = pl.BlockSpec((SEQ_PER_STEP, WINDOW, 2 * HEAD_DIM), lambda i: (i, 0, 0))
    cin = pl.BlockSpec((None, SEQ_PER_STEP, WINDOW, 2 * HEAD_DIM), lambda i: (layer, i, 0, 0))
    qblk = pl.BlockSpec((nrow, LANES), lambda i: (i, 0))
    consts, cspecs = _consts([sink_row, g_rows, own_rows])
    return pl.pallas_call(
        _attn_sample_kernel,
        grid=(n_seq // SEQ_PER_STEP,),
        in_specs=[qblk, pl.BlockSpec((SEQ_PER_STEP, 2 * KV_WIDTH), lambda i: (i, 0)), cin, cin] + cspecs,
        out_specs=[qblk, cblk, cblk],
        out_shape=[jax.ShapeDtypeStruct(q.shape, F32),
                   jax.ShapeDtypeStruct(ck.shape[1:], F32),
                   jax.ShapeDtypeStruct(cv.shape[1:], F32)],
        compiler_params=pltpu.CompilerParams(dimension_semantics=("arbitrary",),
                                             vmem_limit_bytes=VMEM_LIMIT_BYTES),
        name=name,
    )(q, kv, ck, cv, *consts)


def _expand_heads(a):
    lead = a.shape[:-1]
    a = a.reshape(lead + (N_KV_HEADS, Q_PER_KV, 1, HEAD_DIM))
    sel = jnp.eye(N_KV_HEADS, dtype=a.dtype).reshape(N_KV_HEADS, 1, N_KV_HEADS, 1)
    return (a * sel).reshape(lead + (Q_EXP,))


def _pair_heads(a):
    lead = a.shape[:-1]
    a = a.reshape(lead + (N_KV_HEADS, Q_PER_KV, HEAD_DIM))
    return jnp.swapaxes(a, -3, -2).reshape(lead + (ATTN_WIDTH,))


def _prep_weights(ln_g, ln_b, ffn1_w_in, ffn1_w_out, ffn2_w_in, ffn2_w_out, w_in, ssm_lam_re, ssm_lam_im, ssm_log_dt,
                  ssm_b_re, ssm_b_im, ssm_c_re, ssm_c_im, ssm_d, glu_w, glu_b, attn_sinks, g_ssm_out, g_attn_out,
                  w_out):
    o1, o2 = SSM_WIDTH, SSM_WIDTH + ATTN_WIDTH
    rows_t = lambda a: jnp.swapaxes(a, 1, 2)
    params, a_flat = _ssm_table_inputs(ssm_lam_re, ssm_lam_im, ssm_log_dt, ssm_b_re, ssm_b_im, ssm_c_re, ssm_c_im)
    tables = _ssm_tables_call(params, a_flat)
    head_order = jnp.array([Q_PER_KV * kvh + o for kvh in range(N_KV_HEADS) for o in ATTN_HEAD_ORDER])
    g_exp = _expand_heads(g_attn_out).reshape(DEPTH, N_HEADS, LANES)
    w_in_b = w_in.astype(BF16)
    w_out_b = w_out.astype(BF16)
    stacked = dict(
        w_in=w_in_b,
        w_in_exp=jnp.concatenate([w_in_b[:, :, :o1], _expand_heads(w_in_b[:, :, o1:o2]), w_in_b[:, :, o2:]], axis=2),
        wo_s=w_out_b[:, :SSM_WIDTH],
        wo_a=rows_t(_pair_heads(rows_t(w_out_b[:, SSM_WIDTH:]))),
        wo_a_exp=rows_t(_expand_heads(rows_t(w_out_b[:, SSM_WIDTH:]))),
        a_flat=a_flat,
        d_1=ssm_d.reshape(DEPTH, N_SLAB, 1, LANES),
        gluw=glu_w.astype(BF16),
        glub=glu_b.reshape(DEPTH, 1, 2 * SSM_WIDTH),
        g_ssm=g_ssm_out.reshape(DEPTH, 1, SSM_WIDTH),
        sinks=jnp.repeat(attn_sinks[:, head_order], WINDOW, axis=1)[:, None, :],
        g_pair=_pair_heads(g_attn_out).reshape(DEPTH, Q_PER_KV, 1, LANES),
        sink_rows=jnp.tile(attn_sinks, (1, SEQ_PER_STEP))[:, :, None],
        g_rows=jnp.tile(g_exp, (1, SEQ_PER_STEP, 1)),
    )
    w = {name: [(arr, l) for l in range(DEPTH)] for name, arr in stacked.items()}
    w['ffn1'] = [(ffn1_w_in, ffn1_w_out, l) for l in range(DEPTH)]
    w['ffn2'] = [(ffn2_w_in, ffn2_w_out, l) for l in range(DEPTH)]
    w['tab'] = [tuple((t, l) for t in tables) for l in range(DEPTH)]
    n_ln = ln_g.shape[1]
    ln_g3 = ln_g.reshape(DEPTH * n_ln, 1, D_MODEL)
    ln_b3 = ln_b.reshape(DEPTH * n_ln, 1, D_MODEL)
    w['ln'] = [[((ln_g3, l * n_ln + i), (ln_b3, l * n_ln + i)) for i in range(n_ln)] for l in range(DEPTH)]
    w['own_rows'] = jnp.tile(_expand_heads(jnp.ones((ATTN_WIDTH,), F32)).reshape(N_HEADS, LANES), (SEQ_PER_STEP, 1))
    return w


def _prompt_mixer(u, q, kv, l, w, batch, seq):
    ssm_n, hre, him = _ssm_prompt_call(u, w['tab'][l], w['d_1'][l], w['gluw'][l],
                                       w['glub'][l], w['g_ssm'][l], batch=batch, seq=seq, name=f"p_ssm_{l}")
    att_n = _attn_prompt_call(q, kv, w['sinks'][l], w['g_pair'][l], batch=batch, seq=seq, name=f"p_attn_{l}")
    kvw = kv.reshape(batch, seq, 2 * KV_WIDTH)[:, -WINDOW:].reshape(batch, WINDOW, 2, N_KV_HEADS, HEAD_DIM)
    return (ssm_n, att_n, hre.reshape(batch, N_SSM_GROUPS, SSM_STATE), him.reshape(batch, N_SSM_GROUPS, SSM_STATE),
            kvw[:, :, 0], kvw[:, :, 1])


def _sample_mixer(u, q, kv, l, w, h0_re, h0_im, k_buf, v_buf):
    n_seq = kv.shape[0]
    ssm_n, hre, him = _ssm_sample_call(u, h0_re.reshape(n_seq, N_STATES), h0_im.reshape(n_seq, N_STATES),
                                       w['tab'][l], w['a_flat'][l], w['d_1'][l], w['gluw'][l], w['glub'][l], w['g_ssm'][l],
                                       name=f"s_ssm_{l}")
    att, kn, vn = _attn_sample_call(q.reshape(n_seq * N_HEADS, LANES), kv, k_buf, v_buf, l,
                                    w['sink_rows'][l], w['g_rows'][l], w['own_rows'], name=f"s_attn_{l}")
    untranspose = lambda t: jnp.transpose(t.reshape(n_seq, N_KV_HEADS, HEAD_DIM, WINDOW), (0, 3, 1, 2))
    return (ssm_n, att.reshape(n_seq, Q_EXP), hre.reshape(n_seq, N_SSM_GROUPS, SSM_STATE),
            him.reshape(n_seq, N_SSM_GROUPS, SSM_STATE), untranspose(kn), untranspose(vn))


def _trunks(x_prompt, x_sample, h0_re, h0_im, k_buf, v_buf, w, *, tm):
    Bn, L, _ = x_prompt.shape
    n_seq = x_sample.shape[0]
    xp = x_prompt.reshape(Bn * L, D_MODEL)
    xs = x_sample.reshape(n_seq, D_MODEL)
    outs_p, outs_s = [], []
    mix_p = mix_s = None
    for l in range(DEPTH + 1):
        if l > 0:
            (xp,), (xs,) = _rows_call(xp, w['ffn2'][l - 1], w['ln'][l - 1][2], mix=mix_p,
                                      sample=dict(x=xs, mix=mix_s), tm=tm, name=f"mix_ffn2_{l - 1}")
        if l == DEPTH:
            break
        (xp, u, q, kv), (xs, us, qs, kvs) = _rows_call(xp, w['ffn1'][l], w['ln'][l][0], proj=w['w_in'][l],
                                                       sample=dict(x=xs, proj=w['w_in_exp'][l]), tm=tm,
                                                       name=f"ffn1_{l}")
        ssm_p, att_p, *state_p = _prompt_mixer(u, q, kv, l, w, Bn, L)
        ssm_s, att_s, *state_s = _sample_mixer(us, qs, kvs, l, w, h0_re[l], h0_im[l], k_buf, v_buf)
        mix_p = (ssm_p, att_p, w['wo_s'][l], w['wo_a'][l], w['ln'][l][1][0], w['ln'][l][1][1])
        mix_s = (ssm_s, att_s, w['wo_a_exp'][l])
        outs_p.append(state_p)
        outs_s.append(state_s)
    stack = lambda outs: tuple(jnp.stack([o[i] for o in outs]) for i in (2, 3, 0, 1))
    return (xp.reshape(Bn, L, D_MODEL), xs.reshape(n_seq, 1, D_MODEL)), stack(outs_p), stack(outs_s)


def kernel(x_prompt, x_sample, cache_k_win, cache_v_win, state_ssm_re, state_ssm_im, ln_g, ln_b, ffn1_w_in, ffn1_w_out, ffn2_w_in, ffn2_w_out, w_in, ssm_lam_re, ssm_lam_im, ssm_log_dt, ssm_b_re, ssm_b_im, ssm_c_re, ssm_c_im, ssm_d, glu_w, glu_b, attn_sinks, g_ssm_out, g_attn_out, w_out):
    w = _prep_weights(ln_g, ln_b, ffn1_w_in, ffn1_w_out, ffn2_w_in, ffn2_w_out, w_in, ssm_lam_re, ssm_lam_im,
                      ssm_log_dt, ssm_b_re, ssm_b_im, ssm_c_re, ssm_c_im, ssm_d, glu_w, glu_b, attn_sinks,
                      g_ssm_out, g_attn_out, w_out)
    n_seq = x_sample.shape[0]
    transposed = lambda c: jnp.transpose(c, (0, 1, 3, 4, 2)).reshape(DEPTH, n_seq, N_KV_HEADS * HEAD_DIM, WINDOW)
    (y_prompt, y_sample), (kp, vp, hrp, hip), (ks_, vs_, hrs, his) = _trunks(
        x_prompt, x_sample, state_ssm_re, state_ssm_im, transposed(cache_k_win), transposed(cache_v_win), w, tm=512)
    return (y_prompt, y_sample, kp, vp, hrp, hip, ks_, vs_, hrs, his)
```

```python
import functools

import jax
import jax.numpy as jnp
import numpy as np
from jax import lax
from jax.experimental import pallas as pl
from jax.experimental.pallas import tpu as pltpu

D_MODEL = 1024
DEPTH = 2
SSM_WIDTH = 512
SSM_GROUP = 16
N_SSM_GROUPS = 32
SSM_STATE = 64
N_STATES = N_SSM_GROUPS * SSM_STATE
ATTN_WIDTH = 512
HEAD_DIM = 64
N_HEADS = 8
N_KV_HEADS = 2
Q_PER_KV = 4
KV_WIDTH = 128
WINDOW = 128
ATTN_SCALE = HEAD_DIM ** -0.5
D_FF = 2816
ALPHA = (2.0 * DEPTH) ** 0.25
LN_EPS = 1e-5
RMS_EPS = 1e-6

ROW_GROUP = 256
W_STAGE_SLOTS = 4
W_GU_STAGE_ROWS = 64
W_DN_STAGE_ROWS = 352
LANES = 128
SUBLANES = 8
VMEM_LIMIT_BYTES = 56 * 1024 * 1024
N_SLAB = SSM_WIDTH // LANES
GROUPS_PER_SLAB = LANES // SSM_GROUP
SLAB_STATES = GROUPS_PER_SLAB * SSM_STATE
Q_EXP = N_HEADS * LANES
SSM_CHUNK = 8
SSM_ROWS = 2048
ATTN_BLOCKS = 4
ATTN_HEAD_ORDER = (0, 2, 1, 3)
MASKED = -1e30

F32 = jnp.float32
BF16 = jnp.bfloat16


def _layer_norm(r, g, b):
    mu = jnp.mean(r, axis=-1, keepdims=True)
    c = r - mu
    var = jnp.mean(c * c, axis=-1, keepdims=True)
    return c * lax.rsqrt(var + LN_EPS) * g + b


def _rms_norm(y, g):
    return y * lax.rsqrt(jnp.mean(y * y, axis=-1, keepdims=True) + RMS_EPS) * g


def _const(c, n_grid=1):
    if isinstance(c, tuple):
        arr, idx = c
        shape = (None,) + arr.shape[1:]
        index = (idx,) + (0,) * (arr.ndim - 1)
    else:
        arr, shape, index = c, c.shape, (0,) * c.ndim
    imap = (lambda i: index) if n_grid == 1 else (lambda i, j: index)
    return arr, pl.BlockSpec(shape, imap, pipeline_mode=pl.Buffered(1))


def _consts(cs, n_grid=1):
    arrs, specs = zip(*[_const(c, n_grid) for c in cs])
    return list(arrs), list(specs)


def _load_cast(src_hbm, layer, dst_ref, stage_ref, sem_ref):
    n_slots, rows = stage_ref.shape[:2]
    n_chunks = dst_ref.shape[0] // rows

    def copy(c):
        return pltpu.make_async_copy(src_hbm.at[layer, pl.ds(c * rows, rows), :], stage_ref.at[c % n_slots],
                                     sem_ref.at[c % n_slots])

    for c in range(min(n_slots - 1, n_chunks)):
        copy(c).start()
    for c in range(n_chunks):
        copy(c).wait()
        dst_ref[pl.ds(c * rows, rows), :] = stage_ref[c % n_slots].astype(BF16)
        if c + n_slots - 1 < n_chunks:
            copy(c + n_slots - 1).start()


def _rows_kernel(*refs, has_mix, has_proj, has_sample, n_sub, layer):
    it = iter(refs)
    take = lambda n: tuple(next(it) for _ in range(n))
    (x_ref,) = take(1)
    if has_mix:
        ssm_ref, att_ref, wo_s_ref, wo_a_ref, gm_ref, bm_ref = take(6)
    wgu_hbm, wdn_hbm, g_ref, b_ref = take(4)
    wp_ref = take(1)[0] if has_proj else None
    if has_sample:
        (xs_ref,) = take(1)
        if has_mix:
            ssm_s_ref, att_s_ref, pack_ref = take(3)
        pad_ref = take(1)[0] if has_proj else None
    n_out = 4 if has_proj else 1
    outs = take(n_out)
    outs_s = take(n_out) if has_sample else None
    wgu_ref, wdn_ref, stage_gu, stage_dn, sem_gu, sem_dn = take(6)

    @pl.when(pl.program_id(0) == 0)
    def _():
        _load_cast(wgu_hbm, layer, wgu_ref, stage_gu, sem_gu)
        _load_cast(wdn_hbm, layer, wdn_ref, stage_dn, sem_dn)

    def mix_inputs(ssm_r, att_r, rows):
        ssm = jnp.concatenate([ssm_r[j, rows, :] for j in range(N_SLAB)], axis=1).astype(BF16)
        return ssm, att_r[rows, :].astype(BF16)

    def run(xs, mixes, wo_a, wp):
        if has_mix:
            ms = [jnp.dot(ssm, wo_s_ref[...], preferred_element_type=F32)
                  + jnp.dot(att, wo_a[...], preferred_element_type=F32) for ssm, att in mixes]
            xs = [_layer_norm(ALPHA * x + m, gm_ref[...], bm_ref[...]) for x, m in zip(xs, ms)]
        gus = [jnp.dot(x.astype(BF16), wgu_ref[...], preferred_element_type=F32) for x in xs]
        hs = [(gu[:, :D_FF] * jax.nn.sigmoid(gu[:, :D_FF]) * gu[:, D_FF:]).astype(BF16) for gu in gus]
        ys = [jnp.dot(h, wdn_ref[...], preferred_element_type=F32) for h in hs]
        xs = [_layer_norm(ALPHA * x + 0.5 * y, g_ref[...], b_ref[...]) for x, y in zip(xs, ys)]
        if has_proj:
            zs = [jnp.dot(x.astype(BF16), wp[...], preferred_element_type=F32) for x in xs]
        else:
            zs = [None] * len(xs)
        return xs, zs

    def store(out_refs, rows, x, z):
        out_refs[0][rows, :] = x
        if has_proj:
            u_ref, q_ref, kv_ref = out_refs[1:]
            q_cols = q_ref.shape[1]
            for j in range(N_SLAB):
                u_ref[j, rows, :] = z[:, j * LANES:(j + 1) * LANES]
            q_ref[rows, :] = z[:, SSM_WIDTH:SSM_WIDTH + q_cols].astype(q_ref.dtype)
            kv_ref[rows, :] = z[:, SSM_WIDTH + q_cols:]

    sub = x_ref.shape[0] // n_sub
    groups = [slice(r * sub, (r + 1) * sub) for r in range(n_sub)]
    mixes = [mix_inputs(ssm_ref, att_ref, rows) for rows in groups] if has_mix else None
    xs, zs = run([x_ref[rows, :] for rows in groups], mixes, wo_a_ref if has_mix else None, wp_ref)
    for rows, x, z in zip(groups, xs, zs):
        store(outs, rows, x, z)

    if has_sample:
        @pl.when(pl.program_id(0) == pl.num_programs(0) - 1)
        def _():
            every = slice(None)
            mixes_s = None
            if has_mix:
                ssm, att_padded = mix_inputs(ssm_s_ref, att_s_ref, every)
                att = jnp.dot(att_padded, pack_ref[...], preferred_element_type=F32).astype(BF16)
                mixes_s = [(ssm, att)]
            xs_s, zs_s = run([xs_ref[...]], mixes_s, wo_a_ref if has_mix else None, wp_ref)
            z = zs_s[0]
            if has_proj:
                q = z[:, SSM_WIDTH:SSM_WIDTH + ATTN_WIDTH].astype(BF16)
                z = jnp.concatenate([z[:, :SSM_WIDTH], jnp.dot(q, pad_ref[...], preferred_element_type=F32),
                                     z[:, SSM_WIDTH + ATTN_WIDTH:]], axis=1)
            store(outs_s, every, xs_s[0], z)


def _head_selectors():
    h = np.arange(ATTN_WIDTH) // HEAD_DIM
    d = np.arange(ATTN_WIDTH) % HEAD_DIM
    kvh, pr = h // Q_PER_KV, h % Q_PER_KV
    padded_col = h * LANES + kvh * HEAD_DIM + d
    packed_col = pr * LANES + kvh * HEAD_DIM + d
    pad = np.zeros((ATTN_WIDTH, Q_EXP), np.float32)
    pad[np.arange(ATTN_WIDTH), padded_col] = 1.0
    pack = np.zeros((Q_EXP, ATTN_WIDTH), np.float32)
    pack[padded_col, packed_col] = 1.0
    return jnp.asarray(pad, BF16), jnp.asarray(pack, BF16)


def _rows_call(x, ffn, ln, mix=None, proj=None, sample=None, *, tm, name):
    M = x.shape[0]
    assert M % tm == 0
    row = lambda w: pl.BlockSpec((tm, w), lambda i: (i, 0))
    slab = lambda n: pl.BlockSpec((n, tm, LANES), lambda i: (0, i, 0))
    whole = lambda a: pl.BlockSpec(a.shape, lambda i: (0,) * len(a.shape))
    args = [x]
    specs = [row(D_MODEL)]
    if mix is not None:
        ssm_n, att_n = mix[:2]
        arrs, cspecs = _consts(mix[2:])
        args += [ssm_n, att_n] + arrs
        specs += [slab(N_SLAB), row(att_n.shape[1])] + cspecs
    w_gu, w_dn, layer = ffn
    arrs, cspecs = _consts(list(ln))
    args += [w_gu, w_dn] + arrs
    specs += [pl.BlockSpec(memory_space=pl.ANY), pl.BlockSpec(memory_space=pl.ANY)] + cspecs

    def proj_outputs(rows, q_cols, q_dtype):
        return [jax.ShapeDtypeStruct((N_SLAB, rows, LANES), F32), jax.ShapeDtypeStruct((rows, q_cols), q_dtype),
                jax.ShapeDtypeStruct((rows, 2 * KV_WIDTH), F32)]

    out_shape = [jax.ShapeDtypeStruct((M, D_MODEL), F32)]
    out_specs = [row(D_MODEL)]
    if proj is not None:
        arr, cspec = _const(proj)
        args.append(arr)
        specs.append(cspec)
        q_cols = arr.shape[-1] - SSM_WIDTH - 2 * KV_WIDTH
        out_shape += proj_outputs(M, q_cols, BF16)
        out_specs += [slab(N_SLAB), row(q_cols), row(2 * KV_WIDTH)]
    n_main = len(out_shape)
    if sample is not None:
        xs = sample['x']
        args.append(xs)
        specs.append(whole(xs))
        pad, pack = _head_selectors()
        if mix is not None:
            ssm_s, att_s = sample['mix']
            arr, cspec = _const(pack)
            args += [ssm_s, att_s, arr]
            specs += [whole(ssm_s), whole(att_s), cspec]
        sample_out = [jax.ShapeDtypeStruct(xs.shape, F32)]
        if proj is not None:
            arr, cspec = _const(pad)
            args.append(arr)
            specs.append(cspec)
            sample_out += proj_outputs(xs.shape[0], Q_EXP, F32)
        out_shape += sample_out
        out_specs += [whole(o) for o in sample_out]
    outs = pl.pallas_call(
        functools.partial(_rows_kernel, has_mix=mix is not None, has_proj=proj is not None,
                          has_sample=sample is not None, n_sub=max(1, tm // ROW_GROUP), layer=layer),
        grid=(M // tm,),
        in_specs=specs,
        out_specs=out_specs,
        out_shape=out_shape,
        scratch_shapes=[pltpu.VMEM(w_gu.shape[1:], BF16), pltpu.VMEM(w_dn.shape[1:], BF16),
                        pltpu.VMEM((W_STAGE_SLOTS, W_GU_STAGE_ROWS, w_gu.shape[2]), F32),
                        pltpu.VMEM((W_STAGE_SLOTS, W_DN_STAGE_ROWS, w_dn.shape[2]), F32),
                        pltpu.SemaphoreType.DMA((W_STAGE_SLOTS,)), pltpu.SemaphoreType.DMA((W_STAGE_SLOTS,))],
        compiler_params=pltpu.CompilerParams(dimension_semantics=("arbitrary",),
                                             vmem_limit_bytes=VMEM_LIMIT_BYTES),
        name=name,
    )(*args)
    return outs[:n_main], (outs[n_main:] if sample is not None else None)


def _cmul(xr, xi, yr, yi):
    return xr * yr - xi * yi, xr * yi + xi * yr


def _ssm_tables_kernel(p_ref, af_ref, wst_ref, cat_ref, strip_ref, ak_ref, tr_ref):
    t1 = SSM_CHUNK
    a_re, a_im, lam_re, lam_im, b_re, b_im, c_re, c_im = (p_ref[k] for k in range(8))
    num_re, num_im = a_re - 1.0, a_im
    den = lam_re * lam_re + lam_im * lam_im
    f_re = (num_re * lam_re + num_im * lam_im) / den
    f_im = (num_im * lam_re - num_re * lam_im) / den
    bb_re, bb_im = _cmul(f_re, f_im, b_re, b_im)

    def same_group(shape, row_div, col_div):
        r = lax.broadcasted_iota(jnp.int32, shape, 0) // row_div
        c = lax.broadcasted_iota(jnp.int32, shape, 1) // col_div
        return (r == c).astype(F32)

    m_state = same_group((LANES, SLAB_STATES), SSM_GROUP, SSM_STATE)
    m_chan = 0.5 * same_group((LANES, LANES), SSM_GROUP, SSM_GROUP)

    def expand(x):
        return (jnp.concatenate([x] * (SLAB_STATES // LANES), axis=1) * m_state).astype(BF16)

    def dot_nt(a, b):
        return lax.dot_general(a, b, (((1,), (1,)), ((), ())), precision=lax.Precision.HIGHEST,
                               preferred_element_type=F32)

    pw_re, pw_im = jnp.ones_like(a_re), jnp.zeros_like(a_re)
    kd = []
    for l in range(t1):
        ab_re, ab_im = _cmul(pw_re, pw_im, bb_re, bb_im)
        s = t1 - 1 - l
        kd_l = []
        for j in range(N_SLAB):
            rows = slice(j * LANES, (j + 1) * LANES)
            wst_ref[j, s * LANES:(s + 1) * LANES, :SLAB_STATES] = expand(ab_re[rows])
            wst_ref[j, s * LANES:(s + 1) * LANES, SLAB_STATES:] = expand(ab_im[rows])
            k = dot_nt(ab_re[rows], c_re[rows]) - dot_nt(ab_im[rows], c_im[rows])
            kd_l.append((k * m_chan).astype(BF16))
        kd.append(kd_l)
        pw_re, pw_im = _cmul(pw_re, pw_im, a_re, a_im)
        cf_re = c_re * pw_re - c_im * pw_im
        cf_im = -(c_re * pw_im + c_im * pw_re)
        for j in range(N_SLAB):
            rows = slice(j * LANES, (j + 1) * LANES)
            cat_ref[j, l * LANES:(l + 1) * LANES, :SLAB_STATES] = expand(cf_re[rows])
            cat_ref[j, l * LANES:(l + 1) * LANES, SLAB_STATES:] = expand(cf_im[rows])
    for j in range(N_SLAB):
        for rho in range(t1):
            for c in range(2):
                lag = t1 - 2 - rho + c
                blk = kd[lag][j] if lag >= 0 else jnp.zeros((LANES, LANES), BF16)
                strip_ref[j, rho * LANES:(rho + 1) * LANES, c * LANES:(c + 1) * LANES] = blk

    f_re, f_im = af_ref[0], af_ref[1]
    base_re, base_im = f_re, f_im
    for _ in range(t1 - 1):
        base_re, base_im = _cmul(base_re, base_im, f_re, f_im)
    row = lax.broadcasted_iota(jnp.int32, (SUBLANES, N_STATES), 0)
    pws = [(base_re, base_im)]
    for _ in range(SUBLANES - 1):
        pws.append(_cmul(pws[-1][0], pws[-1][1], base_re, base_im))
    for n, k in enumerate((1, 2, 4)):
        for part in range(2):
            ak_ref[2 * n + part] = jnp.where(row >= k, jnp.broadcast_to(pws[k - 1][part], row.shape), 0.0)
    for part in range(2):
        acc = jnp.zeros(row.shape, F32)
        for r in range(SUBLANES):
            acc = jnp.where(row == r, jnp.broadcast_to(pws[r][part], row.shape), acc)
        tr_ref[part] = acc


def _ssm_tables_call(params, a_flat):
    t1 = SSM_CHUNK
    shapes = [((N_SLAB, t1 * LANES, 2 * SLAB_STATES), BF16),
              ((N_SLAB, t1 * LANES, 2 * SLAB_STATES), BF16),
              ((N_SLAB, t1 * LANES, 2 * LANES), BF16),
              ((6, SUBLANES, N_STATES), F32),
              ((2, SUBLANES, N_STATES), F32)]
    per_layer = lambda shape: pl.BlockSpec((None,) + tuple(shape), lambda l: (l,) + (0,) * len(shape))
    return pl.pallas_call(
        _ssm_tables_kernel,
        grid=(DEPTH,),
        in_specs=[per_layer(params.shape[1:]), per_layer(a_flat.shape[1:])],
        out_specs=[per_layer(s) for s, _ in shapes],
        out_shape=[jax.ShapeDtypeStruct((DEPTH,) + s, d) for s, d in shapes],
        compiler_params=pltpu.CompilerParams(dimension_semantics=("arbitrary",),
                                             vmem_limit_bytes=VMEM_LIMIT_BYTES),
        name="ssm_tables",
    )(params, a_flat)


def _ssm_table_inputs(lam_re, lam_im, log_dt, b_re, b_im, c_re, c_im):
    dt = jnp.exp(log_dt)[..., None]
    mag = jnp.exp(lam_re * dt)
    a_re = mag * jnp.cos(lam_im * dt)
    a_im = mag * jnp.sin(lam_im * dt)
    rep = lambda x: jnp.repeat(x, SSM_GROUP, axis=1)
    flat = lambda x: x.reshape(DEPTH, SSM_WIDTH, SSM_STATE)
    params = jnp.stack([rep(a_re), rep(a_im), rep(lam_re), rep(lam_im),
                        flat(jnp.swapaxes(b_re, 2, 3)), flat(jnp.swapaxes(b_im, 2, 3)), flat(c_re), flat(c_im)],
                       axis=1)
    params = jnp.concatenate([params, params], axis=-1)
    a_flat = jnp.stack([a_re.reshape(DEPTH, 1, N_STATES), a_im.reshape(DEPTH, 1, N_STATES)], axis=1)
    return params, a_flat


def _glu_rms(y, gluw_ref, glub_ref, g_ref):
    g = jax.nn.gelu(y).astype(BF16)
    zz = jnp.dot(g, gluw_ref[...], preferred_element_type=F32) + glub_ref[...]
    o = zz[:, :SSM_WIDTH] * jax.nn.sigmoid(zz[:, SSM_WIDTH:])
    return _rms_norm(o, g_ref[...])


def _ssm_kernel(u_ref, wst_ref, cat_ref, strip_ref, ak_ref, tr_ref, d_ref, gluw_ref, glub_ref, g_ref,
                o_ref, hre_ref, him_ref,
                ucb_ref, sre_ref, sim_ref, hcr_ref, hci_ref):
    t1 = SSM_CHUNK
    nc = SSM_ROWS // t1
    i = pl.program_id(1)

    @pl.when(i == 0)
    def _():
        hcr_ref[...] = jnp.zeros_like(hcr_ref)
        hci_ref[...] = jnp.zeros_like(hci_ref)

    for j in range(N_SLAB):
        for s in range(t1):
            ucb_ref[j, :, s * LANES:(s + 1) * LANES] = u_ref[j, pl.ds(s, nc, stride=t1), :].astype(BF16)

    for j in range(N_SLAB):
        st = jnp.dot(ucb_ref[j], wst_ref[j], preferred_element_type=F32)
        sre_ref[:, j * SLAB_STATES:(j + 1) * SLAB_STATES] = st[:, :SLAB_STATES]
        sim_ref[:, j * SLAB_STATES:(j + 1) * SLAB_STATES] = st[:, SLAB_STATES:]

    row = lax.broadcasted_iota(jnp.int32, (SUBLANES, SLAB_STATES), 0)
    for j in range(N_SLAB):
        cols = pl.ds(j * SLAB_STATES, SLAB_STATES)

        def body(gi, carry, cols=cols):
            hr, hi = carry
            r0 = pl.multiple_of(gi * SUBLANES, SUBLANES)
            xr = sre_ref[pl.ds(r0, SUBLANES), cols]
            xi = sim_ref[pl.ds(r0, SUBLANES), cols]
            for n, k in enumerate((1, 2, 4)):
                akr = ak_ref[2 * n, :, cols]
                aki = ak_ref[2 * n + 1, :, cols]
                sr = pltpu.roll(xr, k, 0)
                si = pltpu.roll(xi, k, 0)
                xr, xi = xr + akr * sr - aki * si, xi + akr * si + aki * sr
            tr = tr_ref[0, :, cols]
            ti = tr_ref[1, :, cols]
            er = xr + tr * hr - ti * hi
            ei = xi + tr * hi + ti * hr
            sre_ref[pl.ds(r0, SUBLANES), cols] = jnp.where(row >= 1, pltpu.roll(er, 1, 0), hr)
            sim_ref[pl.ds(r0, SUBLANES), cols] = jnp.where(row >= 1, pltpu.roll(ei, 1, 0), hi)
            last = SUBLANES - 1
            return (jnp.broadcast_to(er[last:last + 1], er.shape), jnp.broadcast_to(ei[last:last + 1], ei.shape))

        hr, hi = lax.fori_loop(0, nc // SUBLANES, body, (hcr_ref[:, cols], hci_ref[:, cols]), unroll=True)
        hcr_ref[:, cols] = hr
        hci_ref[:, cols] = hi

    hre_ref[0] = hcr_ref[...]
    him_ref[0] = hci_ref[...]

    ys = []
    for j in range(N_SLAB):
        cols = slice(j * SLAB_STATES, (j + 1) * SLAB_STATES)
        hprev = jnp.concatenate([sre_ref[:, cols], sim_ref[:, cols]], axis=1).astype(BF16)
        y = lax.dot_general(hprev, cat_ref[j], (((1,), (1,)), ((), ())), preferred_element_type=F32)
        parts = []
        for p2 in range(t1 // 2):
            kk = (2 * p2 + 2) * LANES
            yi = jnp.dot(ucb_ref[j, :, :kk], strip_ref[j, (t1 - 2 - 2 * p2) * LANES:, :],
                         preferred_element_type=F32)
            parts.append(y[:, 2 * p2 * LANES:(2 * p2 + 2) * LANES] + yi)
        ys.append(jnp.concatenate(parts, axis=1))

    for t in range(t1):
        y = jnp.concatenate([ys[j][:, t * LANES:(t + 1) * LANES] + d_ref[j] * u_ref[j, pl.ds(t, nc, stride=t1), :]
                             for j in range(N_SLAB)], axis=1)
        n = _glu_rms(y, gluw_ref, glub_ref, g_ref)
        for j in range(N_SLAB):
            o_ref[j, pl.ds(t, nc, stride=t1), :] = n[:, j * LANES:(j + 1) * LANES]


def _ssm_prompt_call(u, tab, d_t, gluw, glub, g, *, batch, seq, name):
    t1 = SSM_CHUNK
    nc = SSM_ROWS // t1
    steps = seq // SSM_ROWS
    blk = pl.BlockSpec((N_SLAB, SSM_ROWS, LANES), lambda b, i: (0, b * steps + i, 0))
    st = pl.BlockSpec((1, SUBLANES, N_STATES), lambda b, i: (b, 0, 0))
    consts, cspecs = _consts(list(tab) + [d_t, gluw, glub, g], 2)
    out, hre, him = pl.pallas_call(
        _ssm_kernel,
        grid=(batch, steps),
        in_specs=[blk] + cspecs,
        out_specs=[blk, st, st],
        out_shape=[jax.ShapeDtypeStruct(u.shape, F32),
                   jax.ShapeDtypeStruct((batch, SUBLANES, N_STATES), F32),
                   jax.ShapeDtypeStruct((batch, SUBLANES, N_STATES), F32)],
        scratch_shapes=[pltpu.VMEM((N_SLAB, nc, t1 * LANES), BF16),
                        pltpu.VMEM((nc, N_STATES), F32),
                        pltpu.VMEM((nc, N_STATES), F32),
                        pltpu.VMEM((SUBLANES, N_STATES), F32),
                        pltpu.VMEM((SUBLANES, N_STATES), F32)],
        compiler_params=pltpu.CompilerParams(dimension_semantics=("arbitrary", "arbitrary"),
                                             vmem_limit_bytes=VMEM_LIMIT_BYTES),
        name=name,
    )(u, *consts)
    return out, hre[:, 0], him[:, 0]


def _ssm_sample_kernel(u_ref, h0r_ref, h0i_ref, wst_ref, cat_ref, kd_ref, a_ref, d_ref, gluw_ref, glub_ref, g_ref,
                       o_ref, hr_ref, hi_ref):
    ys = []
    for j in range(N_SLAB):
        cols = slice(j * SLAB_STATES, (j + 1) * SLAB_STATES)
        uf = u_ref[j]
        ub = uf.astype(BF16)
        st = jnp.dot(ub, wst_ref[j], preferred_element_type=F32)
        h0r = h0r_ref[:, cols]
        h0i = h0i_ref[:, cols]
        ar = a_ref[0, :, cols]
        ai = a_ref[1, :, cols]
        hr_ref[:, cols] = ar * h0r - ai * h0i + st[:, :SLAB_STATES]
        hi_ref[:, cols] = ar * h0i + ai * h0r + st[:, SLAB_STATES:]
        hcat = jnp.concatenate([h0r, h0i], axis=1).astype(BF16)
        y = lax.dot_general(hcat, cat_ref[j], (((1,), (1,)), ((), ())), preferred_element_type=F32)
        y = y + jnp.dot(ub, kd_ref[j], preferred_element_type=F32)
        ys.append(y + d_ref[j] * uf)
    n = _glu_rms(jnp.concatenate(ys, axis=1), gluw_ref, glub_ref, g_ref)
    for j in range(N_SLAB):
        o_ref[j] = n[:, j * LANES:(j + 1) * LANES]


def _ssm_sample_call(u, h0r, h0i, tab, a_flat, d_t, gluw, glub, g, *, name):
    n_seq = u.shape[1]
    (wst, layer), (cat, _), (strip, _) = tab[:3]
    last = SSM_CHUNK - 1
    whole = lambda a: pl.BlockSpec(a.shape, lambda i: (0,) * len(a.shape))
    outs = [jax.ShapeDtypeStruct(u.shape, F32),
            jax.ShapeDtypeStruct((n_seq, N_STATES), F32),
            jax.ShapeDtypeStruct((n_seq, N_STATES), F32)]
    consts, cspecs = _consts([a_flat, d_t, gluw, glub, g])
    return pl.pallas_call(
        _ssm_sample_kernel,
        grid=(1,),
        in_specs=[whole(u), whole(h0r), whole(h0i),
                  pl.BlockSpec((None, N_SLAB, LANES, 2 * SLAB_STATES), lambda i: (layer, 0, last, 0)),
                  pl.BlockSpec((None, N_SLAB, LANES, 2 * SLAB_STATES), lambda i: (layer, 0, 0, 0)),
                  pl.BlockSpec((None, N_SLAB, LANES, LANES), lambda i: (layer, 0, last, 1))] + cspecs,
        out_specs=[whole(o) for o in outs],
        out_shape=outs,
        compiler_params=pltpu.CompilerParams(dimension_semantics=("arbitrary",),
                                             vmem_limit_bytes=VMEM_LIMIT_BYTES),
        name=name,
    )(u, h0r, h0i, wst, cat, strip, *consts)


def _attn_kernel(sink_ref, q_ref, kvc_ref, kvp_ref, g_ref, o_ref):
    i = pl.program_id(1)
    kj = lax.broadcasted_iota(jnp.int32, (2 * WINDOW, WINDOW), 0)
    qi = lax.broadcasted_iota(jnp.int32, (2 * WINDOW, WINDOW), 1)
    band = (kj >= qi) & (kj <= qi + WINDOW)
    low = qi < HEAD_DIM
    for sb in range(ATTN_BLOCKS):
        blk = slice(sb * WINDOW, (sb + 1) * WINDOW)
        kc = kvc_ref[blk, :]
        if sb == 0:
            kp = kvp_ref[...]
            valid = band & ((i > 0) | (kj >= WINDOW))
        else:
            kp = kvc_ref[(sb - 1) * WINDOW:sb * WINDOW, :]
            valid = band
        valid2 = jnp.concatenate([valid, valid], axis=1)
        kcat = jnp.concatenate([kp[:, :KV_WIDTH], kc[:, :KV_WIDTH]], axis=0) * ATTN_SCALE
        swap = pltpu.roll(kcat, HEAD_DIM, 1)
        zero = jnp.zeros_like(kcat)
        k_sel = [[jnp.where(low, kcat, zero).astype(BF16), jnp.where(low, zero, swap).astype(BF16)],
                 [jnp.where(low, swap, zero).astype(BF16), jnp.where(low, zero, kcat).astype(BF16)]]
        v_t = jnp.concatenate([kp[:, KV_WIDTH:], kc[:, KV_WIDTH:]], axis=0).T.astype(BF16)
        parts = []
        for kvh in range(N_KV_HEADS):
            q_pair = jnp.concatenate([q_ref[blk, (2 * kvh) * LANES:(2 * kvh + 1) * LANES],
                                      q_ref[blk, (2 * kvh + 1) * LANES:(2 * kvh + 2) * LANES]], axis=0)
            for parity in range(2):
                s_t = lax.dot_general(k_sel[kvh][parity], q_pair, (((1,), (1,)), ((), ())),
                                      preferred_element_type=F32)
                parts.append(jnp.where(valid2, s_t, MASKED))
        s_all = jnp.concatenate(parts, axis=1)
        sink = sink_ref[...]
        m = jnp.maximum(jnp.max(s_all, axis=0, keepdims=True), sink)
        p_all = jnp.exp(s_all - m)
        inv_den = 1.0 / (jnp.sum(p_all, axis=0, keepdims=True) + jnp.exp(sink - m))
        p_all = p_all.astype(BF16)
        half = Q_PER_KV * WINDOW
        o_kv = [jnp.dot(v_t[kvh * HEAD_DIM:(kvh + 1) * HEAD_DIM], p_all[:, kvh * half:(kvh + 1) * half],
                        preferred_element_type=F32) * inv_den[:, kvh * half:(kvh + 1) * half]
                for kvh in range(N_KV_HEADS)]
        tiles = []
        for pr in range(Q_PER_KV):
            lanes = slice(ATTN_HEAD_ORDER.index(pr) * WINDOW, (ATTN_HEAD_ORDER.index(pr) + 1) * WINDOW)
            tiles.append(jnp.concatenate([o_kv[0][:, lanes], o_kv[1][:, lanes]], axis=0))
        ss = None
        for tile in tiles:
            t = jnp.sum(tile * tile, axis=0, keepdims=True)
            ss = t if ss is None else ss + t
        inv = lax.rsqrt(ss / ATTN_WIDTH + RMS_EPS)
        for pr, tile in enumerate(tiles):
            o_ref[blk, pr * LANES:(pr + 1) * LANES] = ((tile * inv).T * g_ref[pr]).astype(o_ref.dtype)


def _attn_prompt_call(q, kv, sinks, g_perm, *, batch, seq, name):
    rows = ATTN_BLOCKS * WINDOW
    nb = seq // rows
    M = kv.shape[0]
    cur = lambda b, i: (b * nb + i, 0)
    prev = lambda b, i: ((b * nb + i) * ATTN_BLOCKS - jnp.minimum(i, 1), 0)
    (sinks, g_perm), (sink_spec, g_spec) = _consts([sinks, g_perm], 2)
    return pl.pallas_call(
        _attn_kernel,
        grid=(batch, nb),
        in_specs=[sink_spec,
                  pl.BlockSpec((rows, ATTN_WIDTH), cur),
                  pl.BlockSpec((rows, 2 * KV_WIDTH), cur),
                  pl.BlockSpec((WINDOW, 2 * KV_WIDTH), prev),
                  g_spec],
        out_specs=pl.BlockSpec((rows, ATTN_WIDTH), cur),
        out_shape=jax.ShapeDtypeStruct((M, ATTN_WIDTH), BF16),
        compiler_params=pltpu.CompilerParams(dimension_semantics=("arbitrary", "arbitrary"),
                                             vmem_limit_bytes=VMEM_LIMIT_BYTES),
        name=name,
    )(sinks, q, kv, kv, g_perm)


SEQ_PER_STEP = LANES // N_HEADS


def _attn_sample_kernel(q_ref, kv_ref, ckt_ref, cvt_ref, sink_ref, g_ref, own_ref, o_ref, kot_ref, vot_ref):
    nrow = SEQ_PER_STEP * N_HEADS
    qb = q_ref[...].astype(BF16)
    k_new = kv_ref[:, :KV_WIDTH]
    v_new = kv_ref[:, KV_WIDTH:]
    row_seq = lax.broadcasted_iota(jnp.int32, (nrow, LANES), 0) // N_HEADS
    lane_seq = lax.broadcasted_iota(jnp.int32, (nrow, LANES), 1) // N_HEADS
    lane = lax.broadcasted_iota(jnp.int32, (nrow, LANES), 1)
    pick = (lax.broadcasted_iota(jnp.int32, (nrow, SEQ_PER_STEP), 0) // N_HEADS
            == lax.broadcasted_iota(jnp.int32, (nrow, SEQ_PER_STEP), 1)).astype(BF16)

    qf = q_ref[...]
    heads_of = lambda a, n: a[n * N_HEADS:(n + 1) * N_HEADS]
    s = jnp.concatenate([jnp.dot(heads_of(qf, n).astype(BF16), ckt_ref[n].astype(BF16), preferred_element_type=F32)
                         for n in range(SEQ_PER_STEP)], axis=0) * ATTN_SCALE
    k_rows = jnp.dot(pick, k_new.astype(BF16), preferred_element_type=F32)
    s_new = jnp.sum(qb.astype(F32) * k_rows, axis=-1, keepdims=True) * ATTN_SCALE
    sink = sink_ref[...]
    m = jnp.maximum(jnp.maximum(jnp.max(s, axis=-1, keepdims=True), s_new), sink)
    p = jnp.exp(s - m)
    p_new = jnp.exp(s_new - m)
    inv = 1.0 / (jnp.sum(p, axis=-1, keepdims=True) + p_new + jnp.exp(sink - m))
    pn = p * inv
    o = jnp.concatenate([lax.dot_general(heads_of(pn, n).astype(BF16), cvt_ref[n].astype(BF16),
                                         (((1,), (1,)), ((), ())), preferred_element_type=F32)
                         for n in range(SEQ_PER_STEP)], axis=0)
    v_rows = jnp.dot(pick, v_new.astype(BF16), preferred_element_type=F32)
    o = o + (p_new * inv).astype(BF16).astype(F32) * v_rows

    hi = lax.Precision.HIGHEST
    row_sums = jnp.dot(o * o * own_ref[...], jnp.ones((LANES, LANES), F32), precision=hi, preferred_element_type=F32)
    ss = jnp.dot((row_seq == lane_seq).astype(F32), row_sums, precision=hi, preferred_element_type=F32)
    o_ref[...] = o * lax.rsqrt(ss / ATTN_WIDTH + RMS_EPS) * g_ref[...]

    pad = jnp.zeros((LANES - SEQ_PER_STEP, KV_WIDTH), F32)
    k_cols = jnp.concatenate([k_new, pad], axis=0).T
    v_cols = jnp.concatenate([v_new, pad], axis=0).T
    last = lane == WINDOW - 1
    for n in range(SEQ_PER_STEP):
        kot_ref[n] = jnp.where(last, pltpu.roll(k_cols, WINDOW - 1 - n, 1), pltpu.roll(ckt_ref[n], WINDOW - 1, 1))
        vot_ref[n] = jnp.where(last, pltpu.roll(v_cols, WINDOW - 1 - n, 1), pltpu.roll(cvt_ref[n], WINDOW - 1, 1))


def _attn_sample_call(q, kv, ck, cv, layer, sink_row, g_rows, own_rows, *, name):
    n_seq = kv.shape[0]
    assert n_seq % SEQ_PER_STEP == 0
    nrow = SEQ_PER_STEP * N_HEADS
    cblk = pl.BlockSpec((SEQ_PER_STEP, WINDOW, 2 * HEAD_DIM), lambda i: (i, 0, 0))
    cin = pl.BlockSpec((None, SEQ_PER_STEP, WINDOW, 2 * HEAD_DIM), lambda i: (layer, i, 0, 0))
    qblk = pl.BlockSpec((nrow, LANES), lambda i: (i, 0))
    consts, cspecs = _consts([sink_row, g_rows, own_rows])
    return pl.pallas_call(
        _attn_sample_kernel,
        grid=(n_seq // SEQ_PER_STEP,),
        in_specs=[qblk, pl.BlockSpec((SEQ_PER_STEP, 2 * KV_WIDTH), lambda i: (i, 0)), cin, cin] + cspecs,
        out_specs=[qblk, cblk, cblk],
        out_shape=[jax.ShapeDtypeStruct(q.shape, F32),
                   jax.ShapeDtypeStruct(ck.shape[1:], F32),
                   jax.ShapeDtypeStruct(cv.shape[1:], F32)],
        compiler_params=pltpu.CompilerParams(dimension_semantics=("arbitrary",),
                                             vmem_limit_bytes=VMEM_LIMIT_BYTES),
        name=name,
    )(q, kv, ck, cv, *consts)


def _expand_heads(a):
    lead = a.shape[:-1]
    a = a.reshape(lead + (N_KV_HEADS, Q_PER_KV, 1, HEAD_DIM))
    sel = jnp.eye(N_KV_HEADS, dtype=a.dtype).reshape(N_KV_HEADS, 1, N_KV_HEADS, 1)
    return (a * sel).reshape(lead + (Q_EXP,))


def _pair_heads(a):
    lead = a.shape[:-1]
    a = a.reshape(lead + (N_KV_HEADS, Q_PER_KV, HEAD_DIM))
    return jnp.swapaxes(a, -3, -2).reshape(lead + (ATTN_WIDTH,))


def _prep_weights(ln_g, ln_b, ffn1_w_in, ffn1_w_out, ffn2_w_in, ffn2_w_out, w_in, ssm_lam_re, ssm_lam_im, ssm_log_dt,
                  ssm_b_re, ssm_b_im, ssm_c_re, ssm_c_im, ssm_d, glu_w, glu_b, attn_sinks, g_ssm_out, g_attn_out,
                  w_out):
    rows_t =lambda a: jnp.swapaxes(a, 1, 2)
    params, a_flat = _ssm_table_inputs(ssm_lam_re, ssm_lam_im, ssm_log_dt, ssm_b_re, ssm_b_im, ssm_c_re, ssm_c_im)
    tables = _ssm_tables_call(params, a_flat)
    head_order = jnp.array([Q_PER_KV * kvh + o for kvh in range(N_KV_HEADS) for o in ATTN_HEAD_ORDER])
    g_exp = _expand_heads(g_attn_out).reshape(DEPTH, N_HEADS, LANES)
    w_in_b = w_in.astype(BF16)
    w_out_b = w_out.astype(BF16)
    stacked = dict(
        w_in=w_in_b,
        wo_s=w_out_b[:, :SSM_WIDTH],
        wo_a=rows_t(_pair_heads(rows_t(w_out_b[:, SSM_WIDTH:]))),
        a_flat=a_flat,
        d_1=ssm_d.reshape(DEPTH, N_SLAB, 1, LANES),
        gluw=glu_w.astype(BF16),
        glub=glu_b.reshape(DEPTH, 1, 2 * SSM_WIDTH),
        g_ssm=g_ssm_out.reshape(DEPTH, 1, SSM_WIDTH),
        sinks=jnp.repeat(attn_sinks[:, head_order], WINDOW, axis=1)[:, None, :],
        g_pair=_pair_heads(g_attn_out).reshape(DEPTH, Q_PER_KV, 1, LANES),
        sink_rows=jnp.tile(attn_sinks, (1, SEQ_PER_STEP))[:, :, None],
        g_rows=jnp.tile(g_exp, (1, SEQ_PER_STEP, 1)),
    )
    w = {name: [(arr, l) for l in range(DEPTH)] for name, arr in stacked.items()}
    w['ffn1'] = [(ffn1_w_in, ffn1_w_out, l) for l in range(DEPTH)]
    w['ffn2'] = [(ffn2_w_in, ffn2_w_out, l) for l in range(DEPTH)]
    w['tab'] = [tuple((t, l) for t in tables) for l in range(DEPTH)]
    n_ln = ln_g.shape[1]
    ln_g3 = ln_g.reshape(DEPTH * n_ln, 1, D_MODEL)
    ln_b3 = ln_b.reshape(DEPTH * n_ln, 1, D_MODEL)
    w['ln'] = [[((ln_g3, l * n_ln + i), (ln_b3, l * n_ln + i)) for i in range(n_ln)] for l in range(DEPTH)]
    w['own_rows'] = jnp.tile(_expand_heads(jnp.ones((ATTN_WIDTH,), F32)).reshape(N_HEADS, LANES), (SEQ_PER_STEP, 1))
    return w


def _prompt_mixer(u, q, kv, l, w, batch, seq):
    ssm_n, hre, him = _ssm_prompt_call(u, w['tab'][l], w['d_1'][l], w['gluw'][l],
                                       w['glub'][l], w['g_ssm'][l], batch=batch, seq=seq, name=f"p_ssm_{l}")
    att_n = _attn_prompt_call(q, kv, w['sinks'][l], w['g_pair'][l], batch=batch, seq=seq, name=f"p_attn_{l}")
    kvw = kv.reshape(batch, seq, 2 * KV_WIDTH)[:, -WINDOW:].reshape(batch, WINDOW, 2, N_KV_HEADS, HEAD_DIM)
    return (ssm_n, att_n, hre.reshape(batch, N_SSM_GROUPS, SSM_STATE), him.reshape(batch, N_SSM_GROUPS, SSM_STATE),
            kvw[:, :, 0], kvw[:, :, 1])


def _sample_mixer(u, q, kv, l, w, h0_re, h0_im, k_buf, v_buf):
    n_seq = kv.shape[0]
    ssm_n, hre, him = _ssm_sample_call(u, h0_re.reshape(n_seq, N_STATES), h0_im.reshape(n_seq, N_STATES),
                                       w['tab'][l], w['a_flat'][l], w['d_1'][l], w['gluw'][l], w['glub'][l], w['g_ssm'][l],
                                       name=f"s_ssm_{l}")
    att, kn, vn = _attn_sample_call(q.reshape(n_seq * N_HEADS, LANES), kv, k_buf, v_buf, l,
                                    w['sink_rows'][l], w['g_rows'][l], w['own_rows'], name=f"s_attn_{l}")
    untranspose = lambda t: jnp.transpose(t.reshape(n_seq, N_KV_HEADS, HEAD_DIM, WINDOW), (0, 3, 1, 2))
    return (ssm_n, att.reshape(n_seq, Q_EXP), hre.reshape(n_seq, N_SSM_GROUPS, SSM_STATE),
            him.reshape(n_seq, N_SSM_GROUPS, SSM_STATE), untranspose(kn), untranspose(vn))


def _trunks(x_prompt, x_sample, h0_re, h0_im, k_buf, v_buf, w, *, tm):
    Bn, L, _ = x_prompt.shape
    n_seq = x_sample.shape[0]
    xp = x_prompt.reshape(Bn * L, D_MODEL)
    xs = x_sample.reshape(n_seq, D_MODEL)
    outs_p, outs_s = [], []
    mix_p = mix_s = None
    for l in range(DEPTH + 1):
        if l > 0:
            (xp,), (xs,) = _rows_call(xp, w['ffn2'][l - 1], w['ln'][l - 1][2], mix=mix_p,
                                      sample=dict(x=xs, mix=mix_s), tm=tm, name=f"mix_ffn2_{l - 1}")
        if l == DEPTH:
            break
        (xp, u, q, kv), (xs, us, qs, kvs) = _rows_call(xp, w['ffn1'][l], w['ln'][l][0], proj=w['w_in'][l],
                                                       sample=dict(x=xs), tm=tm,
                                                       name=f"ffn1_{l}")
        ssm_p, att_p, *state_p = _prompt_mixer(u, q, kv, l, w, Bn, L)
        ssm_s, att_s, *state_s = _sample_mixer(us, qs, kvs, l, w, h0_re[l], h0_im[l], k_buf, v_buf)
        mix_p = (ssm_p, att_p, w['wo_s'][l], w['wo_a'][l], w['ln'][l][1][0], w['ln'][l][1][1])
        mix_s = (ssm_s, att_s)
        outs_p.append(state_p)
        outs_s.append(state_s)
    stack = lambda outs: tuple(jnp.stack([o[i] for o in outs]) for i in (2, 3, 0, 1))
    return (xp.reshape(Bn, L, D_MODEL), xs.reshape(n_seq, 1, D_MODEL)), stack(outs_p), stack(outs_s)


def kernel(x_prompt, x_sample, cache_k_win, cache_v_win, state_ssm_re, state_ssm_im, ln_g, ln_b, ffn1_w_in, ffn1_w_out, ffn2_w_in, ffn2_w_out, w_in, ssm_lam_re, ssm_lam_im, ssm_log_dt, ssm_b_re, ssm_b_im, ssm_c_re, ssm_c_im, ssm_d, glu_w, glu_b, attn_sinks, g_ssm_out, g_attn_out, w_out):
    w = _prep_weights(ln_g, ln_b, ffn1_w_in, ffn1_w_out, ffn2_w_in, ffn2_w_out, w_in, ssm_lam_re, ssm_lam_im,
                      ssm_log_dt, ssm_b_re, ssm_b_im, ssm_c_re, ssm_c_im, ssm_d, glu_w, glu_b, attn_sinks,
                      g_ssm_out, g_attn_out, w_out)
    n_seq = x_sample.shape[0]
    transposed = lambda c: jnp.transpose(c, (0, 1, 3, 4, 2)).reshape(DEPTH, n_seq, N_KV_HEADS * HEAD_DIM, WINDOW)
    (y_prompt, y_sample), (kp, vp, hrp, hip), (ks_, vs_, hrs, his) = _trunks(
        x_prompt, x_sample, state_ssm_re, state_ssm_im, transposed(cache_k_win), transposed(cache_v_win), w, tm=512)
    return (y_prompt, y_sample, kp, vp, hrp, hip, ks_, vs_, hrs, his)
```

```python
import functools

import jax
import jax.numpy as jnp
import numpy as np
from jax import lax
from jax.experimental import pallas as pl
from jax.experimental.pallas import tpu as pltpu

D_MODEL = 1024
DEPTH = 2
SSM_WIDTH = 512
SSM_GROUP = 16
N_SSM_GROUPS = 32
SSM_STATE = 64
N_STATES = N_SSM_GROUPS * SSM_STATE
ATTN_WIDTH = 512
HEAD_DIM = 64
N_HEADS = 8
N_KV_HEADS = 2
Q_PER_KV = 4
KV_WIDTH = 128
WINDOW = 128
ATTN_SCALE = HEAD_DIM ** -0.5
D_FF = 2816
ALPHA = (2.0 * DEPTH) ** 0.25
LN_EPS = 1e-5
RMS_EPS = 1e-6

ROW_GROUP = 256
W_STAGE_SLOTS = 4
W_GU_STAGE_ROWS = 64
W_DN_STAGE_ROWS = 352
LANES = 128
SUBLANES = 8
VMEM_LIMIT_BYTES = 56 * 1024 * 1024
N_SLAB = SSM_WIDTH // LANES
GROUPS_PER_SLAB = LANES // SSM_GROUP
SLAB_STATES = GROUPS_PER_SLAB * SSM_STATE
Q_EXP = N_HEADS * LANES
SSM_CHUNK = 8
SSM_ROWS = 2048
ATTN_BLOCKS = 8
ATTN_HEAD_ORDER = (0, 2, 1, 3)
MASKED = -1e30

F32 = jnp.float32
BF16 = jnp.bfloat16


def _layer_norm(r, g, b):
    mu = jnp.mean(r, axis=-1, keepdims=True)
    c = r - mu
    var = jnp.mean(c * c, axis=-1, keepdims=True)
    return c * lax.rsqrt(var + LN_EPS) * g + b


def _rms_norm(y, g):
    return y * lax.rsqrt(jnp.mean(y * y, axis=-1, keepdims=True) + RMS_EPS) * g


def _const(c, n_grid=1):
    if isinstance(c, tuple):
        arr, idx = c
        shape = (None,) + arr.shape[1:]
        index = (idx,) + (0,) * (arr.ndim - 1)
    else:
        arr, shape, index = c, c.shape, (0,) * c.ndim
    imap = (lambda i: index) if n_grid == 1 else (lambda i, j: index)
    return arr, pl.BlockSpec(shape, imap, pipeline_mode=pl.Buffered(1))


def _consts(cs, n_grid=1):
    arrs, specs = zip(*[_const(c, n_grid) for c in cs])
    return list(arrs), list(specs)


def _load_cast(src_hbm, layer, dst_ref, stage_ref, sem_ref):
    n_slots, rows = stage_ref.shape[:2]
    n_chunks = dst_ref.shape[0] // rows

    def copy(c):
        return pltpu.make_async_copy(src_hbm.at[layer, pl.ds(c * rows, rows), :], stage_ref.at[c % n_slots],
                                     sem_ref.at[c % n_slots])

    for c in range(min(n_slots - 1, n_chunks)):
        copy(c).start()
    for c in range(n_chunks):
        copy(c).wait()
        dst_ref[pl.ds(c * rows, rows), :] = stage_ref[c % n_slots].astype(BF16)
        if c + n_slots - 1 < n_chunks:
            copy(c + n_slots - 1).start()


def _rows_kernel(*refs, has_mix, has_proj, has_sample, n_sub, layer):
    it = iter(refs)
    take = lambda n: tuple(next(it) for _ in range(n))
    (x_ref,) = take(1)
    if has_mix:
        ssm_ref, att_ref, wo_s_ref, wo_a_ref, gm_ref, bm_ref = take(6)
    wgu_hbm, wdn_hbm, g_ref, b_ref = take(4)
    wp_ref = take(1)[0] if has_proj else None
    if has_sample:
        (xs_ref,) = take(1)
        if has_mix:
            ssm_s_ref, att_s_ref, pack_ref = take(3)
        pad_ref = take(1)[0] if has_proj else None
    n_out = 4 if has_proj else 1
    outs = take(n_out)
    outs_s = take(n_out) if has_sample else None
    wgu_ref, wdn_ref, stage_gu, stage_dn, sem_gu, sem_dn = take(6)

    @pl.when(pl.program_id(0) == 0)
    def _():
        _load_cast(wgu_hbm, layer, wgu_ref, stage_gu, sem_gu)
        _load_cast(wdn_hbm, layer, wdn_ref, stage_dn, sem_dn)

    def mix_inputs(ssm_r, att_r, rows):
        ssm = jnp.concatenate([ssm_r[j, rows, :] for j in range(N_SLAB)], axis=1).astype(BF16)
        return ssm, att_r[rows, :].astype(BF16)

    def run(xs, mixes, wo_a, wp):
        if has_mix:
            ms = [jnp.dot(ssm, wo_s_ref[...], preferred_element_type=F32)
                  + jnp.dot(att, wo_a[...], preferred_element_type=F32) for ssm, att in mixes]
            xs = [_layer_norm(ALPHA * x + m, gm_ref[...], bm_ref[...]) for x, m in zip(xs, ms)]
        gus = [jnp.dot(x.astype(BF16), wgu_ref[...], preferred_element_type=F32) for x in xs]
        hs = [(gu[:, :D_FF] * jax.nn.sigmoid(gu[:, :D_FF]) * gu[:, D_FF:]).astype(BF16) for gu in gus]
        ys = [jnp.dot(h, wdn_ref[...], preferred_element_type=F32) for h in hs]
        xs = [_layer_norm(ALPHA * x + 0.5 * y, g_ref[...], b_ref[...]) for x, y in zip(xs, ys)]
        if has_proj:
            zs = [jnp.dot(x.astype(BF16), wp[...], preferred_element_type=F32) for x in xs]
        else:
            zs = [None] * len(xs)
        return xs, zs

    def store(out_refs, rows, x, z):
        out_refs[0][rows, :] = x
        if has_proj:
            u_ref, q_ref, kv_ref = out_refs[1:]
            q_cols = q_ref.shape[1]
            for j in range(N_SLAB):
                u_ref[j, rows, :] = z[:, j * LANES:(j + 1) * LANES]
            q_ref[rows, :] = z[:, SSM_WIDTH:SSM_WIDTH + q_cols].astype(q_ref.dtype)
            kv_ref[rows, :] = z[:, SSM_WIDTH + q_cols:]

    sub = x_ref.shape[0] // n_sub
    groups = [slice(r * sub, (r + 1) * sub) for r in range(n_sub)]
    mixes = [mix_inputs(ssm_ref, att_ref, rows) for rows in groups] if has_mix else None
    xs, zs = run([x_ref[rows, :] for rows in groups], mixes, wo_a_ref if has_mix else None, wp_ref)
    for rows, x, z in zip(groups, xs, zs):
        store(outs, rows, x, z)

    if has_sample:
        @pl.when(pl.program_id(0) == pl.num_programs(0) - 1)
        def _():
            every = slice(None)
            mixes_s = None
            if has_mix:
                ssm, att_padded = mix_inputs(ssm_s_ref, att_s_ref, every)
                att = jnp.dot(att_padded, pack_ref[...], preferred_element_type=F32).astype(BF16)
                mixes_s = [(ssm, att)]
            xs_s, zs_s = run([xs_ref[...]], mixes_s, wo_a_ref if has_mix else None, wp_ref)
            z = zs_s[0]
            if has_proj:
                q = z[:, SSM_WIDTH:SSM_WIDTH + ATTN_WIDTH].astype(BF16)
                z = jnp.concatenate([z[:, :SSM_WIDTH], jnp.dot(q, pad_ref[...], preferred_element_type=F32),
                                     z[:, SSM_WIDTH + ATTN_WIDTH:]], axis=1)
            store(outs_s, every, xs_s[0], z)


def _head_selectors():
    h = np.arange(ATTN_WIDTH) // HEAD_DIM
    d = np.arange(ATTN_WIDTH) % HEAD_DIM
    kvh, pr = h // Q_PER_KV, h % Q_PER_KV
    padded_col = h * LANES + kvh * HEAD_DIM + d
    packed_col = pr * LANES + kvh * HEAD_DIM + d
    pad = np.zeros((ATTN_WIDTH, Q_EXP), np.float32)
    pad[np.arange(ATTN_WIDTH), padded_col] = 1.0
    pack = np.zeros((Q_EXP, ATTN_WIDTH), np.float32)
    pack[padded_col, packed_col] = 1.0
    return jnp.asarray(pad, BF16), jnp.asarray(pack, BF16)


def _rows_call(x, ffn, ln, mix=None, proj=None, sample=None, *, tm, name):
    M = x.shape[0]
    assert M % tm == 0
    row = lambda w: pl.BlockSpec((tm, w), lambda i: (i, 0))
    slab = lambda n: pl.BlockSpec((n, tm, LANES), lambda i: (0, i, 0))
    whole = lambda a: pl.BlockSpec(a.shape, lambda i: (0,) * len(a.shape))
    args = [x]
    specs = [row(D_MODEL)]
    if mix is not None:
        ssm_n, att_n = mix[:2]
        arrs, cspecs = _consts(mix[2:])
        args += [ssm_n, att_n] + arrs
        specs += [slab(N_SLAB), row(att_n.shape[1])] + cspecs
    w_gu, w_dn, layer = ffn
    arrs, cspecs = _consts(list(ln))
    args += [w_gu, w_dn] + arrs
    specs += [pl.BlockSpec(memory_space=pl.ANY), pl.BlockSpec(memory_space=pl.ANY)] + cspecs

    def proj_outputs(rows, q_cols, q_dtype):
        return [jax.ShapeDtypeStruct((N_SLAB, rows, LANES), F32), jax.ShapeDtypeStruct((rows, q_cols), q_dtype),
                jax.ShapeDtypeStruct((rows, 2 * KV_WIDTH), F32)]

    out_shape = [jax.ShapeDtypeStruct((M, D_MODEL), F32)]
    out_specs = [row(D_MODEL)]
    if proj is not None:
        arr, cspec = _const(proj)
        args.append(arr)
        specs.append(cspec)
        q_cols = arr.shape[-1] - SSM_WIDTH - 2 * KV_WIDTH
        out_shape += proj_outputs(M, q_cols, BF16)
        out_specs += [slab(N_SLAB), row(q_cols), row(2 * KV_WIDTH)]
    n_main = len(out_shape)
    if sample is not None:
        xs = sample['x']
        args.append(xs)
        specs.append(whole(xs))
        pad, pack = _head_selectors()
        if mix is not None:
            ssm_s, att_s = sample['mix']
            arr, cspec = _const(pack)
            args += [ssm_s, att_s, arr]
            specs += [whole(ssm_s), whole(att_s), cspec]
        sample_out = [jax.ShapeDtypeStruct(xs.shape, F32)]
        if proj is not None:
            arr, cspec = _const(pad)
            args.append(arr)
            specs.append(cspec)
            sample_out += proj_outputs(xs.shape[0], Q_EXP, F32)
        out_shape += sample_out
        out_specs += [whole(o) for o in sample_out]
    outs = pl.pallas_call(
        functools.partial(_rows_kernel, has_mix=mix is not None, has_proj=proj is not None,
                          has_sample=sample is not None, n_sub=max(1, tm // ROW_GROUP), layer=layer),
        grid=(M // tm,),
        in_specs=specs,
        out_specs=out_specs,
        out_shape=out_shape,
        scratch_shapes=[pltpu.VMEM(w_gu.shape[1:], BF16), pltpu.VMEM(w_dn.shape[1:], BF16),
                        pltpu.VMEM((W_STAGE_SLOTS, W_GU_STAGE_ROWS, w_gu.shape[2]), F32),
                        pltpu.VMEM((W_STAGE_SLOTS, W_DN_STAGE_ROWS, w_dn.shape[2]), F32),
                        pltpu.SemaphoreType.DMA((W_STAGE_SLOTS,)), pltpu.SemaphoreType.DMA((W_STAGE_SLOTS,))],
        compiler_params=pltpu.CompilerParams(dimension_semantics=("arbitrary",),
                                             vmem_limit_bytes=VMEM_LIMIT_BYTES),
        name=name,
    )(*args)
    return outs[:n_main], (outs[n_main:] if sample is not None else None)


def _cmul(xr, xi, yr, yi):
    return xr * yr - xi * yi, xr * yi + xi * yr


def _ssm_tables_kernel(p_ref, af_ref, wst_ref, cat_ref, strip_ref, ak_ref, tr_ref):
    t1 = SSM_CHUNK
    a_re, a_im, lam_re, lam_im, b_re, b_im, c_re, c_im = (p_ref[k] for k in range(8))
    num_re, num_im = a_re - 1.0, a_im
    den = lam_re * lam_re + lam_im * lam_im
    f_re = (num_re * lam_re + num_im * lam_im) / den
    f_im = (num_im * lam_re - num_re * lam_im) / den
    bb_re, bb_im = _cmul(f_re, f_im, b_re, b_im)

    def same_group(shape, row_div, col_div):
        r = lax.broadcasted_iota(jnp.int32, shape, 0) // row_div
        c = lax.broadcasted_iota(jnp.int32, shape, 1) // col_div
        return (r == c).astype(F32)

    m_state = same_group((LANES, SLAB_STATES), SSM_GROUP, SSM_STATE)
    m_chan = 0.5 * same_group((LANES, LANES), SSM_GROUP, SSM_GROUP)

    def expand(x):
        return (jnp.concatenate([x] * (SLAB_STATES // LANES), axis=1) * m_state).astype(BF16)

    def dot_nt(a, b):
        return lax.dot_general(a, b, (((1,), (1,)), ((), ())), precision=lax.Precision.HIGHEST,
                               preferred_element_type=F32)

    pw_re, pw_im = jnp.ones_like(a_re), jnp.zeros_like(a_re)
    kd = []
    for l in range(t1):
        ab_re, ab_im = _cmul(pw_re, pw_im, bb_re, bb_im)
        s = t1 - 1 - l
        kd_l = []
        for j in range(N_SLAB):
            rows = slice(j * LANES, (j + 1) * LANES)
            wst_ref[j, s * LANES:(s + 1) * LANES, :SLAB_STATES] = expand(ab_re[rows])
            wst_ref[j, s * LANES:(s + 1) * LANES, SLAB_STATES:] = expand(ab_im[rows])
            k = dot_nt(ab_re[rows], c_re[rows]) - dot_nt(ab_im[rows], c_im[rows])
            kd_l.append((k * m_chan).astype(BF16))
        kd.append(kd_l)
        pw_re, pw_im = _cmul(pw_re, pw_im, a_re, a_im)
        cf_re = c_re * pw_re - c_im * pw_im
        cf_im = -(c_re * pw_im + c_im * pw_re)
        for j in range(N_SLAB):
            rows = slice(j * LANES, (j + 1) * LANES)
            cat_ref[j, l * LANES:(l + 1) * LANES, :SLAB_STATES] = expand(cf_re[rows])
            cat_ref[j, l * LANES:(l + 1) * LANES, SLAB_STATES:] = expand(cf_im[rows])
    for j in range(N_SLAB):
        for rho in range(t1):
            for c in range(2):
                lag = t1 - 2 - rho + c
                blk = kd[lag][j] if lag >= 0 else jnp.zeros((LANES, LANES), BF16)
                strip_ref[j, rho * LANES:(rho + 1) * LANES, c * LANES:(c + 1) * LANES] = blk

    f_re, f_im = af_ref[0], af_ref[1]
    base_re, base_im = f_re, f_im
    for _ in range(t1 - 1):
        base_re, base_im = _cmul(base_re, base_im, f_re, f_im)
    row = lax.broadcasted_iota(jnp.int32, (SUBLANES, N_STATES), 0)
    pws = [(base_re, base_im)]
    for _ in range(SUBLANES - 1):
        pws.append(_cmul(pws[-1][0], pws[-1][1], base_re, base_im))
    for n, k in enumerate((1, 2, 4)):
        for part in range(2):
            ak_ref[2 * n + part] = jnp.where(row >= k, jnp.broadcast_to(pws[k - 1][part], row.shape), 0.0)
    for part in range(2):
        acc = jnp.zeros(row.shape, F32)
        for r in range(SUBLANES):
            acc = jnp.where(row == r, jnp.broadcast_to(pws[r][part], row.shape), acc)
        tr_ref[part] = acc


def _ssm_tables_call(params, a_flat):
    t1 = SSM_CHUNK
    shapes = [((N_SLAB, t1 * LANES, 2 * SLAB_STATES), BF16),
              ((N_SLAB, t1 * LANES, 2 * SLAB_STATES), BF16),
              ((N_SLAB, t1 * LANES, 2 * LANES), BF16),
              ((6, SUBLANES, N_STATES), F32),
              ((2, SUBLANES, N_STATES), F32)]
    per_layer = lambda shape: pl.BlockSpec((None,) + tuple(shape), lambda l: (l,) + (0,) * len(shape))
    return pl.pallas_call(
        _ssm_tables_kernel,
        grid=(DEPTH,),
        in_specs=[per_layer(params.shape[1:]), per_layer(a_flat.shape[1:])],
        out_specs=[per_layer(s) for s, _ in shapes],
        out_shape=[jax.ShapeDtypeStruct((DEPTH,) + s, d) for s, d in shapes],
        compiler_params=pltpu.CompilerParams(dimension_semantics=("arbitrary",),
                                             vmem_limit_bytes=VMEM_LIMIT_BYTES),
        name="ssm_tables",
    )(params, a_flat)


def _ssm_table_inputs(lam_re, lam_im, log_dt, b_re, b_im, c_re, c_im):
    dt = jnp.exp(log_dt)[..., None]
    mag = jnp.exp(lam_re * dt)
    a_re = mag * jnp.cos(lam_im * dt)
    a_im = mag * jnp.sin(lam_im * dt)
    rep = lambda x: jnp.repeat(x, SSM_GROUP, axis=1)
    flat = lambda x: x.reshape(DEPTH, SSM_WIDTH, SSM_STATE)
    params = jnp.stack([rep(a_re), rep(a_im), rep(lam_re), rep(lam_im),
                        flat(jnp.swapaxes(b_re, 2, 3)), flat(jnp.swapaxes(b_im, 2, 3)), flat(c_re), flat(c_im)],
                       axis=1)
    params = jnp.concatenate([params, params], axis=-1)
    a_flat = jnp.stack([a_re.reshape(DEPTH, 1, N_STATES), a_im.reshape(DEPTH, 1, N_STATES)], axis=1)
    return params, a_flat


def _glu_rms(y, gluw_ref, glub_ref, g_ref):
    g = jax.nn.gelu(y).astype(BF16)
    zz = jnp.dot(g, gluw_ref[...], preferred_element_type=F32) + glub_ref[...]
    o = zz[:, :SSM_WIDTH] * jax.nn.sigmoid(zz[:, SSM_WIDTH:])
    return _rms_norm(o, g_ref[...])


def _ssm_kernel(u_ref, wst_ref, cat_ref, strip_ref, ak_ref, tr_ref, d_ref, gluw_ref, glub_ref, g_ref,
                o_ref, hre_ref, him_ref,
                ucb_ref, sre_ref, sim_ref, hcr_ref, hci_ref):
    t1 = SSM_CHUNK
    nc = SSM_ROWS // t1
    i = pl.program_id(1)

    @pl.when(i == 0)
    def _():
        hcr_ref[...] = jnp.zeros_like(hcr_ref)
        hci_ref[...] = jnp.zeros_like(hci_ref)

    for j in range(N_SLAB):
        for s in range(t1):
            ucb_ref[j, :, s * LANES:(s + 1) * LANES] = u_ref[j, pl.ds(s, nc, stride=t1), :].astype(BF16)

    for j in range(N_SLAB):
        st = jnp.dot(ucb_ref[j], wst_ref[j], preferred_element_type=F32)
        sre_ref[:, j * SLAB_STATES:(j + 1) * SLAB_STATES] = st[:, :SLAB_STATES]
        sim_ref[:, j * SLAB_STATES:(j + 1) * SLAB_STATES] = st[:, SLAB_STATES:]

    row = lax.broadcasted_iota(jnp.int32, (SUBLANES, SLAB_STATES), 0)
    for j in range(N_SLAB):
        cols = pl.ds(j * SLAB_STATES, SLAB_STATES)

        def body(gi, carry, cols=cols):
            hr, hi = carry
            r0 = pl.multiple_of(gi * SUBLANES, SUBLANES)
            xr = sre_ref[pl.ds(r0, SUBLANES), cols]
            xi = sim_ref[pl.ds(r0, SUBLANES), cols]
            for n, k in enumerate((1, 2, 4)):
                akr = ak_ref[2 * n, :, cols]
                aki = ak_ref[2 * n + 1, :, cols]
                sr = pltpu.roll(xr, k, 0)
                si = pltpu.roll(xi, k, 0)
                xr, xi = xr + akr * sr - aki * si, xi + akr * si + aki * sr
            tr = tr_ref[0, :, cols]
            ti = tr_ref[1, :, cols]
            er = xr + tr * hr - ti * hi
            ei = xi + tr * hi + ti * hr
            sre_ref[pl.ds(r0, SUBLANES), cols] = jnp.where(row >= 1, pltpu.roll(er, 1, 0), hr)
            sim_ref[pl.ds(r0, SUBLANES), cols] = jnp.where(row >= 1, pltpu.roll(ei, 1, 0), hi)
            last = SUBLANES - 1
            return (jnp.broadcast_to(er[last:last + 1], er.shape), jnp.broadcast_to(ei[last:last + 1], ei.shape))

        hr, hi = lax.fori_loop(0, nc // SUBLANES, body, (hcr_ref[:, cols], hci_ref[:, cols]), unroll=True)
        hcr_ref[:, cols] = hr
        hci_ref[:, cols] = hi

    hre_ref[0] = hcr_ref[...]
    him_ref[0] = hci_ref[...]

    ys = []
    for j in range(N_SLAB):
        cols = slice(j * SLAB_STATES, (j + 1) * SLAB_STATES)
        hprev = jnp.concatenate([sre_ref[:, cols], sim_ref[:, cols]], axis=1).astype(BF16)
        y = lax.dot_general(hprev, cat_ref[j], (((1,), (1,)), ((), ())), preferred_element_type=F32)
        parts = []
        for p2 in range(t1 // 2):
            kk = (2 * p2 + 2) * LANES
            yi = jnp.dot(ucb_ref[j, :, :kk], strip_ref[j, (t1 - 2 - 2 * p2) * LANES:, :],
                         preferred_element_type=F32)
            parts.append(y[:, 2 * p2 * LANES:(2 * p2 + 2) * LANES] + yi)
        ys.append(jnp.concatenate(parts, axis=1))

    for t in range(t1):
        y = jnp.concatenate([ys[j][:, t * LANES:(t + 1) * LANES] + d_ref[j] * u_ref[j, pl.ds(t, nc, stride=t1), :]
                             for j in range(N_SLAB)], axis=1)
        n = _glu_rms(y, gluw_ref, glub_ref, g_ref)
        for j in range(N_SLAB):
            o_ref[j, pl.ds(t, nc, stride=t1), :] = n[:, j * LANES:(j + 1) * LANES]


def _ssm_prompt_call(u, tab, d_t, gluw, glub, g, *, batch, seq, name):
    t1 = SSM_CHUNK
    nc = SSM_ROWS // t1
    steps = seq // SSM_ROWS
    blk = pl.BlockSpec((N_SLAB, SSM_ROWS, LANES), lambda b, i: (0, b * steps + i, 0))
    st = pl.BlockSpec((1, SUBLANES, N_STATES), lambda b, i: (b, 0, 0))
    consts, cspecs = _consts(list(tab) + [d_t, gluw, glub, g], 2)
    out, hre, him = pl.pallas_call(
        _ssm_kernel,
        grid=(batch, steps),
        in_specs=[blk] + cspecs,
        out_specs=[blk, st, st],
        out_shape=[jax.ShapeDtypeStruct(u.shape, F32),
                   jax.ShapeDtypeStruct((batch, SUBLANES, N_STATES), F32),
                   jax.ShapeDtypeStruct((batch, SUBLANES, N_STATES), F32)],
        scratch_shapes=[pltpu.VMEM((N_SLAB, nc, t1 * LANES), BF16),
                        pltpu.VMEM((nc, N_STATES), F32),
                        pltpu.VMEM((nc, N_STATES), F32),
                        pltpu.VMEM((SUBLANES, N_STATES), F32),
                        pltpu.VMEM((SUBLANES, N_STATES), F32)],
        compiler_params=pltpu.CompilerParams(dimension_semantics=("arbitrary", "arbitrary"),
                                             vmem_limit_bytes=VMEM_LIMIT_BYTES),
        name=name,
    )(u, *consts)
    return out, hre[:, 0], him[:, 0]


def _ssm_sample_kernel(u_ref, h0r_ref, h0i_ref, wst_ref, cat_ref, kd_ref, a_ref, d_ref, gluw_ref, glub_ref, g_ref,
                       o_ref, hr_ref, hi_ref):
    ys = []
    for j in range(N_SLAB):
        cols = slice(j * SLAB_STATES, (j + 1) * SLAB_STATES)
        uf = u_ref[j]
        ub = uf.astype(BF16)
        st = jnp.dot(ub, wst_ref[j], preferred_element_type=F32)
        h0r = h0r_ref[:, cols]
        h0i = h0i_ref[:, cols]
        ar = a_ref[0, :, cols]
        ai = a_ref[1, :, cols]
        hr_ref[:, cols] = ar * h0r - ai * h0i + st[:, :SLAB_STATES]
        hi_ref[:, cols] = ar * h0i + ai * h0r + st[:, SLAB_STATES:]
        hcat = jnp.concatenate([h0r, h0i], axis=1).astype(BF16)
        y = lax.dot_general(hcat, cat_ref[j], (((1,), (1,)), ((), ())), preferred_element_type=F32)
        y = y + jnp.dot(ub, kd_ref[j], preferred_element_type=F32)
        ys.append(y + d_ref[j] * uf)
    n = _glu_rms(jnp.concatenate(ys, axis=1), gluw_ref, glub_ref, g_ref)
    for j in range(N_SLAB):
        o_ref[j] = n[:, j * LANES:(j + 1) * LANES]


def _ssm_sample_call(u, h0r, h0i, tab, a_flat, d_t, gluw, glub, g, *, name):
    n_seq = u.shape[1]
    (wst, layer), (cat, _), (strip, _) = tab[:3]
    last = SSM_CHUNK - 1
    whole = lambda a: pl.BlockSpec(a.shape, lambda i: (0,) * len(a.shape))
    outs = [jax.ShapeDtypeStruct(u.shape, F32),
            jax.ShapeDtypeStruct((n_seq, N_STATES), F32),
            jax.ShapeDtypeStruct((n_seq, N_STATES), F32)]
    consts, cspecs = _consts([a_flat, d_t, gluw, glub, g])
    return pl.pallas_call(
        _ssm_sample_kernel,
        grid=(1,),
        in_specs=[whole(u), whole(h0r), whole(h0i),
                  pl.BlockSpec((None, N_SLAB, LANES, 2 * SLAB_STATES), lambda i: (layer, 0, last, 0)),
                  pl.BlockSpec((None, N_SLAB, LANES, 2 * SLAB_STATES), lambda i: (layer, 0, 0, 0)),
                  pl.BlockSpec((None, N_SLAB, LANES, LANES), lambda i: (layer, 0, last, 1))] + cspecs,
        out_specs=[whole(o) for o in outs],
        out_shape=outs,
        compiler_params=pltpu.CompilerParams(dimension_semantics=("arbitrary",),
                                             vmem_limit_bytes=VMEM_LIMIT_BYTES),
        name=name,
    )(u, h0r, h0i, wst, cat, strip, *consts)


def _attn_kernel(sink_ref, q_ref, kvc_ref, kvp_ref, g_ref, o_ref):
    i = pl.program_id(1)
    kj = lax.broadcasted_iota(jnp.int32, (2 * WINDOW, WINDOW), 0)
    qi = lax.broadcasted_iota(jnp.int32, (2 * WINDOW, WINDOW), 1)
    band = (kj >= qi) & (kj <= qi + WINDOW)
    low = qi < HEAD_DIM
    for sb in range(ATTN_BLOCKS):
        blk = slice(sb * WINDOW, (sb + 1) * WINDOW)
        kc = kvc_ref[blk, :]
        if sb == 0:
            kp = kvp_ref[...]
            valid = band & ((i > 0) | (kj >= WINDOW))
        else:
            kp = kvc_ref[(sb - 1) * WINDOW:sb * WINDOW, :]
            valid = band
        valid2 = jnp.concatenate([valid, valid], axis=1)
        kcat = jnp.concatenate([kp[:, :KV_WIDTH], kc[:, :KV_WIDTH]], axis=0) * ATTN_SCALE
        swap = pltpu.roll(kcat, HEAD_DIM, 1)
        zero = jnp.zeros_like(kcat)
        k_sel = [[jnp.where(low, kcat, zero).astype(BF16), jnp.where(low, zero, swap).astype(BF16)],
                 [jnp.where(low, swap, zero).astype(BF16), jnp.where(low, zero, kcat).astype(BF16)]]
        v_t = jnp.concatenate([kp[:, KV_WIDTH:], kc[:, KV_WIDTH:]], axis=0).T.astype(BF16)
        parts = []
        for kvh in range(N_KV_HEADS):
            q_pair = jnp.concatenate([q_ref[blk, (2 * kvh) * LANES:(2 * kvh + 1) * LANES],
                                      q_ref[blk, (2 * kvh + 1) * LANES:(2 * kvh + 2) * LANES]], axis=0)
            for parity in range(2):
                s_t = lax.dot_general(k_sel[kvh][parity], q_pair, (((1,), (1,)), ((), ())),
                                      preferred_element_type=F32)
                parts.append(jnp.where(valid2, s_t, MASKED))
        s_all = jnp.concatenate(parts, axis=1)
        sink = sink_ref[...]
        m = jnp.maximum(jnp.max(s_all, axis=0, keepdims=True), sink)
        p_all = jnp.exp(s_all - m)
        inv_den = 1.0 / (jnp.sum(p_all, axis=0, keepdims=True) + jnp.exp(sink - m))
        p_all = p_all.astype(BF16)
        half = Q_PER_KV * WINDOW
        o_kv = [jnp.dot(v_t[kvh * HEAD_DIM:(kvh + 1) * HEAD_DIM], p_all[:, kvh * half:(kvh + 1) * half],
                        preferred_element_type=F32) * inv_den[:, kvh * half:(kvh + 1) * half]
                for kvh in range(N_KV_HEADS)]
        tiles = []
        for pr in range(Q_PER_KV):
            lanes = slice(ATTN_HEAD_ORDER.index(pr) * WINDOW, (ATTN_HEAD_ORDER.index(pr) + 1) * WINDOW)
            tiles.append(jnp.concatenate([o_kv[0][:, lanes], o_kv[1][:, lanes]], axis=0))
        ss = None
        for tile in tiles:
            t = jnp.sum(tile * tile, axis=0, keepdims=True)
            ss = t if ss is None else ss + t
        inv = lax.rsqrt(ss / ATTN_WIDTH + RMS_EPS)
        for pr, tile in enumerate(tiles):
            o_ref[blk, pr * LANES:(pr + 1) * LANES] = ((tile * inv).T * g_ref[pr]).astype(o_ref.dtype)


def _attn_prompt_call(q, kv, sinks, g_perm, *, batch, seq, name):
    rows = ATTN_BLOCKS * WINDOW
    nb = seq // rows
    M = kv.shape[0]
    cur = lambda b, i: (b * nb + i, 0)
    prev = lambda b, i: ((b * nb + i) * ATTN_BLOCKS - jnp.minimum(i, 1), 0)
    (sinks, g_perm), (sink_spec, g_spec) = _consts([sinks, g_perm], 2)
    return pl.pallas_call(
        _attn_kernel,
        grid=(batch, nb),
        in_specs=[sink_spec,
                  pl.BlockSpec((rows, ATTN_WIDTH), cur),
                  pl.BlockSpec((rows, 2 * KV_WIDTH), cur),
                  pl.BlockSpec((WINDOW, 2 * KV_WIDTH), prev),
                  g_spec],
        out_specs=pl.BlockSpec((rows, ATTN_WIDTH), cur),
        out_shape=jax.ShapeDtypeStruct((M, ATTN_WIDTH), BF16),
        compiler_params=pltpu.CompilerParams(dimension_semantics=("arbitrary", "arbitrary"),
                                             vmem_limit_bytes=VMEM_LIMIT_BYTES),
        name=name,
    )(sinks, q, kv, kv, g_perm)


SEQ_PER_STEP = LANES // N_HEADS


def _attn_sample_kernel(q_ref, kv_ref, ckt_ref, cvt_ref, sink_ref, g_ref, own_ref, o_ref, kot_ref, vot_ref):
    nrow = SEQ_PER_STEP * N_HEADS
    qb = q_ref[...].astype(BF16)
    k_new = kv_ref[:, :KV_WIDTH]
    v_new = kv_ref[:, KV_WIDTH:]
    row_seq = lax.broadcasted_iota(jnp.int32, (nrow, LANES), 0) // N_HEADS
    lane_seq = lax.broadcasted_iota(jnp.int32, (nrow, LANES), 1) // N_HEADS
    lane = lax.broadcasted_iota(jnp.int32, (nrow, LANES), 1)
    pick = (lax.broadcasted_iota(jnp.int32, (nrow, SEQ_PER_STEP), 0) // N_HEADS
            == lax.broadcasted_iota(jnp.int32, (nrow, SEQ_PER_STEP), 1)).astype(BF16)

    qf = q_ref[...]
    heads_of = lambda a, n: a[n * N_HEADS:(n + 1) * N_HEADS]
    s = jnp.concatenate([jnp.dot(heads_of(qf, n).astype(BF16), ckt_ref[n].astype(BF16), preferred_element_type=F32)
                         for n in range(SEQ_PER_STEP)], axis=0) * ATTN_SCALE
    k_rows = jnp.dot(pick, k_new.astype(BF16), preferred_element_type=F32)
    s_new = jnp.sum(qb.astype(F32) * k_rows, axis=-1, keepdims=True) * ATTN_SCALE
    sink = sink_ref[...]
    m = jnp.maximum(jnp.maximum(jnp.max(s, axis=-1, keepdims=True), s_new), sink)
    p = jnp.exp(s - m)
    p_new = jnp.exp(s_new - m)
    inv = 1.0 / (jnp.sum(p, axis=-1, keepdims=True) + p_new + jnp.exp(sink - m))
    pn = p * inv
    o = jnp.concatenate([lax.dot_general(heads_of(pn, n).astype(BF16), cvt_ref[n].astype(BF16),
                                         (((1,), (1,)), ((), ())), preferred_element_type=F32)
                         for n in range(SEQ_PER_STEP)], axis=0)
    v_rows = jnp.dot(pick, v_new.astype(BF16), preferred_element_type=F32)
    o = o + (p_new * inv).astype(BF16).astype(F32) * v_rows

    hi = lax.Precision.HIGHEST
    row_sums = jnp.dot(o * o * own_ref[...], jnp.ones((LANES, LANES), F32), precision=hi, preferred_element_type=F32)
    ss = jnp.dot((row_seq == lane_seq).astype(F32), row_sums, precision=hi, preferred_element_type=F32)
    o_ref[...] = o * lax.rsqrt(ss / ATTN_WIDTH + RMS_EPS) * g_ref[...]

    pad = jnp.zeros((LANES - SEQ_PER_STEP, KV_WIDTH), F32)
    k_cols = jnp.concatenate([k_new, pad], axis=0).T
    v_cols = jnp.concatenate([v_new, pad], axis=0).T
    last = lane == WINDOW - 1
    for n in range(SEQ_PER_STEP):
        kot_ref[n] = jnp.where(last, pltpu.roll(k_cols, WINDOW - 1 - n, 1), pltpu.roll(ckt_ref[n], WINDOW - 1, 1))
        vot_ref[n] = jnp.where(last, pltpu.roll(v_cols, WINDOW - 1 - n, 1), pltpu.roll(cvt_ref[n], WINDOW - 1, 1))


def _attn_sample_call(q, kv, ck, cv, layer, sink_row, g_rows, own_rows, *, name):
    n_seq = kv.shape[0]
    assert n_seq % SEQ_PER_STEP == 0
    nrow = SEQ_PER_STEP * N_HEADS
    cblk = pl.BlockSpec((SEQ_PER_STEP, WINDOW, 2 * HEAD_DIM), lambda i: (i, 0, 0))
    cin = pl.BlockSpec((None, SEQ_PER_STEP, WINDOW, 2 * HEAD_DIM), lambda i: (layer, i, 0, 0))
    qblk = pl.BlockSpec((nrow, LANES), lambda i: (i, 0))
    consts, cspecs = _consts([sink_row, g_rows, own_rows])
    return pl.pallas_call(
        _attn_sample_kernel,
        grid=(n_seq // SEQ_PER_STEP,),
        in_specs=[qblk, pl.BlockSpec((SEQ_PER_STEP, 2 * KV_WIDTH), lambda i: (i, 0)), cin, cin] + cspecs,
        out_specs=[qblk, cblk, cblk],
        out_shape=[jax.ShapeDtypeStruct(q.shape, F32),
                   jax.ShapeDtypeStruct(ck.shape[1:], F32),
                   jax.ShapeDtypeStruct(cv.shape[1:], F32)],
        compiler_params=pltpu.CompilerParams(dimension_semantics=("arbitrary",),
                                             vmem_limit_bytes=VMEM_LIMIT_BYTES),
        name=name,
    )(q, kv, ck, cv, *consts)


def _expand_heads(a):
    lead = a.shape[:-1]
    a = a.reshape(lead + (N_KV_HEADS, Q_PER_KV, 1, HEAD_DIM))
    sel = jnp.eye(N_KV_HEADS, dtype=a.dtype).reshape(N_KV_HEADS, 1, N_KV_HEADS, 1)
    return (a * sel).reshape(lead + (Q_EXP,))


def _pair_heads(a):
    lead = a.shape[:-1]
    a = a.reshape(lead + (N_KV_HEADS, Q_PER_KV, HEAD_DIM))
    return jnp.swapaxes(a, -3, -2).reshape(lead + (ATTN_WIDTH,))


def _prep_weights(ln_g, ln_b, ffn1_w_in, ffn1_w_out, ffn2_w_in, ffn2_w_out, w_in, ssm_lam_re, ssm_lam_im, ssm_log_dt,
                  ssm_b_re, ssm_b_im, ssm_c_re, ssm_c_im, ssm_d, glu_w, glu_b, attn_sinks, g_ssm_out, g_attn_out,
                  w_out):
    rows_t = lambda a: jnp.swapaxes(a, 1, 2)
    params, a_flat = _ssm_table_inputs(ssm_lam_re, ssm_lam_im, ssm_log_dt, ssm_b_re, ssm_b_im, ssm_c_re, ssm_c_im)
    tables = _ssm_tables_call(params, a_flat)
    head_order = jnp.array([Q_PER_KV * kvh + o for kvh in range(N_KV_HEADS) for o in ATTN_HEAD_ORDER])
    g_exp = _expand_heads(g_attn_out).reshape(DEPTH, N_HEADS, LANES)
    w_in_b = w_in.astype(BF16)
    w_out_b = w_out.astype(BF16)
    stacked = dict(
        w_in=w_in_b,
        wo_s=w_out_b[:, :SSM_WIDTH],
        wo_a=rows_t(_pair_heads(rows_t(w_out_b[:, SSM_WIDTH:]))),
        a_flat=a_flat,
        d_1=ssm_d.reshape(DEPTH, N_SLAB, 1, LANES),
        gluw=glu_w.astype(BF16),
        glub=glu_b.reshape(DEPTH, 1, 2 * SSM_WIDTH),
        g_ssm=g_ssm_out.reshape(DEPTH, 1, SSM_WIDTH),
        sinks=jnp.repeat(attn_sinks[:, head_order], WINDOW, axis=1)[:, None, :],
        g_pair=_pair_heads(g_attn_out).reshape(DEPTH, Q_PER_KV, 1, LANES),
        sink_rows=jnp.tile(attn_sinks, (1, SEQ_PER_STEP))[:, :, None],
        g_rows=jnp.tile(g_exp, (1, SEQ_PER_STEP, 1)),
    )
    w = {name: [(arr, l) for l in range(DEPTH)] for name, arr in stacked.items()}
    w['ffn1'] = [(ffn1_w_in, ffn1_w_out, l) for l in range(DEPTH)]
    w['ffn2'] = [(ffn2_w_in, ffn2_w_out, l) for l in range(DEPTH)]
    w['tab'] = [tuple((t, l) for t in tables) for l in range(DEPTH)]
    n_ln = ln_g.shape[1]
    ln_g3 = ln_g.reshape(DEPTH * n_ln, 1, D_MODEL)
    ln_b3 = ln_b.reshape(DEPTH * n_ln, 1, D_MODEL)
    w['ln'] = [[((ln_g3, l * n_ln + i), (ln_b3, l * n_ln + i)) for i in range(n_ln)] for l in range(DEPTH)]
    w['own_rows'] = jnp.tile(_expand_heads(jnp.ones((ATTN_WIDTH,), F32)).reshape(N_HEADS, LANES), (SEQ_PER_STEP, 1))
    return w


def _prompt_mixer(u, q, kv, l, w, batch, seq):
    ssm_n, hre, him = _ssm_prompt_call(u, w['tab'][l], w['d_1'][l], w['gluw'][l],
                                       w['glub'][l], w['g_ssm'][l], batch=batch, seq=seq, name=f"p_ssm_{l}")
    att_n = _attn_prompt_call(q, kv, w['sinks'][l], w['g_pair'][l], batch=batch, seq=seq, name=f"p_attn_{l}")
    kvw = kv.reshape(batch, seq, 2 * KV_WIDTH)[:, -WINDOW:].reshape(batch, WINDOW, 2, N_KV_HEADS, HEAD_DIM)
    return (ssm_n, att_n, hre.reshape(batch, N_SSM_GROUPS, SSM_STATE), him.reshape(batch, N_SSM_GROUPS, SSM_STATE),
            kvw[:, :, 0], kvw[:, :, 1])


def _sample_mixer(u, q, kv, l, w, h0_re, h0_im, k_buf, v_buf):
    n_seq = kv.shape[0]
    ssm_n, hre, him = _ssm_sample_call(u, h0_re.reshape(n_seq, N_STATES), h0_im.reshape(n_seq, N_STATES),
                                       w['tab'][l], w['a_flat'][l], w['d_1'][l], w['gluw'][l], w['glub'][l], w['g_ssm'][l],
                                       name=f"s_ssm_{l}")
    att, kn, vn = _attn_sample_call(q.reshape(n_seq * N_HEADS, LANES), kv, k_buf, v_buf, l,
                                    w['sink_rows'][l], w['g_rows'][l], w['own_rows'], name=f"s_attn_{l}")
    untranspose = lambda t: jnp.transpose(t.reshape(n_seq, N_KV_HEADS, HEAD_DIM, WINDOW), (0, 3, 1, 2))
    return (ssm_n, att.reshape(n_seq, Q_EXP), hre.reshape(n_seq, N_SSM_GROUPS, SSM_STATE),
            him.reshape(n_seq, N_SSM_GROUPS, SSM_STATE), untranspose(kn), untranspose(vn))


def _trunks(x_prompt, x_sample, h0_re, h0_im, k_buf, v_buf, w, *, tm):
    Bn, L, _ = x_prompt.shape
    n_seq = x_sample.shape[0]
    xp = x_prompt.reshape(Bn * L, D_MODEL)
    xs = x_sample.reshape(n_seq, D_MODEL)
    outs_p, outs_s = [], []
    mix_p = mix_s = None
    for l in range(DEPTH + 1):
        if l > 0:
            (xp,), (xs,) = _rows_call(xp, w['ffn2'][l - 1], w['ln'][l - 1][2], mix=mix_p,
                                      sample=dict(x=xs, mix=mix_s), tm=tm, name=f"mix_ffn2_{l - 1}")
        if l == DEPTH:
            break
        (xp, u, q, kv), (xs, us, qs, kvs) = _rows_call(xp, w['ffn1'][l], w['ln'][l][0], proj=w['w_in'][l],
                                                       sample=dict(x=xs), tm=tm,
                                                       name=f"ffn1_{l}")
        ssm_p, att_p, *state_p = _prompt_mixer(u, q, kv, l, w, Bn, L)
        ssm_s, att_s, *state_s = _sample_mixer(us, qs, kvs, l, w, h0_re[l], h0_im[l], k_buf, v_buf)
        mix_p = (ssm_p, att_p, w['wo_s'][l], w['wo_a'][l], w['ln'][l][1][0], w['ln'][l][1][1])
        mix_s = (ssm_s, att_s)
        outs_p.append(state_p)
        outs_s.append(state_s)
    stack = lambda outs: tuple(jnp.stack([o[i] for o in outs]) for i in (2, 3, 0, 1))
    return (xp.reshape(Bn, L, D_MODEL), xs.reshape(n_seq, 1, D_MODEL)), stack(outs_p), stack(outs_s)


def kernel(x_prompt, x_sample, cache_k_win, cache_v_win, state_ssm_re, state_ssm_im, ln_g, ln_b, ffn1_w_in, ffn1_w_out, ffn2_w_in, ffn2_w_out, w_in, ssm_lam_re, ssm_lam_im, ssm_log_dt, ssm_b_re, ssm_b_im, ssm_c_re, ssm_c_im, ssm_d, glu_w, glu_b, attn_sinks, g_ssm_out, g_attn_out, w_out):
    w = _prep_weights(ln_g, ln_b, ffn1_w_in, ffn1_w_out, ffn2_w_in, ffn2_w_out, w_in, ssm_lam_re, ssm_lam_im,
                      ssm_log_dt, ssm_b_re, ssm_b_im, ssm_c_re, ssm_c_im, ssm_d, glu_w, glu_b, attn_sinks,
                      g_ssm_out, g_attn_out, w_out)
    n_seq = x_sample.shape[0]
    transposed = lambda c: jnp.transpose(c, (0, 1, 3, 4, 2)).reshape(DEPTH, n_seq, N_KV_HEADS * HEAD_DIM, WINDOW)
    (y_prompt, y_sample), (kp, vp, hrp, hip), (ks_, vs_, hrs, his) = _trunks(
        x_prompt, x_sample, state_ssm_re, state_ssm_im, transposed(cache_k_win), transposed(cache_v_win), w, tm=512)
    return (y_prompt, y_sample, kp, vp, hrp, hip, ks_, vs_, hrs, his)
```

```python
import functools

import jax
import jax.numpy as jnp
import numpy as np
from jax import lax
from jax.experimental import pallas as pl
from jax.experimental.pallas import tpu as pltpu

D_MODEL = 1024
DEPTH = 2
SSM_WIDTH = 512
SSM_GROUP = 16
N_SSM_GROUPS = 32
SSM_STATE = 64
N_STATES = N_SSM_GROUPS * SSM_STATE
ATTN_WIDTH = 512
HEAD_DIM = 64
N_HEADS = 8
N_KV_HEADS = 2
Q_PER_KV = 4
KV_WIDTH = 128
WINDOW = 128
ATTN_SCALE = HEAD_DIM ** -0.5
D_FF = 2816
ALPHA = (2.0 * DEPTH) ** 0.25
LN_EPS = 1e-5
RMS_EPS = 1e-6

ROW_GROUP = 256
W_STAGE_SLOTS = 4
W_GU_STAGE_ROWS = 64
W_DN_STAGE_ROWS = 352
LANES = 128
SUBLANES = 8
VMEM_LIMIT_BYTES = 56 * 1024 * 1024
N_SLAB = SSM_WIDTH // LANES
GROUPS_PER_SLAB = LANES // SSM_GROUP
SLAB_STATES = GROUPS_PER_SLAB * SSM_STATE
Q_EXP = N_HEADS * LANES
SSM_CHUNK = 8
SSM_ROWS = 2048
ATTN_BLOCKS = 8
ATTN_HEAD_ORDER = (0, 2, 1, 3)
MASKED = -1e30

F32 = jnp.float32
BF16 = jnp.bfloat16


def _layer_norm(r, g, b):
    mu = jnp.mean(r, axis=-1, keepdims=True)
    c = r - mu
    var = jnp.mean(c * c, axis=-1, keepdims=True)
    return c * lax.rsqrt(var + LN_EPS) * g + b


def _rms_norm(y, g):
    return y * lax.rsqrt(jnp.mean(y * y, axis=-1, keepdims=True) + RMS_EPS) * g


def _const(c, n_grid=1):
    if isinstance(c, tuple):
        arr, idx = c
        shape = (None,) + arr.shape[1:]
        index = (idx,) + (0,) * (arr.ndim - 1)
    else:
        arr, shape, index = c, c.shape, (0,) * c.ndim
    imap = (lambda i: index) if n_grid == 1 else (lambda i, j: index)
    return arr, pl.BlockSpec(shape, imap, pipeline_mode=pl.Buffered(1))


def _consts(cs, n_grid=1):
    arrs, specs = zip(*[_const(c, n_grid) for c in cs])
    return list(arrs), list(specs)


def _load_cast(src_hbm, layer, dst_ref, stage_ref, sem_ref):
    n_slots, rows = stage_ref.shape[:2]
    n_chunks = dst_ref.shape[0] // rows

    def copy(c):
        return pltpu.make_async_copy(src_hbm.at[layer, pl.ds(c * rows, rows), :], stage_ref.at[c % n_slots],
                                     sem_ref.at[c % n_slots])

    for c in range(min(n_slots - 1, n_chunks)):
        copy(c).start()
    for c in range(n_chunks):
        copy(c).wait()
        dst_ref[pl.ds(c * rows, rows), :] = stage_ref[c % n_slots].astype(BF16)
        if c + n_slots - 1 < n_chunks:
            copy(c + n_slots - 1).start()


def _rows_kernel(*refs, has_mix, has_proj, has_sample, n_sub, layer):
    it = iter(refs)
    take = lambda n: tuple(next(it) for _ in range(n))
    (x_ref,) = take(1)
    if has_mix:
        ssm_ref, att_ref, wo_s_ref, wo_a_ref, gm_ref, bm_ref = take(6)
    wgu_hbm, wdn_hbm, g_ref, b_ref = take(4)
    wp_ref = take(1)[0] if has_proj else None
    if has_sample:
        (xs_ref,) = take(1)
        if has_mix:
            ssm_s_ref, att_s_ref, pack_ref = take(3)
        pad_ref = take(1)[0] if has_proj else None
    n_out = 4 if has_proj else 1
    outs = take(n_out)
    outs_s = take(n_out) if has_sample else None
    wgu_ref, wdn_ref, stage_gu, stage_dn, sem_gu, sem_dn = take(6)

    @pl.when(pl.program_id(0) == 0)
    def _():
        _load_cast(wgu_hbm, layer, wgu_ref, stage_gu, sem_gu)
        _load_cast(wdn_hbm, layer, wdn_ref, stage_dn, sem_dn)

    def mix_inputs(ssm_r, att_r, rows):
        ssm = jnp.concatenate([ssm_r[j, rows, :] for j in range(N_SLAB)], axis=1).astype(BF16)
        return ssm, att_r[rows, :].astype(BF16)

    def run(xs, mixes, wo_a, wp):
        if has_mix:
            ms = [jnp.dot(ssm, wo_s_ref[...], preferred_element_type=F32)
                  + jnp.dot(att, wo_a[...], preferred_element_type=F32) for ssm, att in mixes]
            xs = [_layer_norm(ALPHA * x + m, gm_ref[...], bm_ref[...]) for x, m in zip(xs, ms)]
        gus = [jnp.dot(x.astype(BF16), wgu_ref[...], preferred_element_type=F32) for x in xs]
        hs = [(gu[:, :D_FF] * jax.nn.sigmoid(gu[:, :D_FF]) * gu[:, D_FF:]).astype(BF16) for gu in gus]
        ys = [jnp.dot(h, wdn_ref[...], preferred_element_type=F32) for h in hs]
        xs = [_layer_norm(ALPHA * x + 0.5 * y, g_ref[...], b_ref[...]) for x, y in zip(xs, ys)]
        if has_proj:
            zs = [jnp.dot(x.astype(BF16), wp[...], preferred_element_type=F32) for x in xs]
        else:
            zs = [None] * len(xs)
        return xs, zs

    def store(out_refs, rows, x, z):
        out_refs[0][rows, :] = x
        if has_proj:
            u_ref, q_ref, kv_ref = out_refs[1:]
            q_cols = q_ref.shape[1]
            for j in range(N_SLAB):
                u_ref[j, rows, :] = z[:, j * LANES:(j + 1) * LANES]
            q_ref[rows, :] = z[:, SSM_WIDTH:SSM_WIDTH + q_cols].astype(q_ref.dtype)
            kv_ref[rows, :] = z[:, SSM_WIDTH + q_cols:]

    sub = x_ref.shape[0] // n_sub
    groups = [slice(r * sub, (r + 1) * sub) for r in range(n_sub)]
    mixes = [mix_inputs(ssm_ref, att_ref, rows) for rows in groups] if has_mix else None
    xs, zs = run([x_ref[rows, :] for rows in groups], mixes, wo_a_ref if has_mix else None, wp_ref)
    for rows, x, z in zip(groups, xs, zs):
        store(outs, rows, x, z)

    if has_sample:
        @pl.when(pl.program_id(0) == pl.num_programs(0) - 1)
        def _():
            every = slice(None)
            mixes_s = None
            if has_mix:
                ssm, att_padded = mix_inputs(ssm_s_ref, att_s_ref, every)
                att = jnp.dot(att_padded, pack_ref[...], preferred_element_type=F32).astype(BF16)
                mixes_s = [(ssm, att)]
            xs_s, zs_s = run([xs_ref[...]], mixes_s, wo_a_ref if has_mix else None, wp_ref)
            z = zs_s[0]
            if has_proj:
                q = z[:, SSM_WIDTH:SSM_WIDTH + ATTN_WIDTH].astype(BF16)
                z = jnp.concatenate([z[:, :SSM_WIDTH], jnp.dot(q, pad_ref[...], preferred_element_type=F32),
                                     z[:, SSM_WIDTH + ATTN_WIDTH:]], axis=1)
            store(outs_s, every, xs_s[0], z)


def _head_selectors():
    h = np.arange(ATTN_WIDTH) // HEAD_DIM
    d = np.arange(ATTN_WIDTH) % HEAD_DIM
    kvh, pr = h // Q_PER_KV, h % Q_PER_KV
    padded_col = h * LANES + kvh * HEAD_DIM + d
    packed_col = pr * LANES + kvh * HEAD_DIM + d
    pad = np.zeros((ATTN_WIDTH, Q_EXP), np.float32)
    pad[np.arange(ATTN_WIDTH), padded_col] = 1.0
    pack = np.zeros((Q_EXP, ATTN_WIDTH), np.float32)
    pack[padded_col, packed_col] = 1.0
    return jnp.asarray(pad, BF16), jnp.asarray(pack, BF16)


def _rows_call(x, ffn, ln, mix=None, proj=None, sample=None, *, tm, name):
    M = x.shape[0]
    assert M % tm == 0
    row = lambda w: pl.BlockSpec((tm, w), lambda i: (i, 0))
    slab = lambda n: pl.BlockSpec((n, tm, LANES), lambda i: (0, i, 0))
    whole = lambda a: pl.BlockSpec(a.shape, lambda i: (0,) * len(a.shape))
    args = [x]
    specs = [row(D_MODEL)]
    if mix is not None:
        ssm_n, att_n = mix[:2]
        arrs, cspecs = _consts(mix[2:])
        args += [ssm_n, att_n] + arrs
        specs += [slab(N_SLAB), row(att_n.shape[1])] + cspecs
    w_gu, w_dn, layer = ffn
    arrs, cspecs = _consts(list(ln))
    args += [w_gu, w_dn] + arrs
    specs += [pl.BlockSpec(memory_space=pl.ANY), pl.BlockSpec(memory_space=pl.ANY)] + cspecs

    def proj_outputs(rows, q_cols, q_dtype):
        return [jax.ShapeDtypeStruct((N_SLAB, rows, LANES), F32), jax.ShapeDtypeStruct((rows, q_cols), q_dtype),
                jax.ShapeDtypeStruct((rows, 2 * KV_WIDTH), F32)]

    out_shape = [jax.ShapeDtypeStruct((M, D_MODEL), F32)]
    out_specs = [row(D_MODEL)]
    if proj is not None:
        arr, cspec = _const(proj)
        args.append(arr)
        specs.append(cspec)
        q_cols = arr.shape[-1] - SSM_WIDTH - 2 * KV_WIDTH
        out_shape += proj_outputs(M, q_cols, BF16)
        out_specs += [slab(N_SLAB), row(q_cols), row(2 * KV_WIDTH)]
    n_main = len(out_shape)
    if sample is not None:
        xs = sample['x']
        args.append(xs)
        specs.append(whole(xs))
        pad, pack = _head_selectors()
        if mix is not None:
            ssm_s, att_s = sample['mix']
            arr, cspec = _const(pack)
            args += [ssm_s, att_s, arr]
            specs += [whole(ssm_s), whole(att_s), cspec]
        sample_out = [jax.ShapeDtypeStruct(xs.shape, F32)]
        if proj is not None:
            arr, cspec = _const(pad)
            args.append(arr)
            specs.append(cspec)
            sample_out += proj_outputs(xs.shape[0], Q_EXP, F32)
        out_shape += sample_out
        out_specs += [whole(o) for o in sample_out]
    outs = pl.pallas_call(
        functools.partial(_rows_kernel, has_mix=mix is not None, has_proj=proj is not None,
                          has_sample=sample is not None, n_sub=max(1, tm // ROW_GROUP), layer=layer),
        grid=(M // tm,),
        in_specs=specs,
        out_specs=out_specs,
        out_shape=out_shape,
        scratch_shapes=[pltpu.VMEM(w_gu.shape[1:], BF16), pltpu.VMEM(w_dn.shape[1:], BF16),
                        pltpu.VMEM((W_STAGE_SLOTS, W_GU_STAGE_ROWS, w_gu.shape[2]), F32),
                        pltpu.VMEM((W_STAGE_SLOTS, W_DN_STAGE_ROWS, w_dn.shape[2]), F32),
                        pltpu.SemaphoreType.DMA((W_STAGE_SLOTS,)), pltpu.SemaphoreType.DMA((W_STAGE_SLOTS,))],
        compiler_params=pltpu.CompilerParams(dimension_semantics=("arbitrary",),
                                             vmem_limit_bytes=VMEM_LIMIT_BYTES),
        name=name,
    )(*args)
    return outs[:n_main], (outs[n_main:] if sample is not None else None)


def _cmul(xr, xi, yr, yi):
    return xr * yr - xi * yi, xr * yi + xi * yr


def _build_ssm_tables(p_ref, af_ref, wst_ref, cat_ref, strip_ref, ak_ref, tr_ref):
    t1 = SSM_CHUNK
    a_re, a_im, lam_re, lam_im, b_re, b_im, c_re, c_im = (p_ref[k] for k in range(8))
    num_re, num_im = a_re - 1.0, a_im
    den = lam_re * lam_re + lam_im * lam_im
    f_re = (num_re * lam_re + num_im * lam_im) / den
    f_im = (num_im * lam_re - num_re * lam_im) / den
    bb_re, bb_im = _cmul(f_re, f_im, b_re, b_im)

    def same_group(shape, row_div, col_div):
        r = lax.broadcasted_iota(jnp.int32, shape, 0) // row_div
        c = lax.broadcasted_iota(jnp.int32, shape, 1) // col_div
        return (r == c).astype(F32)

    m_state = same_group((LANES, SLAB_STATES), SSM_GROUP, SSM_STATE)
    m_chan = 0.5 * same_group((LANES, LANES), SSM_GROUP, SSM_GROUP)

    def expand(x):
        return (jnp.concatenate([x] * (SLAB_STATES // LANES), axis=1) * m_state).astype(BF16)

    def dot_nt(a, b):
        return lax.dot_general(a, b, (((1,), (1,)), ((), ())), precision=lax.Precision.HIGHEST,
                               preferred_element_type=F32)

    pw_re, pw_im = jnp.ones_like(a_re), jnp.zeros_like(a_re)
    kd = []
    for l in range(t1):
        ab_re, ab_im = _cmul(pw_re, pw_im, bb_re, bb_im)
        s = t1 - 1 - l
        kd_l = []
        for j in range(N_SLAB):
            rows = slice(j * LANES, (j + 1) * LANES)
            wst_ref[j, s * LANES:(s + 1) * LANES, :SLAB_STATES] = expand(ab_re[rows])
            wst_ref[j, s * LANES:(s + 1) * LANES, SLAB_STATES:] = expand(ab_im[rows])
            k = dot_nt(ab_re[rows], c_re[rows]) - dot_nt(ab_im[rows], c_im[rows])
            kd_l.append((k * m_chan).astype(BF16))
        kd.append(kd_l)
        pw_re, pw_im = _cmul(pw_re, pw_im, a_re, a_im)
        cf_re = c_re * pw_re - c_im * pw_im
        cf_im = -(c_re * pw_im + c_im * pw_re)
        for j in range(N_SLAB):
            rows = slice(j * LANES, (j + 1) * LANES)
            cat_ref[j, l * LANES:(l + 1) * LANES, :SLAB_STATES] = expand(cf_re[rows])
            cat_ref[j, l * LANES:(l + 1) * LANES, SLAB_STATES:] = expand(cf_im[rows])
    for j in range(N_SLAB):
        for rho in range(t1):
            for c in range(2):
                lag = t1 - 2 - rho + c
                blk = kd[lag][j] if lag >= 0 else jnp.zeros((LANES, LANES), BF16)
                strip_ref[j, rho * LANES:(rho + 1) * LANES, c * LANES:(c + 1) * LANES] = blk

    f_re, f_im = af_ref[0], af_ref[1]
    base_re, base_im = f_re, f_im
    for _ in range(t1 - 1):
        base_re, base_im = _cmul(base_re, base_im, f_re, f_im)
    row = lax.broadcasted_iota(jnp.int32, (SUBLANES, N_STATES), 0)
    pws = [(base_re, base_im)]
    for _ in range(SUBLANES - 1):
        pws.append(_cmul(pws[-1][0], pws[-1][1], base_re, base_im))
    for n, k in enumerate((1, 2, 4)):
        for part in range(2):
            ak_ref[2 * n + part] = jnp.where(row >= k, jnp.broadcast_to(pws[k - 1][part], row.shape), 0.0)
    for part in range(2):
        acc = jnp.zeros(row.shape, F32)
        for r in range(SUBLANES):
            acc = jnp.where(row == r, jnp.broadcast_to(pws[r][part], row.shape), acc)
        tr_ref[part] = acc


def _ssm_table_scratch():
    t1 = SSM_CHUNK
    return [pltpu.VMEM((N_SLAB, t1 * LANES, 2 * SLAB_STATES), BF16),
            pltpu.VMEM((N_SLAB, t1 * LANES, 2 * SLAB_STATES), BF16),
            pltpu.VMEM((N_SLAB, t1 * LANES, 2 * LANES), BF16),
            pltpu.VMEM((6, SUBLANES, N_STATES), F32),
            pltpu.VMEM((2, SUBLANES, N_STATES), F32)]


def _ssm_table_inputs(lam_re, lam_im, log_dt, b_re, b_im, c_re, c_im):
    dt = jnp.exp(log_dt)[..., None]
    mag = jnp.exp(lam_re * dt)
    a_re = mag * jnp.cos(lam_im * dt)
    a_im = mag * jnp.sin(lam_im * dt)
    rep = lambda x: jnp.repeat(x, SSM_GROUP, axis=1)
    flat = lambda x: x.reshape(DEPTH, SSM_WIDTH, SSM_STATE)
    params = jnp.stack([rep(a_re), rep(a_im), rep(lam_re), rep(lam_im),
                        flat(jnp.swapaxes(b_re, 2, 3)), flat(jnp.swapaxes(b_im, 2, 3)), flat(c_re), flat(c_im)],
                       axis=1)
    params = jnp.concatenate([params, params], axis=-1)
    a_flat = jnp.stack([a_re.reshape(DEPTH, 1, N_STATES), a_im.reshape(DEPTH, 1, N_STATES)], axis=1)
    return params, a_flat


def _glu_rms(y, gluw_ref, glub_ref, g_ref):
    g = jax.nn.gelu(y).astype(BF16)
    zz = jnp.dot(g, gluw_ref[...], preferred_element_type=F32) + glub_ref[...]
    o = zz[:, :SSM_WIDTH] * jax.nn.sigmoid(zz[:, SSM_WIDTH:])
    return _rms_norm(o, g_ref[...])


def _ssm_kernel(u_ref, p_ref, af_ref, d_ref, gluw_ref, glub_ref, g_ref,
                o_ref, hre_ref, him_ref, wst1_ref, cat1_ref, kd0_ref,
                wst_ref, cat_ref, strip_ref, ak_ref, tr_ref, ucb_ref, sre_ref, sim_ref, hcr_ref, hci_ref):
    t1 = SSM_CHUNK
    nc = SSM_ROWS // t1
    i = pl.program_id(1)

    @pl.when(jnp.logical_and(pl.program_id(0) == 0, i == 0))
    def _():
        _build_ssm_tables(p_ref, af_ref, wst_ref, cat_ref, strip_ref, ak_ref, tr_ref)
        wst1_ref[...] = wst_ref[:, (t1 - 1) * LANES:, :]
        cat1_ref[...] = cat_ref[:, :LANES, :]
        kd0_ref[...] = strip_ref[:, (t1 - 1) * LANES:, LANES:]

    @pl.when(i == 0)
    def _():
        hcr_ref[...] = jnp.zeros_like(hcr_ref)
        hci_ref[...] = jnp.zeros_like(hci_ref)

    for j in range(N_SLAB):
        for s in range(t1):
            ucb_ref[j, :, s * LANES:(s + 1) * LANES] = u_ref[j, pl.ds(s, nc, stride=t1), :].astype(BF16)

    for j in range(N_SLAB):
        st = jnp.dot(ucb_ref[j], wst_ref[j], preferred_element_type=F32)
        sre_ref[:, j * SLAB_STATES:(j + 1) * SLAB_STATES] = st[:, :SLAB_STATES]
        sim_ref[:, j * SLAB_STATES:(j + 1) * SLAB_STATES] = st[:, SLAB_STATES:]

    row = lax.broadcasted_iota(jnp.int32, (SUBLANES, SLAB_STATES), 0)
    for j in range(N_SLAB):
        cols = pl.ds(j * SLAB_STATES, SLAB_STATES)

        def body(gi, carry, cols=cols):
            hr, hi = carry
            r0 = pl.multiple_of(gi * SUBLANES, SUBLANES)
            xr = sre_ref[pl.ds(r0, SUBLANES), cols]
            xi = sim_ref[pl.ds(r0, SUBLANES), cols]
            for n, k in enumerate((1, 2, 4)):
                akr = ak_ref[2 * n, :, cols]
                aki = ak_ref[2 * n + 1, :, cols]
                sr = pltpu.roll(xr, k, 0)
                si = pltpu.roll(xi, k, 0)
                xr, xi = xr + akr * sr - aki * si, xi + akr * si + aki * sr
            tr = tr_ref[0, :, cols]
            ti = tr_ref[1, :, cols]
            er = xr + tr * hr - ti * hi
            ei = xi + tr * hi + ti * hr
            sre_ref[pl.ds(r0, SUBLANES), cols] = jnp.where(row >= 1, pltpu.roll(er, 1, 0), hr)
            sim_ref[pl.ds(r0, SUBLANES), cols] = jnp.where(row >= 1, pltpu.roll(ei, 1, 0), hi)
            last = SUBLANES - 1
            return (jnp.broadcast_to(er[last:last + 1], er.shape), jnp.broadcast_to(ei[last:last + 1], ei.shape))

        hr, hi = lax.fori_loop(0, nc // SUBLANES, body, (hcr_ref[:, cols], hci_ref[:, cols]), unroll=True)
        hcr_ref[:, cols] = hr
        hci_ref[:, cols] = hi

    hre_ref[0] = hcr_ref[...]
    him_ref[0] = hci_ref[...]

    ys = []
    for j in range(N_SLAB):
        cols = slice(j * SLAB_STATES, (j + 1) * SLAB_STATES)
        hprev = jnp.concatenate([sre_ref[:, cols], sim_ref[:, cols]], axis=1).astype(BF16)
        y = lax.dot_general(hprev, cat_ref[j], (((1,), (1,)), ((), ())), preferred_element_type=F32)
        parts = []
        for p2 in range(t1 // 2):
            kk = (2 * p2 + 2) * LANES
            yi = jnp.dot(ucb_ref[j, :, :kk], strip_ref[j, (t1 - 2 - 2 * p2) * LANES:, :],
                         preferred_element_type=F32)
            parts.append(y[:, 2 * p2 * LANES:(2 * p2 + 2) * LANES] + yi)
        ys.append(jnp.concatenate(parts, axis=1))

    for t in range(t1):
        y = jnp.concatenate([ys[j][:, t * LANES:(t + 1) * LANES] + d_ref[j] * u_ref[j, pl.ds(t, nc, stride=t1), :]
                             for j in range(N_SLAB)], axis=1)
        n = _glu_rms(y, gluw_ref, glub_ref, g_ref)
        for j in range(N_SLAB):
            o_ref[j, pl.ds(t, nc, stride=t1), :] = n[:, j * LANES:(j + 1) * LANES]


def _ssm_prompt_call(u, params, a_flat, d_t, gluw, glub, g, *, batch, seq, name):
    t1 = SSM_CHUNK
    nc = SSM_ROWS // t1
    steps = seq // SSM_ROWS
    blk = pl.BlockSpec((N_SLAB, SSM_ROWS, LANES), lambda b, i: (0, b * steps + i, 0))
    st = pl.BlockSpec((1, SUBLANES, N_STATES), lambda b, i: (b, 0, 0))
    consts, cspecs = _consts([params, a_flat, d_t, gluw, glub, g], 2)
    small = [jax.ShapeDtypeStruct((N_SLAB, LANES, 2 * SLAB_STATES), BF16),
             jax.ShapeDtypeStruct((N_SLAB, LANES, 2 * SLAB_STATES), BF16),
             jax.ShapeDtypeStruct((N_SLAB, LANES, LANES), BF16)]
    whole = lambda a: pl.BlockSpec(a.shape, lambda b, i: (0,) * len(a.shape))
    out, hre, him, *step_tables = pl.pallas_call(
        _ssm_kernel,
        grid=(batch, steps),
        in_specs=[blk] + cspecs,
        out_specs=[blk, st, st] + [whole(s) for s in small],
        out_shape=[jax.ShapeDtypeStruct(u.shape, F32),
                   jax.ShapeDtypeStruct((batch, SUBLANES, N_STATES), F32),
                   jax.ShapeDtypeStruct((batch, SUBLANES, N_STATES), F32)] + small,
        scratch_shapes=_ssm_table_scratch() + [pltpu.VMEM((N_SLAB, nc, t1 * LANES), BF16),
                                               pltpu.VMEM((nc, N_STATES), F32),
                                               pltpu.VMEM((nc, N_STATES), F32),
                                               pltpu.VMEM((SUBLANES, N_STATES), F32),
                                               pltpu.VMEM((SUBLANES, N_STATES), F32)],
        compiler_params=pltpu.CompilerParams(dimension_semantics=("arbitrary", "arbitrary"),
                                             vmem_limit_bytes=VMEM_LIMIT_BYTES),
        name=name,
    )(u, *consts)
    return out, hre[:, 0], him[:, 0], step_tables


def _ssm_sample_kernel(u_ref, h0r_ref, h0i_ref, wst_ref, cat_ref, kd_ref, a_ref, d_ref, gluw_ref, glub_ref, g_ref,
                       o_ref, hr_ref, hi_ref):
    ys = []
    for j in range(N_SLAB):
        cols = slice(j * SLAB_STATES, (j + 1) * SLAB_STATES)
        uf = u_ref[j]
        ub = uf.astype(BF16)
        st = jnp.dot(ub, wst_ref[j], preferred_element_type=F32)
        h0r = h0r_ref[:, cols]
        h0i = h0i_ref[:, cols]
        ar = a_ref[0, :, cols]
        ai = a_ref[1, :, cols]
        hr_ref[:, cols] = ar * h0r - ai * h0i + st[:, :SLAB_STATES]
        hi_ref[:, cols] = ar * h0i + ai * h0r + st[:, SLAB_STATES:]
        hcat = jnp.concatenate([h0r, h0i], axis=1).astype(BF16)
        y = lax.dot_general(hcat, cat_ref[j], (((1,), (1,)), ((), ())), preferred_element_type=F32)
        y = y + jnp.dot(ub, kd_ref[j], preferred_element_type=F32)
        ys.append(y + d_ref[j] * uf)
    n = _glu_rms(jnp.concatenate(ys, axis=1), gluw_ref, glub_ref, g_ref)
    for j in range(N_SLAB):
        o_ref[j] = n[:, j * LANES:(j + 1) * LANES]


def _ssm_sample_call(u, h0r, h0i, step_tables, a_flat, d_t, gluw, glub, g, *, name):
    n_seq = u.shape[1]
    whole = lambda a: pl.BlockSpec(a.shape, lambda i: (0,) * len(a.shape))
    outs = [jax.ShapeDtypeStruct(u.shape, F32),
            jax.ShapeDtypeStruct((n_seq, N_STATES), F32),
            jax.ShapeDtypeStruct((n_seq, N_STATES), F32)]
    consts, cspecs = _consts(list(step_tables) + [a_flat, d_t, gluw, glub, g])
    return pl.pallas_call(
        _ssm_sample_kernel,
        grid=(1,),
        in_specs=[whole(u), whole(h0r), whole(h0i)] + cspecs,
        out_specs=[whole(o) for o in outs],
        out_shape=outs,
        compiler_params=pltpu.CompilerParams(dimension_semantics=("arbitrary",),
                                             vmem_limit_bytes=VMEM_LIMIT_BYTES),
        name=name,
    )(u, h0r, h0i, *consts)


def _attn_kernel(sink_ref, q_ref, kvc_ref, kvp_ref, g_ref, o_ref):
    i = pl.program_id(1)
    kj = lax.broadcasted_iota(jnp.int32, (2 * WINDOW, WINDOW), 0)
    qi = lax.broadcasted_iota(jnp.int32, (2 * WINDOW, WINDOW), 1)
    band = (kj >= qi) & (kj <= qi + WINDOW)
    low = qi < HEAD_DIM
    for sb in range(ATTN_BLOCKS):
        blk = slice(sb * WINDOW, (sb + 1) * WINDOW)
        kc = kvc_ref[blk, :]
        if sb == 0:
            kp = kvp_ref[...]
            valid = band & ((i > 0) | (kj >= WINDOW))
        else:
            kp = kvc_ref[(sb - 1) * WINDOW:sb * WINDOW, :]
            valid = band
        valid2 = jnp.concatenate([valid, valid], axis=1)
        kcat = jnp.concatenate([kp[:, :KV_WIDTH], kc[:, :KV_WIDTH]], axis=0) * ATTN_SCALE
        swap = pltpu.roll(kcat, HEAD_DIM, 1)
        zero = jnp.zeros_like(kcat)
        k_sel = [[jnp.where(low, kcat, zero).astype(BF16), jnp.where(low, zero, swap).astype(BF16)],
                 [jnp.where(low, swap, zero).astype(BF16), jnp.where(low, zero, kcat).astype(BF16)]]
        v_t = jnp.concatenate([kp[:, KV_WIDTH:], kc[:, KV_WIDTH:]], axis=0).T.astype(BF16)
        parts = []
        for kvh in range(N_KV_HEADS):
            q_pair = jnp.concatenate([q_ref[blk, (2 * kvh) * LANES:(2 * kvh + 1) * LANES],
                                      q_ref[blk, (2 * kvh + 1) * LANES:(2 * kvh + 2) * LANES]], axis=0)
            for parity in range(2):
                s_t = lax.dot_general(k_sel[kvh][parity], q_pair, (((1,), (1,)), ((), ())),
                                      preferred_element_type=F32)
                parts.append(jnp.where(valid2, s_t, MASKED))
        s_all = jnp.concatenate(parts, axis=1)
        sink = sink_ref[...]
        m = jnp.maximum(jnp.max(s_all, axis=0, keepdims=True), sink)
        p_all = jnp.exp(s_all - m)
        inv_den = 1.0 / (jnp.sum(p_all, axis=0, keepdims=True) + jnp.exp(sink - m))
        p_all = p_all.astype(BF16)
        half = Q_PER_KV * WINDOW
        o_kv = [jnp.dot(v_t[kvh * HEAD_DIM:(kvh + 1) * HEAD_DIM], p_all[:, kvh * half:(kvh + 1) * half],
                        preferred_element_type=F32) * inv_den[:, kvh * half:(kvh + 1) * half]
                for kvh in range(N_KV_HEADS)]
        tiles = []
        for pr in range(Q_PER_KV):
            lanes = slice(ATTN_HEAD_ORDER.index(pr) * WINDOW, (ATTN_HEAD_ORDER.index(pr) + 1) * WINDOW)
            tiles.append(jnp.concatenate([o_kv[0][:, lanes], o_kv[1][:, lanes]], axis=0))
        ss = None
        for tile in tiles:
            t = jnp.sum(tile * tile, axis=0, keepdims=True)
            ss = t if ss is None else ss + t
        inv = lax.rsqrt(ss / ATTN_WIDTH + RMS_EPS)
        for pr, tile in enumerate(tiles):
            o_ref[blk, pr * LANES:(pr + 1) * LANES] = ((tile * inv).T * g_ref[pr]).astype(o_ref.dtype)


def _attn_prompt_call(q, kv, sinks, g_perm, *, batch, seq, name):
    rows = ATTN_BLOCKS * WINDOW
    nb = seq // rows
    M = kv.shape[0]
    cur = lambda b, i: (b * nb + i, 0)
    prev = lambda b, i: ((b * nb + i) * ATTN_BLOCKS - jnp.minimum(i, 1), 0)
    (sinks, g_perm), (sink_spec, g_spec) = _consts([sinks, g_perm], 2)
    return pl.pallas_call(
        _attn_kernel,
        grid=(batch, nb),
        in_specs=[sink_spec,
                  pl.BlockSpec((rows, ATTN_WIDTH), cur),
                  pl.BlockSpec((rows, 2 * KV_WIDTH), cur),
                  pl.BlockSpec((WINDOW, 2 * KV_WIDTH), prev),
                  g_spec],
        out_specs=pl.BlockSpec((rows, ATTN_WIDTH), cur),
        out_shape=jax.ShapeDtypeStruct((M, ATTN_WIDTH), BF16),
        compiler_params=pltpu.CompilerParams(dimension_semantics=("arbitrary", "arbitrary"),
                                             vmem_limit_bytes=VMEM_LIMIT_BYTES),
        name=name,
    )(sinks, q, kv, kv, g_perm)


SEQ_PER_STEP = LANES // N_HEADS


def _attn_sample_kernel(q_ref, kv_ref, ckt_ref, cvt_ref, sink_ref, g_ref, own_ref, o_ref, kot_ref, vot_ref):
    nrow = SEQ_PER_STEP * N_HEADS
    qb = q_ref[...].astype(BF16)
    k_new = kv_ref[:, :KV_WIDTH]
    v_new = kv_ref[:, KV_WIDTH:]
    row_seq = lax.broadcasted_iota(jnp.int32, (nrow, LANES), 0) // N_HEADS
    lane_seq = lax.broadcasted_iota(jnp.int32, (nrow, LANES), 1) // N_HEADS
    lane = lax.broadcasted_iota(jnp.int32, (nrow, LANES), 1)
    pick = (lax.broadcasted_iota(jnp.int32, (nrow, SEQ_PER_STEP), 0) // N_HEADS
            == lax.broadcasted_iota(jnp.int32, (nrow, SEQ_PER_STEP), 1)).astype(BF16)

    qf = q_ref[...]
    heads_of = lambda a, n: a[n * N_HEADS:(n + 1) * N_HEADS]
    s = jnp.concatenate([jnp.dot(heads_of(qf, n).astype(BF16), ckt_ref[n].astype(BF16), preferred_element_type=F32)
                         for n in range(SEQ_PER_STEP)], axis=0) * ATTN_SCALE
    k_rows = jnp.dot(pick, k_new.astype(BF16), preferred_element_type=F32)
    s_new = jnp.sum(qb.astype(F32) * k_rows, axis=-1, keepdims=True) * ATTN_SCALE
    sink = sink_ref[...]
    m = jnp.maximum(jnp.maximum(jnp.max(s, axis=-1, keepdims=True), s_new), sink)
    p = jnp.exp(s - m)
    p_new = jnp.exp(s_new - m)
    inv = 1.0 / (jnp.sum(p, axis=-1, keepdims=True) + p_new + jnp.exp(sink - m))
    pn = p * inv
    o = jnp.concatenate([lax.dot_general(heads_of(pn, n).astype(BF16), cvt_ref[n].astype(BF16),
                                         (((1,), (1,)), ((), ())), preferred_element_type=F32)
                         for n in range(SEQ_PER_STEP)], axis=0)
    v_rows = jnp.dot(pick, v_new.astype(BF16), preferred_element_type=F32)
    o = o + (p_new * inv).astype(BF16).astype(F32) * v_rows

    hi = lax.Precision.HIGHEST
    row_sums = jnp.dot(o * o * own_ref[...], jnp.ones((LANES, LANES), F32), precision=hi, preferred_element_type=F32)
    ss = jnp.dot((row_seq == lane_seq).astype(F32), row_sums, precision=hi, preferred_element_type=F32)
    o_ref[...] = o * lax.rsqrt(ss / ATTN_WIDTH + RMS_EPS) * g_ref[...]

    pad = jnp.zeros((LANES - SEQ_PER_STEP, KV_WIDTH), F32)
    k_cols = jnp.concatenate([k_new, pad], axis=0).T
    v_cols = jnp.concatenate([v_new, pad], axis=0).T
    last = lane == WINDOW - 1
    for n in range(SEQ_PER_STEP):
        kot_ref[n] = jnp.where(last, pltpu.roll(k_cols, WINDOW - 1 - n, 1), pltpu.roll(ckt_ref[n], WINDOW - 1, 1))
        vot_ref[n] = jnp.where(last, pltpu.roll(v_cols, WINDOW - 1 - n, 1), pltpu.roll(cvt_ref[n], WINDOW - 1, 1))


def _attn_sample_call(q, kv, ck, cv, layer, sink_row, g_rows, own_rows, *, name):
    n_seq = kv.shape[0]
    assert n_seq % SEQ_PER_STEP == 0
    nrow = SEQ_PER_STEP * N_HEADS
    cblk = pl.BlockSpec((SEQ_PER_STEP, WINDOW, 2 * HEAD_DIM), lambda i: (i, 0, 0))
    cin = pl.BlockSpec((None, SEQ_PER_STEP, WINDOW, 2 * HEAD_DIM), lambda i: (layer, i, 0, 0))
    qblk = pl.BlockSpec((nrow, LANES), lambda i: (i, 0))
    consts, cspecs = _consts([sink_row, g_rows, own_rows])
    return pl.pallas_call(
        _attn_sample_kernel,
        grid=(n_seq // SEQ_PER_STEP,),
        in_specs=[qblk, pl.BlockSpec((SEQ_PER_STEP, 2 * KV_WIDTH), lambda i: (i, 0)), cin, cin] + cspecs,
        out_specs=[qblk, cblk, cblk],
        out_shape=[jax.ShapeDtypeStruct(q.shape, F32),
                   jax.ShapeDtypeStruct(ck.shape[1:], F32),
                   jax.ShapeDtypeStruct(cv.shape[1:], F32)],
        compiler_params=pltpu.CompilerParams(dimension_semantics=("arbitrary",),
                                             vmem_limit_bytes=VMEM_LIMIT_BYTES),
        name=name,
    )(q, kv, ck, cv, *consts)


def _expand_heads(a):
    lead = a.shape[:-1]
    a = a.reshape(lead + (N_KV_HEADS, Q_PER_KV, 1, HEAD_DIM))
    sel = jnp.eye(N_KV_HEADS, dtype=a.dtype).reshape(N_KV_HEADS, 1, N_KV_HEADS, 1)
    return (a * sel).reshape(lead + (Q_EXP,))


def _pair_heads(a):
    lead = a.shape[:-1]
    a = a.reshape(lead + (N_KV_HEADS, Q_PER_KV, HEAD_DIM))
    return jnp.swapaxes(a, -3, -2).reshape(lead + (ATTN_WIDTH,))


def _prep_weights(ln_g, ln_b, ffn1_w_in, ffn1_w_out, ffn2_w_in, ffn2_w_out, w_in, ssm_lam_re, ssm_lam_im, ssm_log_dt,
                  ssm_b_re, ssm_b_im, ssm_c_re, ssm_c_im, ssm_d, glu_w, glu_b, attn_sinks, g_ssm_out, g_attn_out,
                  w_out):
    rows_t = lambda a: jnp.swapaxes(a, 1, 2)
    params, a_flat = _ssm_table_inputs(ssm_lam_re, ssm_lam_im, ssm_log_dt, ssm_b_re, ssm_b_im, ssm_c_re, ssm_c_im)
    head_order = jnp.array([Q_PER_KV * kvh + o for kvh in range(N_KV_HEADS) for o in ATTN_HEAD_ORDER])
    g_exp = _expand_heads(g_attn_out).reshape(DEPTH, N_HEADS, LANES)
    w_in_b = w_in.astype(BF16)
    w_out_b = w_out.astype(BF16)
    stacked = dict(
        w_in=w_in_b,
        wo_s=w_out_b[:, :SSM_WIDTH],
        wo_a=rows_t(_pair_heads(rows_t(w_out_b[:, SSM_WIDTH:]))),
        ssm_params=params, a_flat=a_flat,
        d_1=ssm_d.reshape(DEPTH, N_SLAB, 1, LANES),
        gluw=glu_w.astype(BF16),
        glub=glu_b.reshape(DEPTH, 1, 2 * SSM_WIDTH),
        g_ssm=g_ssm_out.reshape(DEPTH, 1, SSM_WIDTH),
        sinks=jnp.repeat(attn_sinks[:, head_order], WINDOW, axis=1)[:, None, :],
        g_pair=_pair_heads(g_attn_out).reshape(DEPTH, Q_PER_KV, 1, LANES),
        sink_rows=jnp.tile(attn_sinks, (1, SEQ_PER_STEP))[:, :, None],
        g_rows=jnp.tile(g_exp, (1, SEQ_PER_STEP, 1)),
    )
    w = {name: [(arr, l) for l in range(DEPTH)] for name, arr in stacked.items()}
    w['ffn1'] = [(ffn1_w_in, ffn1_w_out, l) for l in range(DEPTH)]
    w['ffn2'] = [(ffn2_w_in, ffn2_w_out, l) for l in range(DEPTH)]
    n_ln = ln_g.shape[1]
    ln_g3 = ln_g.reshape(DEPTH * n_ln, 1, D_MODEL)
    ln_b3 = ln_b.reshape(DEPTH * n_ln, 1, D_MODEL)
    w['ln'] = [[((ln_g3, l * n_ln + i), (ln_b3, l * n_ln + i)) for i in range(n_ln)] for l in range(DEPTH)]
    w['own_rows'] = jnp.tile(_expand_heads(jnp.ones((ATTN_WIDTH,), F32)).reshape(N_HEADS, LANES), (SEQ_PER_STEP, 1))
    return w


def _prompt_mixer(u, q, kv, l, w, batch, seq):
    ssm_n, hre, him, step_tables = _ssm_prompt_call(u, w['ssm_params'][l], w['a_flat'][l], w['d_1'][l], w['gluw'][l],
                                                    w['glub'][l], w['g_ssm'][l], batch=batch, seq=seq,
                                                    name=f"p_ssm_{l}")
    att_n = _attn_prompt_call(q, kv, w['sinks'][l], w['g_pair'][l], batch=batch, seq=seq, name=f"p_attn_{l}")
    kvw = kv.reshape(batch, seq, 2 * KV_WIDTH)[:, -WINDOW:].reshape(batch, WINDOW, 2, N_KV_HEADS, HEAD_DIM)
    return (ssm_n, att_n, hre.reshape(batch, N_SSM_GROUPS, SSM_STATE), him.reshape(batch, N_SSM_GROUPS, SSM_STATE),
            kvw[:, :, 0], kvw[:, :, 1]), step_tables


def _sample_mixer(u, q, kv, l, w, step_tables, h0_re, h0_im, k_buf, v_buf):
    n_seq = kv.shape[0]
    ssm_n, hre, him = _ssm_sample_call(u, h0_re.reshape(n_seq, N_STATES), h0_im.reshape(n_seq, N_STATES),
                                       step_tables, w['a_flat'][l], w['d_1'][l], w['gluw'][l], w['glub'][l],
                                       w['g_ssm'][l], name=f"s_ssm_{l}")
    att, kn, vn = _attn_sample_call(q.reshape(n_seq * N_HEADS, LANES), kv, k_buf, v_buf, l,
                                    w['sink_rows'][l], w['g_rows'][l], w['own_rows'], name=f"s_attn_{l}")
    untranspose = lambda t: jnp.transpose(t.reshape(n_seq, N_KV_HEADS, HEAD_DIM, WINDOW), (0, 3, 1, 2))
    return (ssm_n, att.reshape(n_seq, Q_EXP), hre.reshape(n_seq, N_SSM_GROUPS, SSM_STATE),
            him.reshape(n_seq, N_SSM_GROUPS, SSM_STATE), untranspose(kn), untranspose(vn))


def _trunks(x_prompt, x_sample, h0_re, h0_im, k_buf, v_buf, w, *, tm):
    Bn, L, _ = x_prompt.shape
    n_seq = x_sample.shape[0]
    xp = x_prompt.reshape(Bn * L, D_MODEL)
    xs = x_sample.reshape(n_seq, D_MODEL)
    outs_p, outs_s = [], []
    mix_p = mix_s = None
    for l in range(DEPTH + 1):
        if l > 0:
            (xp,), (xs,) = _rows_call(xp, w['ffn2'][l - 1], w['ln'][l - 1][2], mix=mix_p,
                                      sample=dict(x=xs, mix=mix_s), tm=tm, name=f"mix_ffn2_{l - 1}")
        if l == DEPTH:
            break
        (xp, u, q, kv), (xs, us, qs, kvs) = _rows_call(xp, w['ffn1'][l], w['ln'][l][0], proj=w['w_in'][l],
                                                       sample=dict(x=xs), tm=tm,
                                                       name=f"ffn1_{l}")
        (ssm_p, att_p, *state_p), step_tables = _prompt_mixer(u, q, kv, l, w, Bn, L)
        ssm_s, att_s, *state_s = _sample_mixer(us, qs, kvs, l, w, step_tables, h0_re[l], h0_im[l], k_buf, v_buf)
        mix_p = (ssm_p, att_p, w['wo_s'][l], w['wo_a'][l], w['ln'][l][1][0], w['ln'][l][1][1])
        mix_s = (ssm_s, att_s)
        outs_p.append(state_p)
        outs_s.append(state_s)
    stack = lambda outs: tuple(jnp.stack([o[i] for o in outs]) for i in (2, 3, 0, 1))
    return (xp.reshape(Bn, L, D_MODEL), xs.reshape(n_seq, 1, D_MODEL)), stack(outs_p), stack(outs_s)


def kernel(x_prompt, x_sample, cache_k_win, cache_v_win, state_ssm_re, state_ssm_im, ln_g, ln_b, ffn1_w_in, ffn1_w_out, ffn2_w_in, ffn2_w_out, w_in, ssm_lam_re, ssm_lam_im, ssm_log_dt, ssm_b_re, ssm_b_im, ssm_c_re, ssm_c_im, ssm_d, glu_w, glu_b, attn_sinks, g_ssm_out, g_attn_out, w_out):
    w = _prep_weights(ln_g, ln_b, ffn1_w_in, ffn1_w_out, ffn2_w_in, ffn2_w_out, w_in, ssm_lam_re, ssm_lam_im,
                      ssm_log_dt, ssm_b_re, ssm_b_im, ssm_c_re, ssm_c_im, ssm_d, glu_w, glu_b, attn_sinks,
                      g_ssm_out, g_attn_out, w_out)
    n_seq = x_sample.shape[0]
    transposed = lambda c: jnp.transpose(c, (0, 1, 3, 4, 2)).reshape(DEPTH, n_seq, N_KV_HEADS * HEAD_DIM, WINDOW)
    (y_prompt, y_sample), (kp, vp, hrp, hip), (ks_, vs_, hrs, his) = _trunks(
        x_prompt, x_sample, state_ssm_re, state_ssm_im, transposed(cache_k_win), transposed(cache_v_win), w, tm=512)
    return (y_prompt, y_sample, kp, vp, hrp, hip, ks_, vs_, hrs, his)
```

```python
import functools

import jax
import jax.numpy as jnp
import numpy as np
from jax import lax
from jax.experimental import pallas as pl
from jax.experimental.pallas import tpu as pltpu

D_MODEL = 1024
DEPTH = 2
SSM_WIDTH = 512
SSM_GROUP = 16
N_SSM_GROUPS = 32
SSM_STATE = 64
N_STATES = N_SSM_GROUPS * SSM_STATE
ATTN_WIDTH = 512
HEAD_DIM = 64
N_HEADS = 8
N_KV_HEADS = 2
Q_PER_KV = 4
KV_WIDTH = 128
WINDOW = 128
ATTN_SCALE = HEAD_DIM ** -0.5
D_FF = 2816
ALPHA = (2.0 * DEPTH) ** 0.25
LN_EPS = 1e-5
RMS_EPS = 1e-6

ROW_GROUP = 256
W_STAGE_SLOTS = 4
W_GU_STAGE_ROWS = 64
W_DN_STAGE_ROWS = 352
LANES = 128
SUBLANES = 8
VMEM_LIMIT_BYTES = 56 * 1024 * 1024
N_SLAB = SSM_WIDTH // LANES
GROUPS_PER_SLAB = LANES // SSM_GROUP
SLAB_STATES = GROUPS_PER_SLAB * SSM_STATE
Q_EXP = N_HEADS * LANES
SSM_CHUNK = 8
SSM_ROWS = 2048
ATTN_BLOCKS = 16
ATTN_HEAD_ORDER = (0, 2, 1, 3)
MASKED = -1e30

F32 = jnp.float32
BF16 = jnp.bfloat16


def _layer_norm(r, g, b):
    mu = jnp.mean(r, axis=-1, keepdims=True)
    c = r - mu
    var = jnp.mean(c * c, axis=-1, keepdims=True)
    return c * lax.rsqrt(var + LN_EPS) * g + b


def _rms_norm(y, g):
    return y * lax.rsqrt(jnp.mean(y * y, axis=-1, keepdims=True) + RMS_EPS) * g


def _const(c, n_grid=1):
    if isinstance(c, tuple):
        arr, idx = c
        shape = (None,) + arr.shape[1:]
        index = (idx,) + (0,) * (arr.ndim - 1)
    else:
        arr, shape, index = c, c.shape, (0,) * c.ndim
    imap = (lambda i: index) if n_grid == 1 else (lambda i, j: index)
    return arr, pl.BlockSpec(shape, imap, pipeline_mode=pl.Buffered(1))


def _consts(cs, n_grid=1):
    arrs, specs = zip(*[_const(c, n_grid) for c in cs])
    return list(arrs), list(specs)


def _load_cast(src_hbm, layer, dst_ref, stage_ref, sem_ref):
    n_slots, rows = stage_ref.shape[:2]
    n_chunks = dst_ref.shape[0] // rows

    def copy(c):
        return pltpu.make_async_copy(src_hbm.at[layer, pl.ds(c * rows, rows), :], stage_ref.at[c % n_slots],
                                     sem_ref.at[c % n_slots])

    for c in range(min(n_slots - 1, n_chunks)):
        copy(c).start()
    for c in range(n_chunks):
        copy(c).wait()
        dst_ref[pl.ds(c * rows, rows), :] = stage_ref[c % n_slots].astype(BF16)
        if c + n_slots - 1 < n_chunks:
            copy(c + n_slots - 1).start()


def _rows_kernel(*refs, has_mix, has_proj, has_sample, n_sub, layer):
    it = iter(refs)
    take = lambda n: tuple(next(it) for _ in range(n))
    (x_ref,) = take(1)
    if has_mix:
        ssm_ref, att_ref, wo_s_ref, wo_a_ref, gm_ref, bm_ref = take(6)
    wgu_hbm, wdn_hbm, g_ref, b_ref = take(4)
    wp_ref = take(1)[0] if has_proj else None
    if has_sample:
        (xs_ref,) = take(1)
        if has_mix:
            ssm_s_ref, att_s_ref, pack_ref = take(3)
        pad_ref = take(1)[0] if has_proj else None
    n_out = 4 if has_proj else 1
    outs = take(n_out)
    outs_s = take(n_out) if has_sample else None
    wgu_ref, wdn_ref, stage_gu, stage_dn, sem_gu, sem_dn = take(6)

    @pl.when(pl.program_id(0) == 0)
    def _():
        _load_cast(wgu_hbm, layer, wgu_ref, stage_gu, sem_gu)
        _load_cast(wdn_hbm, layer, wdn_ref, stage_dn, sem_dn)

    def mix_inputs(ssm_r, att_r, rows):
        ssm = jnp.concatenate([ssm_r[j, rows, :] for j in range(N_SLAB)], axis=1).astype(BF16)
        return ssm, att_r[rows, :].astype(BF16)

    def run(xs, mixes, wo_a, wp):
        if has_mix:
            ms = [jnp.dot(ssm, wo_s_ref[...], preferred_element_type=F32)
                  + jnp.dot(att, wo_a[...], preferred_element_type=F32) for ssm, att in mixes]
            xs = [_layer_norm(ALPHA * x + m, gm_ref[...], bm_ref[...]) for x, m in zip(xs, ms)]
        gus = [jnp.dot(x.astype(BF16), wgu_ref[...], preferred_element_type=F32) for x in xs]
        hs = [(gu[:, :D_FF] * jax.nn.sigmoid(gu[:, :D_FF]) * gu[:, D_FF:]).astype(BF16) for gu in gus]
        ys = [jnp.dot(h, wdn_ref[...], preferred_element_type=F32) for h in hs]
        xs = [_layer_norm(ALPHA * x + 0.5 * y, g_ref[...], b_ref[...]) for x, y in zip(xs, ys)]
        if has_proj:
            zs = [jnp.dot(x.astype(BF16), wp[...], preferred_element_type=F32) for x in xs]
        else:
            zs = [None] * len(xs)
        return xs, zs

    def store(out_refs, rows, x, z):
        out_refs[0][rows, :] = x
        if has_proj:
            u_ref, q_ref, kv_ref = out_refs[1:]
            q_cols = q_ref.shape[1]
            for j in range(N_SLAB):
                u_ref[j, rows, :] = z[:, j * LANES:(j + 1) * LANES]
            q_ref[rows, :] = z[:, SSM_WIDTH:SSM_WIDTH + q_cols].astype(q_ref.dtype)
            kv_ref[rows, :] = z[:, SSM_WIDTH + q_cols:]

    sub = x_ref.shape[0] // n_sub
    groups = [slice(r * sub, (r + 1) * sub) for r in range(n_sub)]
    mixes = [mix_inputs(ssm_ref, att_ref, rows) for rows in groups] if has_mix else None
    xs, zs = run([x_ref[rows, :] for rows in groups], mixes, wo_a_ref if has_mix else None, wp_ref)
    for rows, x, z in zip(groups, xs, zs):
        store(outs, rows, x, z)

    if has_sample:
        @pl.when(pl.program_id(0) == pl.num_programs(0) - 1)
        def _():
            every = slice(None)
            mixes_s = None
            if has_mix:
                ssm, att_padded = mix_inputs(ssm_s_ref, att_s_ref, every)
                att = jnp.dot(att_padded, pack_ref[...], preferred_element_type=F32).astype(BF16)
                mixes_s = [(ssm, att)]
            xs_s, zs_s = run([xs_ref[...]], mixes_s, wo_a_ref if has_mix else None, wp_ref)
            z = zs_s[0]
            if has_proj:
                q = z[:, SSM_WIDTH:SSM_WIDTH + ATTN_WIDTH].astype(BF16)
                z = jnp.concatenate([z[:, :SSM_WIDTH], jnp.dot(q, pad_ref[...], preferred_element_type=F32),
                                     z[:, SSM_WIDTH + ATTN_WIDTH:]], axis=1)
            store(outs_s, every, xs_s[0], z)


def _head_selectors():
    h = np.arange(ATTN_WIDTH) // HEAD_DIM
    d = np.arange(ATTN_WIDTH) % HEAD_DIM
    kvh, pr = h // Q_PER_KV, h % Q_PER_KV
    padded_col = h * LANES + kvh * HEAD_DIM + d
    packed_col = pr * LANES + kvh * HEAD_DIM + d
    pad = np.zeros((ATTN_WIDTH, Q_EXP), np.float32)
    pad[np.arange(ATTN_WIDTH), padded_col] = 1.0
    pack = np.zeros((Q_EXP, ATTN_WIDTH), np.float32)
    pack[padded_col, packed_col] = 1.0
    return jnp.asarray(pad, BF16), jnp.asarray(pack, BF16)


def _rows_call(x, ffn, ln, mix=None, proj=None, sample=None, *, tm, name):
    M = x.shape[0]
    assert M % tm == 0
    row = lambda w: pl.BlockSpec((tm, w), lambda i: (i, 0))
    slab = lambda n: pl.BlockSpec((n, tm, LANES), lambda i: (0, i, 0))
    whole = lambda a: pl.BlockSpec(a.shape, lambda i: (0,) * len(a.shape))
    args = [x]
    specs = [row(D_MODEL)]
    if mix is not None:
        ssm_n, att_n = mix[:2]
        arrs, cspecs = _consts(mix[2:])
        args += [ssm_n, att_n] + arrs
        specs += [slab(N_SLAB), row(att_n.shape[1])] + cspecs
    w_gu, w_dn, layer = ffn
    arrs, cspecs = _consts(list(ln))
    args += [w_gu, w_dn] + arrs
    specs += [pl.BlockSpec(memory_space=pl.ANY), pl.BlockSpec(memory_space=pl.ANY)] + cspecs

    def proj_outputs(rows, q_cols, q_dtype):
        return [jax.ShapeDtypeStruct((N_SLAB, rows, LANES), F32), jax.ShapeDtypeStruct((rows, q_cols), q_dtype),
                jax.ShapeDtypeStruct((rows, 2 * KV_WIDTH), F32)]

    out_shape = [jax.ShapeDtypeStruct((M, D_MODEL), F32)]
    out_specs = [row(D_MODEL)]
    if proj is not None:
        arr, cspec = _const(proj)
        args.append(arr)
        specs.append(cspec)
        q_cols = arr.shape[-1] - SSM_WIDTH - 2 * KV_WIDTH
        out_shape += proj_outputs(M, q_cols, BF16)
        out_specs += [slab(N_SLAB), row(q_cols), row(2 * KV_WIDTH)]
    n_main = len(out_shape)
    if sample is not None:
        xs = sample['x']
        args.append(xs)
        specs.append(whole(xs))
        pad, pack = _head_selectors()
        if mix is not None:
            ssm_s, att_s = sample['mix']
            arr, cspec = _const(pack)
            args += [ssm_s, att_s, arr]
            specs += [whole(ssm_s), whole(att_s), cspec]
        sample_out = [jax.ShapeDtypeStruct(xs.shape, F32)]
        if proj is not None:
            arr, cspec = _const(pad)
            args.append(arr)
            specs.append(cspec)
            sample_out += proj_outputs(xs.shape[0], Q_EXP, F32)
        out_shape += sample_out
        out_specs += [whole(o) for o in sample_out]
    outs = pl.pallas_call(
        functools.partial(_rows_kernel, has_mix=mix is not None, has_proj=proj is not None,
                          has_sample=sample is not None, n_sub=max(1, tm // ROW_GROUP), layer=layer),
        grid=(M // tm,),
        in_specs=specs,
        out_specs=out_specs,
        out_shape=out_shape,
        scratch_shapes=[pltpu.VMEM(w_gu.shape[1:], BF16), pltpu.VMEM(w_dn.shape[1:], BF16),
                        pltpu.VMEM((W_STAGE_SLOTS, W_GU_STAGE_ROWS, w_gu.shape[2]), F32),
                        pltpu.VMEM((W_STAGE_SLOTS, W_DN_STAGE_ROWS, w_dn.shape[2]), F32),
                        pltpu.SemaphoreType.DMA((W_STAGE_SLOTS,)), pltpu.SemaphoreType.DMA((W_STAGE_SLOTS,))],
        compiler_params=pltpu.CompilerParams(dimension_semantics=("arbitrary",),
                                             vmem_limit_bytes=VMEM_LIMIT_BYTES),
        name=name,
    )(*args)
    return outs[:n_main], (outs[n_main:] if sample is not None else None)


def _cmul(xr, xi, yr, yi):
    return xr * yr - xi * yi, xr * yi + xi * yr


def _build_ssm_tables(p_ref, af_ref, wst_ref, cat_ref, strip_ref, ak_ref, tr_ref):
    t1 = SSM_CHUNK
    a_re, a_im, lam_re, lam_im, b_re, b_im, c_re, c_im = (p_ref[k] for k in range(8))
    num_re, num_im = a_re - 1.0, a_im
    den = lam_re * lam_re + lam_im * lam_im
    f_re = (num_re * lam_re + num_im * lam_im) / den
    f_im = (num_im * lam_re - num_re * lam_im) / den
    bb_re, bb_im = _cmul(f_re, f_im, b_re, b_im)

    def same_group(shape, row_div, col_div):
        r = lax.broadcasted_iota(jnp.int32, shape, 0) // row_div
        c = lax.broadcasted_iota(jnp.int32, shape, 1) // col_div
        return (r == c).astype(F32)

    m_state = same_group((LANES, SLAB_STATES), SSM_GROUP, SSM_STATE)
    m_chan = 0.5 * same_group((LANES, LANES), SSM_GROUP, SSM_GROUP)

    def expand(x):
        return (jnp.concatenate([x] * (SLAB_STATES // LANES), axis=1) * m_state).astype(BF16)

    def dot_nt(a, b):
        return lax.dot_general(a, b, (((1,), (1,)), ((), ())), precision=lax.Precision.HIGHEST,
                               preferred_element_type=F32)

    pw_re, pw_im = jnp.ones_like(a_re), jnp.zeros_like(a_re)
    kd = []
    for l in range(t1):
        ab_re, ab_im = _cmul(pw_re, pw_im, bb_re, bb_im)
        s = t1 - 1 - l
        kd_l = []
        for j in range(N_SLAB):
            rows = slice(j * LANES, (j + 1) * LANES)
            wst_ref[j, s * LANES:(s + 1) * LANES, :SLAB_STATES] = expand(ab_re[rows])
            wst_ref[j, s * LANES:(s + 1) * LANES, SLAB_STATES:] = expand(ab_im[rows])
            k = dot_nt(ab_re[rows], c_re[rows]) - dot_nt(ab_im[rows], c_im[rows])
            kd_l.append((k * m_chan).astype(BF16))
        kd.append(kd_l)
        pw_re, pw_im = _cmul(pw_re, pw_im, a_re, a_im)
        cf_re = c_re * pw_re - c_im * pw_im
        cf_im = -(c_re * pw_im + c_im * pw_re)
        for j in range(N_SLAB):
            rows = slice(j * LANES, (j + 1) * LANES)
            cat_ref[j, l * LANES:(l + 1) * LANES, :SLAB_STATES] = expand(cf_re[rows])
            cat_ref[j, l * LANES:(l + 1) * LANES, SLAB_STATES:] = expand(cf_im[rows])
    for j in range(N_SLAB):
        for rho in range(t1):
            for c in range(2):
                lag = t1 - 2 - rho + c
                blk = kd[lag][j] if lag >= 0 else jnp.zeros((LANES, LANES), BF16)
                strip_ref[j, rho * LANES:(rho + 1) * LANES, c * LANES:(c + 1) * LANES] = blk

    f_re, f_im = af_ref[0], af_ref[1]
    base_re, base_im = f_re, f_im
    for _ in range(t1 - 1):
        base_re, base_im = _cmul(base_re, base_im, f_re, f_im)
    row = lax.broadcasted_iota(jnp.int32, (SUBLANES, N_STATES), 0)
    pws = [(base_re, base_im)]
    for _ in range(SUBLANES - 1):
        pws.append(_cmul(pws[-1][0], pws[-1][1], base_re, base_im))
    for n, k in enumerate((1, 2, 4)):
        for part in range(2):
            ak_ref[2 * n + part] = jnp.where(row >= k, jnp.broadcast_to(pws[k - 1][part], row.shape), 0.0)
    for part in range(2):
        acc = jnp.zeros(row.shape, F32)
        for r in range(SUBLANES):
            acc = jnp.where(row == r, jnp.broadcast_to(pws[r][part], row.shape), acc)
        tr_ref[part] = acc


def _ssm_table_scratch():
    t1 = SSM_CHUNK
    return [pltpu.VMEM((N_SLAB, t1 * LANES, 2 * SLAB_STATES), BF16),
            pltpu.VMEM((N_SLAB, t1 * LANES, 2 * SLAB_STATES), BF16),
            pltpu.VMEM((N_SLAB, t1 * LANES, 2 * LANES), BF16),
            pltpu.VMEM((6, SUBLANES, N_STATES), F32),
            pltpu.VMEM((2, SUBLANES, N_STATES), F32)]


def _ssm_table_inputs(lam_re, lam_im, log_dt, b_re, b_im, c_re, c_im):
    dt = jnp.exp(log_dt)[..., None]
    mag = jnp.exp(lam_re * dt)
    a_re = mag * jnp.cos(lam_im * dt)
    a_im = mag * jnp.sin(lam_im * dt)
    rep = lambda x: jnp.repeat(x, SSM_GROUP, axis=1)
    flat = lambda x: x.reshape(DEPTH, SSM_WIDTH, SSM_STATE)
    params = jnp.stack([rep(a_re), rep(a_im), rep(lam_re), rep(lam_im),
                        flat(jnp.swapaxes(b_re, 2, 3)), flat(jnp.swapaxes(b_im, 2, 3)), flat(c_re), flat(c_im)],
                       axis=1)
    params = jnp.concatenate([params, params], axis=-1)
    a_flat = jnp.stack([a_re.reshape(DEPTH, 1, N_STATES), a_im.reshape(DEPTH, 1, N_STATES)], axis=1)
    return params, a_flat


def _glu_rms(y, gluw_ref, glub_ref, g_ref):
    g = jax.nn.gelu(y).astype(BF16)
    zz = jnp.dot(g, gluw_ref[...], preferred_element_type=F32) + glub_ref[...]
    o = zz[:, :SSM_WIDTH] * jax.nn.sigmoid(zz[:, SSM_WIDTH:])
    return _rms_norm(o, g_ref[...])


def _ssm_kernel(u_ref, p_ref, af_ref, d_ref, gluw_ref, glub_ref, g_ref,
                o_ref, hre_ref, him_ref, wst1_ref, cat1_ref, kd0_ref,
                wst_ref, cat_ref, strip_ref, ak_ref, tr_ref, ucb_ref, sre_ref, sim_ref, hcr_ref, hci_ref):
    t1 = SSM_CHUNK
    nc = SSM_ROWS // t1
    i = pl.program_id(1)

    @pl.when(jnp.logical_and(pl.program_id(0) == 0, i == 0))
    def _():
        _build_ssm_tables(p_ref, af_ref, wst_ref, cat_ref, strip_ref, ak_ref, tr_ref)
        wst1_ref[...] = wst_ref[:, (t1 - 1) * LANES:, :]
        cat1_ref[...] = cat_ref[:, :LANES, :]
        kd0_ref[...] = strip_ref[:, (t1 - 1) * LANES:, LANES:]

    @pl.when(i == 0)
    def _():
        hcr_ref[...] = jnp.zeros_like(hcr_ref)
        hci_ref[...] = jnp.zeros_like(hci_ref)

    for j in range(N_SLAB):
        for s in range(t1):
            ucb_ref[j, :, s * LANES:(s + 1) * LANES] = u_ref[j, pl.ds(s, nc, stride=t1), :].astype(BF16)

    for j in range(N_SLAB):
        st = jnp.dot(ucb_ref[j], wst_ref[j], preferred_element_type=F32)
        sre_ref[:, j * SLAB_STATES:(j + 1) * SLAB_STATES] = st[:, :SLAB_STATES]
        sim_ref[:, j * SLAB_STATES:(j + 1) * SLAB_STATES] = st[:, SLAB_STATES:]

    row = lax.broadcasted_iota(jnp.int32, (SUBLANES, SLAB_STATES), 0)
    for j in range(N_SLAB):
        cols = pl.ds(j * SLAB_STATES, SLAB_STATES)

        def body(gi, carry, cols=cols):
            hr, hi = carry
            r0 = pl.multiple_of(gi * SUBLANES, SUBLANES)
            xr = sre_ref[pl.ds(r0, SUBLANES), cols]
            xi = sim_ref[pl.ds(r0, SUBLANES), cols]
            for n, k in enumerate((1, 2, 4)):
                akr = ak_ref[2 * n, :, cols]
                aki = ak_ref[2 * n + 1, :, cols]
                sr = pltpu.roll(xr, k, 0)
                si = pltpu.roll(xi, k, 0)
                xr, xi = xr + akr * sr - aki * si, xi + akr * si + aki * sr
            tr = tr_ref[0, :, cols]
            ti = tr_ref[1, :, cols]
            er = xr + tr * hr - ti * hi
            ei = xi + tr * hi + ti * hr
            sre_ref[pl.ds(r0, SUBLANES), cols] = jnp.where(row >= 1, pltpu.roll(er, 1, 0), hr)
            sim_ref[pl.ds(r0, SUBLANES), cols] = jnp.where(row >= 1, pltpu.roll(ei, 1, 0), hi)
            last = SUBLANES - 1
            return (jnp.broadcast_to(er[last:last + 1], er.shape), jnp.broadcast_to(ei[last:last + 1], ei.shape))

        hr, hi = lax.fori_loop(0, nc // SUBLANES, body, (hcr_ref[:, cols], hci_ref[:, cols]), unroll=True)
        hcr_ref[:, cols] = hr
        hci_ref[:, cols] = hi

    hre_ref[0] = hcr_ref[...]
    him_ref[0] = hci_ref[...]

    ys = []
    for j in range(N_SLAB):
        cols = slice(j * SLAB_STATES, (j + 1) * SLAB_STATES)
        hprev = jnp.concatenate([sre_ref[:, cols], sim_ref[:, cols]], axis=1).astype(BF16)
        y = lax.dot_general(hprev, cat_ref[j], (((1,), (1,)), ((), ())), preferred_element_type=F32)
        parts = []
        for p2 in range(t1 // 2):
            kk = (2 * p2 + 2) * LANES
            yi = jnp.dot(ucb_ref[j, :, :kk], strip_ref[j, (t1 - 2 - 2 * p2) * LANES:, :],
                         preferred_element_type=F32)
            parts.append(y[:, 2 * p2 * LANES:(2 * p2 + 2) * LANES] + yi)
        ys.append(jnp.concatenate(parts, axis=1))

    for t in range(t1):
        y = jnp.concatenate([ys[j][:, t * LANES:(t + 1) * LANES] + d_ref[j] * u_ref[j, pl.ds(t, nc, stride=t1), :]
                             for j in range(N_SLAB)], axis=1)
        n = _glu_rms(y, gluw_ref, glub_ref, g_ref)
        for j in range(N_SLAB):
            o_ref[j, pl.ds(t, nc, stride=t1), :] = n[:, j * LANES:(j + 1) * LANES]


def _ssm_prompt_call(u, params, a_flat, d_t, gluw, glub, g, *, batch, seq, name):
    t1 = SSM_CHUNK
    nc = SSM_ROWS // t1
    steps = seq // SSM_ROWS
    blk = pl.BlockSpec((N_SLAB, SSM_ROWS, LANES), lambda b, i: (0, b * steps + i, 0))
    st = pl.BlockSpec((1, SUBLANES, N_STATES), lambda b, i: (b, 0, 0))
    consts, cspecs = _consts([params, a_flat, d_t, gluw, glub, g], 2)
    small = [jax.ShapeDtypeStruct((N_SLAB, LANES, 2 * SLAB_STATES), BF16),
             jax.ShapeDtypeStruct((N_SLAB, LANES, 2 * SLAB_STATES), BF16),
             jax.ShapeDtypeStruct((N_SLAB, LANES, LANES), BF16)]
    whole = lambda a: pl.BlockSpec(a.shape, lambda b, i: (0,) * len(a.shape))
    out, hre, him, *step_tables = pl.pallas_call(
        _ssm_kernel,
        grid=(batch, steps),
        in_specs=[blk] + cspecs,
        out_specs=[blk, st, st] + [whole(s) for s in small],
        out_shape=[jax.ShapeDtypeStruct(u.shape, F32),
                   jax.ShapeDtypeStruct((batch, SUBLANES, N_STATES), F32),
                   jax.ShapeDtypeStruct((batch, SUBLANES, N_STATES), F32)] + small,
        scratch_shapes=_ssm_table_scratch() + [pltpu.VMEM((N_SLAB, nc, t1 * LANES), BF16),
                                               pltpu.VMEM((nc, N_STATES), F32),
                                               pltpu.VMEM((nc, N_STATES), F32),
                                               pltpu.VMEM((SUBLANES, N_STATES), F32),
                                               pltpu.VMEM((SUBLANES, N_STATES), F32)],
        compiler_params=pltpu.CompilerParams(dimension_semantics=("arbitrary", "arbitrary"),
                                             vmem_limit_bytes=VMEM_LIMIT_BYTES),
        name=name,
    )(u, *consts)
    return out, hre[:, 0], him[:, 0], step_tables


def _ssm_sample_kernel(u_ref, h0r_ref, h0i_ref, wst_ref, cat_ref, kd_ref, a_ref, d_ref, gluw_ref, glub_ref, g_ref,
                       o_ref, hr_ref, hi_ref):
    ys = []
    for j in range(N_SLAB):
        cols = slice(j * SLAB_STATES, (j + 1) * SLAB_STATES)
        uf = u_ref[j]
        ub = uf.astype(BF16)
        st = jnp.dot(ub, wst_ref[j], preferred_element_type=F32)
        h0r = h0r_ref[:, cols]
        h0i = h0i_ref[:, cols]
        ar = a_ref[0, :, cols]
        ai = a_ref[1, :, cols]
        hr_ref[:, cols] = ar * h0r - ai * h0i + st[:, :SLAB_STATES]
        hi_ref[:, cols] = ar * h0i + ai * h0r + st[:, SLAB_STATES:]
        hcat = jnp.concatenate([h0r, h0i], axis=1).astype(BF16)
        y = lax.dot_general(hcat, cat_ref[j], (((1,), (1,)), ((), ())), preferred_element_type=F32)
        y = y + jnp.dot(ub, kd_ref[j], preferred_element_type=F32)
        ys.append(y + d_ref[j] * uf)
    n = _glu_rms(jnp.concatenate(ys, axis=1), gluw_ref, glub_ref, g_ref)
    for j in range(N_SLAB):
        o_ref[j] = n[:, j * LANES:(j + 1) * LANES]


def _ssm_sample_call(u, h0r, h0i, step_tables, a_flat, d_t, gluw, glub, g, *, name):
    n_seq = u.shape[1]
    whole = lambda a: pl.BlockSpec(a.shape, lambda i: (0,) * len(a.shape))
    outs = [jax.ShapeDtypeStruct(u.shape, F32),
            jax.ShapeDtypeStruct((n_seq, N_STATES), F32),
            jax.ShapeDtypeStruct((n_seq, N_STATES), F32)]
    consts, cspecs = _consts(list(step_tables) + [a_flat, d_t, gluw, glub, g])
    return pl.pallas_call(
        _ssm_sample_kernel,
        grid=(1,),
        in_specs=[whole(u), whole(h0r), whole(h0i)] + cspecs,
        out_specs=[whole(o) for o in outs],
        out_shape=outs,
        compiler_params=pltpu.CompilerParams(dimension_semantics=("arbitrary",),
                                             vmem_limit_bytes=VMEM_LIMIT_BYTES),
        name=name,
    )(u, h0r, h0i, *consts)


def _attn_kernel(sink_ref, q_ref, kvc_ref, kvp_ref, g_ref, o_ref):
    i = pl.program_id(1)
    kj = lax.broadcasted_iota(jnp.int32, (2 * WINDOW, WINDOW), 0)
    qi = lax.broadcasted_iota(jnp.int32, (2 * WINDOW, WINDOW), 1)
    band = (kj >= qi) & (kj <= qi + WINDOW)
    low = qi < HEAD_DIM
    for sb in range(ATTN_BLOCKS):
        blk = slice(sb * WINDOW, (sb + 1) * WINDOW)
        kc = kvc_ref[blk, :]
        if sb == 0:
            kp = kvp_ref[...]
            valid = band & ((i > 0) | (kj >= WINDOW))
        else:
            kp = kvc_ref[(sb - 1) * WINDOW:sb * WINDOW, :]
            valid = band
        valid2 = jnp.concatenate([valid, valid], axis=1)
        kcat = jnp.concatenate([kp[:, :KV_WIDTH], kc[:, :KV_WIDTH]], axis=0) * ATTN_SCALE
        swap = pltpu.roll(kcat, HEAD_DIM, 1)
        zero = jnp.zeros_like(kcat)
        k_sel = [[jnp.where(low, kcat, zero).astype(BF16), jnp.where(low, zero, swap).astype(BF16)],
                 [jnp.where(low, swap, zero).astype(BF16), jnp.where(low, zero, kcat).astype(BF16)]]
        v_t = jnp.concatenate([kp[:, KV_WIDTH:], kc[:, KV_WIDTH:]], axis=0).T.astype(BF16)
        parts = []
        for kvh in range(N_KV_HEADS):
            q_pair = jnp.concatenate([q_ref[blk, (2 * kvh) * LANES:(2 * kvh + 1) * LANES],
                                      q_ref[blk, (2 * kvh + 1) * LANES:(2 * kvh + 2) * LANES]], axis=0)
            for parity in range(2):
                s_t = lax.dot_general(k_sel[kvh][parity], q_pair, (((1,), (1,)), ((), ())),
                                      preferred_element_type=F32)
                parts.append(jnp.where(valid2, s_t, MASKED))
        s_all = jnp.concatenate(parts, axis=1)
        sink = sink_ref[...]
        m = jnp.maximum(jnp.max(s_all, axis=0, keepdims=True), sink)
        p_all = jnp.exp(s_all - m)
        inv_den = 1.0 / (jnp.sum(p_all, axis=0, keepdims=True) + jnp.exp(sink - m))
        p_all = p_all.astype(BF16)
        half = Q_PER_KV * WINDOW
        o_kv = [jnp.dot(v_t[kvh * HEAD_DIM:(kvh + 1) * HEAD_DIM], p_all[:, kvh * half:(kvh + 1) * half],
                        preferred_element_type=F32) * inv_den[:, kvh * half:(kvh + 1) * half]
                for kvh in range(N_KV_HEADS)]
        tiles = []
        for pr in range(Q_PER_KV):
            lanes = slice(ATTN_HEAD_ORDER.index(pr) * WINDOW, (ATTN_HEAD_ORDER.index(pr) + 1) * WINDOW)
            tiles.append(jnp.concatenate([o_kv[0][:, lanes], o_kv[1][:, lanes]], axis=0))
        ss = None
        for tile in tiles:
            t = jnp.sum(tile * tile, axis=0, keepdims=True)
            ss = t if ss is None else ss + t
        inv = lax.rsqrt(ss / ATTN_WIDTH + RMS_EPS)
        for pr, tile in enumerate(tiles):
            o_ref[blk, pr * LANES:(pr + 1) * LANES] = ((tile * inv).T * g_ref[pr]).astype(o_ref.dtype)


def _attn_prompt_call(q, kv, sinks, g_perm, *, batch, seq, name):
    rows = ATTN_BLOCKS * WINDOW
    nb = seq // rows
    M = kv.shape[0]
    cur = lambda b, i: (b * nb + i, 0)
    prev = lambda b, i: ((b * nb + i) * ATTN_BLOCKS - jnp.minimum(i, 1), 0)
    (sinks, g_perm), (sink_spec, g_spec) = _consts([sinks, g_perm], 2)
    return pl.pallas_call(
        _attn_kernel,
        grid=(batch, nb),
        in_specs=[sink_spec,
                  pl.BlockSpec((rows, ATTN_WIDTH), cur),
                  pl.BlockSpec((rows, 2 * KV_WIDTH), cur),
                  pl.BlockSpec((WINDOW, 2 * KV_WIDTH), prev),
                  g_spec],
        out_specs=pl.BlockSpec((rows, ATTN_WIDTH), cur),
        out_shape=jax.ShapeDtypeStruct((M, ATTN_WIDTH), BF16),
        compiler_params=pltpu.CompilerParams(dimension_semantics=("arbitrary", "arbitrary"),
                                             vmem_limit_bytes=VMEM_LIMIT_BYTES),
        name=name,
    )(sinks, q, kv, kv, g_perm)


SEQ_PER_STEP = LANES // N_HEADS


def _attn_sample_kernel(q_ref, kv_ref, ckt_ref, cvt_ref, sink_ref, g_ref, own_ref, o_ref, kot_ref, vot_ref):
    nrow = SEQ_PER_STEP * N_HEADS
    qb = q_ref[...].astype(BF16)
    k_new = kv_ref[:, :KV_WIDTH]
    v_new = kv_ref[:, KV_WIDTH:]
    row_seq = lax.broadcasted_iota(jnp.int32, (nrow, LANES), 0) // N_HEADS
    lane_seq = lax.broadcasted_iota(jnp.int32, (nrow, LANES), 1) // N_HEADS
    lane = lax.broadcasted_iota(jnp.int32, (nrow, LANES), 1)
    pick = (lax.broadcasted_iota(jnp.int32, (nrow, SEQ_PER_STEP), 0) // N_HEADS
            == lax.broadcasted_iota(jnp.int32, (nrow, SEQ_PER_STEP), 1)).astype(BF16)

    qf = q_ref[...]
    heads_of = lambda a, n: a[n * N_HEADS:(n + 1) * N_HEADS]
    s = jnp.concatenate([jnp.dot(heads_of(qf, n).astype(BF16), ckt_ref[n].astype(BF16), preferred_element_type=F32)
                         for n in range(SEQ_PER_STEP)], axis=0) * ATTN_SCALE
    k_rows = jnp.dot(pick, k_new.astype(BF16), preferred_element_type=F32)
    s_new = jnp.sum(qb.astype(F32) * k_rows, axis=-1, keepdims=True) * ATTN_SCALE
    sink = sink_ref[...]
    m = jnp.maximum(jnp.maximum(jnp.max(s, axis=-1, keepdims=True), s_new), sink)
    p = jnp.exp(s - m)
    p_new = jnp.exp(s_new - m)
    inv = 1.0 / (jnp.sum(p, axis=-1, keepdims=True) + p_new + jnp.exp(sink - m))
    pn = p * inv
    o = jnp.concatenate([lax.dot_general(heads_of(pn, n).astype(BF16), cvt_ref[n].astype(BF16),
                                         (((1,), (1,)), ((), ())), preferred_element_type=F32)
                         for n in range(SEQ_PER_STEP)], axis=0)
    v_rows = jnp.dot(pick, v_new.astype(BF16), preferred_element_type=F32)
    o = o + (p_new * inv).astype(BF16).astype(F32) * v_rows

    hi = lax.Precision.HIGHEST
    row_sums = jnp.dot(o * o * own_ref[...], jnp.ones((LANES, LANES), F32), precision=hi, preferred_element_type=F32)
    ss = jnp.dot((row_seq == lane_seq).astype(F32), row_sums, precision=hi, preferred_element_type=F32)
    o_ref[...] = o * lax.rsqrt(ss / ATTN_WIDTH + RMS_EPS) * g_ref[...]

    pad = jnp.zeros((LANES - SEQ_PER_STEP, KV_WIDTH), F32)
    k_cols = jnp.concatenate([k_new, pad], axis=0).T
    v_cols = jnp.concatenate([v_new, pad], axis=0).T
    last = lane == WINDOW - 1
    for n in range(SEQ_PER_STEP):
        kot_ref[n] = jnp.where(last, pltpu.roll(k_cols, WINDOW - 1 - n, 1), pltpu.roll(ckt_ref[n], WINDOW - 1, 1))
        vot_ref[n] = jnp.where(last, pltpu.roll(v_cols, WINDOW - 1 - n, 1), pltpu.roll(cvt_ref[n], WINDOW - 1, 1))


def _attn_sample_call(q, kv, ck, cv, layer, sink_row, g_rows, own_rows, *, name):
    n_seq = kv.shape[0]
    assert n_seq % SEQ_PER_STEP == 0
    nrow = SEQ_PER_STEP * N_HEADS
    cblk = pl.BlockSpec((SEQ_PER_STEP, WINDOW, 2 * HEAD_DIM), lambda i: (i, 0, 0))
    cin = pl.BlockSpec((None, SEQ_PER_STEP, WINDOW, 2 * HEAD_DIM), lambda i: (layer, i, 0, 0))
    qblk = pl.BlockSpec((nrow, LANES), lambda i: (i, 0))
    consts, cspecs = _consts([sink_row, g_rows, own_rows])
    return pl.pallas_call(
        _attn_sample_kernel,
        grid=(n_seq // SEQ_PER_STEP,),
        in_specs=[qblk, pl.BlockSpec((SEQ_PER_STEP, 2 * KV_WIDTH), lambda i: (i, 0)), cin, cin] + cspecs,
        out_specs=[qblk, cblk, cblk],
        out_shape=[jax.ShapeDtypeStruct(q.shape, F32),
                   jax.ShapeDtypeStruct(ck.shape[1:], F32),
                   jax.ShapeDtypeStruct(cv.shape[1:], F32)],
        compiler_params=pltpu.CompilerParams(dimension_semantics=("arbitrary",),
                                             vmem_limit_bytes=VMEM_LIMIT_BYTES),
        name=name,
    )(q, kv, ck, cv, *consts)


def _expand_heads(a):
    lead = a.shape[:-1]
    a = a.reshape(lead + (N_KV_HEADS, Q_PER_KV, 1, HEAD_DIM))
    sel = jnp.eye(N_KV_HEADS, dtype=a.dtype).reshape(N_KV_HEADS, 1, N_KV_HEADS, 1)
    return (a * sel).reshape(lead + (Q_EXP,))


def _pair_heads(a):
    lead = a.shape[:-1]
    a = a.reshape(lead + (N_KV_HEADS, Q_PER_KV, HEAD_DIM))
    return jnp.swapaxes(a, -3, -2).reshape(lead + (ATTN_WIDTH,))


def _prep_weights(ln_g, ln_b, ffn1_w_in, ffn1_w_out, ffn2_w_in, ffn2_w_out, w_in, ssm_lam_re, ssm_lam_im, ssm_log_dt,
                  ssm_b_re, ssm_b_im, ssm_c_re, ssm_c_im, ssm_d, glu_w, glu_b, attn_sinks, g_ssm_out, g_attn_out,
                  w_out):
    rows_t = lambda a: jnp.swapaxes(a, 1, 2)
    params, a_flat = _ssm_table_inputs(ssm_lam_re, ssm_lam_im, ssm_log_dt, ssm_b_re, ssm_b_im, ssm_c_re, ssm_c_im)
    head_order = jnp.array([Q_PER_KV * kvh + o for kvh in range(N_KV_HEADS) for o in ATTN_HEAD_ORDER])
    g_exp = _expand_heads(g_attn_out).reshape(DEPTH, N_HEADS, LANES)
    w_in_b = w_in.astype(BF16)
    w_out_b = w_out.astype(BF16)
    stacked = dict(
        w_in=w_in_b,
        wo_s=w_out_b[:, :SSM_WIDTH],
        wo_a=rows_t(_pair_heads(rows_t(w_out_b[:, SSM_WIDTH:]))),
        ssm_params=params, a_flat=a_flat,
        d_1=ssm_d.reshape(DEPTH, N_SLAB, 1, LANES),
        gluw=glu_w.astype(BF16),
        glub=glu_b.reshape(DEPTH, 1, 2 * SSM_WIDTH),
        g_ssm=g_ssm_out.reshape(DEPTH, 1, SSM_WIDTH),
        sinks=jnp.repeat(attn_sinks[:, head_order], WINDOW, axis=1)[:, None, :],
        g_pair=_pair_heads(g_attn_out).reshape(DEPTH, Q_PER_KV, 1, LANES),
        sink_rows=jnp.tile(attn_sinks, (1, SEQ_PER_STEP))[:, :, None],
        g_rows=jnp.tile(g_exp, (1, SEQ_PER_STEP, 1)),
    )
    w = {name: [(arr, l) for l in range(DEPTH)] for name, arr in stacked.items()}
    w['ffn1'] = [(ffn1_w_in, ffn1_w_out, l) for l in range(DEPTH)]
    w['ffn2'] = [(ffn2_w_in, ffn2_w_out, l) for l in range(DEPTH)]
    n_ln = ln_g.shape[1]
    ln_g3 = ln_g.reshape(DEPTH * n_ln, 1, D_MODEL)
    ln_b3 = ln_b.reshape(DEPTH * n_ln, 1, D_MODEL)
    w['ln'] = [[((ln_g3, l * n_ln + i), (ln_b3, l * n_ln + i)) for i in range(n_ln)] for l in range(DEPTH)]
    w['own_rows'] = jnp.tile(_expand_heads(jnp.ones((ATTN_WIDTH,), F32)).reshape(N_HEADS, LANES), (SEQ_PER_STEP, 1))
    return w


def _prompt_mixer(u, q, kv, l, w, batch, seq):
    ssm_n, hre, him, step_tables = _ssm_prompt_call(u, w['ssm_params'][l], w['a_flat'][l], w['d_1'][l], w['gluw'][l],
                                                    w['glub'][l], w['g_ssm'][l], batch=batch, seq=seq,
                                                    name=f"p_ssm_{l}")
    att_n = _attn_prompt_call(q, kv, w['sinks'][l], w['g_pair'][l], batch=batch, seq=seq, name=f"p_attn_{l}")
    kvw = kv.reshape(batch, seq, 2 * KV_WIDTH)[:, -WINDOW:].reshape(batch, WINDOW, 2, N_KV_HEADS, HEAD_DIM)
    return (ssm_n, att_n, hre.reshape(batch, N_SSM_GROUPS, SSM_STATE), him.reshape(batch, N_SSM_GROUPS, SSM_STATE),
            kvw[:, :, 0], kvw[:, :, 1]), step_tables


def _sample_mixer(u, q, kv, l, w, step_tables, h0_re, h0_im, k_buf, v_buf):
    n_seq = kv.shape[0]
    ssm_n, hre, him = _ssm_sample_call(u, h0_re.reshape(n_seq, N_STATES), h0_im.reshape(n_seq, N_STATES),
                                       step_tables, w['a_flat'][l], w['d_1'][l], w['gluw'][l], w['glub'][l],
                                       w['g_ssm'][l], name=f"s_ssm_{l}")
    att, kn, vn = _attn_sample_call(q.reshape(n_seq * N_HEADS, LANES), kv, k_buf, v_buf, l,
                                    w['sink_rows'][l], w['g_rows'][l], w['own_rows'], name=f"s_attn_{l}")
    untranspose = lambda t: jnp.transpose(t.reshape(n_seq, N_KV_HEADS, HEAD_DIM, WINDOW), (0, 3, 1, 2))
    return (ssm_n, att.reshape(n_seq, Q_EXP), hre.reshape(n_seq, N_SSM_GROUPS, SSM_STATE),
            him.reshape(n_seq, N_SSM_GROUPS, SSM_STATE), untranspose(kn), untranspose(vn))


def _trunks(x_prompt, x_sample, h0_re, h0_im, k_buf, v_buf, w, *, tm):
    Bn, L, _ = x_prompt.shape
    n_seq = x_sample.shape[0]
    xp = x_prompt.reshape(Bn * L, D_MODEL)
    xs = x_sample.reshape(n_seq, D_MODEL)
    outs_p, outs_s = [], []
    mix_p = mix_s = None
    for l in range(DEPTH + 1):
        if l > 0:
            (xp,), (xs,) = _rows_call(xp, w['ffn2'][l - 1], w['ln'][l - 1][2], mix=mix_p,
                                      sample=dict(x=xs, mix=mix_s), tm=tm, name=f"mix_ffn2_{l - 1}")
        if l == DEPTH:
            break
        (xp, u, q, kv), (xs, us, qs, kvs) = _rows_call(xp, w['ffn1'][l], w['ln'][l][0], proj=w['w_in'][l],
                                                       sample=dict(x=xs), tm=tm,
                                                       name=f"ffn1_{l}")
        (ssm_p, att_p, *state_p), step_tables = _prompt_mixer(u, q, kv, l, w, Bn, L)
        ssm_s, att_s, *state_s = _sample_mixer(us, qs, kvs, l, w, step_tables, h0_re[l], h0_im[l], k_buf, v_buf)
        mix_p = (ssm_p, att_p, w['wo_s'][l], w['wo_a'][l], w['ln'][l][1][0], w['ln'][l][1][1])
        mix_s = (ssm_s, att_s)
        outs_p.append(state_p)
        outs_s.append(state_s)
    stack = lambda outs: tuple(jnp.stack([o[i] for o in outs]) for i in (2, 3, 0, 1))
    return (xp.reshape(Bn, L, D_MODEL), xs.reshape(n_seq, 1, D_MODEL)), stack(outs_p), stack(outs_s)


def kernel(x_prompt, x_sample, cache_k_win, cache_v_win, state_ssm_re, state_ssm_im, ln_g, ln_b, ffn1_w_in, ffn1_w_out, ffn2_w_in, ffn2_w_out, w_in, ssm_lam_re, ssm_lam_im, ssm_log_dt, ssm_b_re, ssm_b_im, ssm_c_re, ssm_c_im, ssm_d, glu_w, glu_b, attn_sinks, g_ssm_out, g_attn_out, w_out):
    w = _prep_weights(ln_g, ln_b, ffn1_w_in, ffn1_w_out, ffn2_w_in, ffn2_w_out, w_in, ssm_lam_re, ssm_lam_im,
                      ssm_log_dt, ssm_b_re, ssm_b_im, ssm_c_re, ssm_c_im, ssm_d, glu_w, glu_b, attn_sinks,
                      g_ssm_out, g_attn_out, w_out)
    n_seq = x_sample.shape[0]
    transposed = lambda c: jnp.transpose(c, (0, 1, 3, 4, 2)).reshape(DEPTH, n_seq, N_KV_HEADS * HEAD_DIM, WINDOW)
    (y_prompt, y_sample), (kp, vp, hrp, hip), (ks_, vs_, hrs, his) = _trunks(
        x_prompt, x_sample, state_ssm_re, state_ssm_im, transposed(cache_k_win), transposed(cache_v_win), w, tm=512)
    return (y_prompt, y_sample, kp, vp, hrp, hip, ks_, vs_, hrs, his)
```

```python
import functools

import jax
import jax.numpy as jnp
import numpy as np
from jax import lax
from jax.experimental import pallas as pl
from jax.experimental.pallas import tpu as pltpu

D_MODEL = 1024
DEPTH = 2
SSM_WIDTH = 512
SSM_GROUP = 16
N_SSM_GROUPS = 32
SSM_STATE = 64
N_STATES = N_SSM_GROUPS * SSM_STATE
ATTN_WIDTH = 512
HEAD_DIM = 64
N_HEADS = 8
N_KV_HEADS = 2
Q_PER_KV = 4
KV_WIDTH = 128
WINDOW = 128
ATTN_SCALE = HEAD_DIM ** -0.5
D_FF = 2816
ALPHA = (2.0 * DEPTH) ** 0.25
LN_EPS = 1e-5
RMS_EPS = 1e-6

ROW_GROUP = 256
W_STAGE_SLOTS = 8
W_GU_STAGE_ROWS = 32
W_DN_STAGE_ROWS = 176
LANES = 128
SUBLANES = 8
VMEM_LIMIT_BYTES = 56 * 1024 * 1024
N_SLAB = SSM_WIDTH // LANES
GROUPS_PER_SLAB = LANES // SSM_GROUP
SLAB_STATES = GROUPS_PER_SLAB * SSM_STATE
Q_EXP = N_HEADS * LANES
SSM_CHUNK = 8
SSM_ROWS = 2048
ATTN_BLOCKS = 16
ATTN_HEAD_ORDER = (0, 2, 1, 3)
MASKED = -1e30

F32 = jnp.float32
BF16 = jnp.bfloat16


def _layer_norm(r, g, b):
    mu = jnp.mean(r, axis=-1, keepdims=True)
    c = r - mu
    var = jnp.mean(c * c, axis=-1, keepdims=True)
    return c * lax.rsqrt(var + LN_EPS) * g + b


def _rms_norm(y, g):
    return y * lax.rsqrt(jnp.mean(y * y, axis=-1, keepdims=True) + RMS_EPS) * g


def _const(c, n_grid=1):
    if isinstance(c, tuple):
        arr, idx = c
        shape = (None,) + arr.shape[1:]
        index = (idx,) + (0,) * (arr.ndim - 1)
    else:
        arr, shape, index = c, c.shape, (0,) * c.ndim
    imap = (lambda i: index) if n_grid == 1 else (lambda i, j: index)
    return arr, pl.BlockSpec(shape, imap, pipeline_mode=pl.Buffered(1))


def _consts(cs, n_grid=1):
    arrs, specs = zip(*[_const(c, n_grid) for c in cs])
    return list(arrs), list(specs)


def _load_cast(src_hbm, layer, dst_ref, stage_ref, sem_ref):
    n_slots, rows = stage_ref.shape[:2]
    n_chunks = dst_ref.shape[0] // rows

    def copy(c):
        return pltpu.make_async_copy(src_hbm.at[layer, pl.ds(c * rows, rows), :], stage_ref.at[c % n_slots],
                                     sem_ref.at[c % n_slots])

    for c in range(min(n_slots - 1, n_chunks)):
        copy(c).start()
    for c in range(n_chunks):
        copy(c).wait()
        dst_ref[pl.ds(c * rows, rows), :] = stage_ref[c % n_slots].astype(BF16)
        if c + n_slots - 1 < n_chunks:
            copy(c + n_slots - 1).start()


def _rows_kernel(*refs, has_mix, has_proj, has_sample, n_sub, layer):
    it = iter(refs)
    take = lambda n: tuple(next(it) for _ in range(n))
    (x_ref,) = take(1)
    if has_mix:
        ssm_ref, att_ref, wo_s_ref, wo_a_ref, gm_ref, bm_ref = take(6)
    wgu_hbm, wdn_hbm, g_ref, b_ref = take(4)
    wp_ref = take(1)[0] if has_proj else None
    if has_sample:
        (xs_ref,) = take(1)
        if has_mix:
            ssm_s_ref, att_s_ref, pack_ref = take(3)
        pad_ref = take(1)[0] if has_proj else None
    n_out = 4 if has_proj else 1
    outs = take(n_out)
    outs_s = take(n_out) if has_sample else None
    wgu_ref, wdn_ref, stage_gu, stage_dn, sem_gu, sem_dn = take(6)

    @pl.when(pl.program_id(0) == 0)
    def _():
        _load_cast(wgu_hbm, layer, wgu_ref, stage_gu, sem_gu)
        _load_cast(wdn_hbm, layer, wdn_ref, stage_dn, sem_dn)

    def mix_inputs(ssm_r, att_r, rows):
        ssm = jnp.concatenate([ssm_r[j, rows, :] for j in range(N_SLAB)], axis=1).astype(BF16)
        return ssm, att_r[rows, :].astype(BF16)

    def run(xs, mixes, wo_a, wp):
        if has_mix:
            ms = [jnp.dot(ssm, wo_s_ref[...], preferred_element_type=F32)
                  + jnp.dot(att, wo_a[...], preferred_element_type=F32) for ssm, att in mixes]
            xs = [_layer_norm(ALPHA * x + m, gm_ref[...], bm_ref[...]) for x, m in zip(xs, ms)]
        gus = [jnp.dot(x.astype(BF16), wgu_ref[...], preferred_element_type=F32) for x in xs]
        hs = [(gu[:, :D_FF] * jax.nn.sigmoid(gu[:, :D_FF]) * gu[:, D_FF:]).astype(BF16) for gu in gus]
        ys = [jnp.dot(h, wdn_ref[...], preferred_element_type=F32) for h in hs]
        xs = [_layer_norm(ALPHA * x + 0.5 * y, g_ref[...], b_ref[...]) for x, y in zip(xs, ys)]
        if has_proj:
            zs = [jnp.dot(x.astype(BF16), wp[...], preferred_element_type=F32) for x in xs]
        else:
            zs = [None] * len(xs)
        return xs, zs

    def store(out_refs, rows, x, z):
        out_refs[0][rows, :] = x
        if has_proj:
            u_ref, q_ref, kv_ref = out_refs[1:]
            q_cols = q_ref.shape[1]
            for j in range(N_SLAB):
                u_ref[j, rows, :] = z[:, j * LANES:(j + 1) * LANES]
            q_ref[rows, :] = z[:, SSM_WIDTH:SSM_WIDTH + q_cols].astype(q_ref.dtype)
            kv_ref[rows, :] = z[:, SSM_WIDTH + q_cols:]

    sub = x_ref.shape[0] // n_sub
    groups = [slice(r * sub, (r + 1) * sub) for r in range(n_sub)]
    mixes = [mix_inputs(ssm_ref, att_ref, rows) for rows in groups] if has_mix else None
    xs, zs = run([x_ref[rows, :] for rows in groups], mixes, wo_a_ref if has_mix else None, wp_ref)
    for rows, x, z in zip(groups, xs, zs):
        store(outs, rows, x, z)

    if has_sample:
        @pl.when(pl.program_id(0) == pl.num_programs(0) - 1)
        def _():
            every = slice(None)
            mixes_s = None
            if has_mix:
                ssm, att_padded = mix_inputs(ssm_s_ref, att_s_ref, every)
                att = jnp.dot(att_padded, pack_ref[...], preferred_element_type=F32).astype(BF16)
                mixes_s = [(ssm, att)]
            xs_s, zs_s = run([xs_ref[...]], mixes_s, wo_a_ref if has_mix else None, wp_ref)
            z = zs_s[0]
            if has_proj:
                q = z[:, SSM_WIDTH:SSM_WIDTH + ATTN_WIDTH].astype(BF16)
                z = jnp.concatenate([z[:, :SSM_WIDTH], jnp.dot(q, pad_ref[...], preferred_element_type=F32),
                                     z[:, SSM_WIDTH + ATTN_WIDTH:]], axis=1)
            store(outs_s, every, xs_s[0], z)


def _head_selectors():
    h = np.arange(ATTN_WIDTH) // HEAD_DIM
    d = np.arange(ATTN_WIDTH) % HEAD_DIM
    kvh, pr = h // Q_PER_KV, h % Q_PER_KV
    padded_col = h * LANES + kvh * HEAD_DIM + d
    packed_col = pr * LANES + kvh * HEAD_DIM + d
    pad = np.zeros((ATTN_WIDTH, Q_EXP), np.float32)
    pad[np.arange(ATTN_WIDTH), padded_col] = 1.0
    pack = np.zeros((Q_EXP, ATTN_WIDTH), np.float32)
    pack[padded_col, packed_col] = 1.0
    return jnp.asarray(pad, BF16), jnp.asarray(pack, BF16)


def _rows_call(x, ffn, ln, mix=None, proj=None, sample=None, *, tm, name):
    M = x.shape[0]
    assert M % tm == 0
    row = lambda w: pl.BlockSpec((tm, w), lambda i: (i, 0))
    slab = lambda n: pl.BlockSpec((n, tm, LANES), lambda i: (0, i, 0))
    whole = lambda a: pl.BlockSpec(a.shape, lambda i: (0,) * len(a.shape))
    args = [x]
    specs = [row(D_MODEL)]
    if mix is not None:
        ssm_n, att_n = mix[:2]
        arrs, cspecs = _consts(mix[2:])
        args += [ssm_n, att_n] + arrs
        specs += [slab(N_SLAB), row(att_n.shape[1])] + cspecs
    w_gu, w_dn, layer = ffn
    arrs, cspecs = _consts(list(ln))
    args += [w_gu, w_dn] + arrs
    specs += [pl.BlockSpec(memory_space=pl.ANY), pl.BlockSpec(memory_space=pl.ANY)] + cspecs

    def proj_outputs(rows, q_cols, q_dtype):
        return [jax.ShapeDtypeStruct((N_SLAB, rows, LANES), F32), jax.ShapeDtypeStruct((rows, q_cols), q_dtype),
                jax.ShapeDtypeStruct((rows, 2 * KV_WIDTH), F32)]

    out_shape = [jax.ShapeDtypeStruct((M, D_MODEL), F32)]
    out_specs = [row(D_MODEL)]
    if proj is not None:
        arr, cspec = _const(proj)
        args.append(arr)
        specs.append(cspec)
        q_cols = arr.shape[-1] - SSM_WIDTH - 2 * KV_WIDTH
        out_shape += proj_outputs(M, q_cols, BF16)
        out_specs += [slab(N_SLAB), row(q_cols), row(2 * KV_WIDTH)]
    n_main = len(out_shape)
    if sample is not None:
        xs = sample['x']
        args.append(xs)
        specs.append(whole(xs))
        pad, pack = _head_selectors()
        if mix is not None:
            ssm_s, att_s = sample['mix']
            arr, cspec = _const(pack)
            args += [ssm_s, att_s, arr]
            specs += [whole(ssm_s), whole(att_s), cspec]
        sample_out = [jax.ShapeDtypeStruct(xs.shape, F32)]
        if proj is not None:
            arr, cspec = _const(pad)
            args.append(arr)
            specs.append(cspec)
            sample_out += proj_outputs(xs.shape[0], Q_EXP, F32)
        out_shape += sample_out
        out_specs += [whole(o) for o in sample_out]
    outs = pl.pallas_call(
        functools.partial(_rows_kernel, has_mix=mix is not None, has_proj=proj is not None,
                          has_sample=sample is not None, n_sub=max(1, tm // ROW_GROUP), layer=layer),
        grid=(M // tm,),
        in_specs=specs,
        out_specs=out_specs,
        out_shape=out_shape,
        scratch_shapes=[pltpu.VMEM(w_gu.shape[1:], BF16), pltpu.VMEM(w_dn.shape[1:], BF16),
                        pltpu.VMEM((W_STAGE_SLOTS, W_GU_STAGE_ROWS, w_gu.shape[2]), F32),
                        pltpu.VMEM((W_STAGE_SLOTS, W_DN_STAGE_ROWS, w_dn.shape[2]), F32),
                        pltpu.SemaphoreType.DMA((W_STAGE_SLOTS,)), pltpu.SemaphoreType.DMA((W_STAGE_SLOTS,))],
        compiler_params=pltpu.CompilerParams(dimension_semantics=("arbitrary",),
                                             vmem_limit_bytes=VMEM_LIMIT_BYTES),
        name=name,
    )(*args)
    return outs[:n_main], (outs[n_main:] if sample is not None else None)


def _cmul(xr, xi, yr, yi):
    return xr * yr - xi * yi, xr * yi + xi * yr


def _build_ssm_tables(p_ref, af_ref, wst_ref, cat_ref, strip_ref, ak_ref, tr_ref):
    t1 = SSM_CHUNK
    a_re, a_im, lam_re, lam_im, b_re, b_im, c_re, c_im = (p_ref[k] for k in range(8))
    num_re, num_im = a_re - 1.0, a_im
    den = lam_re * lam_re + lam_im * lam_im
    f_re = (num_re * lam_re + num_im * lam_im) / den
    f_im = (num_im * lam_re - num_re * lam_im) / den
    bb_re, bb_im = _cmul(f_re, f_im, b_re, b_im)

    def same_group(shape, row_div, col_div):
        r = lax.broadcasted_iota(jnp.int32, shape, 0) // row_div
        c = lax.broadcasted_iota(jnp.int32, shape, 1) // col_div
        return (r == c).astype(F32)

    m_state = same_group((LANES, SLAB_STATES), SSM_GROUP, SSM_STATE)
    m_chan = 0.5 * same_group((LANES, LANES), SSM_GROUP, SSM_GROUP)

    def expand(x):
        return (jnp.concatenate([x] * (SLAB_STATES // LANES), axis=1) * m_state).astype(BF16)

    def dot_nt(a, b):
        return lax.dot_general(a, b, (((1,), (1,)), ((), ())), precision=lax.Precision.HIGHEST,
                               preferred_element_type=F32)

    pw_re, pw_im = jnp.ones_like(a_re), jnp.zeros_like(a_re)
    kd = []
    for l in range(t1):
        ab_re, ab_im = _cmul(pw_re, pw_im, bb_re, bb_im)
        s = t1 - 1 - l
        kd_l = []
        for j in range(N_SLAB):
            rows = slice(j * LANES, (j + 1) * LANES)
            wst_ref[j, s * LANES:(s + 1) * LANES, :SLAB_STATES] = expand(ab_re[rows])
            wst_ref[j, s * LANES:(s + 1) * LANES, SLAB_STATES:] = expand(ab_im[rows])
            k = dot_nt(ab_re[rows], c_re[rows]) - dot_nt(ab_im[rows], c_im[rows])
            kd_l.append((k * m_chan).astype(BF16))
        kd.append(kd_l)
        pw_re, pw_im = _cmul(pw_re, pw_im, a_re, a_im)
        cf_re = c_re * pw_re - c_im * pw_im
        cf_im = -(c_re * pw_im + c_im * pw_re)
        for j in range(N_SLAB):
            rows = slice(j * LANES, (j + 1) * LANES)
            cat_ref[j, l * LANES:(l + 1) * LANES, :SLAB_STATES] = expand(cf_re[rows])
            cat_ref[j, l * LANES:(l + 1) * LANES, SLAB_STATES:] = expand(cf_im[rows])
    for j in range(N_SLAB):
        for rho in range(t1):
            for c in range(2):
                lag = t1 - 2 - rho + c
                blk = kd[lag][j] if lag >= 0 else jnp.zeros((LANES, LANES), BF16)
                strip_ref[j, rho * LANES:(rho + 1) * LANES, c * LANES:(c + 1) * LANES] = blk

    f_re, f_im = af_ref[0], af_ref[1]
    base_re, base_im = f_re, f_im
    for _ in range(t1 - 1):
        base_re, base_im = _cmul(base_re, base_im, f_re, f_im)
    row = lax.broadcasted_iota(jnp.int32, (SUBLANES, N_STATES), 0)
    pws = [(base_re, base_im)]
    for _ in range(SUBLANES - 1):
        pws.append(_cmul(pws[-1][0], pws[-1][1], base_re, base_im))
    for n, k in enumerate((1, 2, 4)):
        for part in range(2):
            ak_ref[2 * n + part] = jnp.where(row >= k, jnp.broadcast_to(pws[k - 1][part], row.shape), 0.0)
    for part in range(2):
        acc = jnp.zeros(row.shape, F32)
        for r in range(SUBLANES):
            acc = jnp.where(row == r, jnp.broadcast_to(pws[r][part], row.shape), acc)
        tr_ref[part] = acc


def _ssm_table_scratch():
    t1 = SSM_CHUNK
    return [pltpu.VMEM((N_SLAB, t1 * LANES, 2 * SLAB_STATES), BF16),
            pltpu.VMEM((N_SLAB, t1 * LANES, 2 * SLAB_STATES), BF16),
            pltpu.VMEM((N_SLAB, t1 * LANES, 2 * LANES), BF16),
            pltpu.VMEM((6, SUBLANES, N_STATES), F32),
            pltpu.VMEM((2, SUBLANES, N_STATES), F32)]


def _ssm_table_inputs(lam_re, lam_im, log_dt, b_re, b_im, c_re, c_im):
    dt = jnp.exp(log_dt)[..., None]
    mag = jnp.exp(lam_re * dt)
    a_re = mag * jnp.cos(lam_im * dt)
    a_im = mag * jnp.sin(lam_im * dt)
    rep = lambda x: jnp.repeat(x, SSM_GROUP, axis=1)
    flat = lambda x: x.reshape(DEPTH, SSM_WIDTH, SSM_STATE)
    params = jnp.stack([rep(a_re), rep(a_im), rep(lam_re), rep(lam_im),
                        flat(jnp.swapaxes(b_re, 2, 3)), flat(jnp.swapaxes(b_im, 2, 3)), flat(c_re), flat(c_im)],
                       axis=1)
    params = jnp.concatenate([params, params], axis=-1)
    a_flat = jnp.stack([a_re.reshape(DEPTH, 1, N_STATES), a_im.reshape(DEPTH, 1, N_STATES)], axis=1)
    return params, a_flat


def _glu_rms(y, gluw_ref, glub_ref, g_ref):
    g = jax.nn.gelu(y).astype(BF16)
    zz = jnp.dot(g, gluw_ref[...], preferred_element_type=F32) + glub_ref[...]
    o = zz[:, :SSM_WIDTH] * jax.nn.sigmoid(zz[:, SSM_WIDTH:])
    return _rms_norm(o, g_ref[...])


def _ssm_kernel(u_ref, p_ref, af_ref, d_ref, gluw_ref, glub_ref, g_ref,
                o_ref, hre_ref, him_ref, wst1_ref, cat1_ref, kd0_ref,
                wst_ref, cat_ref, strip_ref, ak_ref, tr_ref, ucb_ref, sre_ref, sim_ref, hcr_ref, hci_ref):
    t1 = SSM_CHUNK
    nc = SSM_ROWS // t1
    i = pl.program_id(1)

    @pl.when(jnp.logical_and(pl.program_id(0) == 0, i == 0))
    def _():
        _build_ssm_tables(p_ref, af_ref, wst_ref, cat_ref, strip_ref, ak_ref, tr_ref)
        wst1_ref[...] = wst_ref[:, (t1 - 1) * LANES:, :]
        cat1_ref[...] = cat_ref[:, :LANES, :]
        kd0_ref[...] = strip_ref[:, (t1 - 1) * LANES:, LANES:]

    @pl.when(i == 0)
    def _():
        hcr_ref[...] = jnp.zeros_like(hcr_ref)
        hci_ref[...] = jnp.zeros_like(hci_ref)

    for j in range(N_SLAB):
        for s in range(t1):
            ucb_ref[j, :, s * LANES:(s + 1) * LANES] = u_ref[j, pl.ds(s, nc, stride=t1), :].astype(BF16)

    for j in range(N_SLAB):
        st = jnp.dot(ucb_ref[j], wst_ref[j], preferred_element_type=F32)
        sre_ref[:, j * SLAB_STATES:(j + 1) * SLAB_STATES] = st[:, :SLAB_STATES]
        sim_ref[:, j * SLAB_STATES:(j + 1) * SLAB_STATES] = st[:, SLAB_STATES:]

    row = lax.broadcasted_iota(jnp.int32, (SUBLANES, SLAB_STATES), 0)
    for j in range(N_SLAB):
        cols = pl.ds(j * SLAB_STATES, SLAB_STATES)

        def body(gi, carry, cols=cols):
            hr, hi = carry
            r0 = pl.multiple_of(gi * SUBLANES, SUBLANES)
            xr = sre_ref[pl.ds(r0, SUBLANES), cols]
            xi = sim_ref[pl.ds(r0, SUBLANES), cols]
            for n, k in enumerate((1, 2, 4)):
                akr = ak_ref[2 * n, :, cols]
                aki = ak_ref[2 * n + 1, :, cols]
                sr = pltpu.roll(xr, k, 0)
                si = pltpu.roll(xi, k, 0)
                xr, xi = xr + akr * sr - aki * si, xi + akr * si + aki * sr
            tr = tr_ref[0, :, cols]
            ti = tr_ref[1, :, cols]
            er = xr + tr * hr - ti * hi
            ei = xi + tr * hi + ti * hr
            sre_ref[pl.ds(r0, SUBLANES), cols] = jnp.where(row >= 1, pltpu.roll(er, 1, 0), hr)
            sim_ref[pl.ds(r0, SUBLANES), cols] = jnp.where(row >= 1, pltpu.roll(ei, 1, 0), hi)
            last = SUBLANES - 1
            return (jnp.broadcast_to(er[last:last + 1], er.shape), jnp.broadcast_to(ei[last:last + 1], ei.shape))

        hr, hi = lax.fori_loop(0, nc // SUBLANES, body, (hcr_ref[:, cols], hci_ref[:, cols]), unroll=True)
        hcr_ref[:, cols] = hr
        hci_ref[:, cols] = hi

    hre_ref[0] = hcr_ref[...]
    him_ref[0] = hci_ref[...]

    ys = []
    for j in range(N_SLAB):
        cols = slice(j * SLAB_STATES, (j + 1) * SLAB_STATES)
        hprev = jnp.concatenate([sre_ref[:, cols], sim_ref[:, cols]], axis=1).astype(BF16)
        y = lax.dot_general(hprev, cat_ref[j], (((1,), (1,)), ((), ())), preferred_element_type=F32)
        parts = []
        for p2 in range(t1 // 2):
            kk = (2 * p2 + 2) * LANES
            yi = jnp.dot(ucb_ref[j, :, :kk], strip_ref[j, (t1 - 2 - 2 * p2) * LANES:, :],
                         preferred_element_type=F32)
            parts.append(y[:, 2 * p2 * LANES:(2 * p2 + 2) * LANES] + yi)
        ys.append(jnp.concatenate(parts, axis=1))

    for t in range(t1):
        y = jnp.concatenate([ys[j][:, t * LANES:(t + 1) * LANES] + d_ref[j] * u_ref[j, pl.ds(t, nc, stride=t1), :]
                             for j in range(N_SLAB)], axis=1)
        n = _glu_rms(y, gluw_ref, glub_ref, g_ref)
        for j in range(N_SLAB):
            o_ref[j, pl.ds(t, nc, stride=t1), :] = n[:, j * LANES:(j + 1) * LANES]


def _ssm_prompt_call(u, params, a_flat, d_t, gluw, glub, g, *, batch, seq, name):
    t1 = SSM_CHUNK
    nc = SSM_ROWS // t1
    steps = seq // SSM_ROWS
    blk = pl.BlockSpec((N_SLAB, SSM_ROWS, LANES), lambda b, i: (0, b * steps + i, 0))
    st = pl.BlockSpec((1, SUBLANES, N_STATES), lambda b, i: (b, 0, 0))
    consts, cspecs = _consts([params, a_flat, d_t, gluw, glub, g], 2)
    small = [jax.ShapeDtypeStruct((N_SLAB, LANES, 2 * SLAB_STATES), BF16),
             jax.ShapeDtypeStruct((N_SLAB, LANES, 2 * SLAB_STATES), BF16),
             jax.ShapeDtypeStruct((N_SLAB, LANES, LANES), BF16)]
    whole = lambda a: pl.BlockSpec(a.shape, lambda b, i: (0,) * len(a.shape))
    out, hre, him, *step_tables = pl.pallas_call(
        _ssm_kernel,
        grid=(batch, steps),
        in_specs=[blk] + cspecs,
        out_specs=[blk, st, st] + [whole(s) for s in small],
        out_shape=[jax.ShapeDtypeStruct(u.shape, F32),
                   jax.ShapeDtypeStruct((batch, SUBLANES, N_STATES), F32),
                   jax.ShapeDtypeStruct((batch, SUBLANES, N_STATES), F32)] + small,
        scratch_shapes=_ssm_table_scratch() + [pltpu.VMEM((N_SLAB, nc, t1 * LANES), BF16),
                                               pltpu.VMEM((nc, N_STATES), F32),
                                               pltpu.VMEM((nc, N_STATES), F32),
                                               pltpu.VMEM((SUBLANES, N_STATES), F32),
                                               pltpu.VMEM((SUBLANES, N_STATES), F32)],
        compiler_params=pltpu.CompilerParams(dimension_semantics=("arbitrary", "arbitrary"),
                                             vmem_limit_bytes=VMEM_LIMIT_BYTES),
        name=name,
    )(u, *consts)
    return out, hre[:, 0], him[:, 0], step_tables


def _ssm_sample_kernel(u_ref, h0r_ref, h0i_ref, wst_ref, cat_ref, kd_ref, a_ref, d_ref, gluw_ref, glub_ref, g_ref,
                       o_ref, hr_ref, hi_ref):
    ys = []
    for j in range(N_SLAB):
        cols = slice(j * SLAB_STATES, (j + 1) * SLAB_STATES)
        uf = u_ref[j]
        ub = uf.astype(BF16)
        st = jnp.dot(ub, wst_ref[j], preferred_element_type=F32)
        h0r = h0r_ref[:, cols]
        h0i = h0i_ref[:, cols]
        ar = a_ref[0, :, cols]
        ai = a_ref[1, :, cols]
        hr_ref[:, cols] = ar * h0r - ai * h0i + st[:, :SLAB_STATES]
        hi_ref[:, cols] = ar * h0i + ai * h0r + st[:, SLAB_STATES:]
        hcat = jnp.concatenate([h0r, h0i], axis=1).astype(BF16)
        y = lax.dot_general(hcat, cat_ref[j], (((1,), (1,)), ((), ())), preferred_element_type=F32)
        y = y + jnp.dot(ub, kd_ref[j], preferred_element_type=F32)
        ys.append(y + d_ref[j] * uf)
    n = _glu_rms(jnp.concatenate(ys, axis=1), gluw_ref, glub_ref, g_ref)
    for j in range(N_SLAB):
        o_ref[j] = n[:, j * LANES:(j + 1) * LANES]


def _ssm_sample_call(u, h0r, h0i, step_tables, a_flat, d_t, gluw, glub, g, *, name):
    n_seq = u.shape[1]
    whole = lambda a: pl.BlockSpec(a.shape, lambda i: (0,) * len(a.shape))
    outs = [jax.ShapeDtypeStruct(u.shape, F32),
            jax.ShapeDtypeStruct((n_seq, N_STATES), F32),
            jax.ShapeDtypeStruct((n_seq, N_STATES), F32)]
    consts, cspecs = _consts(list(step_tables) + [a_flat, d_t, gluw, glub, g])
    return pl.pallas_call(
        _ssm_sample_kernel,
        grid=(1,),
        in_specs=[whole(u), whole(h0r), whole(h0i)] + cspecs,
        out_specs=[whole(o) for o in outs],
        out_shape=outs,
        compiler_params=pltpu.CompilerParams(dimension_semantics=("arbitrary",),
                                             vmem_limit_bytes=VMEM_LIMIT_BYTES),
        name=name,
    )(u, h0r, h0i, *consts)


def _attn_kernel(sink_ref, q_ref, kvc_ref, kvp_ref, g_ref, o_ref):
    i = pl.program_id(1)
    kj = lax.broadcasted_iota(jnp.int32, (2 * WINDOW, WINDOW), 0)
    qi = lax.broadcasted_iota(jnp.int32, (2 * WINDOW, WINDOW), 1)
    band = (kj >= qi) & (kj <= qi + WINDOW)
    low = qi < HEAD_DIM
    for sb in range(ATTN_BLOCKS):
        blk = slice(sb * WINDOW, (sb + 1) * WINDOW)
        kc = kvc_ref[blk, :]
        if sb == 0:
            kp = kvp_ref[...]
            valid = band & ((i > 0) | (kj >= WINDOW))
        else:
            kp = kvc_ref[(sb - 1) * WINDOW:sb * WINDOW, :]
            valid = band
        valid2 = jnp.concatenate([valid, valid], axis=1)
        kcat = jnp.concatenate([kp[:, :KV_WIDTH], kc[:, :KV_WIDTH]], axis=0) * ATTN_SCALE
        swap = pltpu.roll(kcat, HEAD_DIM, 1)
        zero = jnp.zeros_like(kcat)
        k_sel = [[jnp.where(low, kcat, zero).astype(BF16), jnp.where(low, zero, swap).astype(BF16)],
                 [jnp.where(low, swap, zero).astype(BF16), jnp.where(low, zero, kcat).astype(BF16)]]
        v_t = jnp.concatenate([kp[:, KV_WIDTH:], kc[:, KV_WIDTH:]], axis=0).T.astype(BF16)
        parts = []
        for kvh in range(N_KV_HEADS):
            q_pair = jnp.concatenate([q_ref[blk, (2 * kvh) * LANES:(2 * kvh + 1) * LANES],
                                      q_ref[blk, (2 * kvh + 1) * LANES:(2 * kvh + 2) * LANES]], axis=0)
            for parity in range(2):
                s_t = lax.dot_general(k_sel[kvh][parity], q_pair, (((1,), (1,)), ((), ())),
                                      preferred_element_type=F32)
                parts.append(jnp.where(valid2, s_t, MASKED))
        s_all = jnp.concatenate(parts, axis=1)
        sink = sink_ref[...]
        m = jnp.maximum(jnp.max(s_all, axis=0, keepdims=True), sink)
        p_all = jnp.exp(s_all - m)
        inv_den = 1.0 / (jnp.sum(p_all, axis=0, keepdims=True) + jnp.exp(sink - m))
        p_all = p_all.astype(BF16)
        half = Q_PER_KV * WINDOW
        o_kv = [jnp.dot(v_t[kvh * HEAD_DIM:(kvh + 1) * HEAD_DIM], p_all[:, kvh * half:(kvh + 1) * half],
                        preferred_element_type=F32) * inv_den[:, kvh * half:(kvh + 1) * half]
                for kvh in range(N_KV_HEADS)]
        tiles = []
        for pr in range(Q_PER_KV):
            lanes = slice(ATTN_HEAD_ORDER.index(pr) * WINDOW, (ATTN_HEAD_ORDER.index(pr) + 1) * WINDOW)
            tiles.append(jnp.concatenate([o_kv[0][:, lanes], o_kv[1][:, lanes]], axis=0))
        ss = None
        for tile in tiles:
            t = jnp.sum(tile * tile, axis=0, keepdims=True)
            ss = t if ss is None else ss + t
        inv = lax.rsqrt(ss / ATTN_WIDTH + RMS_EPS)
        for pr, tile in enumerate(tiles):
            o_ref[blk, pr * LANES:(pr + 1) * LANES] = ((tile * inv).T * g_ref[pr]).astype(o_ref.dtype)


def _attn_prompt_call(q, kv, sinks, g_perm, *, batch, seq, name):
    rows = ATTN_BLOCKS * WINDOW
    nb = seq // rows
    M = kv.shape[0]
    cur = lambda b, i: (b * nb + i, 0)
    prev = lambda b, i: ((b * nb + i) * ATTN_BLOCKS - jnp.minimum(i, 1), 0)
    (sinks, g_perm), (sink_spec, g_spec) = _consts([sinks, g_perm], 2)
    return pl.pallas_call(
        _attn_kernel,
        grid=(batch, nb),
        in_specs=[sink_spec,
                  pl.BlockSpec((rows, ATTN_WIDTH), cur),
                  pl.BlockSpec((rows, 2 * KV_WIDTH), cur),
                  pl.BlockSpec((WINDOW, 2 * KV_WIDTH), prev),
                  g_spec],
        out_specs=pl.BlockSpec((rows, ATTN_WIDTH), cur),
        out_shape=jax.ShapeDtypeStruct((M, ATTN_WIDTH), BF16),
        compiler_params=pltpu.CompilerParams(dimension_semantics=("arbitrary", "arbitrary"),
                                             vmem_limit_bytes=VMEM_LIMIT_BYTES),
        name=name,
    )(sinks, q, kv, kv, g_perm)


SEQ_PER_STEP = LANES // N_HEADS


def _attn_sample_kernel(q_ref, kv_ref, ckt_ref, cvt_ref, sink_ref, g_ref, own_ref, o_ref, kot_ref, vot_ref):
    nrow = SEQ_PER_STEP * N_HEADS
    qb = q_ref[...].astype(BF16)
    k_new = kv_ref[:, :KV_WIDTH]
    v_new = kv_ref[:, KV_WIDTH:]
    row_seq = lax.broadcasted_iota(jnp.int32, (nrow, LANES), 0) // N_HEADS
    lane_seq = lax.broadcasted_iota(jnp.int32, (nrow, LANES), 1) // N_HEADS
    lane = lax.broadcasted_iota(jnp.int32, (nrow, LANES), 1)
    pick = (lax.broadcasted_iota(jnp.int32, (nrow, SEQ_PER_STEP), 0) // N_HEADS
            == lax.broadcasted_iota(jnp.int32, (nrow, SEQ_PER_STEP), 1)).astype(BF16)

    qf = q_ref[...]
    heads_of = lambda a, n: a[n * N_HEADS:(n + 1) * N_HEADS]
    s = jnp.concatenate([jnp.dot(heads_of(qf, n).astype(BF16), ckt_ref[n].astype(BF16), preferred_element_type=F32)
                         for n in range(SEQ_PER_STEP)], axis=0) * ATTN_SCALE
    k_rows = jnp.dot(pick, k_new.astype(BF16), preferred_element_type=F32)
    s_new = jnp.sum(qb.astype(F32) * k_rows, axis=-1, keepdims=True) * ATTN_SCALE
    sink = sink_ref[...]
    m = jnp.maximum(jnp.maximum(jnp.max(s, axis=-1, keepdims=True), s_new), sink)
    p = jnp.exp(s - m)
    p_new = jnp.exp(s_new - m)
    inv = 1.0 / (jnp.sum(p, axis=-1, keepdims=True) + p_new + jnp.exp(sink - m))
    pn = p * inv
    o = jnp.concatenate([lax.dot_general(heads_of(pn, n).astype(BF16), cvt_ref[n].astype(BF16),
                                         (((1,), (1,)), ((), ())), preferred_element_type=F32)
                         for n in range(SEQ_PER_STEP)], axis=0)
    v_rows = jnp.dot(pick, v_new.astype(BF16), preferred_element_type=F32)
    o = o + (p_new * inv).astype(BF16).astype(F32) * v_rows

    hi = lax.Precision.HIGHEST
    row_sums = jnp.dot(o * o * own_ref[...], jnp.ones((LANES, LANES), F32), precision=hi, preferred_element_type=F32)
    ss = jnp.dot((row_seq == lane_seq).astype(F32), row_sums, precision=hi, preferred_element_type=F32)
    o_ref[...] = o * lax.rsqrt(ss / ATTN_WIDTH + RMS_EPS) * g_ref[...]

    pad = jnp.zeros((LANES - SEQ_PER_STEP, KV_WIDTH), F32)
    k_cols = jnp.concatenate([k_new, pad], axis=0).T
    v_cols = jnp.concatenate([v_new, pad], axis=0).T
    last = lane == WINDOW - 1
    for n in range(SEQ_PER_STEP):
        kot_ref[n] = jnp.where(last, pltpu.roll(k_cols, WINDOW - 1 - n, 1), pltpu.roll(ckt_ref[n], WINDOW - 1, 1))
        vot_ref[n] = jnp.where(last, pltpu.roll(v_cols, WINDOW - 1 - n, 1), pltpu.roll(cvt_ref[n], WINDOW - 1, 1))


def _attn_sample_call(q, kv, ck, cv, layer, sink_row, g_rows, own_rows, *, name):
    n_seq = kv.shape[0]
    assert n_seq % SEQ_PER_STEP == 0
    nrow = SEQ_PER_STEP * N_HEADS
    cblk = pl.BlockSpec((SEQ_PER_STEP, WINDOW, 2 * HEAD_DIM), lambda i: (i, 0, 0))
    cin = pl.BlockSpec((None, SEQ_PER_STEP, WINDOW, 2 * HEAD_DIM), lambda i: (layer, i, 0, 0))
    qblk = pl.BlockSpec((nrow, LANES), lambda i: (i, 0))
    consts, cspecs = _consts([sink_row, g_rows, own_rows])
    return pl.pallas_call(
        _attn_sample_kernel,
        grid=(n_seq // SEQ_PER_STEP,),
        in_specs=[qblk, pl.BlockSpec((SEQ_PER_STEP, 2 * KV_WIDTH), lambda i: (i, 0)), cin, cin] + cspecs,
        out_specs=[qblk, cblk, cblk],
        out_shape=[jax.ShapeDtypeStruct(q.shape, F32),
                   jax.ShapeDtypeStruct(ck.shape[1:], F32),
                   jax.ShapeDtypeStruct(cv.shape[1:], F32)],
        compiler_params=pltpu.CompilerParams(dimension_semantics=("arbitrary",),
                                             vmem_limit_bytes=VMEM_LIMIT_BYTES),
        name=name,
    )(q, kv, ck, cv, *consts)


def _expand_heads(a):
    lead = a.shape[:-1]
    a = a.reshape(lead + (N_KV_HEADS, Q_PER_KV, 1, HEAD_DIM))
    sel = jnp.eye(N_KV_HEADS, dtype=a.dtype).reshape(N_KV_HEADS, 1, N_KV_HEADS, 1)
    return (a * sel).reshape(lead + (Q_EXP,))


def _pair_heads(a):
    lead = a.shape[:-1]
    a = a.reshape(lead + (N_KV_HEADS, Q_PER_KV, HEAD_DIM))
    return jnp.swapaxes(a, -3, -2).reshape(lead + (ATTN_WIDTH,))


def _prep_weights(ln_g, ln_b, ffn1_w_in, ffn1_w_out, ffn2_w_in, ffn2_w_out, w_in, ssm_lam_re, ssm_lam_im, ssm_log_dt,
                  ssm_b_re, ssm_b_im, ssm_c_re, ssm_c_im, ssm_d, glu_w, glu_b, attn_sinks, g_ssm_out, g_attn_out,
                  w_out):
    rows_t = lambda a: jnp.swapaxes(a, 1, 2)
    params, a_flat = _ssm_table_inputs(ssm_lam_re, ssm_lam_im, ssm_log_dt, ssm_b_re, ssm_b_im, ssm_c_re, ssm_c_im)
    head_order = jnp.array([Q_PER_KV * kvh + o for kvh in range(N_KV_HEADS) for o in ATTN_HEAD_ORDER])
    g_exp = _expand_heads(g_attn_out).reshape(DEPTH, N_HEADS, LANES)
    w_in_b = w_in.astype(BF16)
    w_out_b = w_out.astype(BF16)
    stacked = dict(
        w_in=w_in_b,
        wo_s=w_out_b[:, :SSM_WIDTH],
        wo_a=rows_t(_pair_heads(rows_t(w_out_b[:, SSM_WIDTH:]))),
        ssm_params=params, a_flat=a_flat,
        d_1=ssm_d.reshape(DEPTH, N_SLAB, 1, LANES),
        gluw=glu_w.astype(BF16),
        glub=glu_b.reshape(DEPTH, 1, 2 * SSM_WIDTH),
        g_ssm=g_ssm_out.reshape(DEPTH, 1, SSM_WIDTH),
        sinks=jnp.repeat(attn_sinks[:, head_order], WINDOW, axis=1)[:, None, :],
        g_pair=_pair_heads(g_attn_out).reshape(DEPTH, Q_PER_KV, 1, LANES),
        sink_rows=jnp.tile(attn_sinks, (1, SEQ_PER_STEP))[:, :, None],
        g_rows=jnp.tile(g_exp, (1, SEQ_PER_STEP, 1)),
    )
    w = {name: [(arr, l) for l in range(DEPTH)] for name, arr in stacked.items()}
    w['ffn1'] = [(ffn1_w_in, ffn1_w_out, l) for l in range(DEPTH)]
    w['ffn2'] = [(ffn2_w_in, ffn2_w_out, l) for l in range(DEPTH)]
    n_ln = ln_g.shape[1]
    ln_g3 = ln_g.reshape(DEPTH * n_ln, 1, D_MODEL)
    ln_b3 = ln_b.reshape(DEPTH * n_ln, 1, D_MODEL)
    w['ln'] = [[((ln_g3, l * n_ln + i), (ln_b3, l * n_ln + i)) for i in range(n_ln)] for l in range(DEPTH)]
    w['own_rows'] = jnp.tile(_expand_heads(jnp.ones((ATTN_WIDTH,), F32)).reshape(N_HEADS, LANES), (SEQ_PER_STEP, 1))
    return w


def _prompt_mixer(u, q, kv, l, w, batch, seq):
    ssm_n, hre, him, step_tables = _ssm_prompt_call(u, w['ssm_params'][l], w['a_flat'][l], w['d_1'][l], w['gluw'][l],
                                                    w['glub'][l], w['g_ssm'][l], batch=batch, seq=seq,
                                                    name=f"p_ssm_{l}")
    att_n = _attn_prompt_call(q, kv, w['sinks'][l], w['g_pair'][l], batch=batch, seq=seq, name=f"p_attn_{l}")
    kvw = kv.reshape(batch, seq, 2 * KV_WIDTH)[:, -WINDOW:].reshape(batch, WINDOW, 2, N_KV_HEADS, HEAD_DIM)
    return (ssm_n, att_n, hre.reshape(batch, N_SSM_GROUPS, SSM_STATE), him.reshape(batch, N_SSM_GROUPS, SSM_STATE),
            kvw[:, :, 0], kvw[:, :, 1]), step_tables


def _sample_mixer(u, q, kv, l, w, step_tables, h0_re, h0_im, k_buf, v_buf):
    n_seq = kv.shape[0]
    ssm_n, hre, him = _ssm_sample_call(u, h0_re.reshape(n_seq, N_STATES), h0_im.reshape(n_seq, N_STATES),
                                       step_tables, w['a_flat'][l], w['d_1'][l], w['gluw'][l], w['glub'][l],
                                       w['g_ssm'][l], name=f"s_ssm_{l}")
    att, kn, vn = _attn_sample_call(q.reshape(n_seq * N_HEADS, LANES), kv, k_buf, v_buf, l,
                                    w['sink_rows'][l], w['g_rows'][l], w['own_rows'], name=f"s_attn_{l}")
    untranspose = lambda t: jnp.transpose(t.reshape(n_seq, N_KV_HEADS, HEAD_DIM, WINDOW), (0, 3, 1, 2))
    return (ssm_n, att.reshape(n_seq, Q_EXP), hre.reshape(n_seq, N_SSM_GROUPS, SSM_STATE),
            him.reshape(n_seq, N_SSM_GROUPS, SSM_STATE), untranspose(kn), untranspose(vn))


def _trunks(x_prompt, x_sample, h0_re, h0_im, k_buf, v_buf, w, *, tm):
    Bn, L, _ = x_prompt.shape
    n_seq = x_sample.shape[0]
    xp = x_prompt.reshape(Bn * L, D_MODEL)
    xs = x_sample.reshape(n_seq, D_MODEL)
    outs_p, outs_s = [], []
    mix_p = mix_s = None
    for l in range(DEPTH + 1):
        if l > 0:
            (xp,), (xs,) = _rows_call(xp, w['ffn2'][l - 1], w['ln'][l - 1][2], mix=mix_p,
                                      sample=dict(x=xs, mix=mix_s), tm=tm, name=f"mix_ffn2_{l - 1}")
        if l == DEPTH:
            break
        (xp, u, q, kv), (xs, us, qs, kvs) = _rows_call(xp, w['ffn1'][l], w['ln'][l][0], proj=w['w_in'][l],
                                                       sample=dict(x=xs), tm=tm,
                                                       name=f"ffn1_{l}")
        (ssm_p, att_p, *state_p), step_tables = _prompt_mixer(u, q, kv, l, w, Bn, L)
        ssm_s, att_s, *state_s = _sample_mixer(us, qs, kvs, l, w, step_tables, h0_re[l], h0_im[l], k_buf, v_buf)
        mix_p = (ssm_p, att_p, w['wo_s'][l], w['wo_a'][l], w['ln'][l][1][0], w['ln'][l][1][1])
        mix_s = (ssm_s, att_s)
        outs_p.append(state_p)
        outs_s.append(state_s)
    stack = lambda outs: tuple(jnp.stack([o[i] for o in outs]) for i in (2, 3, 0, 1))
    return (xp.reshape(Bn, L, D_MODEL), xs.reshape(n_seq, 1, D_MODEL)), stack(outs_p), stack(outs_s)


def kernel(x_prompt, x_sample, cache_k_win, cache_v_win, state_ssm_re, state_ssm_im, ln_g, ln_b, ffn1_w_in, ffn1_w_out, ffn2_w_in, ffn2_w_out, w_in, ssm_lam_re, ssm_lam_im, ssm_log_dt, ssm_b_re, ssm_b_im, ssm_c_re, ssm_c_im, ssm_d, glu_w, glu_b, attn_sinks, g_ssm_out, g_attn_out, w_out):
    w = _prep_weights(ln_g, ln_b, ffn1_w_in, ffn1_w_out, ffn2_w_in, ffn2_w_out, w_in, ssm_lam_re, ssm_lam_im,
                      ssm_log_dt, ssm_b_re, ssm_b_im, ssm_c_re, ssm_c_im, ssm_d, glu_w, glu_b, attn_sinks,
                      g_ssm_out, g_attn_out, w_out)
    n_seq = x_sample.shape[0]
    transposed = lambda c: jnp.transpose(c, (0, 1, 3, 4, 2)).reshape(DEPTH, n_seq, N_KV_HEADS * HEAD_DIM, WINDOW)
    (y_prompt, y_sample), (kp, vp, hrp, hip), (ks_, vs_, hrs, his) = _trunks(
        x_prompt, x_sample, state_ssm_re, state_ssm_im, transposed(cache_k_win), transposed(cache_v_win), w, tm=512)
    return (y_prompt, y_sample, kp, vp, hrp, hip, ks_, vs_, hrs, his)
```

```python
import functools

import jax
import jax.numpy as jnp
import numpy as np
from jax import lax
from jax.experimental import pallas as pl
from jax.experimental.pallas import tpu as pltpu

D_MODEL = 1024
DEPTH = 2
SSM_WIDTH = 512
SSM_GROUP = 16
N_SSM_GROUPS = 32
SSM_STATE = 64
N_STATES = N_SSM_GROUPS * SSM_STATE
ATTN_WIDTH = 512
HEAD_DIM = 64
N_HEADS = 8
N_KV_HEADS = 2
Q_PER_KV = 4
KV_WIDTH = 128
WINDOW = 128
ATTN_SCALE = HEAD_DIM ** -0.5
D_FF = 2816
ALPHA = (2.0 * DEPTH) ** 0.25
LN_EPS = 1e-5
RMS_EPS = 1e-6

ROW_GROUP = 256
W_STAGE_SLOTS = 8
W_GU_STAGE_ROWS = 32
W_DN_STAGE_ROWS = 176
LANES = 128
SUBLANES = 8
VMEM_LIMIT_BYTES = 56 * 1024 * 1024
N_SLAB = SSM_WIDTH // LANES
GROUPS_PER_SLAB = LANES // SSM_GROUP
SLAB_STATES = GROUPS_PER_SLAB * SSM_STATE
Q_EXP = N_HEADS * LANES
SSM_CHUNK = 8
SSM_ROWS = 2048
ATTN_BLOCKS = 16
ATTN_HEAD_ORDER = (0, 2, 1, 3)
MASKED = -1e30

F32 = jnp.float32
BF16 = jnp.bfloat16


def _layer_norm(r, g, b):
    mu = jnp.mean(r, axis=-1, keepdims=True)
    c = r - mu
    var = jnp.mean(c * c, axis=-1, keepdims=True)
    return c * lax.rsqrt(var + LN_EPS) * g + b


def _rms_norm(y, g):
    return y * lax.rsqrt(jnp.mean(y * y, axis=-1, keepdims=True) + RMS_EPS) * g


def _const(c, n_grid=1):
    if isinstance(c, tuple):
        arr, idx = c
        shape = (None,) + arr.shape[1:]
        index = (idx,) + (0,) * (arr.ndim - 1)
    else:
        arr, shape, index = c, c.shape, (0,) * c.ndim
    imap = (lambda i: index) if n_grid == 1 else (lambda i, j: index)
    return arr, pl.BlockSpec(shape, imap, pipeline_mode=pl.Buffered(1))


def _consts(cs, n_grid=1):
    arrs, specs = zip(*[_const(c, n_grid) for c in cs])
    return list(arrs), list(specs)


def _load_cast(src_hbm, layer, dst_ref, stage_ref, sem_ref):
    n_slots, rows = stage_ref.shape[:2]
    n_chunks = dst_ref.shape[0] // rows

    def copy(c):
        return pltpu.make_async_copy(src_hbm.at[layer, pl.ds(c * rows, rows), :], stage_ref.at[c % n_slots],
                                     sem_ref.at[c % n_slots])

    def prime():
        for c in range(min(n_slots - 1, n_chunks)):
            copy(c).start()

    def drain():
        for c in range(n_chunks):
            copy(c).wait()
            dst_ref[pl.ds(c * rows, rows), :] = stage_ref[c % n_slots].astype(BF16)
            if c + n_slots - 1 < n_chunks:
                copy(c + n_slots - 1).start()

    return prime, drain


def _rows_kernel(*refs, has_mix, has_proj, has_sample, n_sub, layer):
    it = iter(refs)
    take = lambda n: tuple(next(it) for _ in range(n))
    (x_ref,) = take(1)
    if has_mix:
        ssm_ref, att_ref, wo_s_ref, wo_a_ref, gm_ref, bm_ref = take(6)
    wgu_hbm, wdn_hbm, g_ref, b_ref = take(4)
    wp_ref = take(1)[0] if has_proj else None
    if has_sample:
        (xs_ref,) = take(1)
        if has_mix:
            ssm_s_ref, att_s_ref, pack_ref = take(3)
        pad_ref = take(1)[0] if has_proj else None
    n_out = 4 if has_proj else 1
    outs = take(n_out)
    outs_s = take(n_out) if has_sample else None
    wgu_ref, wdn_ref, stage_gu, stage_dn, sem_gu, sem_dn = take(6)

    @pl.when(pl.program_id(0) == 0)
    def _():
        prime_gu, drain_gu = _load_cast(wgu_hbm, layer, wgu_ref, stage_gu, sem_gu)
        prime_dn, drain_dn = _load_cast(wdn_hbm, layer, wdn_ref, stage_dn, sem_dn)
        prime_gu()
        prime_dn()
        drain_gu()
        drain_dn()

    def mix_inputs(ssm_r, att_r, rows):
        ssm = jnp.concatenate([ssm_r[j, rows, :] for j in range(N_SLAB)], axis=1).astype(BF16)
        return ssm, att_r[rows, :].astype(BF16)

    def run(xs, mixes, wo_a, wp):
        if has_mix:
            ms = [jnp.dot(ssm, wo_s_ref[...], preferred_element_type=F32)
                  + jnp.dot(att, wo_a[...], preferred_element_type=F32) for ssm, att in mixes]
            xs = [_layer_norm(ALPHA * x + m, gm_ref[...], bm_ref[...]) for x, m in zip(xs, ms)]
        gus = [jnp.dot(x.astype(BF16), wgu_ref[...], preferred_element_type=F32) for x in xs]
        hs = [(gu[:, :D_FF] * jax.nn.sigmoid(gu[:, :D_FF]) * gu[:, D_FF:]).astype(BF16) for gu in gus]
        ys = [jnp.dot(h, wdn_ref[...], preferred_element_type=F32) for h in hs]
        xs = [_layer_norm(ALPHA * x + 0.5 * y, g_ref[...], b_ref[...]) for x, y in zip(xs, ys)]
        if has_proj:
            zs = [jnp.dot(x.astype(BF16), wp[...], preferred_element_type=F32) for x in xs]
        else:
            zs = [None] * len(xs)
        return xs, zs

    def store(out_refs, rows, x, z):
        out_refs[0][rows, :] = x
        if has_proj:
            u_ref, q_ref, kv_ref = out_refs[1:]
            q_cols = q_ref.shape[1]
            for j in range(N_SLAB):
                u_ref[j, rows, :] = z[:, j * LANES:(j + 1) * LANES]
            q_ref[rows, :] = z[:, SSM_WIDTH:SSM_WIDTH + q_cols].astype(q_ref.dtype)
            kv_ref[rows, :] = z[:, SSM_WIDTH + q_cols:]

    sub = x_ref.shape[0] // n_sub
    groups = [slice(r * sub, (r + 1) * sub) for r in range(n_sub)]
    mixes = [mix_inputs(ssm_ref, att_ref, rows) for rows in groups] if has_mix else None
    xs, zs = run([x_ref[rows, :] for rows in groups], mixes, wo_a_ref if has_mix else None, wp_ref)
    for rows, x, z in zip(groups, xs, zs):
        store(outs, rows, x, z)

    if has_sample:
        @pl.when(pl.program_id(0) == pl.num_programs(0) - 1)
        def _():
            every = slice(None)
            mixes_s = None
            if has_mix:
                ssm, att_padded = mix_inputs(ssm_s_ref, att_s_ref, every)
                att = jnp.dot(att_padded, pack_ref[...], preferred_element_type=F32).astype(BF16)
                mixes_s = [(ssm, att)]
            xs_s, zs_s = run([xs_ref[...]], mixes_s, wo_a_ref if has_mix else None, wp_ref)
            z = zs_s[0]
            if has_proj:
                q = z[:, SSM_WIDTH:SSM_WIDTH + ATTN_WIDTH].astype(BF16)
                z = jnp.concatenate([z[:, :SSM_WIDTH], jnp.dot(q, pad_ref[...], preferred_element_type=F32),
                                     z[:, SSM_WIDTH + ATTN_WIDTH:]], axis=1)
            store(outs_s, every, xs_s[0], z)


def _head_selectors():
    h = np.arange(ATTN_WIDTH) // HEAD_DIM
    d = np.arange(ATTN_WIDTH) % HEAD_DIM
    kvh, pr = h // Q_PER_KV, h % Q_PER_KV
    padded_col = h * LANES + kvh * HEAD_DIM + d
    packed_col = pr * LANES + kvh * HEAD_DIM + d
    pad = np.zeros((ATTN_WIDTH, Q_EXP), np.float32)
    pad[np.arange(ATTN_WIDTH), padded_col] = 1.0
    pack = np.zeros((Q_EXP, ATTN_WIDTH), np.float32)
    pack[padded_col, packed_col] = 1.0
    return jnp.asarray(pad, BF16), jnp.asarray(pack, BF16)


def _rows_call(x, ffn, ln, mix=None, proj=None, sample=None, *, tm, name):
    M = x.shape[0]
    assert M % tm == 0
    row = lambda w: pl.BlockSpec((tm, w), lambda i: (i, 0))
    slab = lambda n: pl.BlockSpec((n, tm, LANES), lambda i: (0, i, 0))
    whole = lambda a: pl.BlockSpec(a.shape, lambda i: (0,) * len(a.shape))
    args = [x]
    specs = [row(D_MODEL)]
    if mix is not None:
        ssm_n, att_n = mix[:2]
        arrs, cspecs = _consts(mix[2:])
        args += [ssm_n, att_n] + arrs
        specs += [slab(N_SLAB), row(att_n.shape[1])] + cspecs
    w_gu, w_dn, layer = ffn
    arrs, cspecs = _consts(list(ln))
    args += [w_gu, w_dn] + arrs
    specs += [pl.BlockSpec(memory_space=pl.ANY), pl.BlockSpec(memory_space=pl.ANY)] + cspecs

    def proj_outputs(rows, q_cols, q_dtype):
        return [jax.ShapeDtypeStruct((N_SLAB, rows, LANES), F32), jax.ShapeDtypeStruct((rows, q_cols), q_dtype),
                jax.ShapeDtypeStruct((rows, 2 * KV_WIDTH), F32)]

    out_shape = [jax.ShapeDtypeStruct((M, D_MODEL), F32)]
    out_specs = [row(D_MODEL)]
    if proj is not None:
        arr, cspec = _const(proj)
        args.append(arr)
        specs.append(cspec)
        q_cols = arr.shape[-1] - SSM_WIDTH - 2 * KV_WIDTH
        out_shape += proj_outputs(M, q_cols, BF16)
        out_specs += [slab(N_SLAB), row(q_cols), row(2 * KV_WIDTH)]
    n_main = len(out_shape)
    if sample is not None:
        xs = sample['x']
        args.append(xs)
        specs.append(whole(xs))
        pad, pack = _head_selectors()
        if mix is not None:
            ssm_s, att_s = sample['mix']
            arr, cspec = _const(pack)
            args += [ssm_s, att_s, arr]
            specs += [whole(ssm_s), whole(att_s), cspec]
        sample_out = [jax.ShapeDtypeStruct(xs.shape, F32)]
        if proj is not None:
            arr, cspec = _const(pad)
            args.append(arr)
            specs.append(cspec)
            sample_out += proj_outputs(xs.shape[0], Q_EXP, F32)
        out_shape += sample_out
        out_specs += [whole(o) for o in sample_out]
    outs = pl.pallas_call(
        functools.partial(_rows_kernel, has_mix=mix is not None, has_proj=proj is not None,
                          has_sample=sample is not None, n_sub=max(1, tm // ROW_GROUP), layer=layer),
        grid=(M // tm,),
        in_specs=specs,
        out_specs=out_specs,
        out_shape=out_shape,
        scratch_shapes=[pltpu.VMEM(w_gu.shape[1:], BF16), pltpu.VMEM(w_dn.shape[1:], BF16),
                        pltpu.VMEM((W_STAGE_SLOTS, W_GU_STAGE_ROWS, w_gu.shape[2]), F32),
                        pltpu.VMEM((W_STAGE_SLOTS, W_DN_STAGE_ROWS, w_dn.shape[2]), F32),
                        pltpu.SemaphoreType.DMA((W_STAGE_SLOTS,)), pltpu.SemaphoreType.DMA((W_STAGE_SLOTS,))],
        compiler_params=pltpu.CompilerParams(dimension_semantics=("arbitrary",),
                                             vmem_limit_bytes=VMEM_LIMIT_BYTES),
        name=name,
    )(*args)
    return outs[:n_main], (outs[n_main:] if sample is not None else None)


def _cmul(xr, xi, yr, yi):
    return xr * yr - xi * yi, xr * yi + xi * yr


def _build_ssm_tables(p_ref, af_ref, wst_ref, cat_ref, strip_ref, ak_ref, tr_ref):
    t1 = SSM_CHUNK
    a_re, a_im, lam_re, lam_im, b_re, b_im, c_re, c_im = (p_ref[k] for k in range(8))
    num_re, num_im = a_re - 1.0, a_im
    den = lam_re * lam_re + lam_im * lam_im
    f_re = (num_re * lam_re + num_im * lam_im) / den
    f_im = (num_im * lam_re - num_re * lam_im) / den
    bb_re, bb_im = _cmul(f_re, f_im, b_re, b_im)

    def same_group(shape, row_div, col_div):
        r = lax.broadcasted_iota(jnp.int32, shape, 0) // row_div
        c = lax.broadcasted_iota(jnp.int32, shape, 1) // col_div
        return (r == c).astype(F32)

    m_state = same_group((LANES, SLAB_STATES), SSM_GROUP, SSM_STATE)
    m_chan = 0.5 * same_group((LANES, LANES), SSM_GROUP, SSM_GROUP)

    def expand(x):
        return (jnp.concatenate([x] * (SLAB_STATES // LANES), axis=1) * m_state).astype(BF16)

    def dot_nt(a, b):
        return lax.dot_general(a, b, (((1,), (1,)), ((), ())), precision=lax.Precision.HIGHEST,
                               preferred_element_type=F32)

    pw_re, pw_im = jnp.ones_like(a_re), jnp.zeros_like(a_re)
    kd = []
    for l in range(t1):
        ab_re, ab_im = _cmul(pw_re, pw_im, bb_re, bb_im)
        s = t1 - 1 - l
        kd_l = []
        for j in range(N_SLAB):
            rows = slice(j * LANES, (j + 1) * LANES)
            wst_ref[j, s * LANES:(s + 1) * LANES, :SLAB_STATES] = expand(ab_re[rows])
            wst_ref[j, s * LANES:(s + 1) * LANES, SLAB_STATES:] = expand(ab_im[rows])
            k = dot_nt(ab_re[rows], c_re[rows]) - dot_nt(ab_im[rows], c_im[rows])
            kd_l.append((k * m_chan).astype(BF16))
        kd.append(kd_l)
        pw_re, pw_im = _cmul(pw_re, pw_im, a_re, a_im)
        cf_re = c_re * pw_re - c_im * pw_im
        cf_im = -(c_re * pw_im + c_im * pw_re)
        for j in range(N_SLAB):
            rows = slice(j * LANES, (j + 1) * LANES)
            cat_ref[j, l * LANES:(l + 1) * LANES, :SLAB_STATES] = expand(cf_re[rows])
            cat_ref[j, l * LANES:(l + 1) * LANES, SLAB_STATES:] = expand(cf_im[rows])
    for j in range(N_SLAB):
        for rho in range(t1):
            for c in range(2):
                lag = t1 - 2 - rho + c
                blk = kd[lag][j] if lag >= 0 else jnp.zeros((LANES, LANES), BF16)
                strip_ref[j, rho * LANES:(rho + 1) * LANES, c * LANES:(c + 1) * LANES] = blk

    f_re, f_im = af_ref[0], af_ref[1]
    base_re, base_im = f_re, f_im
    for _ in range(t1 - 1):
        base_re, base_im = _cmul(base_re, base_im, f_re, f_im)
    row = lax.broadcasted_iota(jnp.int32, (SUBLANES, N_STATES), 0)
    pws = [(base_re, base_im)]
    for _ in range(SUBLANES - 1):
        pws.append(_cmul(pws[-1][0], pws[-1][1], base_re, base_im))
    for n, k in enumerate((1, 2, 4)):
        for part in range(2):
            ak_ref[2 * n + part] = jnp.where(row >= k, jnp.broadcast_to(pws[k - 1][part], row.shape), 0.0)
    for part in range(2):
        acc = jnp.zeros(row.shape, F32)
        for r in range(SUBLANES):
            acc = jnp.where(row == r, jnp.broadcast_to(pws[r][part], row.shape), acc)
        tr_ref[part] = acc


def _ssm_table_scratch():
    t1 = SSM_CHUNK
    return [pltpu.VMEM((N_SLAB, t1 * LANES, 2 * SLAB_STATES), BF16),
            pltpu.VMEM((N_SLAB, t1 * LANES, 2 * SLAB_STATES), BF16),
            pltpu.VMEM((N_SLAB, t1 * LANES, 2 * LANES), BF16),
            pltpu.VMEM((6, SUBLANES, N_STATES), F32),
            pltpu.VMEM((2, SUBLANES, N_STATES), F32)]


def _ssm_table_inputs(lam_re, lam_im, log_dt, b_re, b_im, c_re, c_im):
    dt = jnp.exp(log_dt)[..., None]
    mag = jnp.exp(lam_re * dt)
    a_re = mag * jnp.cos(lam_im * dt)
    a_im = mag * jnp.sin(lam_im * dt)
    rep = lambda x: jnp.repeat(x, SSM_GROUP, axis=1)
    flat = lambda x: x.reshape(DEPTH, SSM_WIDTH, SSM_STATE)
    params = jnp.stack([rep(a_re), rep(a_im), rep(lam_re), rep(lam_im),
                        flat(jnp.swapaxes(b_re, 2, 3)), flat(jnp.swapaxes(b_im, 2, 3)), flat(c_re), flat(c_im)],
                       axis=1)
    params = jnp.concatenate([params, params], axis=-1)
    a_flat = jnp.stack([a_re.reshape(DEPTH, 1, N_STATES), a_im.reshape(DEPTH, 1, N_STATES)], axis=1)
    return params, a_flat


def _glu_rms(y, gluw_ref, glub_ref, g_ref):
    g = jax.nn.gelu(y).astype(BF16)
    zz = jnp.dot(g, gluw_ref[...], preferred_element_type=F32) + glub_ref[...]
    o = zz[:, :SSM_WIDTH] * jax.nn.sigmoid(zz[:, SSM_WIDTH:])
    return _rms_norm(o, g_ref[...])


def _ssm_kernel(u_ref, p_ref, af_ref, d_ref, gluw_ref, glub_ref, g_ref,
                o_ref, hre_ref, him_ref, wst1_ref, cat1_ref, kd0_ref,
                wst_ref, cat_ref, strip_ref, ak_ref, tr_ref, ucb_ref, sre_ref, sim_ref, hcr_ref, hci_ref):
    t1 = SSM_CHUNK
    nc = SSM_ROWS // t1
    i = pl.program_id(1)

    @pl.when(jnp.logical_and(pl.program_id(0) == 0, i == 0))
    def _():
        _build_ssm_tables(p_ref, af_ref, wst_ref, cat_ref, strip_ref, ak_ref, tr_ref)
        wst1_ref[...] = wst_ref[:, (t1 - 1) * LANES:, :]
        cat1_ref[...] = cat_ref[:, :LANES, :]
        kd0_ref[...] = strip_ref[:, (t1 - 1) * LANES:, LANES:]

    @pl.when(i == 0)
    def _():
        hcr_ref[...] = jnp.zeros_like(hcr_ref)
        hci_ref[...] = jnp.zeros_like(hci_ref)

    for j in range(N_SLAB):
        for s in range(t1):
            ucb_ref[j, :, s * LANES:(s + 1) * LANES] = u_ref[j, pl.ds(s, nc, stride=t1), :].astype(BF16)

    for j in range(N_SLAB):
        st = jnp.dot(ucb_ref[j], wst_ref[j], preferred_element_type=F32)
        sre_ref[:, j * SLAB_STATES:(j + 1) * SLAB_STATES] = st[:, :SLAB_STATES]
        sim_ref[:, j * SLAB_STATES:(j + 1) * SLAB_STATES] = st[:, SLAB_STATES:]

    row = lax.broadcasted_iota(jnp.int32, (SUBLANES, SLAB_STATES), 0)
    for j in range(N_SLAB):
        cols = pl.ds(j * SLAB_STATES, SLAB_STATES)

        def body(gi, carry, cols=cols):
            hr, hi = carry
            r0 = pl.multiple_of(gi * SUBLANES, SUBLANES)
            xr = sre_ref[pl.ds(r0, SUBLANES), cols]
            xi = sim_ref[pl.ds(r0, SUBLANES), cols]
            for n, k in enumerate((1, 2, 4)):
                akr = ak_ref[2 * n, :, cols]
                aki = ak_ref[2 * n + 1, :, cols]
                sr = pltpu.roll(xr, k, 0)
                si = pltpu.roll(xi, k, 0)
                xr, xi = xr + akr * sr - aki * si, xi + akr * si + aki * sr
            tr = tr_ref[0, :, cols]
            ti = tr_ref[1, :, cols]
            er = xr + tr * hr - ti * hi
            ei = xi + tr * hi + ti * hr
            sre_ref[pl.ds(r0, SUBLANES), cols] = jnp.where(row >= 1, pltpu.roll(er, 1, 0), hr)
            sim_ref[pl.ds(r0, SUBLANES), cols] = jnp.where(row >= 1, pltpu.roll(ei, 1, 0), hi)
            last = SUBLANES - 1
            return (jnp.broadcast_to(er[last:last + 1], er.shape), jnp.broadcast_to(ei[last:last + 1], ei.shape))

        hr, hi = lax.fori_loop(0, nc // SUBLANES, body, (hcr_ref[:, cols], hci_ref[:, cols]), unroll=True)
        hcr_ref[:, cols] = hr
        hci_ref[:, cols] = hi

    hre_ref[0] = hcr_ref[...]
    him_ref[0] = hci_ref[...]

    ys = []
    for j in range(N_SLAB):
        cols = slice(j * SLAB_STATES, (j + 1) * SLAB_STATES)
        hprev = jnp.concatenate([sre_ref[:, cols], sim_ref[:, cols]], axis=1).astype(BF16)
        y = lax.dot_general(hprev, cat_ref[j], (((1,), (1,)), ((), ())), preferred_element_type=F32)
        parts = []
        for p2 in range(t1 // 2):
            kk = (2 * p2 + 2) * LANES
            yi = jnp.dot(ucb_ref[j, :, :kk], strip_ref[j, (t1 - 2 - 2 * p2) * LANES:, :],
                         preferred_element_type=F32)
            parts.append(y[:, 2 * p2 * LANES:(2 * p2 + 2) * LANES] + yi)
        ys.append(jnp.concatenate(parts, axis=1))

    for t in range(t1):
        y = jnp.concatenate([ys[j][:, t * LANES:(t + 1) * LANES] + d_ref[j] * u_ref[j, pl.ds(t, nc, stride=t1), :]
                             for j in range(N_SLAB)], axis=1)
        n = _glu_rms(y, gluw_ref, glub_ref, g_ref)
        for j in range(N_SLAB):
            o_ref[j, pl.ds(t, nc, stride=t1), :] = n[:, j * LANES:(j + 1) * LANES]


def _ssm_prompt_call(u, params, a_flat, d_t, gluw, glub, g, *, batch, seq, name):
    t1 = SSM_CHUNK
    nc = SSM_ROWS // t1
    steps = seq // SSM_ROWS
    blk = pl.BlockSpec((N_SLAB, SSM_ROWS, LANES), lambda b, i: (0, b * steps + i, 0))
    st = pl.BlockSpec((1, SUBLANES, N_STATES), lambda b, i: (b, 0, 0))
    consts, cspecs = _consts([params, a_flat, d_t, gluw, glub, g], 2)
    small = [jax.ShapeDtypeStruct((N_SLAB, LANES, 2 * SLAB_STATES), BF16),
             jax.ShapeDtypeStruct((N_SLAB, LANES, 2 * SLAB_STATES), BF16),
             jax.ShapeDtypeStruct((N_SLAB, LANES, LANES), BF16)]
    whole = lambda a: pl.BlockSpec(a.shape, lambda b, i: (0,) * len(a.shape))
    out, hre, him, *step_tables = pl.pallas_call(
        _ssm_kernel,
        grid=(batch, steps),
        in_specs=[blk] + cspecs,
        out_specs=[blk, st, st] + [whole(s) for s in small],
        out_shape=[jax.ShapeDtypeStruct(u.shape, F32),
                   jax.ShapeDtypeStruct((batch, SUBLANES, N_STATES), F32),
                   jax.ShapeDtypeStruct((batch, SUBLANES, N_STATES), F32)] + small,
        scratch_shapes=_ssm_table_scratch() + [pltpu.VMEM((N_SLAB, nc, t1 * LANES), BF16),
                                               pltpu.VMEM((nc, N_STATES), F32),
                                               pltpu.VMEM((nc, N_STATES), F32),
                                               pltpu.VMEM((SUBLANES, N_STATES), F32),
                                               pltpu.VMEM((SUBLANES, N_STATES), F32)],
        compiler_params=pltpu.CompilerParams(dimension_semantics=("arbitrary", "arbitrary"),
                                             vmem_limit_bytes=VMEM_LIMIT_BYTES),
        name=name,
    )(u, *consts)
    return out, hre[:, 0], him[:, 0], step_tables


def _ssm_sample_kernel(u_ref, h0r_ref, h0i_ref, wst_ref, cat_ref, kd_ref, a_ref, d_ref, gluw_ref, glub_ref, g_ref,
                       o_ref, hr_ref, hi_ref):
    ys = []
    for j in range(N_SLAB):
        cols = slice(j * SLAB_STATES, (j + 1) * SLAB_STATES)
        uf = u_ref[j]
        ub = uf.astype(BF16)
        st = jnp.dot(ub, wst_ref[j], preferred_element_type=F32)
        h0r = h0r_ref[:, cols]
        h0i = h0i_ref[:, cols]
        ar = a_ref[0, :, cols]
        ai = a_ref[1, :, cols]
        hr_ref[:, cols] = ar * h0r - ai * h0i + st[:, :SLAB_STATES]
        hi_ref[:, cols] = ar * h0i + ai * h0r + st[:, SLAB_STATES:]
        hcat = jnp.concatenate([h0r, h0i], axis=1).astype(BF16)
        y = lax.dot_general(hcat, cat_ref[j], (((1,), (1,)), ((), ())), preferred_element_type=F32)
        y = y + jnp.dot(ub, kd_ref[j], preferred_element_type=F32)
        ys.append(y + d_ref[j] * uf)
    n = _glu_rms(jnp.concatenate(ys, axis=1), gluw_ref, glub_ref, g_ref)
    for j in range(N_SLAB):
        o_ref[j] = n[:, j * LANES:(j + 1) * LANES]


def _ssm_sample_call(u, h0r, h0i, step_tables, a_flat, d_t, gluw, glub, g, *, name):
    n_seq = u.shape[1]
    whole = lambda a: pl.BlockSpec(a.shape, lambda i: (0,) * len(a.shape))
    outs = [jax.ShapeDtypeStruct(u.shape, F32),
            jax.ShapeDtypeStruct((n_seq, N_STATES), F32),
            jax.ShapeDtypeStruct((n_seq, N_STATES), F32)]
    consts, cspecs = _consts(list(step_tables) + [a_flat, d_t, gluw, glub, g])
    return pl.pallas_call(
        _ssm_sample_kernel,
        grid=(1,),
        in_specs=[whole(u), whole(h0r), whole(h0i)] + cspecs,
        out_specs=[whole(o) for o in outs],
        out_shape=outs,
        compiler_params=pltpu.CompilerParams(dimension_semantics=("arbitrary",),
                                             vmem_limit_bytes=VMEM_LIMIT_BYTES),
        name=name,
    )(u, h0r, h0i, *consts)


def _attn_kernel(sink_ref, q_ref, kvc_ref, kvp_ref, g_ref, o_ref):
    i = pl.program_id(1)
    kj = lax.broadcasted_iota(jnp.int32, (2 * WINDOW, WINDOW), 0)
    qi = lax.broadcasted_iota(jnp.int32, (2 * WINDOW, WINDOW), 1)
    band = (kj >= qi) & (kj <= qi + WINDOW)
    low = qi < HEAD_DIM
    for sb in range(ATTN_BLOCKS):
        blk = slice(sb * WINDOW, (sb + 1) * WINDOW)
        kc = kvc_ref[blk, :]
        if sb == 0:
            kp = kvp_ref[...]
            valid = band & ((i > 0) | (kj >= WINDOW))
        else:
            kp = kvc_ref[(sb - 1) * WINDOW:sb * WINDOW, :]
            valid = band
        valid2 = jnp.concatenate([valid, valid], axis=1)
        kcat = jnp.concatenate([kp[:, :KV_WIDTH], kc[:, :KV_WIDTH]], axis=0) * ATTN_SCALE
        swap = pltpu.roll(kcat, HEAD_DIM, 1)
        zero = jnp.zeros_like(kcat)
        k_sel = [[jnp.where(low, kcat, zero).astype(BF16), jnp.where(low, zero, swap).astype(BF16)],
                 [jnp.where(low, swap, zero).astype(BF16), jnp.where(low, zero, kcat).astype(BF16)]]
        v_t = jnp.concatenate([kp[:, KV_WIDTH:], kc[:, KV_WIDTH:]], axis=0).T.astype(BF16)
        parts = []
        for kvh in range(N_KV_HEADS):
            q_pair = jnp.concatenate([q_ref[blk, (2 * kvh) * LANES:(2 * kvh + 1) * LANES],
                                      q_ref[blk, (2 * kvh + 1) * LANES:(2 * kvh + 2) * LANES]], axis=0)
            for parity in range(2):
                s_t = lax.dot_general(k_sel[kvh][parity], q_pair, (((1,), (1,)), ((), ())),
                                      preferred_element_type=F32)
                parts.append(jnp.where(valid2, s_t, MASKED))
        s_all = jnp.concatenate(parts, axis=1)
        sink = sink_ref[...]
        m = jnp.maximum(jnp.max(s_all, axis=0, keepdims=True), sink)
        p_all = jnp.exp(s_all - m)
        inv_den = 1.0 / (jnp.sum(p_all, axis=0, keepdims=True) + jnp.exp(sink - m))
        p_all = p_all.astype(BF16)
        half = Q_PER_KV * WINDOW
        o_kv = [jnp.dot(v_t[kvh * HEAD_DIM:(kvh + 1) * HEAD_DIM], p_all[:, kvh * half:(kvh + 1) * half],
                        preferred_element_type=F32) * inv_den[:, kvh * half:(kvh + 1) * half]
                for kvh in range(N_KV_HEADS)]
        tiles = []
        for pr in range(Q_PER_KV):
            lanes = slice(ATTN_HEAD_ORDER.index(pr) * WINDOW, (ATTN_HEAD_ORDER.index(pr) + 1) * WINDOW)
            tiles.append(jnp.concatenate([o_kv[0][:, lanes], o_kv[1][:, lanes]], axis=0))
        ss = None
        for tile in tiles:
            t = jnp.sum(tile * tile, axis=0, keepdims=True)
            ss = t if ss is None else ss + t
        inv = lax.rsqrt(ss / ATTN_WIDTH + RMS_EPS)
        for pr, tile in enumerate(tiles):
            o_ref[blk, pr * LANES:(pr + 1) * LANES] = ((tile * inv).T * g_ref[pr]).astype(o_ref.dtype)


def _attn_prompt_call(q, kv, sinks, g_perm, *, batch, seq, name):
    rows = ATTN_BLOCKS * WINDOW
    nb = seq // rows
    M = kv.shape[0]
    cur = lambda b, i: (b * nb + i, 0)
    prev = lambda b, i: ((b * nb + i) * ATTN_BLOCKS - jnp.minimum(i, 1), 0)
    (sinks, g_perm), (sink_spec, g_spec) = _consts([sinks, g_perm], 2)
    return pl.pallas_call(
        _attn_kernel,
        grid=(batch, nb),
        in_specs=[sink_spec,
                  pl.BlockSpec((rows, ATTN_WIDTH), cur),
                  pl.BlockSpec((rows, 2 * KV_WIDTH), cur),
                  pl.BlockSpec((WINDOW, 2 * KV_WIDTH), prev),
                  g_spec],
        out_specs=pl.BlockSpec((rows, ATTN_WIDTH), cur),
        out_shape=jax.ShapeDtypeStruct((M, ATTN_WIDTH), BF16),
        compiler_params=pltpu.CompilerParams(dimension_semantics=("arbitrary", "arbitrary"),
                                             vmem_limit_bytes=VMEM_LIMIT_BYTES),
        name=name,
    )(sinks, q, kv, kv, g_perm)


SEQ_PER_STEP = LANES // N_HEADS


def _attn_sample_kernel(q_ref, kv_ref, ckt_ref, cvt_ref, sink_ref, g_ref, own_ref, o_ref, kot_ref, vot_ref):
    nrow = SEQ_PER_STEP * N_HEADS
    qb = q_ref[...].astype(BF16)
    k_new = kv_ref[:, :KV_WIDTH]
    v_new = kv_ref[:, KV_WIDTH:]
    row_seq = lax.broadcasted_iota(jnp.int32, (nrow, LANES), 0) // N_HEADS
    lane_seq = lax.broadcasted_iota(jnp.int32, (nrow, LANES), 1) // N_HEADS
    lane = lax.broadcasted_iota(jnp.int32, (nrow, LANES), 1)
    pick = (lax.broadcasted_iota(jnp.int32, (nrow, SEQ_PER_STEP), 0) // N_HEADS
            == lax.broadcasted_iota(jnp.int32, (nrow, SEQ_PER_STEP), 1)).astype(BF16)

    qf = q_ref[...]
    heads_of = lambda a, n: a[n * N_HEADS:(n + 1) * N_HEADS]
    s = jnp.concatenate([jnp.dot(heads_of(qf, n).astype(BF16), ckt_ref[n].astype(BF16), preferred_element_type=F32)
                         for n in range(SEQ_PER_STEP)], axis=0) * ATTN_SCALE
    k_rows = jnp.dot(pick, k_new.astype(BF16), preferred_element_type=F32)
    s_new = jnp.sum(qb.astype(F32) * k_rows, axis=-1, keepdims=True) * ATTN_SCALE
    sink = sink_ref[...]
    m = jnp.maximum(jnp.maximum(jnp.max(s, axis=-1, keepdims=True), s_new), sink)
    p = jnp.exp(s - m)
    p_new = jnp.exp(s_new - m)
    inv = 1.0 / (jnp.sum(p, axis=-1, keepdims=True) + p_new + jnp.exp(sink - m))
    pn = p * inv
    o = jnp.concatenate([lax.dot_general(heads_of(pn, n).astype(BF16), cvt_ref[n].astype(BF16),
                                         (((1,), (1,)), ((), ())), preferred_element_type=F32)
                         for n in range(SEQ_PER_STEP)], axis=0)
    v_rows = jnp.dot(pick, v_new.astype(BF16), preferred_element_type=F32)
    o = o + (p_new * inv).astype(BF16).astype(F32) * v_rows

    hi = lax.Precision.HIGHEST
    row_sums = jnp.dot(o * o * own_ref[...], jnp.ones((LANES, LANES), F32), precision=hi, preferred_element_type=F32)
    ss = jnp.dot((row_seq == lane_seq).astype(F32), row_sums, precision=hi, preferred_element_type=F32)
    o_ref[...] = o * lax.rsqrt(ss / ATTN_WIDTH + RMS_EPS) * g_ref[...]

    pad = jnp.zeros((LANES - SEQ_PER_STEP, KV_WIDTH), F32)
    k_cols = jnp.concatenate([k_new, pad], axis=0).T
    v_cols = jnp.concatenate([v_new, pad], axis=0).T
    last = lane == WINDOW - 1
    for n in range(SEQ_PER_STEP):
        kot_ref[n] = jnp.where(last, pltpu.roll(k_cols, WINDOW - 1 - n, 1), pltpu.roll(ckt_ref[n], WINDOW - 1, 1))
        vot_ref[n] = jnp.where(last, pltpu.roll(v_cols, WINDOW - 1 - n, 1), pltpu.roll(cvt_ref[n], WINDOW - 1, 1))


def _attn_sample_call(q, kv, ck, cv, layer, sink_row, g_rows, own_rows, *, name):
    n_seq = kv.shape[0]
    assert n_seq % SEQ_PER_STEP == 0
    nrow = SEQ_PER_STEP * N_HEADS
    cblk = pl.BlockSpec((SEQ_PER_STEP, WINDOW, 2 * HEAD_DIM), lambda i: (i, 0, 0))
    cin = pl.BlockSpec((None, SEQ_PER_STEP, WINDOW, 2 * HEAD_DIM), lambda i: (layer, i, 0, 0))
    qblk = pl.BlockSpec((nrow, LANES), lambda i: (i, 0))
    consts, cspecs = _consts([sink_row, g_rows, own_rows])
    return pl.pallas_call(
        _attn_sample_kernel,
        grid=(n_seq // SEQ_PER_STEP,),
        in_specs=[qblk, pl.BlockSpec((SEQ_PER_STEP, 2 * KV_WIDTH), lambda i: (i, 0)), cin, cin] + cspecs,
        out_specs=[qblk, cblk, cblk],
        out_shape=[jax.ShapeDtypeStruct(q.shape, F32),
                   jax.ShapeDtypeStruct(ck.shape[1:], F32),
                   jax.ShapeDtypeStruct(cv.shape[1:], F32)],
        compiler_params=pltpu.CompilerParams(dimension_semantics=("arbitrary",),
                                             vmem_limit_bytes=VMEM_LIMIT_BYTES),
        name=name,
    )(q, kv, ck, cv, *consts)


def _expand_heads(a):
    lead = a.shape[:-1]
    a = a.reshape(lead + (N_KV_HEADS, Q_PER_KV, 1, HEAD_DIM))
    sel = jnp.eye(N_KV_HEADS, dtype=a.dtype).reshape(N_KV_HEADS, 1, N_KV_HEADS, 1)
    return (a * sel).reshape(lead + (Q_EXP,))


def _pair_heads(a):
    lead = a.shape[:-1]
    a = a.reshape(lead + (N_KV_HEADS, Q_PER_KV, HEAD_DIM))
    return jnp.swapaxes(a, -3, -2).reshape(lead + (ATTN_WIDTH,))


def _prep_weights(ln_g, ln_b, ffn1_w_in, ffn1_w_out, ffn2_w_in, ffn2_w_out, w_in, ssm_lam_re, ssm_lam_im, ssm_log_dt,
                  ssm_b_re, ssm_b_im, ssm_c_re, ssm_c_im, ssm_d, glu_w, glu_b, attn_sinks, g_ssm_out, g_attn_out,
                  w_out):
    rows_t = lambda a: jnp.swapaxes(a, 1, 2)
    params, a_flat = _ssm_table_inputs(ssm_lam_re, ssm_lam_im, ssm_log_dt, ssm_b_re, ssm_b_im, ssm_c_re, ssm_c_im)
    head_order = jnp.array([Q_PER_KV * kvh + o for kvh in range(N_KV_HEADS) for o in ATTN_HEAD_ORDER])
    g_exp = _expand_heads(g_attn_out).reshape(DEPTH, N_HEADS, LANES)
    w_in_b = w_in.astype(BF16)
    w_out_b = w_out.astype(BF16)
    stacked = dict(
        w_in=w_in_b,
        wo_s=w_out_b[:, :SSM_WIDTH],
        wo_a=rows_t(_pair_heads(rows_t(w_out_b[:, SSM_WIDTH:]))),
        ssm_params=params, a_flat=a_flat,
        d_1=ssm_d.reshape(DEPTH, N_SLAB, 1, LANES),
        gluw=glu_w.astype(BF16),
        glub=glu_b.reshape(DEPTH, 1, 2 * SSM_WIDTH),
        g_ssm=g_ssm_out.reshape(DEPTH, 1, SSM_WIDTH),
        sinks=jnp.repeat(attn_sinks[:, head_order], WINDOW, axis=1)[:, None, :],
        g_pair=_pair_heads(g_attn_out).reshape(DEPTH, Q_PER_KV, 1, LANES),
        sink_rows=jnp.tile(attn_sinks, (1, SEQ_PER_STEP))[:, :, None],
        g_rows=jnp.tile(g_exp, (1, SEQ_PER_STEP, 1)),
    )
    w = {name: [(arr, l) for l in range(DEPTH)] for name, arr in stacked.items()}
    w['ffn1'] = [(ffn1_w_in, ffn1_w_out, l) for l in range(DEPTH)]
    w['ffn2'] = [(ffn2_w_in, ffn2_w_out, l) for l in range(DEPTH)]
    n_ln = ln_g.shape[1]
    ln_g3 = ln_g.reshape(DEPTH * n_ln, 1, D_MODEL)
    ln_b3 = ln_b.reshape(DEPTH * n_ln, 1, D_MODEL)
    w['ln'] = [[((ln_g3, l * n_ln + i), (ln_b3, l * n_ln + i)) for i in range(n_ln)] for l in range(DEPTH)]
    w['own_rows'] = jnp.tile(_expand_heads(jnp.ones((ATTN_WIDTH,), F32)).reshape(N_HEADS, LANES), (SEQ_PER_STEP, 1))
    return w


def _prompt_mixer(u, q, kv, l, w, batch, seq):
    ssm_n, hre, him, step_tables = _ssm_prompt_call(u, w['ssm_params'][l], w['a_flat'][l], w['d_1'][l], w['gluw'][l],
                                                    w['glub'][l], w['g_ssm'][l], batch=batch, seq=seq,
                                                    name=f"p_ssm_{l}")
    att_n = _attn_prompt_call(q, kv, w['sinks'][l], w['g_pair'][l], batch=batch, seq=seq, name=f"p_attn_{l}")
    kvw = kv.reshape(batch, seq, 2 * KV_WIDTH)[:, -WINDOW:].reshape(batch, WINDOW, 2, N_KV_HEADS, HEAD_DIM)
    return (ssm_n, att_n, hre.reshape(batch, N_SSM_GROUPS, SSM_STATE), him.reshape(batch, N_SSM_GROUPS, SSM_STATE),
            kvw[:, :, 0], kvw[:, :, 1]), step_tables


def _sample_mixer(u, q, kv, l, w, step_tables, h0_re, h0_im, k_buf, v_buf):
    n_seq = kv.shape[0]
    ssm_n, hre, him = _ssm_sample_call(u, h0_re.reshape(n_seq, N_STATES), h0_im.reshape(n_seq, N_STATES),
                                       step_tables, w['a_flat'][l], w['d_1'][l], w['gluw'][l], w['glub'][l],
                                       w['g_ssm'][l], name=f"s_ssm_{l}")
    att, kn, vn = _attn_sample_call(q.reshape(n_seq * N_HEADS, LANES), kv, k_buf, v_buf, l,
                                    w['sink_rows'][l], w['g_rows'][l], w['own_rows'], name=f"s_attn_{l}")
    untranspose = lambda t: jnp.transpose(t.reshape(n_seq, N_KV_HEADS, HEAD_DIM, WINDOW), (0, 3, 1, 2))
    return (ssm_n, att.reshape(n_seq, Q_EXP), hre.reshape(n_seq, N_SSM_GROUPS, SSM_STATE),
            him.reshape(n_seq, N_SSM_GROUPS, SSM_STATE), untranspose(kn), untranspose(vn))


def _trunks(x_prompt, x_sample, h0_re, h0_im, k_buf, v_buf, w, *, tm):
    Bn, L, _ = x_prompt.shape
    n_seq = x_sample.shape[0]
    xp = x_prompt.reshape(Bn * L, D_MODEL)
    xs = x_sample.reshape(n_seq, D_MODEL)
    outs_p, outs_s = [], []
    mix_p = mix_s = None
    for l in range(DEPTH + 1):
        if l > 0:
            (xp,), (xs,) = _rows_call(xp, w['ffn2'][l - 1], w['ln'][l - 1][2], mix=mix_p,
                                      sample=dict(x=xs, mix=mix_s), tm=tm, name=f"mix_ffn2_{l - 1}")
        if l == DEPTH:
            break
        (xp, u, q, kv), (xs, us, qs, kvs) = _rows_call(xp, w['ffn1'][l], w['ln'][l][0], proj=w['w_in'][l],
                                                       sample=dict(x=xs), tm=tm,
                                                       name=f"ffn1_{l}")
        (ssm_p, att_p, *state_p), step_tables = _prompt_mixer(u, q, kv, l, w, Bn, L)
        ssm_s, att_s, *state_s = _sample_mixer(us, qs, kvs, l, w, step_tables, h0_re[l], h0_im[l], k_buf, v_buf)
        mix_p = (ssm_p, att_p, w['wo_s'][l], w['wo_a'][l], w['ln'][l][1][0], w['ln'][l][1][1])
        mix_s = (ssm_s, att_s)
        outs_p.append(state_p)
        outs_s.append(state_s)
    stack = lambda outs: tuple(jnp.stack([o[i] for o in outs]) for i in (2, 3, 0, 1))
    return (xp.reshape(Bn, L, D_MODEL), xs.reshape(n_seq, 1, D_MODEL)), stack(outs_p), stack(outs_s)


def kernel(x_prompt, x_sample, cache_k_win, cache_v_win, state_ssm_re, state_ssm_im, ln_g, ln_b, ffn1_w_in, ffn1_w_out, ffn2_w_in, ffn2_w_out, w_in, ssm_lam_re, ssm_lam_im, ssm_log_dt, ssm_b_re, ssm_b_im, ssm_c_re, ssm_c_im, ssm_d, glu_w, glu_b, attn_sinks, g_ssm_out, g_attn_out, w_out):
    w = _prep_weights(ln_g, ln_b, ffn1_w_in, ffn1_w_out, ffn2_w_in, ffn2_w_out, w_in, ssm_lam_re, ssm_lam_im,
                      ssm_log_dt, ssm_b_re, ssm_b_im, ssm_c_re, ssm_c_im, ssm_d, glu_w, glu_b, attn_sinks,
                      g_ssm_out, g_attn_out, w_out)
    n_seq = x_sample.shape[0]
    transposed = lambda c: jnp.transpose(c, (0, 1, 3, 4, 2)).reshape(DEPTH, n_seq, N_KV_HEADS * HEAD_DIM, WINDOW)
    (y_prompt, y_sample), (kp, vp, hrp, hip), (ks_, vs_, hrs, his) = _trunks(
        x_prompt, x_sample, state_ssm_re, state_ssm_im, transposed(cache_k_win), transposed(cache_v_win), w, tm=512)
    return (y_prompt, y_sample, kp, vp, hrp, hip, ks_, vs_, hrs, his)
```
